```python
import jax, jax.numpy as jnp
from jax import lax
import numpy as np

D_MODEL = 2048
BATCH = 8
SEQ = 2048
DEPTH = 1

CHUNK = 64
N_META = 16
D_MIX = D_MODEL
D_POOL = D_MIX // 2
POOL_WINDOWS = (2, 4, 8, 16)
N_POOL_GROUPS = len(POOL_WINDOWS)
POOL_GROUP = D_POOL // N_POOL_GROUPS
D_ATT = D_MIX - D_POOL
HEAD_DIM = 128
N_HEADS = D_ATT // HEAD_DIM
D_IN = D_POOL + 3 * D_ATT + N_HEADS
D_FF = ((8 * D_MODEL // 3 + 255) // 256) * 256
Q_BLOCK = 128
EPS = 1e-6

kernel_name = "hymba_pool_fox_macaron_block"


def rmsnorm(x, g):
    xf = x.astype(jnp.float32)
    y = xf * lax.rsqrt(jnp.mean(xf * xf, axis=-1, keepdims=True) + EPS)
    return (y * g.astype(jnp.float32)).astype(x.dtype)


def swiglu(x, w_gate, w_up, w_down):
    return (jax.nn.silu(x @ w_gate) * (x @ w_up)) @ w_down


def pool_mixer(p, pool_w, pool_scale):
    B, L, _ = p.shape
    pg = p.reshape(B, L, N_POOL_GROUPS, POOL_GROUP)
    c = jnp.cumsum(pg.astype(jnp.float32), axis=1)
    c = jnp.pad(c, ((0, 0), (1, 0), (0, 0), (0, 0)))
    win = jnp.array(POOL_WINDOWS, dtype=jnp.int32)
    end = jnp.arange(1, L + 1, dtype=jnp.int32)[:, None]
    start = jnp.maximum(end - win[None, :], 0)
    gidx = jnp.arange(N_POOL_GROUPS, dtype=jnp.int32)[None, :]
    window_sum = c[:, end, gidx] - c[:, start, gidx]
    count = (end - start).astype(jnp.float32)[None, :, :, None]
    pooled = (window_sum / count - pg.astype(jnp.float32)).astype(p.dtype)
    mixed = jnp.einsum('blgc,gcd->blgd', pooled, pool_w)
    return mixed.reshape(B, L, D_POOL) * pool_scale


def fox_attention(q, k, v, log_f):
    B, L, H, Dh = q.shape
    scale = 1.0 / np.sqrt(Dh).astype(np.float32)
    cum = jnp.cumsum(log_f, axis=-1)
    n_blocks = -(-L // Q_BLOCK)
    Lp = n_blocks * Q_BLOCK
    qp = jnp.pad(q, ((0, 0), (0, Lp - L), (0, 0), (0, 0)))
    cqp = jnp.pad(cum, ((0, 0), (0, 0), (0, Lp - L)))
    qb = qp.reshape(B, n_blocks, Q_BLOCK, H, Dh).transpose(1, 0, 2, 3, 4)
    cqb = cqp.reshape(B, H, n_blocks, Q_BLOCK).transpose(2, 0, 1, 3)
    kpos = jnp.arange(L, dtype=jnp.int32)

    def one_block(args):
        qi, cqi, bi = args
        qpos = bi * Q_BLOCK + jnp.arange(Q_BLOCK, dtype=jnp.int32)
        s = jnp.einsum('bqhd,bkhd->bhqk', qi, k).astype(jnp.float32) * scale
        s = s + (cqi[:, :, :, None] - cum[:, :, None, :])
        s = jnp.where(qpos[:, None] >= kpos[None, :], s, -jnp.inf)
        pr = jax.nn.softmax(s, axis=-1)
        return jnp.einsum('bhqk,bkhd->bqhd', pr.astype(v.dtype), v)

    out = lax.map(one_block, (qb, cqb, jnp.arange(n_blocks, dtype=jnp.int32)))
    out = out.transpose(1, 0, 2, 3, 4).reshape(B, Lp, H * Dh)
    return out[:, :L]


def _fwd_setup_inputs(seed: int = 0) -> dict:
    key = jax.random.key(seed)
    ks = jax.random.split(key, 20)
    f32 = jnp.float32

    def nrm(k, shape, s):
        return jax.random.normal(k, shape, f32) * s

    def gain(k, shape):
        return 1.0 + 0.05 * jax.random.normal(k, shape, f32)

    return {
        "x": jax.random.normal(ks[0], (BATCH, SEQ, D_MODEL), f32),
        "meta_tokens": nrm(ks[1], (N_META, D_MODEL), 1.0),
        "ffn1_norm": gain(ks[2], (DEPTH, D_MODEL)),
        "ffn1_w_gate": nrm(ks[3], (DEPTH, D_MODEL, D_FF), D_MODEL ** -0.5),
        "ffn1_w_up": nrm(ks[4], (DEPTH, D_MODEL, D_FF), D_MODEL ** -0.5),
        "ffn1_w_down": nrm(ks[5], (DEPTH, D_FF, D_MODEL), D_FF ** -0.5),
        "mix_norm": gain(ks[6], (DEPTH, D_MODEL)),
        "w_in": nrm(ks[7], (DEPTH, D_MODEL, D_IN), D_MODEL ** -0.5),
        "b_forget": jax.random.uniform(ks[8], (DEPTH, N_HEADS), f32, minval=1.0, maxval=5.0),
        "q_norm": gain(ks[9], (DEPTH, HEAD_DIM)),
        "k_norm": gain(ks[10], (DEPTH, HEAD_DIM)),
        "pool_w": nrm(ks[11], (DEPTH, N_POOL_GROUPS, POOL_GROUP, POOL_GROUP), POOL_GROUP ** -0.5),
        "pool_scale": 1.0 + 0.1 * jax.random.normal(ks[12], (DEPTH, D_POOL), f32),
        "w_out": nrm(ks[13], (DEPTH, D_MIX, D_MODEL), D_MIX ** -0.5),
        "ffn2_norm": gain(ks[14], (DEPTH, D_MODEL)),
        "ffn2_w_gate": nrm(ks[15], (DEPTH, D_MODEL, D_FF), D_MODEL ** -0.5),
        "ffn2_w_up": nrm(ks[16], (DEPTH, D_MODEL, D_FF), D_MODEL ** -0.5),
        "ffn2_w_down": nrm(ks[17], (DEPTH, D_FF, D_MODEL), D_FF ** -0.5),
    }


def _fwd_reference(x, meta_tokens, ffn1_norm, ffn1_w_gate, ffn1_w_up, ffn1_w_down, mix_norm, w_in,
              b_forget, q_norm, k_norm, pool_w, pool_scale, w_out, ffn2_norm, ffn2_w_gate,
              ffn2_w_up, ffn2_w_down):
    B = x.shape[0]
    meta = jnp.broadcast_to(meta_tokens[None].astype(x.dtype), (B, N_META, D_MODEL))
    h = jnp.concatenate([meta, x], axis=1)
    L = h.shape[1]
    for i in range(DEPTH):
        h = h + 0.5 * swiglu(rmsnorm(h, ffn1_norm[i]), ffn1_w_gate[i], ffn1_w_up[i], ffn1_w_down[i])

        u = rmsnorm(h, mix_norm[i])
        z = u @ w_in[i]
        o = D_POOL
        p = z[..., :o]
        q = z[..., o:o + D_ATT].reshape(B, L, N_HEADS, HEAD_DIM)
        k = z[..., o + D_ATT:o + 2 * D_ATT].reshape(B, L, N_HEADS, HEAD_DIM)
        v = z[..., o + 2 * D_ATT:o + 3 * D_ATT].reshape(B, L, N_HEADS, HEAD_DIM)
        f_logit = z[..., o + 3 * D_ATT:]

        pool_out = pool_mixer(p, pool_w[i], pool_scale[i])

        q = rmsnorm(q, q_norm[i])
        k = rmsnorm(k, k_norm[i])
        log_f = jax.nn.log_sigmoid(f_logit.astype(jnp.float32) + b_forget[i].astype(jnp.float32))
        att_out = fox_attention(q, k, v, log_f.transpose(0, 2, 1))

        mix = jnp.concatenate([pool_out, att_out.astype(pool_out.dtype)], axis=-1)
        h = h + mix @ w_out[i]

        h = h + 0.5 * swiglu(rmsnorm(h, ffn2_norm[i]), ffn2_w_gate[i], ffn2_w_up[i], ffn2_w_down[i])
    return h[:, N_META:]


import jax as _jax
import jax.numpy as _jnp

TWIN_FORMAT = 'train_step'
FWD_PARAMS = ['x', 'meta_tokens', 'ffn1_norm', 'ffn1_w_gate', 'ffn1_w_up', 'ffn1_w_down', 'mix_norm', 'w_in', 'b_forget', 'q_norm', 'k_norm', 'pool_w', 'pool_scale', 'w_out', 'ffn2_norm', 'ffn2_w_gate', 'ffn2_w_up', 'ffn2_w_down']
TWIN_WEIGHTS = ['meta_tokens', 'ffn1_norm', 'ffn1_w_gate', 'ffn1_w_up', 'ffn1_w_down', 'mix_norm', 'w_in', 'b_forget', 'q_norm', 'k_norm', 'pool_w', 'pool_scale', 'w_out', 'ffn2_norm', 'ffn2_w_gate', 'ffn2_w_up', 'ffn2_w_down']
TWIN_DIFF_INPUT = 'x'
TWIN_INPUTS = ['x', 'meta_tokens', 'ffn1_norm', 'ffn1_w_gate', 'ffn1_w_up', 'ffn1_w_down', 'mix_norm', 'w_in', 'b_forget', 'q_norm', 'k_norm', 'pool_w', 'pool_scale', 'w_out', 'ffn2_norm', 'ffn2_w_gate', 'ffn2_w_up', 'ffn2_w_down', 'loss_target', 'm_meta_tokens', 'm_ffn1_norm', 'm_ffn1_w_gate', 'm_ffn1_w_up', 'm_ffn1_w_down', 'm_mix_norm', 'm_w_in', 'm_b_forget', 'm_q_norm', 'm_k_norm', 'm_pool_w', 'm_pool_scale', 'm_w_out', 'm_ffn2_norm', 'm_ffn2_w_gate', 'm_ffn2_w_up', 'm_ffn2_w_down', 'v_meta_tokens', 'v_ffn1_norm', 'v_ffn1_w_gate', 'v_ffn1_w_up', 'v_ffn1_w_down', 'v_mix_norm', 'v_w_in', 'v_b_forget', 'v_q_norm', 'v_k_norm', 'v_pool_w', 'v_pool_scale', 'v_w_out', 'v_ffn2_norm', 'v_ffn2_w_gate', 'v_ffn2_w_up', 'v_ffn2_w_down']
TWIN_OUTPUTS = ['loss', 'grad_x', 'grad_meta_tokens', 'grad_ffn1_norm', 'grad_ffn1_w_gate', 'grad_ffn1_w_up', 'grad_ffn1_w_down', 'grad_mix_norm', 'grad_w_in', 'grad_b_forget', 'grad_q_norm', 'grad_k_norm', 'grad_pool_w', 'grad_pool_scale', 'grad_w_out', 'grad_ffn2_norm', 'grad_ffn2_w_gate', 'grad_ffn2_w_up', 'grad_ffn2_w_down', 'delta_meta_tokens', 'delta_ffn1_norm', 'delta_ffn1_w_gate', 'delta_ffn1_w_up', 'delta_ffn1_w_down', 'delta_mix_norm', 'delta_w_in', 'delta_b_forget', 'delta_q_norm', 'delta_k_norm', 'delta_pool_w', 'delta_pool_scale', 'delta_w_out', 'delta_ffn2_norm', 'delta_ffn2_w_gate', 'delta_ffn2_w_up', 'delta_ffn2_w_down', 'new_m_meta_tokens', 'new_m_ffn1_norm', 'new_m_ffn1_w_gate', 'new_m_ffn1_w_up', 'new_m_ffn1_w_down', 'new_m_mix_norm', 'new_m_w_in', 'new_m_b_forget', 'new_m_q_norm', 'new_m_k_norm', 'new_m_pool_w', 'new_m_pool_scale', 'new_m_w_out', 'new_m_ffn2_norm', 'new_m_ffn2_w_gate', 'new_m_ffn2_w_up', 'new_m_ffn2_w_down', 'new_v_meta_tokens', 'new_v_ffn1_norm', 'new_v_ffn1_w_gate', 'new_v_ffn1_w_up', 'new_v_ffn1_w_down', 'new_v_mix_norm', 'new_v_w_in', 'new_v_b_forget', 'new_v_q_norm', 'new_v_k_norm', 'new_v_pool_w', 'new_v_pool_scale', 'new_v_w_out', 'new_v_ffn2_norm', 'new_v_ffn2_w_gate', 'new_v_ffn2_w_up', 'new_v_ffn2_w_down']
TWIN_LEAF_KINDS = {'loss': 'loss', 'grad_x': 'grad_x', 'grad_meta_tokens': 'grad_w', 'grad_ffn1_norm': 'grad_w', 'grad_ffn1_w_gate': 'grad_w', 'grad_ffn1_w_up': 'grad_w', 'grad_ffn1_w_down': 'grad_w', 'grad_mix_norm': 'grad_w', 'grad_w_in': 'grad_w', 'grad_b_forget': 'grad_w', 'grad_q_norm': 'grad_w', 'grad_k_norm': 'grad_w', 'grad_pool_w': 'grad_w', 'grad_pool_scale': 'grad_w', 'grad_w_out': 'grad_w', 'grad_ffn2_norm': 'grad_w', 'grad_ffn2_w_gate': 'grad_w', 'grad_ffn2_w_up': 'grad_w', 'grad_ffn2_w_down': 'grad_w', 'delta_meta_tokens': 'delta_w', 'delta_ffn1_norm': 'delta_w', 'delta_ffn1_w_gate': 'delta_w', 'delta_ffn1_w_up': 'delta_w', 'delta_ffn1_w_down': 'delta_w', 'delta_mix_norm': 'delta_w', 'delta_w_in': 'delta_w', 'delta_b_forget': 'delta_w', 'delta_q_norm': 'delta_w', 'delta_k_norm': 'delta_w', 'delta_pool_w': 'delta_w', 'delta_pool_scale': 'delta_w', 'delta_w_out': 'delta_w', 'delta_ffn2_norm': 'delta_w', 'delta_ffn2_w_gate': 'delta_w', 'delta_ffn2_w_up': 'delta_w', 'delta_ffn2_w_down': 'delta_w', 'new_m_meta_tokens': 'new_m', 'new_m_ffn1_norm': 'new_m', 'new_m_ffn1_w_gate': 'new_m', 'new_m_ffn1_w_up': 'new_m', 'new_m_ffn1_w_down': 'new_m', 'new_m_mix_norm': 'new_m', 'new_m_w_in': 'new_m', 'new_m_b_forget': 'new_m', 'new_m_q_norm': 'new_m', 'new_m_k_norm': 'new_m', 'new_m_pool_w': 'new_m', 'new_m_pool_scale': 'new_m', 'new_m_w_out': 'new_m', 'new_m_ffn2_norm': 'new_m', 'new_m_ffn2_w_gate': 'new_m', 'new_m_ffn2_w_up': 'new_m', 'new_m_ffn2_w_down': 'new_m', 'new_v_meta_tokens': 'new_v', 'new_v_ffn1_norm': 'new_v', 'new_v_ffn1_w_gate': 'new_v', 'new_v_ffn1_w_up': 'new_v', 'new_v_ffn1_w_down': 'new_v', 'new_v_mix_norm': 'new_v', 'new_v_w_in': 'new_v', 'new_v_b_forget': 'new_v', 'new_v_q_norm': 'new_v', 'new_v_k_norm': 'new_v', 'new_v_pool_w': 'new_v', 'new_v_pool_scale': 'new_v', 'new_v_w_out': 'new_v', 'new_v_ffn2_norm': 'new_v', 'new_v_ffn2_w_gate': 'new_v', 'new_v_ffn2_w_up': 'new_v', 'new_v_ffn2_w_down': 'new_v'}


def _forward(args):
    return _fwd_reference(*[args[k] for k in FWD_PARAMS])


def _output_shape():
    out = _jax.eval_shape(lambda: _forward(_fwd_setup_inputs(0)))
    return out.shape, out.dtype

N_MICROBATCH = 1
ADAM_LR = 0.001
ADAM_B1 = 0.9
ADAM_B2 = 0.999
ADAM_EPS = 1e-08
ADAM_WD = 0.01
ADAM_STEP = 10
PER_EXAMPLE_BATCH_AXIS = {'x': 0, 'loss_target': 0}
SHARED_INPUTS = []
_WEIGHT_DTYPES = {'meta_tokens': _jnp.float32, 'ffn1_norm': _jnp.float32, 'ffn1_w_gate': _jnp.float32, 'ffn1_w_up': _jnp.float32, 'ffn1_w_down': _jnp.float32, 'mix_norm': _jnp.float32, 'w_in': _jnp.float32, 'b_forget': _jnp.float32, 'q_norm': _jnp.float32, 'k_norm': _jnp.float32, 'pool_w': _jnp.float32, 'pool_scale': _jnp.float32, 'w_out': _jnp.float32, 'ffn2_norm': _jnp.float32, 'ffn2_w_gate': _jnp.float32, 'ffn2_w_up': _jnp.float32, 'ffn2_w_down': _jnp.float32}
MOMENT_SCALE = {'meta_tokens': 2.601989e-03, 'ffn1_norm': 1.541540e+00, 'ffn1_w_gate': 2.996254e-02, 'ffn1_w_up': 3.117594e-02, 'ffn1_w_down': 5.141772e-02, 'mix_norm': 3.837722e+00, 'w_in': 1.306197e-01, 'b_forget': 5.010922e+01, 'q_norm': 3.292401e+00, 'k_norm': 3.302097e+00, 'pool_w': 4.380934e-01, 'pool_scale': 6.383811e+00, 'w_out': 1.672104e-01, 'ffn2_norm': 1.537686e+00, 'ffn2_w_gate': 2.060761e-02, 'ffn2_w_up': 2.320158e-02, 'ffn2_w_down': 3.705052e-02}


def _to_microbatches(a, axis):
    t = _jnp.moveaxis(a, axis, 0)
    t = t.reshape((N_MICROBATCH, t.shape[0] // N_MICROBATCH) + t.shape[1:])
    return _jnp.moveaxis(t, 1, axis + 1)


def setup_inputs(seed: int = 0) -> dict:
    inp = _fwd_setup_inputs(seed)
    key = _jax.random.fold_in(_jax.random.key(seed), 7919)
    shape, _ = _output_shape()
    out = dict(inp)
    out["loss_target"] = _jax.random.normal(_jax.random.fold_in(key, 0), shape, _jnp.float32)
    for i, name in enumerate(TWIN_WEIGHTS):
        w = inp[name].astype(_jnp.float32)
        if MOMENT_SCALE is None:
            s = _jnp.sqrt(_jnp.mean(_jnp.square(w)) + 1e-30)
        else:
            s = MOMENT_SCALE[name]
        km, kv = _jax.random.split(_jax.random.fold_in(key, i + 1))
        out[name] = w
        out["m_" + name] = s * _jax.random.normal(km, w.shape, _jnp.float32)
        out["v_" + name] = (s * s) * _jax.random.uniform(kv, w.shape, _jnp.float32, 0.5, 1.5)
    if N_MICROBATCH > 1:
        for name, axis in PER_EXAMPLE_BATCH_AXIS.items():
            out[name] = _to_microbatches(out[name], axis)
    return {'x': out['x'], 'meta_tokens': out['meta_tokens'], 'ffn1_norm': out['ffn1_norm'], 'ffn1_w_gate': out['ffn1_w_gate'], 'ffn1_w_up': out['ffn1_w_up'], 'ffn1_w_down': out['ffn1_w_down'], 'mix_norm': out['mix_norm'], 'w_in': out['w_in'], 'b_forget': out['b_forget'], 'q_norm': out['q_norm'], 'k_norm': out['k_norm'], 'pool_w': out['pool_w'], 'pool_scale': out['pool_scale'], 'w_out': out['w_out'], 'ffn2_norm': out['ffn2_norm'], 'ffn2_w_gate': out['ffn2_w_gate'], 'ffn2_w_up': out['ffn2_w_up'], 'ffn2_w_down': out['ffn2_w_down'], 'loss_target': out['loss_target'], 'm_meta_tokens': out['m_meta_tokens'], 'm_ffn1_norm': out['m_ffn1_norm'], 'm_ffn1_w_gate': out['m_ffn1_w_gate'], 'm_ffn1_w_up': out['m_ffn1_w_up'], 'm_ffn1_w_down': out['m_ffn1_w_down'], 'm_mix_norm': out['m_mix_norm'], 'm_w_in': out['m_w_in'], 'm_b_forget': out['m_b_forget'], 'm_q_norm': out['m_q_norm'], 'm_k_norm': out['m_k_norm'], 'm_pool_w': out['m_pool_w'], 'm_pool_scale': out['m_pool_scale'], 'm_w_out': out['m_w_out'], 'm_ffn2_norm': out['m_ffn2_norm'], 'm_ffn2_w_gate': out['m_ffn2_w_gate'], 'm_ffn2_w_up': out['m_ffn2_w_up'], 'm_ffn2_w_down': out['m_ffn2_w_down'], 'v_meta_tokens': out['v_meta_tokens'], 'v_ffn1_norm': out['v_ffn1_norm'], 'v_ffn1_w_gate': out['v_ffn1_w_gate'], 'v_ffn1_w_up': out['v_ffn1_w_up'], 'v_ffn1_w_down': out['v_ffn1_w_down'], 'v_mix_norm': out['v_mix_norm'], 'v_w_in': out['v_w_in'], 'v_b_forget': out['v_b_forget'], 'v_q_norm': out['v_q_norm'], 'v_k_norm': out['v_k_norm'], 'v_pool_w': out['v_pool_w'], 'v_pool_scale': out['v_pool_scale'], 'v_w_out': out['v_w_out'], 'v_ffn2_norm': out['v_ffn2_norm'], 'v_ffn2_w_gate': out['v_ffn2_w_gate'], 'v_ffn2_w_up': out['v_ffn2_w_up'], 'v_ffn2_w_down': out['v_ffn2_w_down']}


def _loss(weights, diff, rest, loss_target):
    with _jax.named_scope("forward"):
        args = {**rest, TWIN_DIFF_INPUT: diff, **{k: w.astype(_WEIGHT_DTYPES[k]) for k, w in weights.items()}}
        y = _forward(args)
    with _jax.named_scope("loss_head"):
        err = _jnp.square(y.astype(_jnp.float32) - loss_target)
        return 0.5 * _jnp.sum(_jnp.mean(err, axis=-1)) if err.ndim else 0.5 * err


def _adamw(w, g, m, v):
    m = ADAM_B1 * m + (1.0 - ADAM_B1) * g
    v = ADAM_B2 * v + (1.0 - ADAM_B2) * _jnp.square(g)
    m_hat = m / (1.0 - ADAM_B1 ** ADAM_STEP)
    v_hat = v / (1.0 - ADAM_B2 ** ADAM_STEP)
    delta = -ADAM_LR * (m_hat / (_jnp.sqrt(v_hat) + ADAM_EPS) + ADAM_WD * w)
    return delta, m, v


def reference(x, meta_tokens, ffn1_norm, ffn1_w_gate, ffn1_w_up, ffn1_w_down, mix_norm, w_in, b_forget, q_norm, k_norm, pool_w, pool_scale, w_out, ffn2_norm, ffn2_w_gate, ffn2_w_up, ffn2_w_down, loss_target, m_meta_tokens, m_ffn1_norm, m_ffn1_w_gate, m_ffn1_w_up, m_ffn1_w_down, m_mix_norm, m_w_in, m_b_forget, m_q_norm, m_k_norm, m_pool_w, m_pool_scale, m_w_out, m_ffn2_norm, m_ffn2_w_gate, m_ffn2_w_up, m_ffn2_w_down, v_meta_tokens, v_ffn1_norm, v_ffn1_w_gate, v_ffn1_w_up, v_ffn1_w_down, v_mix_norm, v_w_in, v_b_forget, v_q_norm, v_k_norm, v_pool_w, v_pool_scale, v_w_out, v_ffn2_norm, v_ffn2_w_gate, v_ffn2_w_up, v_ffn2_w_down):
    given = dict(x=x, meta_tokens=meta_tokens, ffn1_norm=ffn1_norm, ffn1_w_gate=ffn1_w_gate, ffn1_w_up=ffn1_w_up, ffn1_w_down=ffn1_w_down, mix_norm=mix_norm, w_in=w_in, b_forget=b_forget, q_norm=q_norm, k_norm=k_norm, pool_w=pool_w, pool_scale=pool_scale, w_out=w_out, ffn2_norm=ffn2_norm, ffn2_w_gate=ffn2_w_gate, ffn2_w_up=ffn2_w_up, ffn2_w_down=ffn2_w_down, loss_target=loss_target, m_meta_tokens=m_meta_tokens, m_ffn1_norm=m_ffn1_norm, m_ffn1_w_gate=m_ffn1_w_gate, m_ffn1_w_up=m_ffn1_w_up, m_ffn1_w_down=m_ffn1_w_down, m_mix_norm=m_mix_norm, m_w_in=m_w_in, m_b_forget=m_b_forget, m_q_norm=m_q_norm, m_k_norm=m_k_norm, m_pool_w=m_pool_w, m_pool_scale=m_pool_scale, m_w_out=m_w_out, m_ffn2_norm=m_ffn2_norm, m_ffn2_w_gate=m_ffn2_w_gate, m_ffn2_w_up=m_ffn2_w_up, m_ffn2_w_down=m_ffn2_w_down, v_meta_tokens=v_meta_tokens, v_ffn1_norm=v_ffn1_norm, v_ffn1_w_gate=v_ffn1_w_gate, v_ffn1_w_up=v_ffn1_w_up, v_ffn1_w_down=v_ffn1_w_down, v_mix_norm=v_mix_norm, v_w_in=v_w_in, v_b_forget=v_b_forget, v_q_norm=v_q_norm, v_k_norm=v_k_norm, v_pool_w=v_pool_w, v_pool_scale=v_pool_scale, v_w_out=v_w_out, v_ffn2_norm=v_ffn2_norm, v_ffn2_w_gate=v_ffn2_w_gate, v_ffn2_w_up=v_ffn2_w_up, v_ffn2_w_down=v_ffn2_w_down)
    weights = {n: given[n] for n in TWIN_WEIGHTS}
    shared = {n: given[n] for n in SHARED_INPUTS}
    per_example = {n: given[n] for n in ['x']}
    grad_fn = _jax.value_and_grad(_loss, argnums=(0, 1))

    def one_microbatch(ex, loss_target):
        ex = dict(ex)
        diff = ex.pop(TWIN_DIFF_INPUT)
        return grad_fn(weights, diff, {**shared, **ex}, loss_target)

    if N_MICROBATCH == 1:
        loss, (grad_w, grad_x) = one_microbatch(per_example, given["loss_target"])
    else:
        def body(carry, xs):
            loss_sum, grad_sum = carry
            l_k, (gw_k, gx_k) = one_microbatch(xs[0], xs[1])
            with _jax.named_scope("update"):
                return (loss_sum + l_k, _jax.tree.map(_jnp.add, grad_sum, gw_k)), gx_k

        init = (_jnp.zeros((), _jnp.float32), _jax.tree.map(_jnp.zeros_like, weights))
        (loss, grad_w), grad_x = _jax.lax.scan(body, init, (per_example, given["loss_target"]))
    with _jax.named_scope("update"):
        delta_w, new_m, new_v = {}, {}, {}
        for n in TWIN_WEIGHTS:
            delta_w[n], new_m[n], new_v[n] = _adamw(weights[n], grad_w[n], given["m_" + n], given["v_" + n])
    return (loss, grad_x, *[grad_w[n] for n in TWIN_WEIGHTS], *[delta_w[n] for n in TWIN_WEIGHTS],
            *[new_m[n] for n in TWIN_WEIGHTS], *[new_v[n] for n in TWIN_WEIGHTS])
```

```python
import functools
import math

import jax
import jax.numpy as jnp
from jax import lax
from jax.experimental import pallas as pl
from jax.experimental.pallas import tpu as pltpu

F32 = jnp.float32
BF16 = jnp.bfloat16

N_META = 16
EPS = 1e-6
HEAD_DIM = 128
N_POOL_GROUPS = 4
LANES = 128
SEQ_ALIGN = 128
TQ = 128
VMEM_LIMIT = 56 * 1024 * 1024
ELEMWISE_BLOCK_BYTES = 1 << 20

ADAM_LR = 0.001
ADAM_B1 = 0.9
ADAM_B2 = 0.999
ADAM_EPS = 1e-08
ADAM_WD = 0.01
ADAM_STEP = 10

NT_DIMS = (((1,), (1,)), ((), ()))
NEG = -1e30
MESH = pl.DeviceIdType.MESH


def _cparams(*sem):
    return pltpu.CompilerParams(dimension_semantics=sem, vmem_limit_bytes=VMEM_LIMIT)


def _sigmoid(a):
    return 1.0 / (1.0 + jnp.exp(-a))


def _row_tile(rows, cols, itemsize=4):
    best = None
    for t in range(16, rows + 1, 16):
        if rows % t == 0 and t * cols * itemsize <= ELEMWISE_BLOCK_BYTES:
            best = t
    return best if best is not None else rows


def _mm_nn(x, w, tn, name):
    M, K = x.shape
    N = w.shape[1]

    def body(x_ref, w_ref, o_ref):
        o_ref[...] = jnp.dot(x_ref[...], w_ref[...], preferred_element_type=F32)

    return pl.pallas_call(
        body, name=name, grid=(N // tn,),
        in_specs=[pl.BlockSpec((M, K), lambda j: (0, 0)), pl.BlockSpec((K, tn), lambda j: (0, j))],
        out_specs=pl.BlockSpec((M, tn), lambda j: (0, j)),
        out_shape=jax.ShapeDtypeStruct((M, N), F32),
        compiler_params=_cparams("parallel"),
    )(x, w)


def _ffn_up(n, wg, wu, tn, name):
    M, K = n.shape
    N = wg.shape[1]

    def body(n_ref, wg_ref, wu_ref, a_ref, b_ref, s_ref):
        nv = n_ref[...]
        a = jnp.dot(nv, wg_ref[...], preferred_element_type=F32)
        b = jnp.dot(nv, wu_ref[...], preferred_element_type=F32)
        a_ref[...] = a
        b_ref[...] = b
        s_ref[...] = (a * _sigmoid(a) * b).astype(BF16)

    wspec = pl.BlockSpec((K, tn), lambda j: (0, j))
    ospec = pl.BlockSpec((M, tn), lambda j: (0, j))
    return pl.pallas_call(
        body, name=name, grid=(N // tn,),
        in_specs=[pl.BlockSpec((M, K), lambda j: (0, 0)), wspec, wspec],
        out_specs=[ospec, ospec, ospec],
        out_shape=[jax.ShapeDtypeStruct((M, N), F32), jax.ShapeDtypeStruct((M, N), F32),
                   jax.ShapeDtypeStruct((M, N), BF16)],
        compiler_params=_cparams("parallel"),
    )(n, wg, wu)


def _mm_nn_residual(x, w, res, alpha, tn, tk, name):
    M, K = x.shape
    N = w.shape[1]
    nk = K // tk

    def body(x_ref, w_ref, r_ref, o_ref, acc):
        k = pl.program_id(1)

        @pl.when(k == 0)
        def _():
            acc[...] = jnp.zeros_like(acc)

        acc[...] += jnp.dot(x_ref[...], w_ref[...], preferred_element_type=F32)

        @pl.when(k == nk - 1)
        def _():
            o_ref[...] = r_ref[...] + alpha * acc[...]

    return pl.pallas_call(
        body, name=name, grid=(N // tn, nk),
        in_specs=[pl.BlockSpec((M, tk), lambda j, k: (0, k)), pl.BlockSpec((tk, tn), lambda j, k: (k, j)),
                  pl.BlockSpec((M, tn), lambda j, k: (0, j))],
        out_specs=pl.BlockSpec((M, tn), lambda j, k: (0, j)),
        out_shape=jax.ShapeDtypeStruct((M, N), F32),
        scratch_shapes=[pltpu.VMEM((M, tn), F32)],
        compiler_params=_cparams("parallel", "arbitrary"),
    )(x, w, res)


def _ffn_bwd_hidden(dhb, wd, a, b, tn, name):
    M, K = dhb.shape
    N = wd.shape[0]

    def body(dh_ref, w_ref, a_ref, b_ref, da_ref, db_ref):
        ds = 0.5 * lax.dot_general(dh_ref[...], w_ref[...], NT_DIMS, preferred_element_type=F32)
        av = a_ref[...]
        sig = _sigmoid(av)
        da_ref[...] = (ds * b_ref[...] * (sig * (1.0 + av * (1.0 - sig)))).astype(BF16)
        db_ref[...] = (ds * (av * sig)).astype(BF16)

    ospec = pl.BlockSpec((M, tn), lambda j: (0, j))
    return pl.pallas_call(
        body, name=name, grid=(N // tn,),
        in_specs=[pl.BlockSpec((M, K), lambda j: (0, 0)), pl.BlockSpec((tn, K), lambda j: (j, 0)), ospec, ospec],
        out_specs=[ospec, ospec],
        out_shape=[jax.ShapeDtypeStruct((M, N), BF16), jax.ShapeDtypeStruct((M, N), BF16)],
        compiler_params=_cparams("parallel"),
    )(dhb, wd, a, b)


def _mm_nt(x, w, tn, name):
    M, K = x.shape
    N = w.shape[0]

    def body(x_ref, w_ref, o_ref):
        o_ref[...] = lax.dot_general(x_ref[...], w_ref[...], NT_DIMS, preferred_element_type=F32)

    return pl.pallas_call(
        body, name=name, grid=(N // tn,),
        in_specs=[pl.BlockSpec((M, K), lambda j: (0, 0)), pl.BlockSpec((tn, K), lambda j: (j, 0))],
        out_specs=pl.BlockSpec((M, tn), lambda j: (0, j)),
        out_shape=jax.ShapeDtypeStruct((M, N), F32),
        compiler_params=_cparams("parallel"),
    )(x, w)


def _mm_nt_sum(xs, ws, tn, tk, name):
    npair = len(xs)
    M, K = xs[0].shape
    N = ws[0].shape[0]
    nk = K // tk

    def body(*refs):
        x_refs = refs[:npair]
        w_refs = refs[npair:2 * npair]
        o_ref = refs[2 * npair]
        acc = refs[2 * npair + 1]
        k = pl.program_id(1)

        @pl.when(k == 0)
        def _():
            acc[...] = jnp.zeros_like(acc)

        for x_ref, w_ref in zip(x_refs, w_refs):
            acc[...] += lax.dot_general(x_ref[...], w_ref[...], NT_DIMS, preferred_element_type=F32)

        @pl.when(k == nk - 1)
        def _():
            o_ref[...] = acc[...]

    return pl.pallas_call(
        body, name=name, grid=(N // tn, nk),
        in_specs=[pl.BlockSpec((M, tk), lambda j, k: (0, k))] * npair
        + [pl.BlockSpec((tn, tk), lambda j, k: (j, k))] * npair,
        out_specs=pl.BlockSpec((M, tn), lambda j, k: (0, j)),
        out_shape=jax.ShapeDtypeStruct((M, N), F32),
        scratch_shapes=[pltpu.VMEM((M, tn), F32)],
        compiler_params=_cparams("parallel", "arbitrary"),
    )(*xs, *ws)


def _mm_tn(x, dy, alpha, ti, tn, name, stacked):
    M, Kin = x.shape
    N = dy.shape[1]

    def body(x_ref, dy_ref, o_ref, ob_ref, xt):
        @pl.when(pl.program_id(1) == 0)
        def _():
            xt[...] = x_ref[...].astype(F32).T.astype(BF16)

        r = jnp.dot(xt[...], dy_ref[...], preferred_element_type=F32)
        if alpha != 1.0:
            r = alpha * r
        o_ref[...] = r
        ob_ref[...] = r.astype(BF16)

    if stacked:
        ospec = pl.BlockSpec((None, ti, tn), lambda i, j: (j, i, 0))
        oshape = (N // tn, Kin, tn)
    else:
        ospec = pl.BlockSpec((ti, tn), lambda i, j: (i, j))
        oshape = (Kin, N)
    return pl.pallas_call(
        body, name=name, grid=(Kin // ti, N // tn),
        in_specs=[pl.BlockSpec((M, ti), lambda i, j: (0, i)), pl.BlockSpec((M, tn), lambda i, j: (0, j))],
        out_specs=[ospec, ospec],
        out_shape=[jax.ShapeDtypeStruct(oshape, F32), jax.ShapeDtypeStruct(oshape, BF16)],
        scratch_shapes=[pltpu.VMEM((ti, M), BF16)],
        compiler_params=_cparams("parallel", "arbitrary"),
    )(x, dy)


def _rmsnorm_fwd(h, g, name):
    M, D = h.shape
    tr = _row_tile(M, D)

    def body(h_ref, g_ref, n_ref, r_ref):
        hv = h_ref[...]
        r = lax.rsqrt(jnp.mean(hv * hv, axis=-1, keepdims=True) + EPS)
        n_ref[...] = (hv * r * g_ref[...]).astype(BF16)
        r_ref[...] = r

    return pl.pallas_call(
        body, name=name, grid=(M // tr,),
        in_specs=[pl.BlockSpec((tr, D), lambda i: (i, 0)), pl.BlockSpec((1, D), lambda i: (0, 0))],
        out_specs=[pl.BlockSpec((tr, D), lambda i: (i, 0)), pl.BlockSpec((tr, 1), lambda i: (i, 0))],
        out_shape=[jax.ShapeDtypeStruct((M, D), BF16), jax.ShapeDtypeStruct((M, 1), F32)],
        compiler_params=_cparams("parallel"),
    )(h, g)


def _rmsnorm_bwd(dn, h, r, g, dh_prev, name):
    M, D = h.shape
    tr = _row_tile(M, D)

    def body(dn_ref, h_ref, r_ref, g_ref, dp_ref, dh_ref, dhb_ref, dg_ref):
        i = pl.program_id(0)
        dnv = dn_ref[...]
        hv = h_ref[...]
        rv = r_ref[...]
        w = dnv * g_ref[...]
        c = jnp.mean(w * hv, axis=-1, keepdims=True)
        dh = dp_ref[...] + rv * w - hv * (rv * rv * rv * c)
        dh_ref[...] = dh
        dhb_ref[...] = dh.astype(BF16)
        part = jnp.sum(dnv * (hv * rv), axis=0, keepdims=True)

        @pl.when(i == 0)
        def _():
            dg_ref[...] = part

        @pl.when(i > 0)
        def _():
            dg_ref[...] += part

    row = pl.BlockSpec((tr, D), lambda i: (i, 0))
    vec = pl.BlockSpec((1, D), lambda i: (0, 0))
    return pl.pallas_call(
        body, name=name, grid=(M // tr,),
        in_specs=[row, row, pl.BlockSpec((tr, 1), lambda i: (i, 0)), vec, row],
        out_specs=[row, row, vec],
        out_shape=[jax.ShapeDtypeStruct((M, D), F32), jax.ShapeDtypeStruct((M, D), BF16),
                   jax.ShapeDtypeStruct((1, D), F32)],
        compiler_params=_cparams("arbitrary"),
    )(dn, h, r, g, dh_prev)


def _loss_grad(h, tgt, seq, name):
    M, D = h.shape
    tr = _row_tile(M, D)

    def body(h_ref, t_ref, dh_ref, dhb_ref, loss_ref):
        i = pl.program_id(0)
        row = i * tr + lax.broadcasted_iota(jnp.int32, (tr, 1), 0)
        valid = (row >= N_META) & (row < N_META + seq)
        d = jnp.where(valid, h_ref[...] - t_ref[...], 0.0)
        dh = d * (1.0 / D)
        dh_ref[...] = dh
        dhb_ref[...] = dh.astype(BF16)
        part = (0.5 / D) * jnp.sum(jnp.sum(d * d, axis=1, keepdims=True), axis=0, keepdims=True)

        @pl.when(i == 0)
        def _():
            loss_ref[...] = part

        @pl.when(i > 0)
        def _():
            loss_ref[...] += part

    row = pl.BlockSpec((tr, D), lambda i: (i, 0))
    return pl.pallas_call(
        body, name=name, grid=(M // tr,),
        in_specs=[row, row],
        out_specs=[row, row, pl.BlockSpec((1, 1), lambda i: (0, 0))],
        out_shape=[jax.ShapeDtypeStruct((M, D), F32), jax.ShapeDtypeStruct((M, D), BF16),
                   jax.ShapeDtypeStruct((1, 1), F32)],
        compiler_params=_cparams("arbitrary"),
    )(h, tgt)


def _group_window(g):
    return jnp.where(g == 0, 2, jnp.where(g == 1, 4, jnp.where(g == 2, 8, 16)))


def _pool_fwd(z, pw, psc, name):
    M = z.shape[0]
    C = pw.shape[1]

    def body(p_ref, w_ref, sc_ref, pooled_ref, out_ref):
        g = pl.program_id(0)
        p = p_ref[...]
        t = lax.broadcasted_iota(jnp.int32, (M, 1), 0)
        s = p
        wsum = jnp.zeros_like(p)
        for step in range(N_POOL_GROUPS):
            sh = 1 << step
            s = s + jnp.where(t >= sh, pltpu.roll(s, sh, 0), 0.0)
            wsum = jnp.where(g == step, s, wsum)
        cnt = jnp.minimum(t + 1, _group_window(g)).astype(F32)
        pb = (wsum / cnt - p).astype(BF16)
        pooled_ref[...] = pb
        mixed = jnp.dot(pb, w_ref[...], preferred_element_type=F32)
        out_ref[...] = (mixed * sc_ref[...]).astype(BF16)

    col = pl.BlockSpec((M, C), lambda g: (0, g))
    return pl.pallas_call(
        body, name=name, grid=(N_POOL_GROUPS,),
        in_specs=[col, pl.BlockSpec((None, C, C), lambda g: (g, 0, 0)), pl.BlockSpec((1, C), lambda g: (0, g))],
        out_specs=[col, col],
        out_shape=[jax.ShapeDtypeStruct((M, N_POOL_GROUPS * C), BF16),
                   jax.ShapeDtypeStruct((M, N_POOL_GROUPS * C), BF16)],
        compiler_params=_cparams("parallel"),
    )(z, pw, psc)


def _pool_bwd(dmix, pooled, pw, psc, name):
    M = dmix.shape[0]
    C = pw.shape[1]

    def body(dm_ref, pooled_ref, w_ref, sc_ref, dp_ref, dw_ref, dwb_ref, dsc_ref):
        g = pl.program_id(0)
        dmx = dm_ref[...]
        pb = pooled_ref[...]
        wv = w_ref[...]
        mixed = jnp.dot(pb, wv, preferred_element_type=F32)
        dsc_ref[...] = jnp.sum(dmx * mixed, axis=0, keepdims=True)
        dmixed = (dmx * sc_ref[...]).astype(BF16)
        dw = jnp.dot(pb.astype(F32).T.astype(BF16), dmixed, preferred_element_type=F32)
        dw_ref[...] = dw
        dwb_ref[...] = dw.astype(BF16)
        dpooled = lax.dot_general(dmixed, wv, NT_DIMS, preferred_element_type=F32)
        t = lax.broadcasted_iota(jnp.int32, (M, 1), 0)
        cnt = jnp.minimum(t + 1, _group_window(g)).astype(F32)
        s = dpooled / cnt
        wsum = jnp.zeros_like(s)
        for step in range(N_POOL_GROUPS):
            sh = 1 << step
            s = s + jnp.where(t < M - sh, pltpu.roll(s, M - sh, 0), 0.0)
            wsum = jnp.where(g == step, s, wsum)
        dp_ref[...] = (wsum - dpooled).astype(BF16)

    col = pl.BlockSpec((M, C), lambda g: (0, g))
    wspec = pl.BlockSpec((None, C, C), lambda g: (g, 0, 0))
    vec = pl.BlockSpec((1, C), lambda g: (0, g))
    return pl.pallas_call(
        body, name=name, grid=(N_POOL_GROUPS,),
        in_specs=[col, col, wspec, vec],
        out_specs=[col, wspec, wspec, vec],
        out_shape=[jax.ShapeDtypeStruct((M, N_POOL_GROUPS * C), BF16),
                   jax.ShapeDtypeStruct((N_POOL_GROUPS, C, C), F32),
                   jax.ShapeDtypeStruct((N_POOL_GROUPS, C, C), BF16),
                   jax.ShapeDtypeStruct((1, N_POOL_GROUPS * C), F32)],
        compiler_params=_cparams("parallel"),
    )(dmix, pooled, pw, psc)


def _qkv_prep(z, gq, gk, n_heads, q_col, name):
    M = z.shape[0]
    H = n_heads
    qb = q_col // HEAD_DIM

    def body(q_ref, k_ref, v_ref, gq_ref, gk_ref, qh_ref, kh_ref, vb_ref):
        def norm(xv, g):
            r = lax.rsqrt(jnp.mean(xv * xv, axis=-1, keepdims=True) + EPS)
            return (xv * r * g).astype(BF16)

        qh_ref[...] = norm(q_ref[...], gq_ref[...])
        kh_ref[...] = norm(k_ref[...], gk_ref[...])
        vb_ref[...] = v_ref[...].astype(BF16)

    vec = pl.BlockSpec((1, HEAD_DIM), lambda h: (0, 0))
    out = pl.BlockSpec((M, HEAD_DIM), lambda h: (0, h))
    oshape = jax.ShapeDtypeStruct((M, H * HEAD_DIM), BF16)
    return pl.pallas_call(
        body, name=name, grid=(H,),
        in_specs=[pl.BlockSpec((M, HEAD_DIM), lambda h: (0, qb + h)),
                  pl.BlockSpec((M, HEAD_DIM), lambda h: (0, qb + H + h)),
                  pl.BlockSpec((M, HEAD_DIM), lambda h: (0, qb + 2 * H + h)), vec, vec],
        out_specs=[out, out, out],
        out_shape=[oshape, oshape, oshape],
        compiler_params=_cparams("parallel"),
    )(z, z, z, gq, gk)


def _forget_fwd(z, bpad, f_block, name):
    M = z.shape[0]

    def body(f_ref, b_ref, cum_ref):
        xx = f_ref[...] + b_ref[...]
        c = jnp.minimum(xx, 0.0) - jnp.log(1.0 + jnp.exp(-jnp.abs(xx)))
        t = lax.broadcasted_iota(jnp.int32, (M, 1), 0)
        sh = 1
        while sh < M:
            c = c + jnp.where(t >= sh, pltpu.roll(c, sh, 0), 0.0)
            sh *= 2
        cum_ref[...] = c.T

    return pl.pallas_call(
        body, name=name, grid=(1,),
        in_specs=[pl.BlockSpec((M, LANES), lambda i: (0, f_block)), pl.BlockSpec((1, LANES), lambda i: (0, 0))],
        out_specs=pl.BlockSpec((LANES, M), lambda i: (0, 0)),
        out_shape=jax.ShapeDtypeStruct((LANES, M), F32),
        compiler_params=_cparams("arbitrary"),
    )(z, bpad)


def _col_to_row(col):
    n = col.shape[0]
    return jnp.transpose(jnp.broadcast_to(col, (n, LANES)))[0:1, :]


def _attn_fwd(qh, kh, vb, cum_c, cum_r, name):
    M = qh.shape[0]
    H = qh.shape[1] // HEAD_DIM
    scale = 1.0 / math.sqrt(HEAD_DIM)

    def body(q_ref, k_ref, v_ref, cq_ref, ck_ref, o_ref, lc_ref, lr_ref):
        i = pl.program_id(1)
        s = lax.dot_general(q_ref[...], k_ref[...], NT_DIMS, preferred_element_type=F32) * scale
        s = s + (cq_ref[...] - ck_ref[...])
        row = i * TQ + lax.broadcasted_iota(jnp.int32, (TQ, 1), 0)
        col = lax.broadcasted_iota(jnp.int32, (1, M), 1)
        s = jnp.where(row >= col, s, NEG)
        m = jnp.max(s, axis=1, keepdims=True)
        p = jnp.exp(s - m)
        l = jnp.sum(p, axis=1, keepdims=True)
        pn = (p / l).astype(BF16)
        o_ref[...] = jnp.dot(pn, v_ref[...], preferred_element_type=F32).astype(BF16)
        lse = m + jnp.log(l)
        lc_ref[...] = lse
        lr_ref[...] = _col_to_row(lse)

    full = pl.BlockSpec((M, HEAD_DIM), lambda h, i: (0, h))
    tile = pl.BlockSpec((TQ, HEAD_DIM), lambda h, i: (i, h))
    colv = pl.BlockSpec((None, TQ, 1), lambda h, i: (h, i, 0))
    rowv_full = pl.BlockSpec((None, 1, M), lambda h, i: (h, 0, 0))
    rowv = pl.BlockSpec((None, 1, TQ), lambda h, i: (h, 0, i))
    return pl.pallas_call(
        body, name=name, grid=(H, M // TQ),
        in_specs=[tile, full, full, colv, rowv_full],
        out_specs=[tile, colv, rowv],
        out_shape=[jax.ShapeDtypeStruct((M, H * HEAD_DIM), BF16), jax.ShapeDtypeStruct((H, M, 1), F32),
                   jax.ShapeDtypeStruct((H, 1, M), F32)],
        compiler_params=_cparams("parallel", "parallel"),
    )(qh, kh, vb, cum_c, cum_r)


def _attn_bwd_q(qh, kh, vb, dob, cum_c, cum_r, lse_c, name):
    M = qh.shape[0]
    H = qh.shape[1] // HEAD_DIM
    scale = 1.0 / math.sqrt(HEAD_DIM)

    def body(q_ref, k_ref, v_ref, do_ref, cq_ref, ck_ref, l_ref, dq_ref, dr_ref, dcq_ref):
        i = pl.program_id(1)
        s = lax.dot_general(q_ref[...], k_ref[...], NT_DIMS, preferred_element_type=F32) * scale
        s = s + (cq_ref[...] - ck_ref[...])
        row = i * TQ + lax.broadcasted_iota(jnp.int32, (TQ, 1), 0)
        col = lax.broadcasted_iota(jnp.int32, (1, M), 1)
        p = jnp.exp(jnp.where(row >= col, s, NEG) - l_ref[...])
        dp = lax.dot_general(do_ref[...], v_ref[...], NT_DIMS, preferred_element_type=F32)
        delta = jnp.sum(p * dp, axis=1, keepdims=True)
        ds = p * (dp - delta)
        dq_ref[...] = jnp.dot((ds * scale).astype(BF16), k_ref[...], preferred_element_type=F32)
        dr_ref[...] = _col_to_row(delta)
        dcq_ref[...] = jnp.sum(ds, axis=1, keepdims=True)

    full = pl.BlockSpec((M, HEAD_DIM), lambda h, i: (0, h))
    tile = pl.BlockSpec((TQ, HEAD_DIM), lambda h, i: (i, h))
    colv = pl.BlockSpec((None, TQ, 1), lambda h, i: (h, i, 0))
    rowv_full = pl.BlockSpec((None, 1, M), lambda h, i: (h, 0, 0))
    rowv = pl.BlockSpec((None, 1, TQ), lambda h, i: (h, 0, i))
    return pl.pallas_call(
        body, name=name, grid=(H, M // TQ),
        in_specs=[tile, full, full, tile, colv, rowv_full, colv],
        out_specs=[tile, rowv, colv],
        out_shape=[jax.ShapeDtypeStruct((M, H * HEAD_DIM), F32), jax.ShapeDtypeStruct((H, 1, M), F32),
                   jax.ShapeDtypeStruct((H, M, 1), F32)],
        compiler_params=_cparams("parallel", "parallel"),
    )(qh, kh, vb, dob, cum_c, cum_r, lse_c)


def _attn_bwd_kv(qh, kh, vb, dob, cum_c, cum_r, lse_r, delta_r, name):
    M = qh.shape[0]
    H = qh.shape[1] // HEAD_DIM
    scale = 1.0 / math.sqrt(HEAD_DIM)

    def body(k_ref, v_ref, q_ref, do_ref, cq_ref, ck_ref, l_ref, d_ref, dk_ref, dv_ref, dck_ref):
        j = pl.program_id(1)
        st = lax.dot_general(k_ref[...], q_ref[...], NT_DIMS, preferred_element_type=F32) * scale
        st = st + (cq_ref[...] - ck_ref[...])
        krow = j * TQ + lax.broadcasted_iota(jnp.int32, (TQ, 1), 0)
        qcol = lax.broadcasted_iota(jnp.int32, (1, M), 1)
        pt = jnp.exp(jnp.where(qcol >= krow, st, NEG) - l_ref[...])
        dpt = lax.dot_general(v_ref[...], do_ref[...], NT_DIMS, preferred_element_type=F32)
        dst = pt * (dpt - d_ref[...])
        dv_ref[...] = jnp.dot(pt.astype(BF16), do_ref[...], preferred_element_type=F32).astype(BF16)
        dk_ref[...] = jnp.dot((dst * scale).astype(BF16), q_ref[...], preferred_element_type=F32)
        dck_ref[...] = -jnp.sum(dst, axis=1, keepdims=True)

    full = pl.BlockSpec((M, HEAD_DIM), lambda h, j: (0, h))
    tile = pl.BlockSpec((TQ, HEAD_DIM), lambda h, j: (j, h))
    colv = pl.BlockSpec((None, TQ, 1), lambda h, j: (h, j, 0))
    rowv_full = pl.BlockSpec((None, 1, M), lambda h, j: (h, 0, 0))
    return pl.pallas_call(
        body, name=name, grid=(H, M // TQ),
        in_specs=[tile, tile, full, full, rowv_full, colv, rowv_full, rowv_full],
        out_specs=[tile, tile, colv],
        out_shape=[jax.ShapeDtypeStruct((M, H * HEAD_DIM), F32), jax.ShapeDtypeStruct((M, H * HEAD_DIM), BF16),
                   jax.ShapeDtypeStruct((H, M, 1), F32)],
        compiler_params=_cparams("parallel", "parallel"),
    )(kh, vb, qh, dob, cum_r, cum_c, lse_r, delta_r)


def _qk_norm_bwd(dqh, dkh, z, gq, gk, n_heads, q_col, name):
    M = z.shape[0]
    H = n_heads
    qb = q_col // HEAD_DIM

    def body(dqh_ref, dkh_ref, q_ref, k_ref, gq_ref, gk_ref, dq_ref, dk_ref, dgq_ref, dgk_ref):
        h = pl.program_id(0)

        def one(dy, xv, g):
            r = lax.rsqrt(jnp.mean(xv * xv, axis=-1, keepdims=True) + EPS)
            w = dy * g
            c = jnp.mean(w * xv, axis=-1, keepdims=True)
            dx = r * w - xv * (r * r * r * c)
            return dx.astype(BF16), jnp.sum(dy * (xv * r), axis=0, keepdims=True)

        dq, dgq = one(dqh_ref[...], q_ref[...], gq_ref[...])
        dk, dgk = one(dkh_ref[...], k_ref[...], gk_ref[...])
        dq_ref[...] = dq
        dk_ref[...] = dk

        @pl.when(h == 0)
        def _():
            dgq_ref[...] = dgq
            dgk_ref[...] = dgk

        @pl.when(h > 0)
        def _():
            dgq_ref[...] += dgq
            dgk_ref[...] += dgk

    vec = pl.BlockSpec((1, HEAD_DIM), lambda h: (0, 0))
    head = pl.BlockSpec((M, HEAD_DIM), lambda h: (0, h))
    return pl.pallas_call(
        body, name=name, grid=(H,),
        in_specs=[head, head, pl.BlockSpec((M, HEAD_DIM), lambda h: (0, qb + h)),
                  pl.BlockSpec((M, HEAD_DIM), lambda h: (0, qb + H + h)), vec, vec],
        out_specs=[head, head, vec, vec],
        out_shape=[jax.ShapeDtypeStruct((M, H * HEAD_DIM), BF16), jax.ShapeDtypeStruct((M, H * HEAD_DIM), BF16),
                   jax.ShapeDtypeStruct((1, HEAD_DIM), F32), jax.ShapeDtypeStruct((1, HEAD_DIM), F32)],
        compiler_params=_cparams("arbitrary"),
    )(dqh, dkh, z, z, gq, gk)


def _forget_bwd(dcq, dck, z, bpad, f_block, name):
    H, M, _ = dcq.shape

    def body(dcq_ref, dck_ref, f_ref, b_ref, dfl_ref, db_ref):
        lane = lax.broadcasted_iota(jnp.int32, (1, LANES), 1)
        d = jnp.zeros((M, LANES), F32)
        for h in range(H):
            d = d + (dcq_ref[h] + dck_ref[h]) * (lane == h).astype(F32)
        t = lax.broadcasted_iota(jnp.int32, (M, 1), 0)
        sh = 1
        while sh < M:
            d = d + jnp.where(t < M - sh, pltpu.roll(d, M - sh, 0), 0.0)
            sh *= 2
        xx = f_ref[...] + b_ref[...]
        dfl = d * (1.0 / (1.0 + jnp.exp(xx)))
        dfl_ref[...] = dfl.astype(BF16)
        db_ref[...] = jnp.sum(dfl, axis=0, keepdims=True)

    colv = pl.BlockSpec((H, M, 1), lambda i: (0, 0, 0))
    return pl.pallas_call(
        body, name=name, grid=(1,),
        in_specs=[colv, colv, pl.BlockSpec((M, LANES), lambda i: (0, f_block)),
                  pl.BlockSpec((1, LANES), lambda i: (0, 0))],
        out_specs=[pl.BlockSpec((M, LANES), lambda i: (0, 0)), pl.BlockSpec((1, LANES), lambda i: (0, 0))],
        out_shape=[jax.ShapeDtypeStruct((M, LANES), BF16), jax.ShapeDtypeStruct((1, LANES), F32)],
        compiler_params=_cparams("arbitrary"),
    )(dcq, dck, z, bpad)


def _ffn_fwd(h, g, wg, wu, wd, tag):
    n, r = _rmsnorm_fwd(h, g, f"{tag}_norm")
    a, b, s = _ffn_up(n, wg, wu, 256, f"{tag}_up")
    h_out = _mm_nn_residual(s, wd, h, 0.5, 512, 512, f"{tag}_down")
    return h_out, (n, r, a, b, s)


def _ffn_bwd(dh, dhb, h, g, wg, wu, wd, saved, tag):
    n, r, a, b, s = saved
    n_shards = 4
    da, db = _ffn_bwd_hidden(dhb, wd, a, b, 256, f"{tag}_bwd_hidden")
    dwd = _mm_tn(s, dhb, 0.5, 512, 512, f"{tag}_dw_down", stacked=False)
    dwg = _mm_tn(n, da, 1.0, 512, wg.shape[1] // n_shards, f"{tag}_dw_gate", stacked=True)
    dwu = _mm_tn(n, db, 1.0, 512, wu.shape[1] // n_shards, f"{tag}_dw_up", stacked=True)
    dn = _mm_nt_sum([da, db], [wg, wu], 512, 512, f"{tag}_dn")
    dh_in, dhb_in, dg = _rmsnorm_bwd(dn, h, r, g, dh, f"{tag}_norm_bwd")
    return dh_in, dhb_in, dg, dwg, dwu, dwd


def _local_step(x, target, W):
    seq, D = x.shape
    L = N_META + seq
    Lp = -(-L // SEQ_ALIGN) * SEQ_ALIGN
    pad = jnp.zeros((Lp - L, D), F32)
    h0 = jnp.concatenate([W["meta"], x, pad], axis=0)
    tgt = jnp.concatenate([jnp.zeros((N_META, D), F32), target, pad], axis=0)

    d_pool = W["pool_scale"].shape[1]
    n_heads = W["b_forget"].shape[1]
    d_att = n_heads * HEAD_DIM
    f_col = d_pool + 3 * d_att
    f_block = f_col // LANES
    bpad = jnp.pad(W["b_forget"], ((0, 0), (0, LANES - n_heads)))

    h1, ffn1 = _ffn_fwd(h0, W["ffn1_norm"], W["wg1"], W["wu1"], W["wd1"], "ffn1")
    u, r_mix = _rmsnorm_fwd(h1, W["mix_norm"], "mix_norm")
    z = _mm_nn(u, W["win"], 384, "in_proj")
    pooled, pool_out = _pool_fwd(z, W["pool_w"], W["pool_scale"], "pool_fwd")
    qh, kh, vb = _qkv_prep(z, W["q_norm"], W["k_norm"], n_heads, d_pool, "qkv_prep")
    cum_t = _forget_fwd(z, bpad, f_block, "forget_fwd")[:n_heads]
    cum_c = cum_t.reshape(n_heads, Lp, 1)
    cum_r = cum_t.reshape(n_heads, 1, Lp)
    att, lse_c, lse_r = _attn_fwd(qh, kh, vb, cum_c, cum_r, "attn_fwd")
    mix = jnp.concatenate([pool_out, att], axis=1)
    h2 = _mm_nn_residual(mix, W["wout"], h1, 1.0, 512, 512, "out_proj")
    h3, ffn2 = _ffn_fwd(h2, W["ffn2_norm"], W["wg2"], W["wu2"], W["wd2"], "ffn2")

    dh3, dh3b, loss = _loss_grad(h3, tgt, seq, "loss")
    dh2, dh2b, dg_ffn2, dwg2, dwu2, dwd2 = _ffn_bwd(
        dh3, dh3b, h2, W["ffn2_norm"], W["wg2"], W["wu2"], W["wd2"], ffn2, "ffn2")

    dmix = _mm_nt(dh2b, W["wout"], 512, "out_proj_bwd")
    dwout = _mm_tn(mix, dh2b, 1.0, 512, 512, "dw_out", stacked=False)
    dp, dpw, dpwb, dpsc = _pool_bwd(dmix, pooled, W["pool_w"], W["pool_scale"], "pool_bwd")
    dob = dmix[:, d_pool:].astype(BF16)
    dqh, delta_r, dcq = _attn_bwd_q(qh, kh, vb, dob, cum_c, cum_r, lse_c, "attn_bwd_q")
    dkh, dv, dck = _attn_bwd_kv(qh, kh, vb, dob, cum_c, cum_r, lse_r, delta_r, "attn_bwd_kv")
    dq, dk, dgq, dgk = _qk_norm_bwd(dqh, dkh, z, W["q_norm"], W["k_norm"], n_heads, d_pool, "qk_norm_bwd")
    dfl, dbf = _forget_bwd(dcq, dck, z, bpad, f_block, "forget_bwd")
    dz = jnp.concatenate([dp, dq, dk, dv, dfl], axis=1)
    dwin = _mm_tn(u, dz, 1.0, 512, 384, "dw_in", stacked=False)
    du = _mm_nt_sum([dz], [W["win"]], 512, 384, "in_proj_bwd")
    dh1, dh1b, dg_mix = _rmsnorm_bwd(du, h1, r_mix, W["mix_norm"], dh2, "mix_norm_bwd")

    dh0, _, dg_ffn1, dwg1, dwu1, dwd1 = _ffn_bwd(
        dh1, dh1b, h0, W["ffn1_norm"], W["wg1"], W["wu1"], W["wd1"], ffn1, "ffn1")

    grads = dict(
        x=dh0[N_META:L], meta=dh0[:N_META],
        ffn1_norm=dg_ffn1, mix_norm=dg_mix, ffn2_norm=dg_ffn2, q_norm=dgq, k_norm=dgk,
        b_forget=dbf[:, :n_heads], pool_scale=dpsc,
        wg1=dwg1, wu1=dwu1, wd1=dwd1, win=dwin, wout=dwout, pool_w=(dpw, dpwb),
        wg2=dwg2, wu2=dwu2, wd2=dwd2,
    )
    return loss[0, 0], grads


HBM_SPEC = pl.BlockSpec(memory_space=pltpu.HBM)
N_CHIPS = 4


def _chip_peers():
    x, y, c = lax.axis_index("x"), lax.axis_index("y"), lax.axis_index("c")
    flips = [(1 - x, y), (x, 1 - y), (1 - x, 1 - y)]
    return 2 * x + y, [((px, py, c), 2 * px + py) for px, py in flips]


def _all_gather_chips(shards):
    n = len(shards)

    def body(*refs):
        in_refs = refs[:n]
        out_refs = refs[n:2 * n]
        send_sems, recv_sems, local_sems = refs[2 * n:]
        me, peers = _chip_peers()
        copies = []
        for a in range(n):
            own = pltpu.make_async_copy(in_refs[a], out_refs[a].at[me], local_sems.at[a])
            own.start()
            copies.append(own)
        sends = []
        for a in range(n):
            for k, (dev, _) in enumerate(peers):
                cp = pltpu.make_async_remote_copy(
                    src_ref=in_refs[a], dst_ref=out_refs[a].at[me], send_sem=send_sems.at[a, k],
                    recv_sem=recv_sems.at[a, k], device_id=dev, device_id_type=MESH)
                cp.start()
                sends.append(cp)
        for a in range(n):
            for k, (dev, pidx) in enumerate(peers):
                pltpu.make_async_remote_copy(
                    src_ref=in_refs[a], dst_ref=out_refs[a].at[pidx], send_sem=send_sems.at[a, k],
                    recv_sem=recv_sems.at[a, k], device_id=dev, device_id_type=MESH).wait_recv()
        for cp in sends:
            cp.wait_send()
        for cp in copies:
            cp.wait()

    return pl.pallas_call(
        body, name="gather_weights",
        in_specs=[HBM_SPEC] * n, out_specs=[HBM_SPEC] * n,
        out_shape=[jax.ShapeDtypeStruct((N_CHIPS,) + s.shape, s.dtype) for s in shards],
        scratch_shapes=[pltpu.SemaphoreType.DMA((n, 3)), pltpu.SemaphoreType.DMA((n, 3)),
                        pltpu.SemaphoreType.DMA((n,))],
    )(*shards)


def _scatter_slabs(stacked):
    n = len(stacked)

    def body(*refs):
        in_refs = refs[:n]
        out_refs = refs[n:2 * n]
        send_sems, recv_sems = refs[2 * n:]
        _, peers = _chip_peers()
        sends = []
        for a in range(n):
            for k, (dev, pidx) in enumerate(peers):
                cp = pltpu.make_async_remote_copy(
                    src_ref=in_refs[a].at[pidx], dst_ref=out_refs[a].at[k], send_sem=send_sems.at[a, k],
                    recv_sem=recv_sems.at[a, k], device_id=dev, device_id_type=MESH)
                cp.start()
                sends.append(cp)
        for cp in sends:
            cp.wait_recv()
        for cp in sends:
            cp.wait_send()

    return pl.pallas_call(
        body, name="scatter_grads",
        in_specs=[HBM_SPEC] * n, out_specs=[HBM_SPEC] * n,
        out_shape=[jax.ShapeDtypeStruct((3,) + s.shape[1:], s.dtype) for s in stacked],
        scratch_shapes=[pltpu.SemaphoreType.DMA((n, 3)), pltpu.SemaphoreType.DMA((n, 3))],
    )(*stacked)


def _swap_sibling(arrays):
    n = len(arrays)

    def body(*refs):
        in_refs = refs[:n]
        out_refs = refs[n:2 * n]
        send_sems, recv_sems = refs[2 * n:]
        sib = (lax.axis_index("x"), lax.axis_index("y"), 1 - lax.axis_index("c"))
        sends = []
        for a in range(n):
            cp = pltpu.make_async_remote_copy(
                src_ref=in_refs[a], dst_ref=out_refs[a], send_sem=send_sems.at[a], recv_sem=recv_sems.at[a],
                device_id=sib, device_id_type=MESH)
            cp.start()
            sends.append(cp)
        for cp in sends:
            cp.wait_recv()
        for cp in sends:
            cp.wait_send()

    return pl.pallas_call(
        body, name="swap_sibling",
        in_specs=[HBM_SPEC] * n, out_specs=[HBM_SPEC] * n,
        out_shape=[jax.ShapeDtypeStruct(s.shape, s.dtype) for s in arrays],
        scratch_shapes=[pltpu.SemaphoreType.DMA((n,)), pltpu.SemaphoreType.DMA((n,))],
    )(*arrays)


def _all_reduce_small(v):
    R, C = v.shape
    n_dev = 8

    def body(v_ref, o_ref, buf, send_sems, recv_sems):
        x, y, c = lax.axis_index("x"), lax.axis_index("y"), lax.axis_index("c")
        me = 4 * x + 2 * y + c
        buf[me] = v_ref[...]
        sends = []
        for k in range(1, n_dev):
            px, py, pc = x ^ ((k >> 2) & 1), y ^ ((k >> 1) & 1), c ^ (k & 1)
            cp = pltpu.make_async_remote_copy(
                src_ref=v_ref, dst_ref=buf.at[me], send_sem=send_sems.at[k - 1], recv_sem=recv_sems.at[k - 1],
                device_id=(px, py, pc), device_id_type=MESH)
            cp.start()
            sends.append((cp, 4 * px + 2 * py + pc))
        for k in range(1, n_dev):
            cp, pidx = sends[k - 1]
            pltpu.make_async_remote_copy(
                src_ref=v_ref, dst_ref=buf.at[pidx], send_sem=send_sems.at[k - 1], recv_sem=recv_sems.at[k - 1],
                device_id=(x, y, c), device_id_type=MESH).wait_recv()
        for cp, _ in sends:
            cp.wait_send()
        acc = buf[0]
        for d in range(1, n_dev):
            acc = acc + buf[d]
        o_ref[...] = acc

    vm = pl.BlockSpec(memory_space=pltpu.VMEM)
    return pl.pallas_call(
        body, name="all_reduce_small",
        in_specs=[vm], out_specs=vm,
        out_shape=jax.ShapeDtypeStruct((R, C), F32),
        scratch_shapes=[pltpu.VMEM((n_dev, R, C), F32), pltpu.SemaphoreType.DMA((n_dev - 1,)),
                        pltpu.SemaphoreType.DMA((n_dev - 1,))],
    )(v)


def _sum_parts(own, recv, name):
    R, C = own.shape
    tr = _row_tile(R, C)

    def body(o_ref, r_ref, out_ref):
        acc = o_ref[...]
        for k in range(3):
            acc = acc + r_ref[k].astype(F32)
        out_ref[...] = acc

    return pl.pallas_call(
        body, name=name, grid=(R // tr,),
        in_specs=[pl.BlockSpec((tr, C), lambda i: (i, 0)), pl.BlockSpec((3, tr, C), lambda i: (0, i, 0))],
        out_specs=pl.BlockSpec((tr, C), lambda i: (i, 0)),
        out_shape=jax.ShapeDtypeStruct((R, C), F32),
        compiler_params=_cparams("parallel"),
    )(own, recv)


def _adamw(parts, w, m, v, name):
    R, C = w.shape
    tr = _row_tile(R, C)
    npart = len(parts)
    c1 = 1.0 - ADAM_B1 ** ADAM_STEP
    c2 = 1.0 - ADAM_B2 ** ADAM_STEP

    def body(*refs):
        p_refs = refs[:npart]
        w_ref, m_ref, v_ref, g_ref, d_ref, nm_ref, nv_ref = refs[npart:]
        g = p_refs[0][...]
        for p_ref in p_refs[1:]:
            g = g + p_ref[...]
        nm = ADAM_B1 * m_ref[...] + (1.0 - ADAM_B1) * g
        nv = ADAM_B2 * v_ref[...] + (1.0 - ADAM_B2) * (g * g)
        m_hat = nm / c1
        v_hat = nv / c2
        g_ref[...] = g
        d_ref[...] = -ADAM_LR * (m_hat / (jnp.sqrt(v_hat) + ADAM_EPS) + ADAM_WD * w_ref[...])
        nm_ref[...] = nm
        nv_ref[...] = nv

    blk = pl.BlockSpec((tr, C), lambda i: (i, 0))
    shape = jax.ShapeDtypeStruct((R, C), F32)
    return pl.pallas_call(
        body, name=name, grid=(R // tr,),
        in_specs=[blk] * (npart + 3), out_specs=[blk] * 4, out_shape=[shape] * 4,
        compiler_params=_cparams("parallel"),
    )(*parts, w, m, v)


SMALL_NAMES = ("ffn1_norm", "mix_norm", "ffn2_norm", "pool_scale", "q_norm", "k_norm", "b_forget")
SMALL_COLS = 1024


def _pack_small(vals):
    rows = [vals[n].reshape(-1, SMALL_COLS) for n in ("ffn1_norm", "mix_norm", "ffn2_norm", "pool_scale")]
    tail = jnp.concatenate([vals["q_norm"].reshape(-1), vals["k_norm"].reshape(-1), vals["b_forget"].reshape(-1)])
    rows.append(jnp.pad(tail, (0, SMALL_COLS - tail.shape[0])).reshape(1, SMALL_COLS))
    return jnp.concatenate(rows, axis=0)


def _unpack_small(packed, like):
    out = {}
    r = 0
    for n in ("ffn1_norm", "mix_norm", "ffn2_norm", "pool_scale"):
        k = like[n].size // SMALL_COLS
        out[n] = packed[r:r + k].reshape(like[n].shape)
        r += k
    o = 0
    for n in ("q_norm", "k_norm", "b_forget"):
        k = like[n].size
        out[n] = packed[r, o:o + k].reshape(like[n].shape)
        o += k
    return out


def kernel(x, meta_tokens, ffn1_norm, ffn1_w_gate, ffn1_w_up, ffn1_w_down, mix_norm, w_in, b_forget, q_norm, k_norm, pool_w, pool_scale, w_out, ffn2_norm, ffn2_w_gate, ffn2_w_up, ffn2_w_down, loss_target, m_meta_tokens, m_ffn1_norm, m_ffn1_w_gate, m_ffn1_w_up, m_ffn1_w_down, m_mix_norm, m_w_in, m_b_forget, m_q_norm, m_k_norm, m_pool_w, m_pool_scale, m_w_out, m_ffn2_norm, m_ffn2_w_gate, m_ffn2_w_up, m_ffn2_w_down, v_meta_tokens, v_ffn1_norm, v_ffn1_w_gate, v_ffn1_w_up, v_ffn1_w_down, v_mix_norm, v_w_in, v_b_forget, v_q_norm, v_k_norm, v_pool_w, v_pool_scale, v_w_out, v_ffn2_norm, v_ffn2_w_gate, v_ffn2_w_up, v_ffn2_w_down):
    wts = dict(meta_tokens=meta_tokens, ffn1_norm=ffn1_norm, ffn1_w_gate=ffn1_w_gate, ffn1_w_up=ffn1_w_up,
               ffn1_w_down=ffn1_w_down, mix_norm=mix_norm, w_in=w_in, b_forget=b_forget, q_norm=q_norm,
               k_norm=k_norm, pool_w=pool_w, pool_scale=pool_scale, w_out=w_out, ffn2_norm=ffn2_norm,
               ffn2_w_gate=ffn2_w_gate, ffn2_w_up=ffn2_w_up, ffn2_w_down=ffn2_w_down)
    mom = dict(meta_tokens=m_meta_tokens, ffn1_norm=m_ffn1_norm, ffn1_w_gate=m_ffn1_w_gate, ffn1_w_up=m_ffn1_w_up,
               ffn1_w_down=m_ffn1_w_down, mix_norm=m_mix_norm, w_in=m_w_in, b_forget=m_b_forget, q_norm=m_q_norm,
               k_norm=m_k_norm, pool_w=m_pool_w, pool_scale=m_pool_scale, w_out=m_w_out, ffn2_norm=m_ffn2_norm,
               ffn2_w_gate=m_ffn2_w_gate, ffn2_w_up=m_ffn2_w_up, ffn2_w_down=m_ffn2_w_down)
    var = dict(meta_tokens=v_meta_tokens, ffn1_norm=v_ffn1_norm, ffn1_w_gate=v_ffn1_w_gate, ffn1_w_up=v_ffn1_w_up,
               ffn1_w_down=v_ffn1_w_down, mix_norm=v_mix_norm, w_in=v_w_in, b_forget=v_b_forget, q_norm=v_q_norm,
               k_norm=v_k_norm, pool_w=v_pool_w, pool_scale=v_pool_scale, w_out=v_w_out, ffn2_norm=v_ffn2_norm,
               ffn2_w_gate=v_ffn2_w_gate, ffn2_w_up=v_ffn2_w_up, ffn2_w_down=v_ffn2_w_down)
    order = list(wts)
    me = 2 * lax.axis_index("x") + lax.axis_index("y")

    D = x.shape[2]
    d_in_shard = w_in.shape[2]
    d_in = N_CHIPS * d_in_shard
    d_in_shard_pad = -(-d_in_shard // LANES) * LANES
    n_heads = b_forget.shape[1]
    d_in_pad = (d_in - n_heads) + LANES

    big = ("ffn1_w_gate", "ffn1_w_up", "ffn1_w_down", "w_in", "w_out", "pool_w", "ffn2_w_gate", "ffn2_w_up",
           "ffn2_w_down")
    shards = {n: wts[n][0].astype(BF16) for n in big}
    shards["w_in"] = jnp.pad(shards["w_in"], ((0, 0), (0, d_in_shard_pad - d_in_shard)))
    gathered = _all_gather_chips([shards[n] for n in big] + [meta_tokens])
    G = dict(zip(big + ("meta_tokens",), gathered))

    def cols(st):
        return jnp.transpose(st, (1, 0, 2)).reshape(st.shape[1], -1)

    def rows(st):
        return st.reshape(-1, st.shape[2])

    win = cols(G["w_in"][:, :, :d_in_shard])
    W = dict(
        meta=cols(G["meta_tokens"]),
        wg1=cols(G["ffn1_w_gate"]), wu1=cols(G["ffn1_w_up"]), wd1=rows(G["ffn1_w_down"]),
        win=jnp.pad(win, ((0, 0), (0, d_in_pad - d_in))), wout=rows(G["w_out"]),
        pool_w=jnp.transpose(G["pool_w"], (1, 0, 2, 3)).reshape(N_POOL_GROUPS, pool_w.shape[3], pool_w.shape[3]),
        wg2=cols(G["ffn2_w_gate"]), wu2=cols(G["ffn2_w_up"]), wd2=rows(G["ffn2_w_down"]),
        ffn1_norm=ffn1_norm, mix_norm=mix_norm, ffn2_norm=ffn2_norm, q_norm=q_norm, k_norm=k_norm,
        b_forget=b_forget, pool_scale=pool_scale,
    )

    loss_part, gr = _local_step(x[0], loss_target[0], W)
    loss = lax.psum(loss_part, ("x", "y", "c"))

    def split_rows(pair):
        return tuple(a.reshape(N_CHIPS, -1, a.shape[1]) for a in pair)

    def split_win(pair):
        out = []
        for a in pair:
            a = jnp.transpose(a[:, :d_in].reshape(D, N_CHIPS, d_in_shard), (1, 0, 2))
            out.append(jnp.pad(a, ((0, 0), (0, 0), (0, d_in_shard_pad - d_in_shard))))
        return tuple(out)

    def split_pool(pair):
        c = pool_w.shape[3]
        r = pool_w.shape[2]
        return tuple(jnp.transpose(a.reshape(N_POOL_GROUPS, N_CHIPS, r, c), (1, 0, 2, 3)).reshape(N_CHIPS, -1, c)
                     for a in pair)

    slabs = dict(
        ffn1_w_gate=gr["wg1"], ffn1_w_up=gr["wu1"], ffn1_w_down=split_rows(gr["wd1"]),
        w_in=split_win(gr["win"]), w_out=split_rows(gr["wout"]), pool_w=split_pool(gr["pool_w"]),
        ffn2_w_gate=gr["wg2"], ffn2_w_up=gr["wu2"], ffn2_w_down=split_rows(gr["wd2"]),
    )
    received = _scatter_slabs([slabs[n][1] for n in big])
    partial = [
        _sum_parts(lax.dynamic_index_in_dim(slabs[n][0], me, 0, keepdims=False), rcv, f"sum_{n}")
        for n, rcv in zip(big, received)
    ]
    sibling = _swap_sibling(partial)

    out_g, out_d, out_m, out_v = {}, {}, {}, {}
    for n, own, sib in zip(big, partial, sibling):
        shape = wts[n].shape
        r, c = own.shape
        if n == "w_in":
            own, sib = own[:, :d_in_shard], sib[:, :d_in_shard]
            c = d_in_shard
        res = _adamw([own, sib], wts[n].reshape(r, c), mom[n].reshape(r, c), var[n].reshape(r, c), f"adamw_{n}")
        out_g[n], out_d[n], out_m[n], out_v[n] = (a.reshape(shape) for a in res)

    small_g = _pack_small({n: gr[n] for n in SMALL_NAMES})
    meta_rows = gr["meta"].reshape(-1, SMALL_COLS)
    total = _all_reduce_small(jnp.concatenate([small_g, meta_rows], axis=0))
    n_small = small_g.shape[0]
    res = _adamw([total[:n_small]], _pack_small({n: wts[n] for n in SMALL_NAMES}),
                 _pack_small({n: mom[n] for n in SMALL_NAMES}), _pack_small({n: var[n] for n in SMALL_NAMES}),
                 "adamw_small")
    for dst, packed in zip((out_g, out_d, out_m, out_v), res):
        dst.update(_unpack_small(packed, wts))
    meta_cols = meta_tokens.shape[1]
    meta_g = lax.dynamic_slice_in_dim(total[n_small:].reshape(N_META, D), me * meta_cols, meta_cols, axis=1)
    res = _adamw([meta_g], meta_tokens, m_meta_tokens, v_meta_tokens, "adamw_meta")
    out_g["meta_tokens"], out_d["meta_tokens"], out_m["meta_tokens"], out_v["meta_tokens"] = res

    grad_x = gr["x"].reshape(x.shape)
    return (loss, grad_x, *[out_g[n] for n in order], *[out_d[n] for n in order], *[out_m[n] for n in order],
            *[out_v[n] for n in order])
```

```python
import functools
import math

import jax
import jax.numpy as jnp
from jax import lax
from jax.experimental import pallas as pl
from jax.experimental.pallas import tpu as pltpu

F32 = jnp.float32
BF16 = jnp.bfloat16

N_META = 16
EPS = 1e-6
HEAD_DIM = 128
N_POOL_GROUPS = 4
LANES = 128
SEQ_ALIGN = 128
TQ = 128
VMEM_LIMIT = 56 * 1024 * 1024
ELEMWISE_BLOCK_BYTES = 1 << 20

ADAM_LR = 0.001
ADAM_B1 = 0.9
ADAM_B2 = 0.999
ADAM_EPS = 1e-08
ADAM_WD = 0.01
ADAM_STEP = 10

NT_DIMS = (((1,), (1,)), ((), ()))
NEG = -1e30
MESH = pl.DeviceIdType.MESH


def _cparams(*sem):
    return pltpu.CompilerParams(dimension_semantics=sem, vmem_limit_bytes=VMEM_LIMIT)


def _sigmoid(a):
    return 1.0 / (1.0 + jnp.exp(-a))


def _row_tile(rows, cols, itemsize=4):
    best = None
    for t in range(16, rows + 1, 16):
        if rows % t == 0 and t * cols * itemsize <= ELEMWISE_BLOCK_BYTES:
            best = t
    return best if best is not None else rows


def _mm_nn(x, w, tn, name):
    M, K = x.shape
    N = w.shape[1]

    def body(x_ref, w_ref, o_ref):
        o_ref[...] = jnp.dot(x_ref[...], w_ref[...], preferred_element_type=F32)

    return pl.pallas_call(
        body, name=name, grid=(N // tn,),
        in_specs=[pl.BlockSpec((M, K), lambda j: (0, 0)), pl.BlockSpec((K, tn), lambda j: (0, j))],
        out_specs=pl.BlockSpec((M, tn), lambda j: (0, j)),
        out_shape=jax.ShapeDtypeStruct((M, N), F32),
        compiler_params=_cparams("parallel"),
    )(x, w)


def _ffn_up(n, wg, wu, tn, name):
    M, K = n.shape
    N = wg.shape[1]

    def body(n_ref, wg_ref, wu_ref, a_ref, b_ref, s_ref):
        nv = n_ref[...]
        a = jnp.dot(nv, wg_ref[...], preferred_element_type=F32)
        b = jnp.dot(nv, wu_ref[...], preferred_element_type=F32)
        a_ref[...] = a
        b_ref[...] = b
        s_ref[...] = (a * _sigmoid(a) * b).astype(BF16)

    wspec = pl.BlockSpec((K, tn), lambda j: (0, j))
    ospec = pl.BlockSpec((M, tn), lambda j: (0, j))
    return pl.pallas_call(
        body, name=name, grid=(N // tn,),
        in_specs=[pl.BlockSpec((M, K), lambda j: (0, 0)), wspec, wspec],
        out_specs=[ospec, ospec, ospec],
        out_shape=[jax.ShapeDtypeStruct((M, N), F32), jax.ShapeDtypeStruct((M, N), F32),
                   jax.ShapeDtypeStruct((M, N), BF16)],
        compiler_params=_cparams("parallel"),
    )(n, wg, wu)


def _mm_nn_residual(x, w, res, alpha, tn, tk, name):
    M, K = x.shape
    N = w.shape[1]
    nk = K // tk

    def body(x_ref, w_ref, r_ref, o_ref, acc):
        k = pl.program_id(1)

        @pl.when(k == 0)
        def _():
            acc[...] = jnp.zeros_like(acc)

        acc[...] += jnp.dot(x_ref[...], w_ref[...], preferred_element_type=F32)

        @pl.when(k == nk - 1)
        def _():
            o_ref[...] = r_ref[...] + alpha * acc[...]

    return pl.pallas_call(
        body, name=name, grid=(N // tn, nk),
        in_specs=[pl.BlockSpec((M, tk), lambda j, k: (0, k)), pl.BlockSpec((tk, tn), lambda j, k: (k, j)),
                  pl.BlockSpec((M, tn), lambda j, k: (0, j))],
        out_specs=pl.BlockSpec((M, tn), lambda j, k: (0, j)),
        out_shape=jax.ShapeDtypeStruct((M, N), F32),
        scratch_shapes=[pltpu.VMEM((M, tn), F32)],
        compiler_params=_cparams("parallel", "arbitrary"),
    )(x, w, res)


def _ffn_bwd_hidden(dhb, wd, a, b, tn, name):
    M, K = dhb.shape
    N = wd.shape[0]

    def body(dh_ref, w_ref, a_ref, b_ref, da_ref, db_ref):
        ds = 0.5 * lax.dot_general(dh_ref[...], w_ref[...], NT_DIMS, preferred_element_type=F32)
        av = a_ref[...]
        sig = _sigmoid(av)
        da_ref[...] = (ds * b_ref[...] * (sig * (1.0 + av * (1.0 - sig)))).astype(BF16)
        db_ref[...] = (ds * (av * sig)).astype(BF16)

    ospec = pl.BlockSpec((M, tn), lambda j: (0, j))
    return pl.pallas_call(
        body, name=name, grid=(N // tn,),
        in_specs=[pl.BlockSpec((M, K), lambda j: (0, 0)), pl.BlockSpec((tn, K), lambda j: (j, 0)), ospec, ospec],
        out_specs=[ospec, ospec],
        out_shape=[jax.ShapeDtypeStruct((M, N), BF16), jax.ShapeDtypeStruct((M, N), BF16)],
        compiler_params=_cparams("parallel"),
    )(dhb, wd, a, b)


def _mm_nt(x, w, tn, name):
    M, K = x.shape
    N = w.shape[0]

    def body(x_ref, w_ref, o_ref):
        o_ref[...] = lax.dot_general(x_ref[...], w_ref[...], NT_DIMS, preferred_element_type=F32)

    return pl.pallas_call(
        body, name=name, grid=(N // tn,),
        in_specs=[pl.BlockSpec((M, K), lambda j: (0, 0)), pl.BlockSpec((tn, K), lambda j: (j, 0))],
        out_specs=pl.BlockSpec((M, tn), lambda j: (0, j)),
        out_shape=jax.ShapeDtypeStruct((M, N), F32),
        compiler_params=_cparams("parallel"),
    )(x, w)


def _mm_nt_sum(xs, ws, tn, tk, name):
    npair = len(xs)
    M, K = xs[0].shape
    N = ws[0].shape[0]
    nk = K // tk

    def body(*refs):
        x_refs = refs[:npair]
        w_refs = refs[npair:2 * npair]
        o_ref = refs[2 * npair]
        acc = refs[2 * npair + 1]
        k = pl.program_id(1)

        @pl.when(k == 0)
        def _():
            acc[...] = jnp.zeros_like(acc)

        for x_ref, w_ref in zip(x_refs, w_refs):
            acc[...] += lax.dot_general(x_ref[...], w_ref[...], NT_DIMS, preferred_element_type=F32)

        @pl.when(k == nk - 1)
        def _():
            o_ref[...] = acc[...]

    return pl.pallas_call(
        body, name=name, grid=(N // tn, nk),
        in_specs=[pl.BlockSpec((M, tk), lambda j, k: (0, k))] * npair
        + [pl.BlockSpec((tn, tk), lambda j, k: (j, k))] * npair,
        out_specs=pl.BlockSpec((M, tn), lambda j, k: (0, j)),
        out_shape=jax.ShapeDtypeStruct((M, N), F32),
        scratch_shapes=[pltpu.VMEM((M, tn), F32)],
        compiler_params=_cparams("parallel", "arbitrary"),
    )(*xs, *ws)


def _mm_tn(x, dy, alpha, ti, tn, name, stacked):
    M, Kin = x.shape
    N = dy.shape[1]

    def body(x_ref, dy_ref, ob_ref, xt):
        @pl.when(pl.program_id(1) == 0)
        def _():
            xt[...] = x_ref[...].astype(F32).T.astype(BF16)

        r = jnp.dot(xt[...], dy_ref[...], preferred_element_type=F32)
        if alpha != 1.0:
            r = alpha * r
        ob_ref[...] = r.astype(BF16)

    if stacked:
        ospec = pl.BlockSpec((None, ti, tn), lambda i, j: (j, i, 0))
        oshape = (N // tn, Kin, tn)
    else:
        ospec = pl.BlockSpec((ti, tn), lambda i, j: (i, j))
        oshape = (Kin, N)
    return pl.pallas_call(
        body, name=name, grid=(Kin // ti, N // tn),
        in_specs=[pl.BlockSpec((M, ti), lambda i, j: (0, i)), pl.BlockSpec((M, tn), lambda i, j: (0, j))],
        out_specs=ospec,
        out_shape=jax.ShapeDtypeStruct(oshape, BF16),
        scratch_shapes=[pltpu.VMEM((ti, M), BF16)],
        compiler_params=_cparams("parallel", "arbitrary"),
    )(x, dy)


def _rmsnorm_fwd(h, g, name):
    M, D = h.shape
    tr = _row_tile(M, D)

    def body(h_ref, g_ref, n_ref, r_ref):
        hv = h_ref[...]
        r = lax.rsqrt(jnp.mean(hv * hv, axis=-1, keepdims=True) + EPS)
        n_ref[...] = (hv * r * g_ref[...]).astype(BF16)
        r_ref[...] = r

    return pl.pallas_call(
        body, name=name, grid=(M // tr,),
        in_specs=[pl.BlockSpec((tr, D), lambda i: (i, 0)), pl.BlockSpec((1, D), lambda i: (0, 0))],
        out_specs=[pl.BlockSpec((tr, D), lambda i: (i, 0)), pl.BlockSpec((tr, 1), lambda i: (i, 0))],
        out_shape=[jax.ShapeDtypeStruct((M, D), BF16), jax.ShapeDtypeStruct((M, 1), F32)],
        compiler_params=_cparams("parallel"),
    )(h, g)


def _rmsnorm_bwd(dn, h, r, g, dh_prev, name):
    M, D = h.shape
    tr = _row_tile(M, D)

    def body(dn_ref, h_ref, r_ref, g_ref, dp_ref, dh_ref, dhb_ref, dg_ref):
        i = pl.program_id(0)
        dnv = dn_ref[...]
        hv = h_ref[...]
        rv = r_ref[...]
        w = dnv * g_ref[...]
        c = jnp.mean(w * hv, axis=-1, keepdims=True)
        dh = dp_ref[...] + rv * w - hv * (rv * rv * rv * c)
        dh_ref[...] = dh
        dhb_ref[...] = dh.astype(BF16)
        part = jnp.sum(dnv * (hv * rv), axis=0, keepdims=True)

        @pl.when(i == 0)
        def _():
            dg_ref[...] = part

        @pl.when(i > 0)
        def _():
            dg_ref[...] += part

    row = pl.BlockSpec((tr, D), lambda i: (i, 0))
    vec = pl.BlockSpec((1, D), lambda i: (0, 0))
    return pl.pallas_call(
        body, name=name, grid=(M // tr,),
        in_specs=[row, row, pl.BlockSpec((tr, 1), lambda i: (i, 0)), vec, row],
        out_specs=[row, row, vec],
        out_shape=[jax.ShapeDtypeStruct((M, D), F32), jax.ShapeDtypeStruct((M, D), BF16),
                   jax.ShapeDtypeStruct((1, D), F32)],
        compiler_params=_cparams("arbitrary"),
    )(dn, h, r, g, dh_prev)


def _loss_grad(h, tgt, seq, name):
    M, D = h.shape
    tr = _row_tile(M, D)

    def body(h_ref, t_ref, dh_ref, dhb_ref, loss_ref):
        i = pl.program_id(0)
        row = i * tr + lax.broadcasted_iota(jnp.int32, (tr, 1), 0)
        valid = (row >= N_META) & (row < N_META + seq)
        d = jnp.where(valid, h_ref[...] - t_ref[...], 0.0)
        dh = d * (1.0 / D)
        dh_ref[...] = dh
        dhb_ref[...] = dh.astype(BF16)
        part = (0.5 / D) * jnp.sum(jnp.sum(d * d, axis=1, keepdims=True), axis=0, keepdims=True)

        @pl.when(i == 0)
        def _():
            loss_ref[...] = part

        @pl.when(i > 0)
        def _():
            loss_ref[...] += part

    row = pl.BlockSpec((tr, D), lambda i: (i, 0))
    return pl.pallas_call(
        body, name=name, grid=(M // tr,),
        in_specs=[row, row],
        out_specs=[row, row, pl.BlockSpec((1, 1), lambda i: (0, 0))],
        out_shape=[jax.ShapeDtypeStruct((M, D), F32), jax.ShapeDtypeStruct((M, D), BF16),
                   jax.ShapeDtypeStruct((1, 1), F32)],
        compiler_params=_cparams("arbitrary"),
    )(h, tgt)


def _group_window(g):
    return jnp.where(g == 0, 2, jnp.where(g == 1, 4, jnp.where(g == 2, 8, 16)))


def _pool_fwd(z, pw, psc, name):
    M = z.shape[0]
    C = pw.shape[1]

    def body(p_ref, w_ref, sc_ref, pooled_ref, out_ref):
        g = pl.program_id(0)
        p = p_ref[...]
        t = lax.broadcasted_iota(jnp.int32, (M, 1), 0)
        s = p
        wsum = jnp.zeros_like(p)
        for step in range(N_POOL_GROUPS):
            sh = 1 << step
            s = s + jnp.where(t >= sh, pltpu.roll(s, sh, 0), 0.0)
            wsum = jnp.where(g == step, s, wsum)
        cnt = jnp.minimum(t + 1, _group_window(g)).astype(F32)
        pb = (wsum / cnt - p).astype(BF16)
        pooled_ref[...] = pb
        mixed = jnp.dot(pb, w_ref[...], preferred_element_type=F32)
        out_ref[...] = (mixed * sc_ref[...]).astype(BF16)

    col = pl.BlockSpec((M, C), lambda g: (0, g))
    return pl.pallas_call(
        body, name=name, grid=(N_POOL_GROUPS,),
        in_specs=[col, pl.BlockSpec((None, C, C), lambda g: (g, 0, 0)), pl.BlockSpec((1, C), lambda g: (0, g))],
        out_specs=[col, col],
        out_shape=[jax.ShapeDtypeStruct((M, N_POOL_GROUPS * C), BF16),
                   jax.ShapeDtypeStruct((M, N_POOL_GROUPS * C), BF16)],
        compiler_params=_cparams("parallel"),
    )(z, pw, psc)


def _pool_bwd(dmix, pooled, pw, psc, name):
    M = dmix.shape[0]
    C = pw.shape[1]

    def body(dm_ref, pooled_ref, w_ref, sc_ref, dp_ref, dwb_ref, dsc_ref):
        g = pl.program_id(0)
        dmx = dm_ref[...]
        pb = pooled_ref[...]
        wv = w_ref[...]
        mixed = jnp.dot(pb, wv, preferred_element_type=F32)
        dsc_ref[...] = jnp.sum(dmx * mixed, axis=0, keepdims=True)
        dmixed = (dmx * sc_ref[...]).astype(BF16)
        dw = jnp.dot(pb.astype(F32).T.astype(BF16), dmixed, preferred_element_type=F32)
        dwb_ref[...] = dw.astype(BF16)
        dpooled = lax.dot_general(dmixed, wv, NT_DIMS, preferred_element_type=F32)
        t = lax.broadcasted_iota(jnp.int32, (M, 1), 0)
        cnt = jnp.minimum(t + 1, _group_window(g)).astype(F32)
        s = dpooled / cnt
        wsum = jnp.zeros_like(s)
        for step in range(N_POOL_GROUPS):
            sh = 1 << step
            s = s + jnp.where(t < M - sh, pltpu.roll(s, M - sh, 0), 0.0)
            wsum = jnp.where(g == step, s, wsum)
        dp_ref[...] = (wsum - dpooled).astype(BF16)

    col = pl.BlockSpec((M, C), lambda g: (0, g))
    wspec = pl.BlockSpec((None, C, C), lambda g: (g, 0, 0))
    vec = pl.BlockSpec((1, C), lambda g: (0, g))
    return pl.pallas_call(
        body, name=name, grid=(N_POOL_GROUPS,),
        in_specs=[col, col, wspec, vec],
        out_specs=[col, wspec, vec],
        out_shape=[jax.ShapeDtypeStruct((M, N_POOL_GROUPS * C), BF16),
                   jax.ShapeDtypeStruct((N_POOL_GROUPS, C, C), BF16),
                   jax.ShapeDtypeStruct((1, N_POOL_GROUPS * C), F32)],
        compiler_params=_cparams("parallel"),
    )(dmix, pooled, pw, psc)


def _qkv_prep(z, gq, gk, n_heads, q_col, name):
    M = z.shape[0]
    H = n_heads
    qb = q_col // HEAD_DIM

    def body(q_ref, k_ref, v_ref, gq_ref, gk_ref, qh_ref, kh_ref, vb_ref):
        def norm(xv, g):
            r = lax.rsqrt(jnp.mean(xv * xv, axis=-1, keepdims=True) + EPS)
            return (xv * r * g).astype(BF16)

        qh_ref[...] = norm(q_ref[...], gq_ref[...])
        kh_ref[...] = norm(k_ref[...], gk_ref[...])
        vb_ref[...] = v_ref[...].astype(BF16)

    vec = pl.BlockSpec((1, HEAD_DIM), lambda h: (0, 0))
    out = pl.BlockSpec((M, HEAD_DIM), lambda h: (0, h))
    oshape = jax.ShapeDtypeStruct((M, H * HEAD_DIM), BF16)
    return pl.pallas_call(
        body, name=name, grid=(H,),
        in_specs=[pl.BlockSpec((M, HEAD_DIM), lambda h: (0, qb + h)),
                  pl.BlockSpec((M, HEAD_DIM), lambda h: (0, qb + H + h)),
                  pl.BlockSpec((M, HEAD_DIM), lambda h: (0, qb + 2 * H + h)), vec, vec],
        out_specs=[out, out, out],
        out_shape=[oshape, oshape, oshape],
        compiler_params=_cparams("parallel"),
    )(z, z, z, gq, gk)


def _forget_fwd(z, bpad, f_block, name):
    M = z.shape[0]

    def body(f_ref, b_ref, cum_ref):
        xx = f_ref[...] + b_ref[...]
        c = jnp.minimum(xx, 0.0) - jnp.log(1.0 + jnp.exp(-jnp.abs(xx)))
        t = lax.broadcasted_iota(jnp.int32, (M, 1), 0)
        sh = 1
        while sh < M:
            c = c + jnp.where(t >= sh, pltpu.roll(c, sh, 0), 0.0)
            sh *= 2
        cum_ref[...] = c.T

    return pl.pallas_call(
        body, name=name, grid=(1,),
        in_specs=[pl.BlockSpec((M, LANES), lambda i: (0, f_block)), pl.BlockSpec((1, LANES), lambda i: (0, 0))],
        out_specs=pl.BlockSpec((LANES, M), lambda i: (0, 0)),
        out_shape=jax.ShapeDtypeStruct((LANES, M), F32),
        compiler_params=_cparams("arbitrary"),
    )(z, bpad)


def _col_to_row(col):
    n = col.shape[0]
    return jnp.transpose(jnp.broadcast_to(col, (n, LANES)))[0:1, :]


def _attn_fwd(qh, kh, vb, cum_c, cum_r, name):
    M = qh.shape[0]
    H = qh.shape[1] // HEAD_DIM
    scale = 1.0 / math.sqrt(HEAD_DIM)

    def body(q_ref, k_ref, v_ref, cq_ref, ck_ref, o_ref, lc_ref, lr_ref):
        i = pl.program_id(1)
        s = lax.dot_general(q_ref[...], k_ref[...], NT_DIMS, preferred_element_type=F32) * scale
        s = s + (cq_ref[...] - ck_ref[...])
        row = i * TQ + lax.broadcasted_iota(jnp.int32, (TQ, 1), 0)
        col = lax.broadcasted_iota(jnp.int32, (1, M), 1)
        s = jnp.where(row >= col, s, NEG)
        m = jnp.max(s, axis=1, keepdims=True)
        p = jnp.exp(s - m)
        l = jnp.sum(p, axis=1, keepdims=True)
        pn = (p / l).astype(BF16)
        o_ref[...] = jnp.dot(pn, v_ref[...], preferred_element_type=F32).astype(BF16)
        lse = m + jnp.log(l)
        lc_ref[...] = lse
        lr_ref[...] = _col_to_row(lse)

    full = pl.BlockSpec((M, HEAD_DIM), lambda h, i: (0, h))
    tile = pl.BlockSpec((TQ, HEAD_DIM), lambda h, i: (i, h))
    colv = pl.BlockSpec((None, TQ, 1), lambda h, i: (h, i, 0))
    rowv_full = pl.BlockSpec((None, 1, M), lambda h, i: (h, 0, 0))
    rowv = pl.BlockSpec((None, 1, TQ), lambda h, i: (h, 0, i))
    return pl.pallas_call(
        body, name=name, grid=(H, M // TQ),
        in_specs=[tile, full, full, colv, rowv_full],
        out_specs=[tile, colv, rowv],
        out_shape=[jax.ShapeDtypeStruct((M, H * HEAD_DIM), BF16), jax.ShapeDtypeStruct((H, M, 1), F32),
                   jax.ShapeDtypeStruct((H, 1, M), F32)],
        compiler_params=_cparams("parallel", "parallel"),
    )(qh, kh, vb, cum_c, cum_r)


def _attn_bwd_q(qh, kh, vb, dob, cum_c, cum_r, lse_c, name):
    M = qh.shape[0]
    H = qh.shape[1] // HEAD_DIM
    scale = 1.0 / math.sqrt(HEAD_DIM)

    def body(q_ref, k_ref, v_ref, do_ref, cq_ref, ck_ref, l_ref, dq_ref, dr_ref, dcq_ref):
        i = pl.program_id(1)
        s = lax.dot_general(q_ref[...], k_ref[...], NT_DIMS, preferred_element_type=F32) * scale
        s = s + (cq_ref[...] - ck_ref[...])
        row = i * TQ + lax.broadcasted_iota(jnp.int32, (TQ, 1), 0)
        col = lax.broadcasted_iota(jnp.int32, (1, M), 1)
        p = jnp.exp(jnp.where(row >= col, s, NEG) - l_ref[...])
        dp = lax.dot_general(do_ref[...], v_ref[...], NT_DIMS, preferred_element_type=F32)
        delta = jnp.sum(p * dp, axis=1, keepdims=True)
        ds = p * (dp - delta)
        dq_ref[...] = jnp.dot((ds * scale).astype(BF16), k_ref[...], preferred_element_type=F32)
        dr_ref[...] = _col_to_row(delta)
        dcq_ref[...] = jnp.sum(ds, axis=1, keepdims=True)

    full = pl.BlockSpec((M, HEAD_DIM), lambda h, i: (0, h))
    tile = pl.BlockSpec((TQ, HEAD_DIM), lambda h, i: (i, h))
    colv = pl.BlockSpec((None, TQ, 1), lambda h, i: (h, i, 0))
    rowv_full = pl.BlockSpec((None, 1, M), lambda h, i: (h, 0, 0))
    rowv = pl.BlockSpec((None, 1, TQ), lambda h, i: (h, 0, i))
    return pl.pallas_call(
        body, name=name, grid=(H, M // TQ),
        in_specs=[tile, full, full, tile, colv, rowv_full, colv],
        out_specs=[tile, rowv, colv],
        out_shape=[jax.ShapeDtypeStruct((M, H * HEAD_DIM), F32), jax.ShapeDtypeStruct((H, 1, M), F32),
                   jax.ShapeDtypeStruct((H, M, 1), F32)],
        compiler_params=_cparams("parallel", "parallel"),
    )(qh, kh, vb, dob, cum_c, cum_r, lse_c)


def _attn_bwd_kv(qh, kh, vb, dob, cum_c, cum_r, lse_r, delta_r, name):
    M = qh.shape[0]
    H = qh.shape[1] // HEAD_DIM
    scale = 1.0 / math.sqrt(HEAD_DIM)

    def body(k_ref, v_ref, q_ref, do_ref, cq_ref, ck_ref, l_ref, d_ref, dk_ref, dv_ref, dck_ref):
        j = pl.program_id(1)
        st = lax.dot_general(k_ref[...], q_ref[...], NT_DIMS, preferred_element_type=F32) * scale
        st = st + (cq_ref[...] - ck_ref[...])
        krow = j * TQ + lax.broadcasted_iota(jnp.int32, (TQ, 1), 0)
        qcol = lax.broadcasted_iota(jnp.int32, (1, M), 1)
        pt = jnp.exp(jnp.where(qcol >= krow, st, NEG) - l_ref[...])
        dpt = lax.dot_general(v_ref[...], do_ref[...], NT_DIMS, preferred_element_type=F32)
        dst = pt * (dpt - d_ref[...])
        dv_ref[...] = jnp.dot(pt.astype(BF16), do_ref[...], preferred_element_type=F32).astype(BF16)
        dk_ref[...] = jnp.dot((dst * scale).astype(BF16), q_ref[...], preferred_element_type=F32)
        dck_ref[...] = -jnp.sum(dst, axis=1, keepdims=True)

    full = pl.BlockSpec((M, HEAD_DIM), lambda h, j: (0, h))
    tile = pl.BlockSpec((TQ, HEAD_DIM), lambda h, j: (j, h))
    colv = pl.BlockSpec((None, TQ, 1), lambda h, j: (h, j, 0))
    rowv_full = pl.BlockSpec((None, 1, M), lambda h, j: (h, 0, 0))
    return pl.pallas_call(
        body, name=name, grid=(H, M // TQ),
        in_specs=[tile, tile, full, full, rowv_full, colv, rowv_full, rowv_full],
        out_specs=[tile, tile, colv],
        out_shape=[jax.ShapeDtypeStruct((M, H * HEAD_DIM), F32), jax.ShapeDtypeStruct((M, H * HEAD_DIM), BF16),
                   jax.ShapeDtypeStruct((H, M, 1), F32)],
        compiler_params=_cparams("parallel", "parallel"),
    )(kh, vb, qh, dob, cum_r, cum_c, lse_r, delta_r)


def _qk_norm_bwd(dqh, dkh, z, gq, gk, n_heads, q_col, name):
    M = z.shape[0]
    H = n_heads
    qb = q_col // HEAD_DIM

    def body(dqh_ref, dkh_ref, q_ref, k_ref, gq_ref, gk_ref, dq_ref, dk_ref, dgq_ref, dgk_ref):
        h = pl.program_id(0)

        def one(dy, xv, g):
            r = lax.rsqrt(jnp.mean(xv * xv, axis=-1, keepdims=True) + EPS)
            w = dy * g
            c = jnp.mean(w * xv, axis=-1, keepdims=True)
            dx = r * w - xv * (r * r * r * c)
            return dx.astype(BF16), jnp.sum(dy * (xv * r), axis=0, keepdims=True)

        dq, dgq = one(dqh_ref[...], q_ref[...], gq_ref[...])
        dk, dgk = one(dkh_ref[...], k_ref[...], gk_ref[...])
        dq_ref[...] = dq
        dk_ref[...] = dk

        @pl.when(h == 0)
        def _():
            dgq_ref[...] = dgq
            dgk_ref[...] = dgk

        @pl.when(h > 0)
        def _():
            dgq_ref[...] += dgq
            dgk_ref[...] += dgk

    vec = pl.BlockSpec((1, HEAD_DIM), lambda h: (0, 0))
    head = pl.BlockSpec((M, HEAD_DIM), lambda h: (0, h))
    return pl.pallas_call(
        body, name=name, grid=(H,),
        in_specs=[head, head, pl.BlockSpec((M, HEAD_DIM), lambda h: (0, qb + h)),
                  pl.BlockSpec((M, HEAD_DIM), lambda h: (0, qb + H + h)), vec, vec],
        out_specs=[head, head, vec, vec],
        out_shape=[jax.ShapeDtypeStruct((M, H * HEAD_DIM), BF16), jax.ShapeDtypeStruct((M, H * HEAD_DIM), BF16),
                   jax.ShapeDtypeStruct((1, HEAD_DIM), F32), jax.ShapeDtypeStruct((1, HEAD_DIM), F32)],
        compiler_params=_cparams("arbitrary"),
    )(dqh, dkh, z, z, gq, gk)


def _forget_bwd(dcq, dck, z, bpad, f_block, name):
    H, M, _ = dcq.shape

    def body(dcq_ref, dck_ref, f_ref, b_ref, dfl_ref, db_ref):
        lane = lax.broadcasted_iota(jnp.int32, (1, LANES), 1)
        d = jnp.zeros((M, LANES), F32)
        for h in range(H):
            d = d + (dcq_ref[h] + dck_ref[h]) * (lane == h).astype(F32)
        t = lax.broadcasted_iota(jnp.int32, (M, 1), 0)
        sh = 1
        while sh < M:
            d = d + jnp.where(t < M - sh, pltpu.roll(d, M - sh, 0), 0.0)
            sh *= 2
        xx = f_ref[...] + b_ref[...]
        dfl = d * (1.0 / (1.0 + jnp.exp(xx)))
        dfl_ref[...] = dfl.astype(BF16)
        db_ref[...] = jnp.sum(dfl, axis=0, keepdims=True)

    colv = pl.BlockSpec((H, M, 1), lambda i: (0, 0, 0))
    return pl.pallas_call(
        body, name=name, grid=(1,),
        in_specs=[colv, colv, pl.BlockSpec((M, LANES), lambda i: (0, f_block)),
                  pl.BlockSpec((1, LANES), lambda i: (0, 0))],
        out_specs=[pl.BlockSpec((M, LANES), lambda i: (0, 0)), pl.BlockSpec((1, LANES), lambda i: (0, 0))],
        out_shape=[jax.ShapeDtypeStruct((M, LANES), BF16), jax.ShapeDtypeStruct((1, LANES), F32)],
        compiler_params=_cparams("arbitrary"),
    )(dcq, dck, z, bpad)


def _ffn_fwd(h, g, wg, wu, wd, tag):
    n, r = _rmsnorm_fwd(h, g, f"{tag}_norm")
    a, b, s = _ffn_up(n, wg, wu, 256, f"{tag}_up")
    h_out = _mm_nn_residual(s, wd, h, 0.5, 512, 512, f"{tag}_down")
    return h_out, (n, r, a, b, s)


def _ffn_bwd(dh, dhb, h, g, wg, wu, wd, saved, tag):
    n, r, a, b, s = saved
    n_shards = 4
    da, db = _ffn_bwd_hidden(dhb, wd, a, b, 256, f"{tag}_bwd_hidden")
    dwd = _mm_tn(s, dhb, 0.5, 512, 512, f"{tag}_dw_down", stacked=False)
    dwg = _mm_tn(n, da, 1.0, 512, wg.shape[1] // n_shards, f"{tag}_dw_gate", stacked=True)
    dwu = _mm_tn(n, db, 1.0, 512, wu.shape[1] // n_shards, f"{tag}_dw_up", stacked=True)
    dn = _mm_nt_sum([da, db], [wg, wu], 512, 512, f"{tag}_dn")
    dh_in, dhb_in, dg = _rmsnorm_bwd(dn, h, r, g, dh, f"{tag}_norm_bwd")
    return dh_in, dhb_in, dg, dwg, dwu, dwd


def _local_step(x, target, W):
    seq, D = x.shape
    L = N_META + seq
    Lp = -(-L // SEQ_ALIGN) * SEQ_ALIGN
    pad = jnp.zeros((Lp - L, D), F32)
    h0 = jnp.concatenate([W["meta"], x, pad], axis=0)
    tgt = jnp.concatenate([jnp.zeros((N_META, D), F32), target, pad], axis=0)

    d_pool = W["pool_scale"].shape[1]
    n_heads = W["b_forget"].shape[1]
    d_att = n_heads * HEAD_DIM
    f_col = d_pool + 3 * d_att
    f_block = f_col // LANES
    bpad = jnp.pad(W["b_forget"], ((0, 0), (0, LANES - n_heads)))

    h1, ffn1 = _ffn_fwd(h0, W["ffn1_norm"], W["wg1"], W["wu1"], W["wd1"], "ffn1")
    u, r_mix = _rmsnorm_fwd(h1, W["mix_norm"], "mix_norm")
    z = _mm_nn(u, W["win"], 384, "in_proj")
    pooled, pool_out = _pool_fwd(z, W["pool_w"], W["pool_scale"], "pool_fwd")
    qh, kh, vb = _qkv_prep(z, W["q_norm"], W["k_norm"], n_heads, d_pool, "qkv_prep")
    cum_t = _forget_fwd(z, bpad, f_block, "forget_fwd")[:n_heads]
    cum_c = cum_t.reshape(n_heads, Lp, 1)
    cum_r = cum_t.reshape(n_heads, 1, Lp)
    att, lse_c, lse_r = _attn_fwd(qh, kh, vb, cum_c, cum_r, "attn_fwd")
    mix = jnp.concatenate([pool_out, att], axis=1)
    h2 = _mm_nn_residual(mix, W["wout"], h1, 1.0, 512, 512, "out_proj")
    h3, ffn2 = _ffn_fwd(h2, W["ffn2_norm"], W["wg2"], W["wu2"], W["wd2"], "ffn2")

    dh3, dh3b, loss = _loss_grad(h3, tgt, seq, "loss")
    dh2, dh2b, dg_ffn2, dwg2, dwu2, dwd2 = _ffn_bwd(
        dh3, dh3b, h2, W["ffn2_norm"], W["wg2"], W["wu2"], W["wd2"], ffn2, "ffn2")

    dmix = _mm_nt(dh2b, W["wout"], 512, "out_proj_bwd")
    dwout = _mm_tn(mix, dh2b, 1.0, 512, 512, "dw_out", stacked=False)
    dp, dpw, dpsc = _pool_bwd(dmix, pooled, W["pool_w"], W["pool_scale"], "pool_bwd")
    dob = dmix[:, d_pool:].astype(BF16)
    dqh, delta_r, dcq = _attn_bwd_q(qh, kh, vb, dob, cum_c, cum_r, lse_c, "attn_bwd_q")
    dkh, dv, dck = _attn_bwd_kv(qh, kh, vb, dob, cum_c, cum_r, lse_r, delta_r, "attn_bwd_kv")
    dq, dk, dgq, dgk = _qk_norm_bwd(dqh, dkh, z, W["q_norm"], W["k_norm"], n_heads, d_pool, "qk_norm_bwd")
    dfl, dbf = _forget_bwd(dcq, dck, z, bpad, f_block, "forget_bwd")
    dz = jnp.concatenate([dp, dq, dk, dv, dfl], axis=1)
    dwin = _mm_tn(u, dz, 1.0, 512, 384, "dw_in", stacked=False)
    du = _mm_nt_sum([dz], [W["win"]], 512, 384, "in_proj_bwd")
    dh1, dh1b, dg_mix = _rmsnorm_bwd(du, h1, r_mix, W["mix_norm"], dh2, "mix_norm_bwd")

    dh0, _, dg_ffn1, dwg1, dwu1, dwd1 = _ffn_bwd(
        dh1, dh1b, h0, W["ffn1_norm"], W["wg1"], W["wu1"], W["wd1"], ffn1, "ffn1")

    grads = dict(
        x=dh0[N_META:L], meta=dh0[:N_META],
        ffn1_norm=dg_ffn1, mix_norm=dg_mix, ffn2_norm=dg_ffn2, q_norm=dgq, k_norm=dgk,
        b_forget=dbf[:, :n_heads], pool_scale=dpsc,
        wg1=dwg1, wu1=dwu1, wd1=dwd1, win=dwin, wout=dwout, pool_w=dpw,
        wg2=dwg2, wu2=dwu2, wd2=dwd2,
    )
    return loss[0, 0], grads


HBM_SPEC = pl.BlockSpec(memory_space=pltpu.HBM)
N_CHIPS = 4


def _chip_peers():
    x, y, c = lax.axis_index("x"), lax.axis_index("y"), lax.axis_index("c")
    flips = [(1 - x, y), (x, 1 - y), (1 - x, 1 - y)]
    return 2 * x + y, [((px, py, c), 2 * px + py) for px, py in flips]


def _gathered_shape(shape, layout):
    if layout == "rows":
        return (N_CHIPS * shape[0],) + shape[1:]
    if layout == "cols":
        return (shape[0], N_CHIPS * shape[1])
    return (N_CHIPS,) + shape


def _all_gather_chips(shards, layouts):
    n = len(shards)
    halves = [s.shape[0] // 2 for s in shards]

    def body(*refs):
        in_refs = refs[:n]
        out_refs = refs[n:2 * n]
        ici_send, ici_recv, d2d_send, d2d_recv, local_sems = refs[2 * n:]
        c = lax.axis_index("c")
        sib = (lax.axis_index("x"), lax.axis_index("y"), 1 - c)
        me, peers = _chip_peers()

        def region(a, chip, half):
            rows_a, h = shards[a].shape[0], halves[a]
            start, size = (0, rows_a) if half is None else (half * h, h)
            if layouts[a] == "rows":
                return out_refs[a].at[pl.ds(chip * rows_a + start, size)]
            if layouts[a] == "cols":
                cols_a = shards[a].shape[1]
                return out_refs[a].at[pl.ds(start, size), pl.ds(chip * cols_a, cols_a)]
            return out_refs[a].at[chip, pl.ds(start, size)]

        own = []
        for a in range(n):
            cp = pltpu.make_async_copy(in_refs[a], region(a, me, None), local_sems.at[a])
            cp.start()
            own.append(cp)
        sends = []
        for a in range(n):
            for k, (dev, _) in enumerate(peers):
                cp = pltpu.make_async_remote_copy(
                    src_ref=in_refs[a].at[pl.ds(c * halves[a], halves[a])], dst_ref=region(a, me, c),
                    send_sem=ici_send.at[a, k], recv_sem=ici_recv.at[a, k], device_id=dev, device_id_type=MESH)
                cp.start()
                sends.append(cp)
        for a in range(n):
            for k, (dev, pidx) in enumerate(peers):
                landed = region(a, pidx, c)
                pltpu.make_async_remote_copy(
                    src_ref=landed, dst_ref=landed, send_sem=ici_send.at[a, k], recv_sem=ici_recv.at[a, k],
                    device_id=dev, device_id_type=MESH).wait_recv()
                fwd = pltpu.make_async_remote_copy(
                    src_ref=landed, dst_ref=landed, send_sem=d2d_send.at[a, k], recv_sem=d2d_recv.at[a, k],
                    device_id=sib, device_id_type=MESH)
                fwd.start()
                sends.append(fwd)
        for a in range(n):
            for k, (_, pidx) in enumerate(peers):
                other = region(a, pidx, 1 - c)
                pltpu.make_async_remote_copy(
                    src_ref=other, dst_ref=other, send_sem=d2d_send.at[a, k], recv_sem=d2d_recv.at[a, k],
                    device_id=sib, device_id_type=MESH).wait_recv()
        for cp in sends:
            cp.wait_send()
        for cp in own:
            cp.wait()

    sem = pltpu.SemaphoreType.DMA((n, 3))
    return pl.pallas_call(
        body, name="gather_weights",
        in_specs=[HBM_SPEC] * n, out_specs=[HBM_SPEC] * n,
        out_shape=[jax.ShapeDtypeStruct(_gathered_shape(s.shape, lay), s.dtype) for s, lay in zip(shards, layouts)],
        scratch_shapes=[sem, sem, sem, sem, pltpu.SemaphoreType.DMA((n,))],
    )(*shards)


def _send_sibling_halves(stacked):
    n = len(stacked)

    def body(*refs):
        in_refs = refs[:n]
        out_refs = refs[n:2 * n]
        send_sems, recv_sems = refs[2 * n:]
        c = lax.axis_index("c")
        sib = (lax.axis_index("x"), lax.axis_index("y"), 1 - c)
        sends = []
        for a in range(n):
            h = stacked[a].shape[1] // 2
            cp = pltpu.make_async_remote_copy(
                src_ref=in_refs[a].at[:, pl.ds((1 - c) * h, h)], dst_ref=out_refs[a], send_sem=send_sems.at[a],
                recv_sem=recv_sems.at[a], device_id=sib, device_id_type=MESH)
            cp.start()
            sends.append(cp)
        for cp in sends:
            cp.wait_recv()
        for cp in sends:
            cp.wait_send()

    return pl.pallas_call(
        body, name="send_sibling_halves",
        in_specs=[HBM_SPEC] * n, out_specs=[HBM_SPEC] * n,
        out_shape=[jax.ShapeDtypeStruct((s.shape[0], s.shape[1] // 2, s.shape[2]), s.dtype) for s in stacked],
        scratch_shapes=[pltpu.SemaphoreType.DMA((n,)), pltpu.SemaphoreType.DMA((n,))],
    )(*stacked)


def _scatter_slabs(stacked):
    n = len(stacked)

    def body(*refs):
        in_refs = refs[:n]
        out_refs = refs[n:2 * n]
        send_sems, recv_sems = refs[2 * n:]
        _, peers = _chip_peers()
        sends = []
        for a in range(n):
            for k, (dev, pidx) in enumerate(peers):
                cp = pltpu.make_async_remote_copy(
                    src_ref=in_refs[a].at[pidx], dst_ref=out_refs[a].at[k], send_sem=send_sems.at[a, k],
                    recv_sem=recv_sems.at[a, k], device_id=dev, device_id_type=MESH)
                cp.start()
                sends.append(cp)
        for cp in sends:
            cp.wait_recv()
        for cp in sends:
            cp.wait_send()

    return pl.pallas_call(
        body, name="scatter_grads",
        in_specs=[HBM_SPEC] * n, out_specs=[HBM_SPEC] * n,
        out_shape=[jax.ShapeDtypeStruct((3,) + s.shape[1:], s.dtype) for s in stacked],
        scratch_shapes=[pltpu.SemaphoreType.DMA((n, 3)), pltpu.SemaphoreType.DMA((n, 3))],
    )(*stacked)


def _join_halves(halves):
    n = len(halves)

    def body(*refs):
        in_refs = refs[:n]
        out_refs = refs[n:2 * n]
        send_sems, recv_sems, local_sems = refs[2 * n:]
        c = lax.axis_index("c")
        sib = (lax.axis_index("x"), lax.axis_index("y"), 1 - c)
        copies = []
        for a in range(n):
            h = halves[a].shape[0]
            mine = out_refs[a].at[pl.ds(c * h, h)]
            own = pltpu.make_async_copy(in_refs[a], mine, local_sems.at[a])
            own.start()
            cp = pltpu.make_async_remote_copy(
                src_ref=in_refs[a], dst_ref=mine, send_sem=send_sems.at[a], recv_sem=recv_sems.at[a],
                device_id=sib, device_id_type=MESH)
            cp.start()
            copies.append((own, cp))
        for a in range(n):
            h = halves[a].shape[0]
            other = out_refs[a].at[pl.ds((1 - c) * h, h)]
            pltpu.make_async_remote_copy(
                src_ref=in_refs[a], dst_ref=other, send_sem=send_sems.at[a], recv_sem=recv_sems.at[a],
                device_id=sib, device_id_type=MESH).wait_recv()
        for own, cp in copies:
            cp.wait_send()
            own.wait()

    return pl.pallas_call(
        body, name="join_halves",
        in_specs=[HBM_SPEC] * n, out_specs=[HBM_SPEC] * n,
        out_shape=[jax.ShapeDtypeStruct((2 * s.shape[0], s.shape[1]), s.dtype) for s in halves],
        scratch_shapes=[pltpu.SemaphoreType.DMA((n,)), pltpu.SemaphoreType.DMA((n,)), pltpu.SemaphoreType.DMA((n,))],
    )(*halves)


def _all_reduce_small(v):
    R, C = v.shape
    n_dev = 8

    def body(v_ref, o_ref, buf, send_sems, recv_sems):
        x, y, c = lax.axis_index("x"), lax.axis_index("y"), lax.axis_index("c")
        me = 4 * x + 2 * y + c
        buf[me] = v_ref[...]
        sends = []
        for k in range(1, n_dev):
            px, py, pc = x ^ ((k >> 2) & 1), y ^ ((k >> 1) & 1), c ^ (k & 1)
            cp = pltpu.make_async_remote_copy(
                src_ref=v_ref, dst_ref=buf.at[me], send_sem=send_sems.at[k - 1], recv_sem=recv_sems.at[k - 1],
                device_id=(px, py, pc), device_id_type=MESH)
            cp.start()
            sends.append((cp, 4 * px + 2 * py + pc))
        for k in range(1, n_dev):
            cp, pidx = sends[k - 1]
            pltpu.make_async_remote_copy(
                src_ref=v_ref, dst_ref=buf.at[pidx], send_sem=send_sems.at[k - 1], recv_sem=recv_sems.at[k - 1],
                device_id=(x, y, c), device_id_type=MESH).wait_recv()
        for cp, _ in sends:
            cp.wait_send()
        acc = buf[0]
        for d in range(1, n_dev):
            acc = acc + buf[d]
        o_ref[...] = acc

    vm = pl.BlockSpec(memory_space=pltpu.VMEM)
    return pl.pallas_call(
        body, name="all_reduce_small",
        in_specs=[vm], out_specs=vm,
        out_shape=jax.ShapeDtypeStruct((R, C), F32),
        scratch_shapes=[pltpu.VMEM((n_dev, R, C), F32), pltpu.SemaphoreType.DMA((n_dev - 1,)),
                        pltpu.SemaphoreType.DMA((n_dev - 1,))],
    )(v)


def _pair_sum(place, own, sib, name):
    S, R, C = own.shape
    h = R // 2
    tr = _row_tile(h, C)
    nt = h // tr

    def body(place_ref, o_ref, s_ref, out_ref):
        out_ref[...] = (o_ref[...].astype(F32) + s_ref[...].astype(F32)).astype(BF16)

    return pl.pallas_call(
        body, name=name,
        grid_spec=pltpu.PrefetchScalarGridSpec(
            num_scalar_prefetch=1, grid=(S, nt),
            in_specs=[pl.BlockSpec((None, tr, C), lambda s, i, p: (s, p[0] * nt + i, 0)),
                      pl.BlockSpec((None, tr, C), lambda s, i, p: (s, i, 0))],
            out_specs=pl.BlockSpec((None, tr, C), lambda s, i, p: (s, i, 0))),
        out_shape=jax.ShapeDtypeStruct((S, h, C), BF16),
        compiler_params=_cparams("parallel", "parallel"),
    )(place, own, sib)


def _sum_slabs(place, own, sib, recv, name):
    S, R, C = own.shape
    h = R // 2
    tr = _row_tile(h, C)
    nt = h // tr

    def body(place_ref, o_ref, s_ref, r_ref, out_ref):
        acc = o_ref[...].astype(F32) + s_ref[...].astype(F32)
        for k in range(3):
            acc = acc + r_ref[k].astype(F32)
        out_ref[...] = acc

    return pl.pallas_call(
        body, name=name,
        grid_spec=pltpu.PrefetchScalarGridSpec(
            num_scalar_prefetch=1, grid=(nt,),
            in_specs=[pl.BlockSpec((None, tr, C), lambda i, p: (p[1], p[0] * nt + i, 0)),
                      pl.BlockSpec((None, tr, C), lambda i, p: (p[1], i, 0)),
                      pl.BlockSpec((3, tr, C), lambda i, p: (0, i, 0))],
            out_specs=pl.BlockSpec((tr, C), lambda i, p: (i, 0))),
        out_shape=jax.ShapeDtypeStruct((h, C), F32),
        compiler_params=_cparams("parallel"),
    )(place, own, sib, recv)


def _adamw(parts, w, m, v, name):
    R, C = w.shape
    tr = _row_tile(R, C)
    npart = len(parts)
    c1 = 1.0 - ADAM_B1 ** ADAM_STEP
    c2 = 1.0 - ADAM_B2 ** ADAM_STEP

    def body(*refs):
        p_refs = refs[:npart]
        w_ref, m_ref, v_ref, g_ref, d_ref, nm_ref, nv_ref = refs[npart:]
        g = p_refs[0][...]
        for p_ref in p_refs[1:]:
            g = g + p_ref[...]
        nm = ADAM_B1 * m_ref[...] + (1.0 - ADAM_B1) * g
        nv = ADAM_B2 * v_ref[...] + (1.0 - ADAM_B2) * (g * g)
        m_hat = nm / c1
        v_hat = nv / c2
        g_ref[...] = g
        d_ref[...] = -ADAM_LR * (m_hat / (jnp.sqrt(v_hat) + ADAM_EPS) + ADAM_WD * w_ref[...])
        nm_ref[...] = nm
        nv_ref[...] = nv

    blk = pl.BlockSpec((tr, C), lambda i: (i, 0))
    shape = jax.ShapeDtypeStruct((R, C), F32)
    return pl.pallas_call(
        body, name=name, grid=(R // tr,),
        in_specs=[blk] * (npart + 3), out_specs=[blk] * 4, out_shape=[shape] * 4,
        compiler_params=_cparams("parallel"),
    )(*parts, w, m, v)


SMALL_NAMES = ("ffn1_norm", "mix_norm", "ffn2_norm", "pool_scale", "q_norm", "k_norm", "b_forget")
SMALL_COLS = 1024


def _pack_small(vals):
    rows = [vals[n].reshape(-1, SMALL_COLS) for n in ("ffn1_norm", "mix_norm", "ffn2_norm", "pool_scale")]
    tail = jnp.concatenate([vals["q_norm"].reshape(-1), vals["k_norm"].reshape(-1), vals["b_forget"].reshape(-1)])
    rows.append(jnp.pad(tail, (0, SMALL_COLS - tail.shape[0])).reshape(1, SMALL_COLS))
    return jnp.concatenate(rows, axis=0)


def _unpack_small(packed, like):
    out = {}
    r = 0
    for n in ("ffn1_norm", "mix_norm", "ffn2_norm", "pool_scale"):
        k = like[n].size // SMALL_COLS
        out[n] = packed[r:r + k].reshape(like[n].shape)
        r += k
    o = 0
    for n in ("q_norm", "k_norm", "b_forget"):
        k = like[n].size
        out[n] = packed[r, o:o + k].reshape(like[n].shape)
        o += k
    return out


def kernel(x, meta_tokens, ffn1_norm, ffn1_w_gate, ffn1_w_up, ffn1_w_down, mix_norm, w_in, b_forget, q_norm, k_norm, pool_w, pool_scale, w_out, ffn2_norm, ffn2_w_gate, ffn2_w_up, ffn2_w_down, loss_target, m_meta_tokens, m_ffn1_norm, m_ffn1_w_gate, m_ffn1_w_up, m_ffn1_w_down, m_mix_norm, m_w_in, m_b_forget, m_q_norm, m_k_norm, m_pool_w, m_pool_scale, m_w_out, m_ffn2_norm, m_ffn2_w_gate, m_ffn2_w_up, m_ffn2_w_down, v_meta_tokens, v_ffn1_norm, v_ffn1_w_gate, v_ffn1_w_up, v_ffn1_w_down, v_mix_norm, v_w_in, v_b_forget, v_q_norm, v_k_norm, v_pool_w, v_pool_scale, v_w_out, v_ffn2_norm, v_ffn2_w_gate, v_ffn2_w_up, v_ffn2_w_down):
    wts = dict(meta_tokens=meta_tokens, ffn1_norm=ffn1_norm, ffn1_w_gate=ffn1_w_gate, ffn1_w_up=ffn1_w_up,
               ffn1_w_down=ffn1_w_down, mix_norm=mix_norm, w_in=w_in, b_forget=b_forget, q_norm=q_norm,
               k_norm=k_norm, pool_w=pool_w, pool_scale=pool_scale, w_out=w_out, ffn2_norm=ffn2_norm,
               ffn2_w_gate=ffn2_w_gate, ffn2_w_up=ffn2_w_up, ffn2_w_down=ffn2_w_down)
    mom = dict(meta_tokens=m_meta_tokens, ffn1_norm=m_ffn1_norm, ffn1_w_gate=m_ffn1_w_gate, ffn1_w_up=m_ffn1_w_up,
               ffn1_w_down=m_ffn1_w_down, mix_norm=m_mix_norm, w_in=m_w_in, b_forget=m_b_forget, q_norm=m_q_norm,
               k_norm=m_k_norm, pool_w=m_pool_w, pool_scale=m_pool_scale, w_out=m_w_out, ffn2_norm=m_ffn2_norm,
               ffn2_w_gate=m_ffn2_w_gate, ffn2_w_up=m_ffn2_w_up, ffn2_w_down=m_ffn2_w_down)
    var = dict(meta_tokens=v_meta_tokens, ffn1_norm=v_ffn1_norm, ffn1_w_gate=v_ffn1_w_gate, ffn1_w_up=v_ffn1_w_up,
               ffn1_w_down=v_ffn1_w_down, mix_norm=v_mix_norm, w_in=v_w_in, b_forget=v_b_forget, q_norm=v_q_norm,
               k_norm=v_k_norm, pool_w=v_pool_w, pool_scale=v_pool_scale, w_out=v_w_out, ffn2_norm=v_ffn2_norm,
               ffn2_w_gate=v_ffn2_w_gate, ffn2_w_up=v_ffn2_w_up, ffn2_w_down=v_ffn2_w_down)
    order = list(wts)
    me = 2 * lax.axis_index("x") + lax.axis_index("y")

    D = x.shape[2]
    d_in_shard = w_in.shape[2]
    d_in = N_CHIPS * d_in_shard
    d_in_shard_pad = -(-d_in_shard // LANES) * LANES
    n_heads = b_forget.shape[1]
    d_in_pad = (d_in - n_heads) + LANES

    big = ("ffn1_w_gate", "ffn1_w_up", "ffn1_w_down", "w_in", "w_out", "pool_w", "ffn2_w_gate", "ffn2_w_up",
           "ffn2_w_down")
    shards = {n: wts[n][0].astype(BF16) for n in big}
    shards["w_in"] = jnp.pad(shards["w_in"], ((0, 0), (0, d_in_shard_pad - d_in_shard)))
    layouts = dict(ffn1_w_gate="cols", ffn1_w_up="cols", ffn1_w_down="rows", w_in="stack", w_out="rows",
                   pool_w="stack", ffn2_w_gate="cols", ffn2_w_up="cols", ffn2_w_down="rows", meta_tokens="stack")
    names = big + ("meta_tokens",)
    shards["meta_tokens"] = meta_tokens
    gathered = _all_gather_chips([shards[n] for n in names], [layouts[n] for n in names])
    G = dict(zip(names, gathered))

    def cols(st):
        return jnp.transpose(st, (1, 0, 2)).reshape(st.shape[1], -1)

    win = cols(G["w_in"][:, :, :d_in_shard])
    W = dict(
        meta=cols(G["meta_tokens"]),
        wg1=G["ffn1_w_gate"], wu1=G["ffn1_w_up"], wd1=G["ffn1_w_down"],
        win=jnp.pad(win, ((0, 0), (0, d_in_pad - d_in))), wout=G["w_out"],
        pool_w=jnp.transpose(G["pool_w"], (1, 0, 2, 3)).reshape(N_POOL_GROUPS, pool_w.shape[3], pool_w.shape[3]),
        wg2=G["ffn2_w_gate"], wu2=G["ffn2_w_up"], wd2=G["ffn2_w_down"],
        ffn1_norm=ffn1_norm, mix_norm=mix_norm, ffn2_norm=ffn2_norm, q_norm=q_norm, k_norm=k_norm,
        b_forget=b_forget, pool_scale=pool_scale,
    )

    loss_part, gr = _local_step(x[0], loss_target[0], W)
    loss = lax.psum(loss_part, ("x", "y", "c"))

    def split_rows(a):
        return a.reshape(N_CHIPS, -1, a.shape[1])

    def split_win(a):
        a = jnp.transpose(a[:, :d_in].reshape(D, N_CHIPS, d_in_shard), (1, 0, 2))
        return jnp.pad(a, ((0, 0), (0, 0), (0, d_in_shard_pad - d_in_shard)))

    def split_pool(a):
        r, c = pool_w.shape[2], pool_w.shape[3]
        return jnp.transpose(a.reshape(N_POOL_GROUPS, N_CHIPS, r, c), (1, 0, 2, 3)).reshape(N_CHIPS, -1, c)

    slabs = dict(
        ffn1_w_gate=gr["wg1"], ffn1_w_up=gr["wu1"], ffn1_w_down=split_rows(gr["wd1"]),
        w_in=split_win(gr["win"]), w_out=split_rows(gr["wout"]), pool_w=split_pool(gr["pool_w"]),
        ffn2_w_gate=gr["wg2"], ffn2_w_up=gr["wu2"], ffn2_w_down=split_rows(gr["wd2"]),
    )
    place = jnp.stack([lax.axis_index("c"), me]).astype(jnp.int32)
    own = [slabs[n] for n in big]
    from_sib = _send_sibling_halves(own)
    pair = [_pair_sum(place, o, s, f"pair_sum_{n}") for n, o, s in zip(big, own, from_sib)]
    received = _scatter_slabs(pair)
    halves = [_sum_slabs(place, o, s, r, f"sum_{n}") for n, o, s, r in zip(big, own, from_sib, received)]
    full = _join_halves(halves)

    out_g, out_d, out_m, out_v = {}, {}, {}, {}
    for n, g in zip(big, full):
        shape = wts[n].shape
        if n == "w_in":
            g = g[:, :d_in_shard]
        r, c = g.shape
        res = _adamw([g], wts[n].reshape(r, c), mom[n].reshape(r, c), var[n].reshape(r, c), f"adamw_{n}")
        out_g[n], out_d[n], out_m[n], out_v[n] = (a.reshape(shape) for a in res)

    small_g = _pack_small({n: gr[n] for n in SMALL_NAMES})
    meta_rows = gr["meta"].reshape(-1, SMALL_COLS)
    total = _all_reduce_small(jnp.concatenate([small_g, meta_rows], axis=0))
    n_small = small_g.shape[0]
    res = _adamw([total[:n_small]], _pack_small({n: wts[n] for n in SMALL_NAMES}),
                 _pack_small({n: mom[n] for n in SMALL_NAMES}), _pack_small({n: var[n] for n in SMALL_NAMES}),
                 "adamw_small")
    for dst, packed in zip((out_g, out_d, out_m, out_v), res):
        dst.update(_unpack_small(packed, wts))
    meta_cols = meta_tokens.shape[1]
    meta_g = lax.dynamic_slice_in_dim(total[n_small:].reshape(N_META, D), me * meta_cols, meta_cols, axis=1)
    res = _adamw([meta_g], meta_tokens, m_meta_tokens, v_meta_tokens, "adamw_meta")
    out_g["meta_tokens"], out_d["meta_tokens"], out_m["meta_tokens"], out_v["meta_tokens"] = res

    grad_x = gr["x"].reshape(x.shape)
    return (loss, grad_x, *[out_g[n] for n in order], *[out_d[n] for n in order], *[out_m[n] for n in order],
            *[out_v[n] for n in order])
```

```python
import functools
import math

import jax
import jax.numpy as jnp
from jax import lax
from jax.experimental import pallas as pl
from jax.experimental.pallas import tpu as pltpu

F32 = jnp.float32
BF16 = jnp.bfloat16

N_META = 16
EPS = 1e-6
HEAD_DIM = 128
N_POOL_GROUPS = 4
LANES = 128
SEQ_ALIGN = 128
TQ = 128
VMEM_LIMIT = 56 * 1024 * 1024
ELEMWISE_BLOCK_BYTES = 1 << 20

ADAM_LR = 0.001
ADAM_B1 = 0.9
ADAM_B2 = 0.999
ADAM_EPS = 1e-08
ADAM_WD = 0.01
ADAM_STEP = 10

NT_DIMS = (((1,), (1,)), ((), ()))
NEG = -1e30
MESH = pl.DeviceIdType.MESH


def _cparams(*sem):
    return pltpu.CompilerParams(dimension_semantics=sem, vmem_limit_bytes=VMEM_LIMIT)


def _sigmoid(a):
    return 1.0 / (1.0 + jnp.exp(-a))


def _row_tile(rows, cols, itemsize=4):
    best = None
    for t in range(16, rows + 1, 16):
        if rows % t == 0 and t * cols * itemsize <= ELEMWISE_BLOCK_BYTES:
            best = t
    return best if best is not None else rows


def _mm_nn(x, w, tn, name):
    M, K = x.shape
    N = w.shape[1]

    def body(x_ref, w_ref, o_ref):
        o_ref[...] = jnp.dot(x_ref[...], w_ref[...], preferred_element_type=F32)

    return pl.pallas_call(
        body, name=name, grid=(N // tn,),
        in_specs=[pl.BlockSpec((M, K), lambda j: (0, 0)), pl.BlockSpec((K, tn), lambda j: (0, j))],
        out_specs=pl.BlockSpec((M, tn), lambda j: (0, j)),
        out_shape=jax.ShapeDtypeStruct((M, N), F32),
        compiler_params=_cparams("parallel"),
    )(x, w)


def _ffn_up(n, wg, wu, tn, name):
    M, K = n.shape
    N = wg.shape[1]

    def body(n_ref, wg_ref, wu_ref, a_ref, b_ref, s_ref):
        nv = n_ref[...]
        a = jnp.dot(nv, wg_ref[...], preferred_element_type=F32)
        b = jnp.dot(nv, wu_ref[...], preferred_element_type=F32)
        a_ref[...] = a
        b_ref[...] = b
        s_ref[...] = (a * _sigmoid(a) * b).astype(BF16)

    wspec = pl.BlockSpec((K, tn), lambda j: (0, j))
    ospec = pl.BlockSpec((M, tn), lambda j: (0, j))
    return pl.pallas_call(
        body, name=name, grid=(N // tn,),
        in_specs=[pl.BlockSpec((M, K), lambda j: (0, 0)), wspec, wspec],
        out_specs=[ospec, ospec, ospec],
        out_shape=[jax.ShapeDtypeStruct((M, N), F32), jax.ShapeDtypeStruct((M, N), F32),
                   jax.ShapeDtypeStruct((M, N), BF16)],
        compiler_params=_cparams("parallel"),
    )(n, wg, wu)


def _mm_nn_residual(x, w, res, alpha, tn, tk, name):
    M, K = x.shape
    N = w.shape[1]
    nk = K // tk

    def body(x_ref, w_ref, r_ref, o_ref, acc):
        k = pl.program_id(1)

        @pl.when(k == 0)
        def _():
            acc[...] = jnp.zeros_like(acc)

        acc[...] += jnp.dot(x_ref[...], w_ref[...], preferred_element_type=F32)

        @pl.when(k == nk - 1)
        def _():
            o_ref[...] = r_ref[...] + alpha * acc[...]

    return pl.pallas_call(
        body, name=name, grid=(N // tn, nk),
        in_specs=[pl.BlockSpec((M, tk), lambda j, k: (0, k)), pl.BlockSpec((tk, tn), lambda j, k: (k, j)),
                  pl.BlockSpec((M, tn), lambda j, k: (0, j))],
        out_specs=pl.BlockSpec((M, tn), lambda j, k: (0, j)),
        out_shape=jax.ShapeDtypeStruct((M, N), F32),
        scratch_shapes=[pltpu.VMEM((M, tn), F32)],
        compiler_params=_cparams("parallel", "arbitrary"),
    )(x, w, res)


def _ffn_bwd_hidden(dhb, wd, a, b, tn, name):
    M, K = dhb.shape
    N = wd.shape[0]

    def body(dh_ref, w_ref, a_ref, b_ref, da_ref, db_ref):
        ds = 0.5 * lax.dot_general(dh_ref[...], w_ref[...], NT_DIMS, preferred_element_type=F32)
        av = a_ref[...]
        sig = _sigmoid(av)
        da_ref[...] = (ds * b_ref[...] * (sig * (1.0 + av * (1.0 - sig)))).astype(BF16)
        db_ref[...] = (ds * (av * sig)).astype(BF16)

    ospec = pl.BlockSpec((M, tn), lambda j: (0, j))
    return pl.pallas_call(
        body, name=name, grid=(N // tn,),
        in_specs=[pl.BlockSpec((M, K), lambda j: (0, 0)), pl.BlockSpec((tn, K), lambda j: (j, 0)), ospec, ospec],
        out_specs=[ospec, ospec],
        out_shape=[jax.ShapeDtypeStruct((M, N), BF16), jax.ShapeDtypeStruct((M, N), BF16)],
        compiler_params=_cparams("parallel"),
    )(dhb, wd, a, b)


def _mm_nt(x, w, tn, name):
    M, K = x.shape
    N = w.shape[0]

    def body(x_ref, w_ref, o_ref):
        o_ref[...] = lax.dot_general(x_ref[...], w_ref[...], NT_DIMS, preferred_element_type=F32)

    return pl.pallas_call(
        body, name=name, grid=(N // tn,),
        in_specs=[pl.BlockSpec((M, K), lambda j: (0, 0)), pl.BlockSpec((tn, K), lambda j: (j, 0))],
        out_specs=pl.BlockSpec((M, tn), lambda j: (0, j)),
        out_shape=jax.ShapeDtypeStruct((M, N), F32),
        compiler_params=_cparams("parallel"),
    )(x, w)


def _mm_nt_sum(xs, ws, tn, tk, name):
    npair = len(xs)
    M, K = xs[0].shape
    N = ws[0].shape[0]
    nk = K // tk

    def body(*refs):
        x_refs = refs[:npair]
        w_refs = refs[npair:2 * npair]
        o_ref = refs[2 * npair]
        acc = refs[2 * npair + 1]
        k = pl.program_id(1)

        @pl.when(k == 0)
        def _():
            acc[...] = jnp.zeros_like(acc)

        for x_ref, w_ref in zip(x_refs, w_refs):
            acc[...] += lax.dot_general(x_ref[...], w_ref[...], NT_DIMS, preferred_element_type=F32)

        @pl.when(k == nk - 1)
        def _():
            o_ref[...] = acc[...]

    return pl.pallas_call(
        body, name=name, grid=(N // tn, nk),
        in_specs=[pl.BlockSpec((M, tk), lambda j, k: (0, k))] * npair
        + [pl.BlockSpec((tn, tk), lambda j, k: (j, k))] * npair,
        out_specs=pl.BlockSpec((M, tn), lambda j, k: (0, j)),
        out_shape=jax.ShapeDtypeStruct((M, N), F32),
        scratch_shapes=[pltpu.VMEM((M, tn), F32)],
        compiler_params=_cparams("parallel", "arbitrary"),
    )(*xs, *ws)


def _mm_tn(x, dy, alpha, ti, tn, name, stacked):
    M, Kin = x.shape
    N = dy.shape[1]

    def body(x_ref, dy_ref, ob_ref, xt):
        @pl.when(pl.program_id(1) == 0)
        def _():
            xt[...] = x_ref[...].astype(F32).T.astype(BF16)

        r = jnp.dot(xt[...], dy_ref[...], preferred_element_type=F32)
        if alpha != 1.0:
            r = alpha * r
        ob_ref[...] = r.astype(BF16)

    if stacked:
        ospec = pl.BlockSpec((None, ti, tn), lambda i, j: (j, i, 0))
        oshape = (N // tn, Kin, tn)
    else:
        ospec = pl.BlockSpec((ti, tn), lambda i, j: (i, j))
        oshape = (Kin, N)
    return pl.pallas_call(
        body, name=name, grid=(Kin // ti, N // tn),
        in_specs=[pl.BlockSpec((M, ti), lambda i, j: (0, i)), pl.BlockSpec((M, tn), lambda i, j: (0, j))],
        out_specs=ospec,
        out_shape=jax.ShapeDtypeStruct(oshape, BF16),
        scratch_shapes=[pltpu.VMEM((ti, M), BF16)],
        compiler_params=_cparams("parallel", "arbitrary"),
    )(x, dy)


def _rmsnorm_fwd(h, g, name):
    M, D = h.shape
    tr = _row_tile(M, D)

    def body(h_ref, g_ref, n_ref, r_ref):
        hv = h_ref[...]
        r = lax.rsqrt(jnp.mean(hv * hv, axis=-1, keepdims=True) + EPS)
        n_ref[...] = (hv * r * g_ref[...]).astype(BF16)
        r_ref[...] = r

    return pl.pallas_call(
        body, name=name, grid=(M // tr,),
        in_specs=[pl.BlockSpec((tr, D), lambda i: (i, 0)), pl.BlockSpec((1, D), lambda i: (0, 0))],
        out_specs=[pl.BlockSpec((tr, D), lambda i: (i, 0)), pl.BlockSpec((tr, 1), lambda i: (i, 0))],
        out_shape=[jax.ShapeDtypeStruct((M, D), BF16), jax.ShapeDtypeStruct((M, 1), F32)],
        compiler_params=_cparams("parallel"),
    )(h, g)


def _rmsnorm_bwd(dn, h, r, g, dh_prev, name):
    M, D = h.shape
    tr = _row_tile(M, D)

    def body(dn_ref, h_ref, r_ref, g_ref, dp_ref, dh_ref, dhb_ref, dg_ref):
        i = pl.program_id(0)
        dnv = dn_ref[...]
        hv = h_ref[...]
        rv = r_ref[...]
        w = dnv * g_ref[...]
        c = jnp.mean(w * hv, axis=-1, keepdims=True)
        dh = dp_ref[...] + rv * w - hv * (rv * rv * rv * c)
        dh_ref[...] = dh
        dhb_ref[...] = dh.astype(BF16)
        part = jnp.sum(dnv * (hv * rv), axis=0, keepdims=True)

        @pl.when(i == 0)
        def _():
            dg_ref[...] = part

        @pl.when(i > 0)
        def _():
            dg_ref[...] += part

    row = pl.BlockSpec((tr, D), lambda i: (i, 0))
    vec = pl.BlockSpec((1, D), lambda i: (0, 0))
    return pl.pallas_call(
        body, name=name, grid=(M // tr,),
        in_specs=[row, row, pl.BlockSpec((tr, 1), lambda i: (i, 0)), vec, row],
        out_specs=[row, row, vec],
        out_shape=[jax.ShapeDtypeStruct((M, D), F32), jax.ShapeDtypeStruct((M, D), BF16),
                   jax.ShapeDtypeStruct((1, D), F32)],
        compiler_params=_cparams("arbitrary"),
    )(dn, h, r, g, dh_prev)


def _loss_grad(h, tgt, seq, name):
    M, D = h.shape
    tr = _row_tile(M, D)

    def body(h_ref, t_ref, dh_ref, dhb_ref, loss_ref):
        i = pl.program_id(0)
        row = i * tr + lax.broadcasted_iota(jnp.int32, (tr, 1), 0)
        valid = (row >= N_META) & (row < N_META + seq)
        d = jnp.where(valid, h_ref[...] - t_ref[...], 0.0)
        dh = d * (1.0 / D)
        dh_ref[...] = dh
        dhb_ref[...] = dh.astype(BF16)
        part = (0.5 / D) * jnp.sum(jnp.sum(d * d, axis=1, keepdims=True), axis=0, keepdims=True)

        @pl.when(i == 0)
        def _():
            loss_ref[...] = part

        @pl.when(i > 0)
        def _():
            loss_ref[...] += part

    row = pl.BlockSpec((tr, D), lambda i: (i, 0))
    return pl.pallas_call(
        body, name=name, grid=(M // tr,),
        in_specs=[row, row],
        out_specs=[row, row, pl.BlockSpec((1, 1), lambda i: (0, 0))],
        out_shape=[jax.ShapeDtypeStruct((M, D), F32), jax.ShapeDtypeStruct((M, D), BF16),
                   jax.ShapeDtypeStruct((1, 1), F32)],
        compiler_params=_cparams("arbitrary"),
    )(h, tgt)


def _group_window(g):
    return jnp.where(g == 0, 2, jnp.where(g == 1, 4, jnp.where(g == 2, 8, 16)))


def _pool_fwd(z, pw, psc, name):
    M = z.shape[0]
    C = pw.shape[1]

    def body(p_ref, w_ref, sc_ref, pooled_ref, out_ref):
        g = pl.program_id(0)
        p = p_ref[...]
        t = lax.broadcasted_iota(jnp.int32, (M, 1), 0)
        s = p
        wsum = jnp.zeros_like(p)
        for step in range(N_POOL_GROUPS):
            sh = 1 << step
            s = s + jnp.where(t >= sh, pltpu.roll(s, sh, 0), 0.0)
            wsum = jnp.where(g == step, s, wsum)
        cnt = jnp.minimum(t + 1, _group_window(g)).astype(F32)
        pb = (wsum / cnt - p).astype(BF16)
        pooled_ref[...] = pb
        mixed = jnp.dot(pb, w_ref[...], preferred_element_type=F32)
        out_ref[...] = (mixed * sc_ref[...]).astype(BF16)

    col = pl.BlockSpec((M, C), lambda g: (0, g))
    return pl.pallas_call(
        body, name=name, grid=(N_POOL_GROUPS,),
        in_specs=[col, pl.BlockSpec((None, C, C), lambda g: (g, 0, 0)), pl.BlockSpec((1, C), lambda g: (0, g))],
        out_specs=[col, col],
        out_shape=[jax.ShapeDtypeStruct((M, N_POOL_GROUPS * C), BF16),
                   jax.ShapeDtypeStruct((M, N_POOL_GROUPS * C), BF16)],
        compiler_params=_cparams("parallel"),
    )(z, pw, psc)


def _pool_bwd(dmix, pooled, pw, psc, name):
    M = dmix.shape[0]
    C = pw.shape[1]

    def body(dm_ref, pooled_ref, w_ref, sc_ref, dp_ref, dwb_ref, dsc_ref):
        g = pl.program_id(0)
        dmx = dm_ref[...]
        pb = pooled_ref[...]
        wv = w_ref[...]
        mixed = jnp.dot(pb, wv, preferred_element_type=F32)
        dsc_ref[...] = jnp.sum(dmx * mixed, axis=0, keepdims=True)
        dmixed = (dmx * sc_ref[...]).astype(BF16)
        dw = jnp.dot(pb.astype(F32).T.astype(BF16), dmixed, preferred_element_type=F32)
        dwb_ref[...] = dw.astype(BF16)
        dpooled = lax.dot_general(dmixed, wv, NT_DIMS, preferred_element_type=F32)
        t = lax.broadcasted_iota(jnp.int32, (M, 1), 0)
        cnt = jnp.minimum(t + 1, _group_window(g)).astype(F32)
        s = dpooled / cnt
        wsum = jnp.zeros_like(s)
        for step in range(N_POOL_GROUPS):
            sh = 1 << step
            s = s + jnp.where(t < M - sh, pltpu.roll(s, M - sh, 0), 0.0)
            wsum = jnp.where(g == step, s, wsum)
        dp_ref[...] = (wsum - dpooled).astype(BF16)

    col = pl.BlockSpec((M, C), lambda g: (0, g))
    wspec = pl.BlockSpec((None, C, C), lambda g: (g, 0, 0))
    vec = pl.BlockSpec((1, C), lambda g: (0, g))
    return pl.pallas_call(
        body, name=name, grid=(N_POOL_GROUPS,),
        in_specs=[col, col, wspec, vec],
        out_specs=[col, wspec, vec],
        out_shape=[jax.ShapeDtypeStruct((M, N_POOL_GROUPS * C), BF16),
                   jax.ShapeDtypeStruct((N_POOL_GROUPS, C, C), BF16),
                   jax.ShapeDtypeStruct((1, N_POOL_GROUPS * C), F32)],
        compiler_params=_cparams("parallel"),
    )(dmix, pooled, pw, psc)


def _qkv_prep(z, gq, gk, n_heads, q_col, name):
    M = z.shape[0]
    H = n_heads
    qb = q_col // HEAD_DIM

    def body(q_ref, k_ref, v_ref, gq_ref, gk_ref, qh_ref, kh_ref, vb_ref):
        def norm(xv, g):
            r = lax.rsqrt(jnp.mean(xv * xv, axis=-1, keepdims=True) + EPS)
            return (xv * r * g).astype(BF16)

        qh_ref[...] = norm(q_ref[...], gq_ref[...])
        kh_ref[...] = norm(k_ref[...], gk_ref[...])
        vb_ref[...] = v_ref[...].astype(BF16)

    vec = pl.BlockSpec((1, HEAD_DIM), lambda h: (0, 0))
    out = pl.BlockSpec((M, HEAD_DIM), lambda h: (0, h))
    oshape = jax.ShapeDtypeStruct((M, H * HEAD_DIM), BF16)
    return pl.pallas_call(
        body, name=name, grid=(H,),
        in_specs=[pl.BlockSpec((M, HEAD_DIM), lambda h: (0, qb + h)),
                  pl.BlockSpec((M, HEAD_DIM), lambda h: (0, qb + H + h)),
                  pl.BlockSpec((M, HEAD_DIM), lambda h: (0, qb + 2 * H + h)), vec, vec],
        out_specs=[out, out, out],
        out_shape=[oshape, oshape, oshape],
        compiler_params=_cparams("parallel"),
    )(z, z, z, gq, gk)


def _forget_fwd(z, bpad, f_block, name):
    M = z.shape[0]

    def body(f_ref, b_ref, cum_ref):
        xx = f_ref[...] + b_ref[...]
        c = jnp.minimum(xx, 0.0) - jnp.log(1.0 + jnp.exp(-jnp.abs(xx)))
        t = lax.broadcasted_iota(jnp.int32, (M, 1), 0)
        sh = 1
        while sh < M:
            c = c + jnp.where(t >= sh, pltpu.roll(c, sh, 0), 0.0)
            sh *= 2
        cum_ref[...] = c.T

    return pl.pallas_call(
        body, name=name, grid=(1,),
        in_specs=[pl.BlockSpec((M, LANES), lambda i: (0, f_block)), pl.BlockSpec((1, LANES), lambda i: (0, 0))],
        out_specs=pl.BlockSpec((LANES, M), lambda i: (0, 0)),
        out_shape=jax.ShapeDtypeStruct((LANES, M), F32),
        compiler_params=_cparams("arbitrary"),
    )(z, bpad)


def _col_to_row(col):
    n = col.shape[0]
    return jnp.transpose(jnp.broadcast_to(col, (n, LANES)))[0:1, :]


def _attn_fwd(qh, kh, vb, cum_c, cum_r, name):
    M = qh.shape[0]
    H = qh.shape[1] // HEAD_DIM
    scale = 1.0 / math.sqrt(HEAD_DIM)

    def body(q_ref, k_ref, v_ref, cq_ref, ck_ref, o_ref, lc_ref, lr_ref):
        i = pl.program_id(1)
        s = lax.dot_general(q_ref[...], k_ref[...], NT_DIMS, preferred_element_type=F32) * scale
        s = s + (cq_ref[...] - ck_ref[...])
        row = i * TQ + lax.broadcasted_iota(jnp.int32, (TQ, 1), 0)
        col = lax.broadcasted_iota(jnp.int32, (1, M), 1)
        s = jnp.where(row >= col, s, NEG)
        m = jnp.max(s, axis=1, keepdims=True)
        p = jnp.exp(s - m)
        l = jnp.sum(p, axis=1, keepdims=True)
        pn = (p / l).astype(BF16)
        o_ref[...] = jnp.dot(pn, v_ref[...], preferred_element_type=F32).astype(BF16)
        lse = m + jnp.log(l)
        lc_ref[...] = lse
        lr_ref[...] = _col_to_row(lse)

    full = pl.BlockSpec((M, HEAD_DIM), lambda h, i: (0, h))
    tile = pl.BlockSpec((TQ, HEAD_DIM), lambda h, i: (i, h))
    colv = pl.BlockSpec((None, TQ, 1), lambda h, i: (h, i, 0))
    rowv_full = pl.BlockSpec((None, 1, M), lambda h, i: (h, 0, 0))
    rowv = pl.BlockSpec((None, 1, TQ), lambda h, i: (h, 0, i))
    return pl.pallas_call(
        body, name=name, grid=(H, M // TQ),
        in_specs=[tile, full, full, colv, rowv_full],
        out_specs=[tile, colv, rowv],
        out_shape=[jax.ShapeDtypeStruct((M, H * HEAD_DIM), BF16), jax.ShapeDtypeStruct((H, M, 1), F32),
                   jax.ShapeDtypeStruct((H, 1, M), F32)],
        compiler_params=_cparams("parallel", "parallel"),
    )(qh, kh, vb, cum_c, cum_r)


def _attn_bwd_q(qh, kh, vb, dob, cum_c, cum_r, lse_c, name):
    M = qh.shape[0]
    H = qh.shape[1] // HEAD_DIM
    scale = 1.0 / math.sqrt(HEAD_DIM)

    def body(q_ref, k_ref, v_ref, do_ref, cq_ref, ck_ref, l_ref, dq_ref, dr_ref, dcq_ref):
        i = pl.program_id(1)
        s = lax.dot_general(q_ref[...], k_ref[...], NT_DIMS, preferred_element_type=F32) * scale
        s = s + (cq_ref[...] - ck_ref[...])
        row = i * TQ + lax.broadcasted_iota(jnp.int32, (TQ, 1), 0)
        col = lax.broadcasted_iota(jnp.int32, (1, M), 1)
        p = jnp.exp(jnp.where(row >= col, s, NEG) - l_ref[...])
        dp = lax.dot_general(do_ref[...], v_ref[...], NT_DIMS, preferred_element_type=F32)
        delta = jnp.sum(p * dp, axis=1, keepdims=True)
        ds = p * (dp - delta)
        dq_ref[...] = jnp.dot((ds * scale).astype(BF16), k_ref[...], preferred_element_type=F32)
        dr_ref[...] = _col_to_row(delta)
        dcq_ref[...] = jnp.sum(ds, axis=1, keepdims=True)

    full = pl.BlockSpec((M, HEAD_DIM), lambda h, i: (0, h))
    tile = pl.BlockSpec((TQ, HEAD_DIM), lambda h, i: (i, h))
    colv = pl.BlockSpec((None, TQ, 1), lambda h, i: (h, i, 0))
    rowv_full = pl.BlockSpec((None, 1, M), lambda h, i: (h, 0, 0))
    rowv = pl.BlockSpec((None, 1, TQ), lambda h, i: (h, 0, i))
    return pl.pallas_call(
        body, name=name, grid=(H, M // TQ),
        in_specs=[tile, full, full, tile, colv, rowv_full, colv],
        out_specs=[tile, rowv, colv],
        out_shape=[jax.ShapeDtypeStruct((M, H * HEAD_DIM), F32), jax.ShapeDtypeStruct((H, 1, M), F32),
                   jax.ShapeDtypeStruct((H, M, 1), F32)],
        compiler_params=_cparams("parallel", "parallel"),
    )(qh, kh, vb, dob, cum_c, cum_r, lse_c)


def _attn_bwd_kv(qh, kh, vb, dob, cum_c, cum_r, lse_r, delta_r, name):
    M = qh.shape[0]
    H = qh.shape[1] // HEAD_DIM
    scale = 1.0 / math.sqrt(HEAD_DIM)

    def body(k_ref, v_ref, q_ref, do_ref, cq_ref, ck_ref, l_ref, d_ref, dk_ref, dv_ref, dck_ref):
        j = pl.program_id(1)
        st = lax.dot_general(k_ref[...], q_ref[...], NT_DIMS, preferred_element_type=F32) * scale
        st = st + (cq_ref[...] - ck_ref[...])
        krow = j * TQ + lax.broadcasted_iota(jnp.int32, (TQ, 1), 0)
        qcol = lax.broadcasted_iota(jnp.int32, (1, M), 1)
        pt = jnp.exp(jnp.where(qcol >= krow, st, NEG) - l_ref[...])
        dpt = lax.dot_general(v_ref[...], do_ref[...], NT_DIMS, preferred_element_type=F32)
        dst = pt * (dpt - d_ref[...])
        dv_ref[...] = jnp.dot(pt.astype(BF16), do_ref[...], preferred_element_type=F32).astype(BF16)
        dk_ref[...] = jnp.dot((dst * scale).astype(BF16), q_ref[...], preferred_element_type=F32)
        dck_ref[...] = -jnp.sum(dst, axis=1, keepdims=True)

    full = pl.BlockSpec((M, HEAD_DIM), lambda h, j: (0, h))
    tile = pl.BlockSpec((TQ, HEAD_DIM), lambda h, j: (j, h))
    colv = pl.BlockSpec((None, TQ, 1), lambda h, j: (h, j, 0))
    rowv_full = pl.BlockSpec((None, 1, M), lambda h, j: (h, 0, 0))
    return pl.pallas_call(
        body, name=name, grid=(H, M // TQ),
        in_specs=[tile, tile, full, full, rowv_full, colv, rowv_full, rowv_full],
        out_specs=[tile, tile, colv],
        out_shape=[jax.ShapeDtypeStruct((M, H * HEAD_DIM), F32), jax.ShapeDtypeStruct((M, H * HEAD_DIM), BF16),
                   jax.ShapeDtypeStruct((H, M, 1), F32)],
        compiler_params=_cparams("parallel", "parallel"),
    )(kh, vb, qh, dob, cum_r, cum_c, lse_r, delta_r)


def _qk_norm_bwd(dqh, dkh, z, gq, gk, n_heads, q_col, name):
    M = z.shape[0]
    H = n_heads
    qb = q_col // HEAD_DIM

    def body(dqh_ref, dkh_ref, q_ref, k_ref, gq_ref, gk_ref, dq_ref, dk_ref, dgq_ref, dgk_ref):
        h = pl.program_id(0)

        def one(dy, xv, g):
            r = lax.rsqrt(jnp.mean(xv * xv, axis=-1, keepdims=True) + EPS)
            w = dy * g
            c = jnp.mean(w * xv, axis=-1, keepdims=True)
            dx = r * w - xv * (r * r * r * c)
            return dx.astype(BF16), jnp.sum(dy * (xv * r), axis=0, keepdims=True)

        dq, dgq = one(dqh_ref[...], q_ref[...], gq_ref[...])
        dk, dgk = one(dkh_ref[...], k_ref[...], gk_ref[...])
        dq_ref[...] = dq
        dk_ref[...] = dk

        @pl.when(h == 0)
        def _():
            dgq_ref[...] = dgq
            dgk_ref[...] = dgk

        @pl.when(h > 0)
        def _():
            dgq_ref[...] += dgq
            dgk_ref[...] += dgk

    vec = pl.BlockSpec((1, HEAD_DIM), lambda h: (0, 0))
    head = pl.BlockSpec((M, HEAD_DIM), lambda h: (0, h))
    return pl.pallas_call(
        body, name=name, grid=(H,),
        in_specs=[head, head, pl.BlockSpec((M, HEAD_DIM), lambda h: (0, qb + h)),
                  pl.BlockSpec((M, HEAD_DIM), lambda h: (0, qb + H + h)), vec, vec],
        out_specs=[head, head, vec, vec],
        out_shape=[jax.ShapeDtypeStruct((M, H * HEAD_DIM), BF16), jax.ShapeDtypeStruct((M, H * HEAD_DIM), BF16),
                   jax.ShapeDtypeStruct((1, HEAD_DIM), F32), jax.ShapeDtypeStruct((1, HEAD_DIM), F32)],
        compiler_params=_cparams("arbitrary"),
    )(dqh, dkh, z, z, gq, gk)


def _forget_bwd(dcq, dck, z, bpad, f_block, name):
    H, M, _ = dcq.shape

    def body(dcq_ref, dck_ref, f_ref, b_ref, dfl_ref, db_ref):
        lane = lax.broadcasted_iota(jnp.int32, (1, LANES), 1)
        d = jnp.zeros((M, LANES), F32)
        for h in range(H):
            d = d + (dcq_ref[h] + dck_ref[h]) * (lane == h).astype(F32)
        t = lax.broadcasted_iota(jnp.int32, (M, 1), 0)
        sh = 1
        while sh < M:
            d = d + jnp.where(t < M - sh, pltpu.roll(d, M - sh, 0), 0.0)
            sh *= 2
        xx = f_ref[...] + b_ref[...]
        dfl = d * (1.0 / (1.0 + jnp.exp(xx)))
        dfl_ref[...] = dfl.astype(BF16)
        db_ref[...] = jnp.sum(dfl, axis=0, keepdims=True)

    colv = pl.BlockSpec((H, M, 1), lambda i: (0, 0, 0))
    return pl.pallas_call(
        body, name=name, grid=(1,),
        in_specs=[colv, colv, pl.BlockSpec((M, LANES), lambda i: (0, f_block)),
                  pl.BlockSpec((1, LANES), lambda i: (0, 0))],
        out_specs=[pl.BlockSpec((M, LANES), lambda i: (0, 0)), pl.BlockSpec((1, LANES), lambda i: (0, 0))],
        out_shape=[jax.ShapeDtypeStruct((M, LANES), BF16), jax.ShapeDtypeStruct((1, LANES), F32)],
        compiler_params=_cparams("arbitrary"),
    )(dcq, dck, z, bpad)


def _ffn_fwd(h, g, wg, wu, wd, tag):
    n, r = _rmsnorm_fwd(h, g, f"{tag}_norm")
    a, b, s = _ffn_up(n, wg, wu, 256, f"{tag}_up")
    h_out = _mm_nn_residual(s, wd, h, 0.5, 512, 512, f"{tag}_down")
    return h_out, (n, r, a, b, s)


def _ffn_bwd(dh, dhb, h, g, wg, wu, wd, saved, tag):
    n, r, a, b, s = saved
    n_shards = 4
    da, db = _ffn_bwd_hidden(dhb, wd, a, b, 256, f"{tag}_bwd_hidden")
    dwd = _mm_tn(s, dhb, 0.5, 512, 512, f"{tag}_dw_down", stacked=False)
    dwg = _mm_tn(n, da, 1.0, 512, wg.shape[1] // n_shards, f"{tag}_dw_gate", stacked=True)
    dwu = _mm_tn(n, db, 1.0, 512, wu.shape[1] // n_shards, f"{tag}_dw_up", stacked=True)
    dn = _mm_nt_sum([da, db], [wg, wu], 512, 512, f"{tag}_dn")
    dh_in, dhb_in, dg = _rmsnorm_bwd(dn, h, r, g, dh, f"{tag}_norm_bwd")
    return dh_in, dhb_in, dg, dwg, dwu, dwd


def _local_step(x, target, W):
    seq, D = x.shape
    L = N_META + seq
    Lp = -(-L // SEQ_ALIGN) * SEQ_ALIGN
    pad = jnp.zeros((Lp - L, D), F32)
    h0 = jnp.concatenate([W["meta"], x, pad], axis=0)
    tgt = jnp.concatenate([jnp.zeros((N_META, D), F32), target, pad], axis=0)

    d_pool = W["pool_scale"].shape[1]
    n_heads = W["b_forget"].shape[1]
    d_att = n_heads * HEAD_DIM
    f_col = d_pool + 3 * d_att
    f_block = f_col // LANES
    bpad = jnp.pad(W["b_forget"], ((0, 0), (0, LANES - n_heads)))

    h1, ffn1 = _ffn_fwd(h0, W["ffn1_norm"], W["wg1"], W["wu1"], W["wd1"], "ffn1")
    u, r_mix = _rmsnorm_fwd(h1, W["mix_norm"], "mix_norm")
    z = _mm_nn(u, W["win"], 384, "in_proj")
    pooled, pool_out = _pool_fwd(z, W["pool_w"], W["pool_scale"], "pool_fwd")
    qh, kh, vb = _qkv_prep(z, W["q_norm"], W["k_norm"], n_heads, d_pool, "qkv_prep")
    cum_t = _forget_fwd(z, bpad, f_block, "forget_fwd")[:n_heads]
    cum_c = cum_t.reshape(n_heads, Lp, 1)
    cum_r = cum_t.reshape(n_heads, 1, Lp)
    att, lse_c, lse_r = _attn_fwd(qh, kh, vb, cum_c, cum_r, "attn_fwd")
    mix = jnp.concatenate([pool_out, att], axis=1)
    h2 = _mm_nn_residual(mix, W["wout"], h1, 1.0, 512, 512, "out_proj")
    h3, ffn2 = _ffn_fwd(h2, W["ffn2_norm"], W["wg2"], W["wu2"], W["wd2"], "ffn2")

    dh3, dh3b, loss = _loss_grad(h3, tgt, seq, "loss")
    dh2, dh2b, dg_ffn2, dwg2, dwu2, dwd2 = _ffn_bwd(
        dh3, dh3b, h2, W["ffn2_norm"], W["wg2"], W["wu2"], W["wd2"], ffn2, "ffn2")

    dmix = _mm_nt(dh2b, W["wout"], 512, "out_proj_bwd")
    dwout = _mm_tn(mix, dh2b, 1.0, 512, 512, "dw_out", stacked=False)
    dp, dpw, dpsc = _pool_bwd(dmix, pooled, W["pool_w"], W["pool_scale"], "pool_bwd")
    dob = dmix[:, d_pool:].astype(BF16)
    dqh, delta_r, dcq = _attn_bwd_q(qh, kh, vb, dob, cum_c, cum_r, lse_c, "attn_bwd_q")
    dkh, dv, dck = _attn_bwd_kv(qh, kh, vb, dob, cum_c, cum_r, lse_r, delta_r, "attn_bwd_kv")
    dq, dk, dgq, dgk = _qk_norm_bwd(dqh, dkh, z, W["q_norm"], W["k_norm"], n_heads, d_pool, "qk_norm_bwd")
    dfl, dbf = _forget_bwd(dcq, dck, z, bpad, f_block, "forget_bwd")
    dz = jnp.concatenate([dp, dq, dk, dv, dfl], axis=1)
    dwin = _mm_tn(u, dz, 1.0, 512, 384, "dw_in", stacked=False)
    du = _mm_nt_sum([dz], [W["win"]], 512, 384, "in_proj_bwd")
    dh1, dh1b, dg_mix = _rmsnorm_bwd(du, h1, r_mix, W["mix_norm"], dh2, "mix_norm_bwd")

    dh0, _, dg_ffn1, dwg1, dwu1, dwd1 = _ffn_bwd(
        dh1, dh1b, h0, W["ffn1_norm"], W["wg1"], W["wu1"], W["wd1"], ffn1, "ffn1")

    grads = dict(
        x=dh0[N_META:L], meta=dh0[:N_META],
        ffn1_norm=dg_ffn1, mix_norm=dg_mix, ffn2_norm=dg_ffn2, q_norm=dgq, k_norm=dgk,
        b_forget=dbf[:, :n_heads], pool_scale=dpsc,
        wg1=dwg1, wu1=dwu1, wd1=dwd1, win=dwin, wout=dwout, pool_w=dpw,
        wg2=dwg2, wu2=dwu2, wd2=dwd2,
    )
    return loss[0, 0], grads


HBM_SPEC = pl.BlockSpec(memory_space=pltpu.HBM)
N_CHIPS = 4


def _chip_peers():
    x, y, c = lax.axis_index("x"), lax.axis_index("y"), lax.axis_index("c")
    flips = [(1 - x, y), (x, 1 - y), (1 - x, 1 - y)]
    return 2 * x + y, [((px, py, c), 2 * px + py) for px, py in flips]


def _gathered_shape(shape, layout):
    if layout == "rows":
        return (N_CHIPS * shape[0],) + shape[1:]
    if layout == "cols":
        return (shape[0], N_CHIPS * shape[1])
    return (N_CHIPS,) + shape


def _cast_place(place, w, layout, dtype, name):
    R, C = w.shape
    tr = _row_tile(R, C)
    nt = R // tr

    def body(place_ref, w_ref, o_ref):
        o_ref[...] = w_ref[...].astype(dtype)

    if layout == "rows":
        ospec = pl.BlockSpec((tr, C), lambda i, p: (p[1] * nt + i, 0))
    elif layout == "cols":
        ospec = pl.BlockSpec((tr, C), lambda i, p: (i, p[1]))
    else:
        ospec = pl.BlockSpec((None, tr, C), lambda i, p: (p[1], i, 0))
    return pl.pallas_call(
        body, name=name,
        grid_spec=pltpu.PrefetchScalarGridSpec(
            num_scalar_prefetch=1, grid=(nt,),
            in_specs=[pl.BlockSpec((tr, C), lambda i, p: (i, 0))], out_specs=ospec),
        out_shape=jax.ShapeDtypeStruct(_gathered_shape((R, C), layout), dtype),
        compiler_params=_cparams("parallel"),
    )(place, w)


def _all_gather_chips(bufs, shard_shapes, layouts):
    n = len(bufs)
    halves = [s[0] // 2 for s in shard_shapes]

    def body(*refs):
        out_refs = refs[n:2 * n]
        ici_send, ici_recv, d2d_send, d2d_recv = refs[2 * n:]
        c = lax.axis_index("c")
        sib = (lax.axis_index("x"), lax.axis_index("y"), 1 - c)
        me, peers = _chip_peers()

        def region(a, chip, half):
            rows_a, h = shard_shapes[a][0], halves[a]
            if layouts[a] == "rows":
                return out_refs[a].at[pl.ds(chip * rows_a + half * h, h)]
            if layouts[a] == "cols":
                cols_a = shard_shapes[a][1]
                return out_refs[a].at[pl.ds(half * h, h), pl.ds(chip * cols_a, cols_a)]
            return out_refs[a].at[chip, pl.ds(half * h, h)]

        sends = []
        for a in range(n):
            for k, (dev, _) in enumerate(peers):
                mine = region(a, me, c)
                cp = pltpu.make_async_remote_copy(
                    src_ref=mine, dst_ref=mine,
                    send_sem=ici_send.at[a, k], recv_sem=ici_recv.at[a, k], device_id=dev, device_id_type=MESH)
                cp.start()
                sends.append(cp)
        for a in range(n):
            for k, (dev, pidx) in enumerate(peers):
                landed = region(a, pidx, c)
                pltpu.make_async_remote_copy(
                    src_ref=landed, dst_ref=landed, send_sem=ici_send.at[a, k], recv_sem=ici_recv.at[a, k],
                    device_id=dev, device_id_type=MESH).wait_recv()
                fwd = pltpu.make_async_remote_copy(
                    src_ref=landed, dst_ref=landed, send_sem=d2d_send.at[a, k], recv_sem=d2d_recv.at[a, k],
                    device_id=sib, device_id_type=MESH)
                fwd.start()
                sends.append(fwd)
        for a in range(n):
            for k, (_, pidx) in enumerate(peers):
                other = region(a, pidx, 1 - c)
                pltpu.make_async_remote_copy(
                    src_ref=other, dst_ref=other, send_sem=d2d_send.at[a, k], recv_sem=d2d_recv.at[a, k],
                    device_id=sib, device_id_type=MESH).wait_recv()
        for cp in sends:
            cp.wait_send()

    sem = pltpu.SemaphoreType.DMA((n, 3))
    return pl.pallas_call(
        body, name="gather_weights",
        in_specs=[HBM_SPEC] * n, out_specs=[HBM_SPEC] * n,
        out_shape=[jax.ShapeDtypeStruct(b.shape, b.dtype) for b in bufs],
        input_output_aliases={a: a for a in range(n)},
        scratch_shapes=[sem, sem, sem, sem],
    )(*bufs)


def _send_sibling_halves(stacked):
    n = len(stacked)

    def body(*refs):
        in_refs = refs[:n]
        out_refs = refs[n:2 * n]
        send_sems, recv_sems = refs[2 * n:]
        c = lax.axis_index("c")
        sib = (lax.axis_index("x"), lax.axis_index("y"), 1 - c)
        sends = []
        for a in range(n):
            h = stacked[a].shape[1] // 2
            cp = pltpu.make_async_remote_copy(
                src_ref=in_refs[a].at[:, pl.ds((1 - c) * h, h)], dst_ref=out_refs[a], send_sem=send_sems.at[a],
                recv_sem=recv_sems.at[a], device_id=sib, device_id_type=MESH)
            cp.start()
            sends.append(cp)
        for cp in sends:
            cp.wait_recv()
        for cp in sends:
            cp.wait_send()

    return pl.pallas_call(
        body, name="send_sibling_halves",
        in_specs=[HBM_SPEC] * n, out_specs=[HBM_SPEC] * n,
        out_shape=[jax.ShapeDtypeStruct((s.shape[0], s.shape[1] // 2, s.shape[2]), s.dtype) for s in stacked],
        scratch_shapes=[pltpu.SemaphoreType.DMA((n,)), pltpu.SemaphoreType.DMA((n,))],
    )(*stacked)


def _scatter_slabs(stacked):
    n = len(stacked)

    def body(*refs):
        in_refs = refs[:n]
        out_refs = refs[n:2 * n]
        send_sems, recv_sems = refs[2 * n:]
        _, peers = _chip_peers()
        sends = []
        for a in range(n):
            for k, (dev, pidx) in enumerate(peers):
                cp = pltpu.make_async_remote_copy(
                    src_ref=in_refs[a].at[pidx], dst_ref=out_refs[a].at[k], send_sem=send_sems.at[a, k],
                    recv_sem=recv_sems.at[a, k], device_id=dev, device_id_type=MESH)
                cp.start()
                sends.append(cp)
        for cp in sends:
            cp.wait_recv()
        for cp in sends:
            cp.wait_send()

    return pl.pallas_call(
        body, name="scatter_grads",
        in_specs=[HBM_SPEC] * n, out_specs=[HBM_SPEC] * n,
        out_shape=[jax.ShapeDtypeStruct((3,) + s.shape[1:], s.dtype) for s in stacked],
        scratch_shapes=[pltpu.SemaphoreType.DMA((n, 3)), pltpu.SemaphoreType.DMA((n, 3))],
    )(*stacked)


def _swap_sibling(arrays):
    n = len(arrays)

    def body(*refs):
        in_refs = refs[:n]
        out_refs = refs[n:2 * n]
        send_sems, recv_sems = refs[2 * n:]
        sib = (lax.axis_index("x"), lax.axis_index("y"), 1 - lax.axis_index("c"))
        sends = []
        for a in range(n):
            cp = pltpu.make_async_remote_copy(
                src_ref=in_refs[a], dst_ref=out_refs[a], send_sem=send_sems.at[a], recv_sem=recv_sems.at[a],
                device_id=sib, device_id_type=MESH)
            cp.start()
            sends.append(cp)
        for cp in sends:
            cp.wait_recv()
        for cp in sends:
            cp.wait_send()

    return pl.pallas_call(
        body, name="swap_sibling",
        in_specs=[HBM_SPEC] * n, out_specs=[HBM_SPEC] * n,
        out_shape=[jax.ShapeDtypeStruct(s.shape, s.dtype) for s in arrays],
        scratch_shapes=[pltpu.SemaphoreType.DMA((n,)), pltpu.SemaphoreType.DMA((n,))],
    )(*arrays)


def _all_reduce_small(v):
    R, C = v.shape
    n_dev = 8

    def body(v_ref, o_ref, buf, send_sems, recv_sems):
        x, y, c = lax.axis_index("x"), lax.axis_index("y"), lax.axis_index("c")
        me = 4 * x + 2 * y + c
        buf[me] = v_ref[...]
        sends = []
        for k in range(1, n_dev):
            px, py, pc = x ^ ((k >> 2) & 1), y ^ ((k >> 1) & 1), c ^ (k & 1)
            cp = pltpu.make_async_remote_copy(
                src_ref=v_ref, dst_ref=buf.at[me], send_sem=send_sems.at[k - 1], recv_sem=recv_sems.at[k - 1],
                device_id=(px, py, pc), device_id_type=MESH)
            cp.start()
            sends.append((cp, 4 * px + 2 * py + pc))
        for k in range(1, n_dev):
            cp, pidx = sends[k - 1]
            pltpu.make_async_remote_copy(
                src_ref=v_ref, dst_ref=buf.at[pidx], send_sem=send_sems.at[k - 1], recv_sem=recv_sems.at[k - 1],
                device_id=(x, y, c), device_id_type=MESH).wait_recv()
        for cp, _ in sends:
            cp.wait_send()
        acc = buf[0]
        for d in range(1, n_dev):
            acc = acc + buf[d]
        o_ref[...] = acc

    vm = pl.BlockSpec(memory_space=pltpu.VMEM)
    return pl.pallas_call(
        body, name="all_reduce_small",
        in_specs=[vm], out_specs=vm,
        out_shape=jax.ShapeDtypeStruct((R, C), F32),
        scratch_shapes=[pltpu.VMEM((n_dev, R, C), F32), pltpu.SemaphoreType.DMA((n_dev - 1,)),
                        pltpu.SemaphoreType.DMA((n_dev - 1,))],
    )(v)


def _pair_sum(place, own, sib, name):
    S, R, C = own.shape
    h = R // 2
    tr = _row_tile(h, C)
    nt = h // tr

    def body(place_ref, o_ref, s_ref, out_ref):
        out_ref[...] = (o_ref[...].astype(F32) + s_ref[...].astype(F32)).astype(BF16)

    return pl.pallas_call(
        body, name=name,
        grid_spec=pltpu.PrefetchScalarGridSpec(
            num_scalar_prefetch=1, grid=(S, nt),
            in_specs=[pl.BlockSpec((None, tr, C), lambda s, i, p: (s, p[0] * nt + i, 0)),
                      pl.BlockSpec((None, tr, C), lambda s, i, p: (s, i, 0))],
            out_specs=pl.BlockSpec((None, tr, C), lambda s, i, p: (s, i, 0))),
        out_shape=jax.ShapeDtypeStruct((S, h, C), BF16),
        compiler_params=_cparams("parallel", "parallel"),
    )(place, own, sib)


def _sum_slabs(place, own, sib, recv, name):
    S, R, C = own.shape
    h = R // 2
    tr = _row_tile(h, C)
    nt = h // tr

    def body(place_ref, o_ref, s_ref, r_ref, out_ref):
        acc = o_ref[...].astype(F32) + s_ref[...].astype(F32)
        for k in range(3):
            acc = acc + r_ref[k].astype(F32)
        out_ref[...] = acc

    return pl.pallas_call(
        body, name=name,
        grid_spec=pltpu.PrefetchScalarGridSpec(
            num_scalar_prefetch=1, grid=(nt,),
            in_specs=[pl.BlockSpec((None, tr, C), lambda i, p: (p[1], p[0] * nt + i, 0)),
                      pl.BlockSpec((None, tr, C), lambda i, p: (p[1], i, 0)),
                      pl.BlockSpec((3, tr, C), lambda i, p: (0, i, 0))],
            out_specs=pl.BlockSpec((tr, C), lambda i, p: (i, 0))),
        out_shape=jax.ShapeDtypeStruct((h, C), F32),
        compiler_params=_cparams("parallel"),
    )(place, own, sib, recv)


def _adamw(parts, w, m, v, name):
    R, C = w.shape
    tr = _row_tile(R, C)
    npart = len(parts)
    c1 = 1.0 - ADAM_B1 ** ADAM_STEP
    c2 = 1.0 - ADAM_B2 ** ADAM_STEP

    def body(*refs):
        p_refs = refs[:npart]
        w_ref, m_ref, v_ref, g_ref, d_ref, nm_ref, nv_ref = refs[npart:]
        g = p_refs[0][...]
        for p_ref in p_refs[1:]:
            g = g + p_ref[...]
        nm = ADAM_B1 * m_ref[...] + (1.0 - ADAM_B1) * g
        nv = ADAM_B2 * v_ref[...] + (1.0 - ADAM_B2) * (g * g)
        m_hat = nm / c1
        v_hat = nv / c2
        g_ref[...] = g
        d_ref[...] = -ADAM_LR * (m_hat / (jnp.sqrt(v_hat) + ADAM_EPS) + ADAM_WD * w_ref[...])
        nm_ref[...] = nm
        nv_ref[...] = nv

    blk = pl.BlockSpec((tr, C), lambda i: (i, 0))
    shape = jax.ShapeDtypeStruct((R, C), F32)
    return pl.pallas_call(
        body, name=name, grid=(R // tr,),
        in_specs=[blk] * (npart + 3), out_specs=[blk] * 4, out_shape=[shape] * 4,
        compiler_params=_cparams("parallel"),
    )(*parts, w, m, v)


def _adamw_halves(place, mine, other, w, m, v, name):
    R, C = w.shape
    h = R // 2
    tr = _row_tile(h, C)
    nt = h // tr
    c1 = 1.0 - ADAM_B1 ** ADAM_STEP
    c2 = 1.0 - ADAM_B2 ** ADAM_STEP

    def body(place_ref, mine_ref, other_ref, w_ref, m_ref, v_ref, g_ref, d_ref, nm_ref, nv_ref):
        is_mine = (pl.program_id(0) // nt) == place_ref[0]
        g = jnp.where(is_mine, mine_ref[...], other_ref[...])
        nm = ADAM_B1 * m_ref[...] + (1.0 - ADAM_B1) * g
        nv = ADAM_B2 * v_ref[...] + (1.0 - ADAM_B2) * (g * g)
        m_hat = nm / c1
        v_hat = nv / c2
        g_ref[...] = g
        d_ref[...] = -ADAM_LR * (m_hat / (jnp.sqrt(v_hat) + ADAM_EPS) + ADAM_WD * w_ref[...])
        nm_ref[...] = nm
        nv_ref[...] = nv

    def half_block(which):
        def index(i, p):
            first = p[0] if which == 0 else 1 - p[0]
            return jnp.clip(i - first * nt, 0, nt - 1), 0

        return pl.BlockSpec((tr, C), index)

    blk = pl.BlockSpec((tr, C), lambda i, p: (i, 0))
    shape = jax.ShapeDtypeStruct((R, C), F32)
    return pl.pallas_call(
        body, name=name,
        grid_spec=pltpu.PrefetchScalarGridSpec(
            num_scalar_prefetch=1, grid=(2 * nt,),
            in_specs=[half_block(0), half_block(1), blk, blk, blk], out_specs=[blk] * 4),
        out_shape=[shape] * 4,
        compiler_params=_cparams("parallel"),
    )(place, mine, other, w, m, v)


SMALL_NAMES = ("ffn1_norm", "mix_norm", "ffn2_norm", "pool_scale", "q_norm", "k_norm", "b_forget")
SMALL_COLS = 1024


def _pack_small(vals):
    rows = [vals[n].reshape(-1, SMALL_COLS) for n in ("ffn1_norm", "mix_norm", "ffn2_norm", "pool_scale")]
    tail = jnp.concatenate([vals["q_norm"].reshape(-1), vals["k_norm"].reshape(-1), vals["b_forget"].reshape(-1)])
    rows.append(jnp.pad(tail, (0, SMALL_COLS - tail.shape[0])).reshape(1, SMALL_COLS))
    return jnp.concatenate(rows, axis=0)


def _unpack_small(packed, like):
    out = {}
    r = 0
    for n in ("ffn1_norm", "mix_norm", "ffn2_norm", "pool_scale"):
        k = like[n].size // SMALL_COLS
        out[n] = packed[r:r + k].reshape(like[n].shape)
        r += k
    o = 0
    for n in ("q_norm", "k_norm", "b_forget"):
        k = like[n].size
        out[n] = packed[r, o:o + k].reshape(like[n].shape)
        o += k
    return out


def kernel(x, meta_tokens, ffn1_norm, ffn1_w_gate, ffn1_w_up, ffn1_w_down, mix_norm, w_in, b_forget, q_norm, k_norm, pool_w, pool_scale, w_out, ffn2_norm, ffn2_w_gate, ffn2_w_up, ffn2_w_down, loss_target, m_meta_tokens, m_ffn1_norm, m_ffn1_w_gate, m_ffn1_w_up, m_ffn1_w_down, m_mix_norm, m_w_in, m_b_forget, m_q_norm, m_k_norm, m_pool_w, m_pool_scale, m_w_out, m_ffn2_norm, m_ffn2_w_gate, m_ffn2_w_up, m_ffn2_w_down, v_meta_tokens, v_ffn1_norm, v_ffn1_w_gate, v_ffn1_w_up, v_ffn1_w_down, v_mix_norm, v_w_in, v_b_forget, v_q_norm, v_k_norm, v_pool_w, v_pool_scale, v_w_out, v_ffn2_norm, v_ffn2_w_gate, v_ffn2_w_up, v_ffn2_w_down):
    wts = dict(meta_tokens=meta_tokens, ffn1_norm=ffn1_norm, ffn1_w_gate=ffn1_w_gate, ffn1_w_up=ffn1_w_up,
               ffn1_w_down=ffn1_w_down, mix_norm=mix_norm, w_in=w_in, b_forget=b_forget, q_norm=q_norm,
               k_norm=k_norm, pool_w=pool_w, pool_scale=pool_scale, w_out=w_out, ffn2_norm=ffn2_norm,
               ffn2_w_gate=ffn2_w_gate, ffn2_w_up=ffn2_w_up, ffn2_w_down=ffn2_w_down)
    mom = dict(meta_tokens=m_meta_tokens, ffn1_norm=m_ffn1_norm, ffn1_w_gate=m_ffn1_w_gate, ffn1_w_up=m_ffn1_w_up,
               ffn1_w_down=m_ffn1_w_down, mix_norm=m_mix_norm, w_in=m_w_in, b_forget=m_b_forget, q_norm=m_q_norm,
               k_norm=m_k_norm, pool_w=m_pool_w, pool_scale=m_pool_scale, w_out=m_w_out, ffn2_norm=m_ffn2_norm,
               ffn2_w_gate=m_ffn2_w_gate, ffn2_w_up=m_ffn2_w_up, ffn2_w_down=m_ffn2_w_down)
    var = dict(meta_tokens=v_meta_tokens, ffn1_norm=v_ffn1_norm, ffn1_w_gate=v_ffn1_w_gate, ffn1_w_up=v_ffn1_w_up,
               ffn1_w_down=v_ffn1_w_down, mix_norm=v_mix_norm, w_in=v_w_in, b_forget=v_b_forget, q_norm=v_q_norm,
               k_norm=v_k_norm, pool_w=v_pool_w, pool_scale=v_pool_scale, w_out=v_w_out, ffn2_norm=v_ffn2_norm,
               ffn2_w_gate=v_ffn2_w_gate, ffn2_w_up=v_ffn2_w_up, ffn2_w_down=v_ffn2_w_down)
    order = list(wts)
    me = 2 * lax.axis_index("x") + lax.axis_index("y")

    D = x.shape[2]
    d_in_shard = w_in.shape[2]
    d_in = N_CHIPS * d_in_shard
    n_heads = b_forget.shape[1]
    d_in_pad = (d_in - n_heads) + LANES

    big = ("ffn1_w_gate", "ffn1_w_up", "ffn1_w_down", "w_in", "w_out", "pool_w", "ffn2_w_gate", "ffn2_w_up",
           "ffn2_w_down")
    place = jnp.stack([lax.axis_index("c"), me]).astype(jnp.int32)
    layouts = dict(ffn1_w_gate="cols", ffn1_w_up="cols", ffn1_w_down="rows", w_in="stack", w_out="rows",
                   pool_w="stack", ffn2_w_gate="cols", ffn2_w_up="cols", ffn2_w_down="rows", meta_tokens="stack")
    names = big + ("meta_tokens",)
    shards2d = {n: wts[n].reshape(-1, wts[n].shape[-1]) for n in names}
    placed = [_cast_place(place, shards2d[n], layouts[n], F32 if n == "meta_tokens" else BF16, f"place_{n}")
              for n in names]
    gathered = _all_gather_chips(placed, [shards2d[n].shape for n in names], [layouts[n] for n in names])
    G = dict(zip(names, gathered))

    def cols(st):
        return jnp.transpose(st, (1, 0, 2)).reshape(st.shape[1], -1)

    win = cols(G["w_in"])
    W = dict(
        meta=cols(G["meta_tokens"]),
        wg1=G["ffn1_w_gate"], wu1=G["ffn1_w_up"], wd1=G["ffn1_w_down"],
        win=jnp.pad(win, ((0, 0), (0, d_in_pad - d_in))), wout=G["w_out"],
        pool_w=jnp.transpose(G["pool_w"].reshape((N_CHIPS,) + pool_w.shape[1:]), (1, 0, 2, 3)).reshape(
            N_POOL_GROUPS, pool_w.shape[3], pool_w.shape[3]),
        wg2=G["ffn2_w_gate"], wu2=G["ffn2_w_up"], wd2=G["ffn2_w_down"],
        ffn1_norm=ffn1_norm, mix_norm=mix_norm, ffn2_norm=ffn2_norm, q_norm=q_norm, k_norm=k_norm,
        b_forget=b_forget, pool_scale=pool_scale,
    )

    loss_part, gr = _local_step(x[0], loss_target[0], W)
    loss = lax.psum(loss_part, ("x", "y", "c"))

    def split_rows(a):
        return a.reshape(N_CHIPS, -1, a.shape[1])

    def split_win(a):
        return jnp.transpose(a[:, :d_in].reshape(D, N_CHIPS, d_in_shard), (1, 0, 2))

    def split_pool(a):
        r, c = pool_w.shape[2], pool_w.shape[3]
        return jnp.transpose(a.reshape(N_POOL_GROUPS, N_CHIPS, r, c), (1, 0, 2, 3)).reshape(N_CHIPS, -1, c)

    slabs = dict(
        ffn1_w_gate=gr["wg1"], ffn1_w_up=gr["wu1"], ffn1_w_down=split_rows(gr["wd1"]),
        w_in=split_win(gr["win"]), w_out=split_rows(gr["wout"]), pool_w=split_pool(gr["pool_w"]),
        ffn2_w_gate=gr["wg2"], ffn2_w_up=gr["wu2"], ffn2_w_down=split_rows(gr["wd2"]),
    )
    own = [slabs[n] for n in big]
    from_sib = _send_sibling_halves(own)
    pair = [_pair_sum(place, o, s, f"pair_sum_{n}") for n, o, s in zip(big, own, from_sib)]
    received = _scatter_slabs(pair)
    halves = [_sum_slabs(place, o, s, r, f"sum_{n}") for n, o, s, r in zip(big, own, from_sib, received)]
    other_halves = _swap_sibling(halves)

    out_g, out_d, out_m, out_v = {}, {}, {}, {}
    for n, mine, other in zip(big, halves, other_halves):
        shape = wts[n].shape
        res = _adamw_halves(place, mine, other, shards2d[n], mom[n].reshape(shards2d[n].shape),
                            var[n].reshape(shards2d[n].shape), f"adamw_{n}")
        out_g[n], out_d[n], out_m[n], out_v[n] = (a.reshape(shape) for a in res)

    small_g = _pack_small({n: gr[n] for n in SMALL_NAMES})
    meta_rows = gr["meta"].reshape(-1, SMALL_COLS)
    total = _all_reduce_small(jnp.concatenate([small_g, meta_rows], axis=0))
    n_small = small_g.shape[0]
    res = _adamw([total[:n_small]], _pack_small({n: wts[n] for n in SMALL_NAMES}),
                 _pack_small({n: mom[n] for n in SMALL_NAMES}), _pack_small({n: var[n] for n in SMALL_NAMES}),
                 "adamw_small")
    for dst, packed in zip((out_g, out_d, out_m, out_v), res):
        dst.update(_unpack_small(packed, wts))
    meta_cols = meta_tokens.shape[1]
    meta_g = lax.dynamic_slice_in_dim(total[n_small:].reshape(N_META, D), me * meta_cols, meta_cols, axis=1)
    res = _adamw([meta_g], meta_tokens, m_meta_tokens, v_meta_tokens, "adamw_meta")
    out_g["meta_tokens"], out_d["meta_tokens"], out_m["meta_tokens"], out_v["meta_tokens"] = res

    grad_x = gr["x"].reshape(x.shape)
    return (loss, grad_x, *[out_g[n] for n in order], *[out_d[n] for n in order], *[out_m[n] for n in order],
            *[out_v[n] for n in order])
```

```python
import functools
import math

import jax
import jax.numpy as jnp
from jax import lax
from jax.experimental import pallas as pl
from jax.experimental.pallas import tpu as pltpu

F32 = jnp.float32
BF16 = jnp.bfloat16

N_META = 16
EPS = 1e-6
HEAD_DIM = 128
N_POOL_GROUPS = 4
LANES = 128
SEQ_ALIGN = 128
TQ = 128
VMEM_LIMIT = 56 * 1024 * 1024
ELEMWISE_BLOCK_BYTES = 1 << 20

ADAM_LR = 0.001
ADAM_B1 = 0.9
ADAM_B2 = 0.999
ADAM_EPS = 1e-08
ADAM_WD = 0.01
ADAM_STEP = 10

NT_DIMS = (((1,), (1,)), ((), ()))
NEG = -1e30
MESH = pl.DeviceIdType.MESH


def _cparams(*sem):
    return pltpu.CompilerParams(dimension_semantics=sem, vmem_limit_bytes=VMEM_LIMIT)


def _sigmoid(a):
    return 1.0 / (1.0 + jnp.exp(-a))


def _row_tile(rows, cols, itemsize=4):
    best = None
    for t in range(16, rows + 1, 16):
        if rows % t == 0 and t * cols * itemsize <= ELEMWISE_BLOCK_BYTES:
            best = t
    return best if best is not None else rows


def _mm_nn(x, w, tn, name):
    M, K = x.shape
    N = w.shape[1]

    def body(x_ref, w_ref, o_ref):
        o_ref[...] = jnp.dot(x_ref[...], w_ref[...], preferred_element_type=F32)

    return pl.pallas_call(
        body, name=name, grid=(N // tn,),
        in_specs=[pl.BlockSpec((M, K), lambda j: (0, 0)), pl.BlockSpec((K, tn), lambda j: (0, j))],
        out_specs=pl.BlockSpec((M, tn), lambda j: (0, j)),
        out_shape=jax.ShapeDtypeStruct((M, N), F32),
        compiler_params=_cparams("parallel"),
    )(x, w)


def _ffn_up(n, wg, wu, tn, name):
    M, K = n.shape
    N = wg.shape[1]

    def body(n_ref, wg_ref, wu_ref, a_ref, b_ref, s_ref):
        nv = n_ref[...]
        a = jnp.dot(nv, wg_ref[...], preferred_element_type=F32)
        b = jnp.dot(nv, wu_ref[...], preferred_element_type=F32)
        a_ref[...] = a
        b_ref[...] = b
        s_ref[...] = (a * _sigmoid(a) * b).astype(BF16)

    wspec = pl.BlockSpec((K, tn), lambda j: (0, j))
    ospec = pl.BlockSpec((M, tn), lambda j: (0, j))
    return pl.pallas_call(
        body, name=name, grid=(N // tn,),
        in_specs=[pl.BlockSpec((M, K), lambda j: (0, 0)), wspec, wspec],
        out_specs=[ospec, ospec, ospec],
        out_shape=[jax.ShapeDtypeStruct((M, N), F32), jax.ShapeDtypeStruct((M, N), F32),
                   jax.ShapeDtypeStruct((M, N), BF16)],
        compiler_params=_cparams("parallel"),
    )(n, wg, wu)


def _mm_nn_residual(x, w, res, alpha, tn, tk, name):
    M, K = x.shape
    N = w.shape[1]
    nk = K // tk

    def body(x_ref, w_ref, r_ref, o_ref, acc):
        k = pl.program_id(1)

        @pl.when(k == 0)
        def _():
            acc[...] = jnp.zeros_like(acc)

        acc[...] += jnp.dot(x_ref[...], w_ref[...], preferred_element_type=F32)

        @pl.when(k == nk - 1)
        def _():
            o_ref[...] = r_ref[...] + alpha * acc[...]

    return pl.pallas_call(
        body, name=name, grid=(N // tn, nk),
        in_specs=[pl.BlockSpec((M, tk), lambda j, k: (0, k)), pl.BlockSpec((tk, tn), lambda j, k: (k, j)),
                  pl.BlockSpec((M, tn), lambda j, k: (0, j))],
        out_specs=pl.BlockSpec((M, tn), lambda j, k: (0, j)),
        out_shape=jax.ShapeDtypeStruct((M, N), F32),
        scratch_shapes=[pltpu.VMEM((M, tn), F32)],
        compiler_params=_cparams("parallel", "arbitrary"),
    )(x, w, res)


def _ffn_bwd_hidden(dhb, wd, a, b, dep, tn, name):
    M, K = dhb.shape
    N = wd.shape[0]

    def body(dh_ref, w_ref, a_ref, b_ref, dep_ref, da_ref, db_ref):
        ds = 0.5 * lax.dot_general(dh_ref[...], w_ref[...], NT_DIMS, preferred_element_type=F32)
        av = a_ref[...]
        sig = _sigmoid(av)
        da_ref[...] = (ds * b_ref[...] * (sig * (1.0 + av * (1.0 - sig)))).astype(BF16)
        db_ref[...] = (ds * (av * sig)).astype(BF16)

    ospec = pl.BlockSpec((M, tn), lambda j: (0, j))
    return pl.pallas_call(
        body, name=name, grid=(N // tn,),
        in_specs=[pl.BlockSpec((M, K), lambda j: (0, 0)), pl.BlockSpec((tn, K), lambda j: (j, 0)), ospec, ospec,
                  pl.BlockSpec(memory_space=pl.ANY)],
        out_specs=[ospec, ospec],
        out_shape=[jax.ShapeDtypeStruct((M, N), BF16), jax.ShapeDtypeStruct((M, N), BF16)],
        compiler_params=_cparams("parallel"),
    )(dhb, wd, a, b, dep)


def _mm_nt(x, w, dep, tn, name):
    M, K = x.shape
    N = w.shape[0]

    def body(x_ref, w_ref, dep_ref, o_ref):
        o_ref[...] = lax.dot_general(x_ref[...], w_ref[...], NT_DIMS, preferred_element_type=F32)

    return pl.pallas_call(
        body, name=name, grid=(N // tn,),
        in_specs=[pl.BlockSpec((M, K), lambda j: (0, 0)), pl.BlockSpec((tn, K), lambda j: (j, 0)),
                  pl.BlockSpec(memory_space=pl.ANY)],
        out_specs=pl.BlockSpec((M, tn), lambda j: (0, j)),
        out_shape=jax.ShapeDtypeStruct((M, N), F32),
        compiler_params=_cparams("parallel"),
    )(x, w, dep)


def _mm_nt_sum(xs, ws, dep, tn, tk, name):
    npair = len(xs)
    M, K = xs[0].shape
    N = ws[0].shape[0]
    nk = K // tk

    def body(*refs):
        x_refs = refs[:npair]
        w_refs = refs[npair:2 * npair]
        o_ref = refs[2 * npair + 1]
        acc = refs[2 * npair + 2]
        k = pl.program_id(1)

        @pl.when(k == 0)
        def _():
            acc[...] = jnp.zeros_like(acc)

        for x_ref, w_ref in zip(x_refs, w_refs):
            acc[...] += lax.dot_general(x_ref[...], w_ref[...], NT_DIMS, preferred_element_type=F32)

        @pl.when(k == nk - 1)
        def _():
            o_ref[...] = acc[...]

    return pl.pallas_call(
        body, name=name, grid=(N // tn, nk),
        in_specs=[pl.BlockSpec((M, tk), lambda j, k: (0, k))] * npair
        + [pl.BlockSpec((tn, tk), lambda j, k: (j, k))] * npair + [pl.BlockSpec(memory_space=pl.ANY)],
        out_specs=pl.BlockSpec((M, tn), lambda j, k: (0, j)),
        out_shape=jax.ShapeDtypeStruct((M, N), F32),
        scratch_shapes=[pltpu.VMEM((M, tn), F32)],
        compiler_params=_cparams("parallel", "arbitrary"),
    )(*xs, *ws, dep)


def _mm_tn(x, dy, dep, alpha, ti, tn, name, stacked):
    M, Kin = x.shape
    N = dy.shape[1]

    def body(x_ref, dy_ref, dep_ref, ob_ref, xt):
        @pl.when(pl.program_id(1) == 0)
        def _():
            xt[...] = x_ref[...].astype(F32).T.astype(BF16)

        r = jnp.dot(xt[...], dy_ref[...], preferred_element_type=F32)
        if alpha != 1.0:
            r = alpha * r
        ob_ref[...] = r.astype(BF16)

    if stacked:
        ospec = pl.BlockSpec((None, ti, tn), lambda i, j: (j, i, 0))
        oshape = (N // tn, Kin, tn)
    else:
        ospec = pl.BlockSpec((ti, tn), lambda i, j: (i, j))
        oshape = (Kin, N)
    return pl.pallas_call(
        body, name=name, grid=(Kin // ti, N // tn),
        in_specs=[pl.BlockSpec((M, ti), lambda i, j: (0, i)), pl.BlockSpec((M, tn), lambda i, j: (0, j)),
                  pl.BlockSpec(memory_space=pl.ANY)],
        out_specs=ospec,
        out_shape=jax.ShapeDtypeStruct(oshape, BF16),
        scratch_shapes=[pltpu.VMEM((ti, M), BF16)],
        compiler_params=_cparams("parallel", "arbitrary"),
    )(x, dy, dep)


def _rmsnorm_fwd(h, g, name):
    M, D = h.shape
    tr = _row_tile(M, D)

    def body(h_ref, g_ref, n_ref, r_ref):
        hv = h_ref[...]
        r = lax.rsqrt(jnp.mean(hv * hv, axis=-1, keepdims=True) + EPS)
        n_ref[...] = (hv * r * g_ref[...]).astype(BF16)
        r_ref[...] = r

    return pl.pallas_call(
        body, name=name, grid=(M // tr,),
        in_specs=[pl.BlockSpec((tr, D), lambda i: (i, 0)), pl.BlockSpec((1, D), lambda i: (0, 0))],
        out_specs=[pl.BlockSpec((tr, D), lambda i: (i, 0)), pl.BlockSpec((tr, 1), lambda i: (i, 0))],
        out_shape=[jax.ShapeDtypeStruct((M, D), BF16), jax.ShapeDtypeStruct((M, 1), F32)],
        compiler_params=_cparams("parallel"),
    )(h, g)


def _rmsnorm_bwd(dn, h, r, g, dh_prev, name):
    M, D = h.shape
    tr = _row_tile(M, D)

    def body(dn_ref, h_ref, r_ref, g_ref, dp_ref, dh_ref, dhb_ref, dg_ref):
        i = pl.program_id(0)
        dnv = dn_ref[...]
        hv = h_ref[...]
        rv = r_ref[...]
        w = dnv * g_ref[...]
        c = jnp.mean(w * hv, axis=-1, keepdims=True)
        dh = dp_ref[...] + rv * w - hv * (rv * rv * rv * c)
        dh_ref[...] = dh
        dhb_ref[...] = dh.astype(BF16)
        part = jnp.sum(dnv * (hv * rv), axis=0, keepdims=True)

        @pl.when(i == 0)
        def _():
            dg_ref[...] = part

        @pl.when(i > 0)
        def _():
            dg_ref[...] += part

    row = pl.BlockSpec((tr, D), lambda i: (i, 0))
    vec = pl.BlockSpec((1, D), lambda i: (0, 0))
    return pl.pallas_call(
        body, name=name, grid=(M // tr,),
        in_specs=[row, row, pl.BlockSpec((tr, 1), lambda i: (i, 0)), vec, row],
        out_specs=[row, row, vec],
        out_shape=[jax.ShapeDtypeStruct((M, D), F32), jax.ShapeDtypeStruct((M, D), BF16),
                   jax.ShapeDtypeStruct((1, D), F32)],
        compiler_params=_cparams("arbitrary"),
    )(dn, h, r, g, dh_prev)


def _loss_grad(h, tgt, seq, name):
    M, D = h.shape
    tr = _row_tile(M, D)

    def body(h_ref, t_ref, dh_ref, dhb_ref, loss_ref):
        i = pl.program_id(0)
        row = i * tr + lax.broadcasted_iota(jnp.int32, (tr, 1), 0)
        valid = (row >= N_META) & (row < N_META + seq)
        d = jnp.where(valid, h_ref[...] - t_ref[...], 0.0)
        dh = d * (1.0 / D)
        dh_ref[...] = dh
        dhb_ref[...] = dh.astype(BF16)
        part = (0.5 / D) * jnp.sum(jnp.sum(d * d, axis=1, keepdims=True), axis=0, keepdims=True)

        @pl.when(i == 0)
        def _():
            loss_ref[...] = part

        @pl.when(i > 0)
        def _():
            loss_ref[...] += part

    row = pl.BlockSpec((tr, D), lambda i: (i, 0))
    return pl.pallas_call(
        body, name=name, grid=(M // tr,),
        in_specs=[row, row],
        out_specs=[row, row, pl.BlockSpec((1, 1), lambda i: (0, 0))],
        out_shape=[jax.ShapeDtypeStruct((M, D), F32), jax.ShapeDtypeStruct((M, D), BF16),
                   jax.ShapeDtypeStruct((1, 1), F32)],
        compiler_params=_cparams("arbitrary"),
    )(h, tgt)


def _group_window(g):
    return jnp.where(g == 0, 2, jnp.where(g == 1, 4, jnp.where(g == 2, 8, 16)))


def _pool_fwd(z, pw, psc, name):
    M = z.shape[0]
    C = pw.shape[1]

    def body(p_ref, w_ref, sc_ref, pooled_ref, out_ref):
        g = pl.program_id(0)
        p = p_ref[...]
        t = lax.broadcasted_iota(jnp.int32, (M, 1), 0)
        s = p
        wsum = jnp.zeros_like(p)
        for step in range(N_POOL_GROUPS):
            sh = 1 << step
            s = s + jnp.where(t >= sh, pltpu.roll(s, sh, 0), 0.0)
            wsum = jnp.where(g == step, s, wsum)
        cnt = jnp.minimum(t + 1, _group_window(g)).astype(F32)
        pb = (wsum / cnt - p).astype(BF16)
        pooled_ref[...] = pb
        mixed = jnp.dot(pb, w_ref[...], preferred_element_type=F32)
        out_ref[...] = (mixed * sc_ref[...]).astype(BF16)

    col = pl.BlockSpec((M, C), lambda g: (0, g))
    return pl.pallas_call(
        body, name=name, grid=(N_POOL_GROUPS,),
        in_specs=[col, pl.BlockSpec((None, C, C), lambda g: (g, 0, 0)), pl.BlockSpec((1, C), lambda g: (0, g))],
        out_specs=[col, col],
        out_shape=[jax.ShapeDtypeStruct((M, N_POOL_GROUPS * C), BF16),
                   jax.ShapeDtypeStruct((M, N_POOL_GROUPS * C), BF16)],
        compiler_params=_cparams("parallel"),
    )(z, pw, psc)


def _pool_bwd(dmix, pooled, pw, psc, name):
    M = dmix.shape[0]
    C = pw.shape[1]

    def body(dm_ref, pooled_ref, w_ref, sc_ref, dp_ref, dwb_ref, dsc_ref):
        g = pl.program_id(0)
        dmx = dm_ref[...]
        pb = pooled_ref[...]
        wv = w_ref[...]
        mixed = jnp.dot(pb, wv, preferred_element_type=F32)
        dsc_ref[...] = jnp.sum(dmx * mixed, axis=0, keepdims=True)
        dmixed = (dmx * sc_ref[...]).astype(BF16)
        dw = jnp.dot(pb.astype(F32).T.astype(BF16), dmixed, preferred_element_type=F32)
        dwb_ref[...] = dw.astype(BF16)
        dpooled = lax.dot_general(dmixed, wv, NT_DIMS, preferred_element_type=F32)
        t = lax.broadcasted_iota(jnp.int32, (M, 1), 0)
        cnt = jnp.minimum(t + 1, _group_window(g)).astype(F32)
        s = dpooled / cnt
        wsum = jnp.zeros_like(s)
        for step in range(N_POOL_GROUPS):
            sh = 1 << step
            s = s + jnp.where(t < M - sh, pltpu.roll(s, M - sh, 0), 0.0)
            wsum = jnp.where(g == step, s, wsum)
        dp_ref[...] = (wsum - dpooled).astype(BF16)

    col = pl.BlockSpec((M, C), lambda g: (0, g))
    wspec = pl.BlockSpec((None, C, C), lambda g: (g, 0, 0))
    vec = pl.BlockSpec((1, C), lambda g: (0, g))
    return pl.pallas_call(
        body, name=name, grid=(N_POOL_GROUPS,),
        in_specs=[col, col, wspec, vec],
        out_specs=[col, wspec, vec],
        out_shape=[jax.ShapeDtypeStruct((M, N_POOL_GROUPS * C), BF16),
                   jax.ShapeDtypeStruct((N_POOL_GROUPS, C, C), BF16),
                   jax.ShapeDtypeStruct((1, N_POOL_GROUPS * C), F32)],
        compiler_params=_cparams("parallel"),
    )(dmix, pooled, pw, psc)


def _qkv_prep(z, gq, gk, n_heads, q_col, name):
    M = z.shape[0]
    H = n_heads
    qb = q_col // HEAD_DIM

    def body(q_ref, k_ref, v_ref, gq_ref, gk_ref, qh_ref, kh_ref, vb_ref):
        def norm(xv, g):
            r = lax.rsqrt(jnp.mean(xv * xv, axis=-1, keepdims=True) + EPS)
            return (xv * r * g).astype(BF16)

        qh_ref[...] = norm(q_ref[...], gq_ref[...])
        kh_ref[...] = norm(k_ref[...], gk_ref[...])
        vb_ref[...] = v_ref[...].astype(BF16)

    vec = pl.BlockSpec((1, HEAD_DIM), lambda h: (0, 0))
    out = pl.BlockSpec((M, HEAD_DIM), lambda h: (0, h))
    oshape = jax.ShapeDtypeStruct((M, H * HEAD_DIM), BF16)
    return pl.pallas_call(
        body, name=name, grid=(H,),
        in_specs=[pl.BlockSpec((M, HEAD_DIM), lambda h: (0, qb + h)),
                  pl.BlockSpec((M, HEAD_DIM), lambda h: (0, qb + H + h)),
                  pl.BlockSpec((M, HEAD_DIM), lambda h: (0, qb + 2 * H + h)), vec, vec],
        out_specs=[out, out, out],
        out_shape=[oshape, oshape, oshape],
        compiler_params=_cparams("parallel"),
    )(z, z, z, gq, gk)


def _forget_fwd(z, bpad, f_block, name):
    M = z.shape[0]

    def body(f_ref, b_ref, cum_ref):
        xx = f_ref[...] + b_ref[...]
        c = jnp.minimum(xx, 0.0) - jnp.log(1.0 + jnp.exp(-jnp.abs(xx)))
        t = lax.broadcasted_iota(jnp.int32, (M, 1), 0)
        sh = 1
        while sh < M:
            c = c + jnp.where(t >= sh, pltpu.roll(c, sh, 0), 0.0)
            sh *= 2
        cum_ref[...] = c.T

    return pl.pallas_call(
        body, name=name, grid=(1,),
        in_specs=[pl.BlockSpec((M, LANES), lambda i: (0, f_block)), pl.BlockSpec((1, LANES), lambda i: (0, 0))],
        out_specs=pl.BlockSpec((LANES, M), lambda i: (0, 0)),
        out_shape=jax.ShapeDtypeStruct((LANES, M), F32),
        compiler_params=_cparams("arbitrary"),
    )(z, bpad)


def _col_to_row(col):
    n = col.shape[0]
    return jnp.transpose(jnp.broadcast_to(col, (n, LANES)))[0:1, :]


def _attn_fwd(qh, kh, vb, cum_c, cum_r, name):
    M = qh.shape[0]
    H = qh.shape[1] // HEAD_DIM
    scale = 1.0 / math.sqrt(HEAD_DIM)

    def body(q_ref, k_ref, v_ref, cq_ref, ck_ref, o_ref, lc_ref, lr_ref):
        i = pl.program_id(1)
        s = lax.dot_general(q_ref[...], k_ref[...], NT_DIMS, preferred_element_type=F32) * scale
        s = s + (cq_ref[...] - ck_ref[...])
        row = i * TQ + lax.broadcasted_iota(jnp.int32, (TQ, 1), 0)
        col = lax.broadcasted_iota(jnp.int32, (1, M), 1)
        s = jnp.where(row >= col, s, NEG)
        m = jnp.max(s, axis=1, keepdims=True)
        p = jnp.exp(s - m)
        l = jnp.sum(p, axis=1, keepdims=True)
        pn = (p / l).astype(BF16)
        o_ref[...] = jnp.dot(pn, v_ref[...], preferred_element_type=F32).astype(BF16)
        lse = m + jnp.log(l)
        lc_ref[...] = lse
        lr_ref[...] = _col_to_row(lse)

    full = pl.BlockSpec((M, HEAD_DIM), lambda h, i: (0, h))
    tile = pl.BlockSpec((TQ, HEAD_DIM), lambda h, i: (i, h))
    colv = pl.BlockSpec((None, TQ, 1), lambda h, i: (h, i, 0))
    rowv_full = pl.BlockSpec((None, 1, M), lambda h, i: (h, 0, 0))
    rowv = pl.BlockSpec((None, 1, TQ), lambda h, i: (h, 0, i))
    return pl.pallas_call(
        body, name=name, grid=(H, M // TQ),
        in_specs=[tile, full, full, colv, rowv_full],
        out_specs=[tile, colv, rowv],
        out_shape=[jax.ShapeDtypeStruct((M, H * HEAD_DIM), BF16), jax.ShapeDtypeStruct((H, M, 1), F32),
                   jax.ShapeDtypeStruct((H, 1, M), F32)],
        compiler_params=_cparams("parallel", "parallel"),
    )(qh, kh, vb, cum_c, cum_r)


def _attn_bwd_q(qh, kh, vb, dob, cum_c, cum_r, lse_c, name):
    M = qh.shape[0]
    H = qh.shape[1] // HEAD_DIM
    scale = 1.0 / math.sqrt(HEAD_DIM)

    def body(q_ref, k_ref, v_ref, do_ref, cq_ref, ck_ref, l_ref, dq_ref, dr_ref, dcq_ref):
        i = pl.program_id(1)
        s = lax.dot_general(q_ref[...], k_ref[...], NT_DIMS, preferred_element_type=F32) * scale
        s = s + (cq_ref[...] - ck_ref[...])
        row = i * TQ + lax.broadcasted_iota(jnp.int32, (TQ, 1), 0)
        col = lax.broadcasted_iota(jnp.int32, (1, M), 1)
        p = jnp.exp(jnp.where(row >= col, s, NEG) - l_ref[...])
        dp = lax.dot_general(do_ref[...], v_ref[...], NT_DIMS, preferred_element_type=F32)
        delta = jnp.sum(p * dp, axis=1, keepdims=True)
        ds = p * (dp - delta)
        dq_ref[...] = jnp.dot((ds * scale).astype(BF16), k_ref[...], preferred_element_type=F32)
        dr_ref[...] = _col_to_row(delta)
        dcq_ref[...] = jnp.sum(ds, axis=1, keepdims=True)

    full = pl.BlockSpec((M, HEAD_DIM), lambda h, i: (0, h))
    tile = pl.BlockSpec((TQ, HEAD_DIM), lambda h, i: (i, h))
    colv = pl.BlockSpec((None, TQ, 1), lambda h, i: (h, i, 0))
    rowv_full = pl.BlockSpec((None, 1, M), lambda h, i: (h, 0, 0))
    rowv = pl.BlockSpec((None, 1, TQ), lambda h, i: (h, 0, i))
    return pl.pallas_call(
        body, name=name, grid=(H, M // TQ),
        in_specs=[tile, full, full, tile, colv, rowv_full, colv],
        out_specs=[tile, rowv, colv],
        out_shape=[jax.ShapeDtypeStruct((M, H * HEAD_DIM), F32), jax.ShapeDtypeStruct((H, 1, M), F32),
                   jax.ShapeDtypeStruct((H, M, 1), F32)],
        compiler_params=_cparams("parallel", "parallel"),
    )(qh, kh, vb, dob, cum_c, cum_r, lse_c)


def _attn_bwd_kv(qh, kh, vb, dob, cum_c, cum_r, lse_r, delta_r, name):
    M = qh.shape[0]
    H = qh.shape[1] // HEAD_DIM
    scale = 1.0 / math.sqrt(HEAD_DIM)

    def body(k_ref, v_ref, q_ref, do_ref, cq_ref, ck_ref, l_ref, d_ref, dk_ref, dv_ref, dck_ref):
        j = pl.program_id(1)
        st = lax.dot_general(k_ref[...], q_ref[...], NT_DIMS, preferred_element_type=F32) * scale
        st = st + (cq_ref[...] - ck_ref[...])
        krow = j * TQ + lax.broadcasted_iota(jnp.int32, (TQ, 1), 0)
        qcol = lax.broadcasted_iota(jnp.int32, (1, M), 1)
        pt = jnp.exp(jnp.where(qcol >= krow, st, NEG) - l_ref[...])
        dpt = lax.dot_general(v_ref[...], do_ref[...], NT_DIMS, preferred_element_type=F32)
        dst = pt * (dpt - d_ref[...])
        dv_ref[...] = jnp.dot(pt.astype(BF16), do_ref[...], preferred_element_type=F32).astype(BF16)
        dk_ref[...] = jnp.dot((dst * scale).astype(BF16), q_ref[...], preferred_element_type=F32)
        dck_ref[...] = -jnp.sum(dst, axis=1, keepdims=True)

    full = pl.BlockSpec((M, HEAD_DIM), lambda h, j: (0, h))
    tile = pl.BlockSpec((TQ, HEAD_DIM), lambda h, j: (j, h))
    colv = pl.BlockSpec((None, TQ, 1), lambda h, j: (h, j, 0))
    rowv_full = pl.BlockSpec((None, 1, M), lambda h, j: (h, 0, 0))
    return pl.pallas_call(
        body, name=name, grid=(H, M // TQ),
        in_specs=[tile, tile, full, full, rowv_full, colv, rowv_full, rowv_full],
        out_specs=[tile, tile, colv],
        out_shape=[jax.ShapeDtypeStruct((M, H * HEAD_DIM), F32), jax.ShapeDtypeStruct((M, H * HEAD_DIM), BF16),
                   jax.ShapeDtypeStruct((H, M, 1), F32)],
        compiler_params=_cparams("parallel", "parallel"),
    )(kh, vb, qh, dob, cum_r, cum_c, lse_r, delta_r)


def _qk_norm_bwd(dqh, dkh, z, gq, gk, n_heads, q_col, name):
    M = z.shape[0]
    H = n_heads
    qb = q_col // HEAD_DIM

    def body(dqh_ref, dkh_ref, q_ref, k_ref, gq_ref, gk_ref, dq_ref, dk_ref, dgq_ref, dgk_ref):
        h = pl.program_id(0)

        def one(dy, xv, g):
            r = lax.rsqrt(jnp.mean(xv * xv, axis=-1, keepdims=True) + EPS)
            w = dy * g
            c = jnp.mean(w * xv, axis=-1, keepdims=True)
            dx = r * w - xv * (r * r * r * c)
            return dx.astype(BF16), jnp.sum(dy * (xv * r), axis=0, keepdims=True)

        dq, dgq = one(dqh_ref[...], q_ref[...], gq_ref[...])
        dk, dgk = one(dkh_ref[...], k_ref[...], gk_ref[...])
        dq_ref[...] = dq
        dk_ref[...] = dk

        @pl.when(h == 0)
        def _():
            dgq_ref[...] = dgq
            dgk_ref[...] = dgk

        @pl.when(h > 0)
        def _():
            dgq_ref[...] += dgq
            dgk_ref[...] += dgk

    vec = pl.BlockSpec((1, HEAD_DIM), lambda h: (0, 0))
    head = pl.BlockSpec((M, HEAD_DIM), lambda h: (0, h))
    return pl.pallas_call(
        body, name=name, grid=(H,),
        in_specs=[head, head, pl.BlockSpec((M, HEAD_DIM), lambda h: (0, qb + h)),
                  pl.BlockSpec((M, HEAD_DIM), lambda h: (0, qb + H + h)), vec, vec],
        out_specs=[head, head, vec, vec],
        out_shape=[jax.ShapeDtypeStruct((M, H * HEAD_DIM), BF16), jax.ShapeDtypeStruct((M, H * HEAD_DIM), BF16),
                   jax.ShapeDtypeStruct((1, HEAD_DIM), F32), jax.ShapeDtypeStruct((1, HEAD_DIM), F32)],
        compiler_params=_cparams("arbitrary"),
    )(dqh, dkh, z, z, gq, gk)


def _forget_bwd(dcq, dck, z, bpad, f_block, name):
    H, M, _ = dcq.shape

    def body(dcq_ref, dck_ref, f_ref, b_ref, dfl_ref, db_ref):
        lane = lax.broadcasted_iota(jnp.int32, (1, LANES), 1)
        d = jnp.zeros((M, LANES), F32)
        for h in range(H):
            d = d + (dcq_ref[h] + dck_ref[h]) * (lane == h).astype(F32)
        t = lax.broadcasted_iota(jnp.int32, (M, 1), 0)
        sh = 1
        while sh < M:
            d = d + jnp.where(t < M - sh, pltpu.roll(d, M - sh, 0), 0.0)
            sh *= 2
        xx = f_ref[...] + b_ref[...]
        dfl = d * (1.0 / (1.0 + jnp.exp(xx)))
        dfl_ref[...] = dfl.astype(BF16)
        db_ref[...] = jnp.sum(dfl, axis=0, keepdims=True)

    colv = pl.BlockSpec((H, M, 1), lambda i: (0, 0, 0))
    return pl.pallas_call(
        body, name=name, grid=(1,),
        in_specs=[colv, colv, pl.BlockSpec((M, LANES), lambda i: (0, f_block)),
                  pl.BlockSpec((1, LANES), lambda i: (0, 0))],
        out_specs=[pl.BlockSpec((M, LANES), lambda i: (0, 0)), pl.BlockSpec((1, LANES), lambda i: (0, 0))],
        out_shape=[jax.ShapeDtypeStruct((M, LANES), BF16), jax.ShapeDtypeStruct((1, LANES), F32)],
        compiler_params=_cparams("arbitrary"),
    )(dcq, dck, z, bpad)


def _ffn_fwd(h, g, up, get_weights, tag):
    n, r = _rmsnorm_fwd(h, g, f"{tag}_norm")
    a, b, s = _ffn_up(n, up["wg"], up["wu"], 256, f"{tag}_up")
    wd = get_weights(f"{tag}_down", s)["wd"]
    h_out = _mm_nn_residual(s, wd, h, 0.5, 512, 512, f"{tag}_down")
    return h_out, (n, r, a, b, s, up["wg"], up["wu"], wd)


def _ffn_bwd(dh, dhb, h, g, saved, dep, put_grads, tag):
    n, r, a, b, s, wg, wu, wd = saved
    n_shards = 4
    da, db = _ffn_bwd_hidden(dhb, wd, a, b, dep, 256, f"{tag}_bwd_hidden")
    dwd = _mm_tn(s, dhb, dep, 0.5, 512, 512, f"{tag}_dw_down", stacked=False)
    dep = put_grads(f"{tag}_w_down", dwd)
    dwg = _mm_tn(n, da, dep, 1.0, 512, wg.shape[1] // n_shards, f"{tag}_dw_gate", stacked=True)
    dep = put_grads(f"{tag}_w_gate", dwg)
    dwu = _mm_tn(n, db, dep, 1.0, 512, wu.shape[1] // n_shards, f"{tag}_dw_up", stacked=True)
    dep = put_grads(f"{tag}_w_up", dwu)
    dn = _mm_nt_sum([da, db], [wg, wu], dep, 512, 512, f"{tag}_dn")
    dh_in, dhb_in, dg = _rmsnorm_bwd(dn, h, r, g, dh, f"{tag}_norm_bwd")
    return dh_in, dhb_in, dg


def _local_step(x, target, S, get_weights, put_grads):
    seq, D = x.shape
    L = N_META + seq
    Lp = -(-L // SEQ_ALIGN) * SEQ_ALIGN
    pad = jnp.zeros((Lp - L, D), F32)
    tgt = jnp.concatenate([jnp.zeros((N_META, D), F32), target, pad], axis=0)

    d_pool = S["pool_scale"].shape[1]
    n_heads = S["b_forget"].shape[1]
    d_att = n_heads * HEAD_DIM
    f_col = d_pool + 3 * d_att
    f_block = f_col // LANES
    bpad = jnp.pad(S["b_forget"], ((0, 0), (0, LANES - n_heads)))

    up1 = get_weights("ffn1_up", None)
    h0 = jnp.concatenate([up1["meta"], x, pad], axis=0)
    h1, ffn1 = _ffn_fwd(h0, S["ffn1_norm"], up1, get_weights, "ffn1")
    Wm = get_weights("mix", h1)
    u, r_mix = _rmsnorm_fwd(h1, S["mix_norm"], "mix_norm")
    z = _mm_nn(u, Wm["win"], 384, "in_proj")
    pooled, pool_out = _pool_fwd(z, Wm["pool_w"], S["pool_scale"], "pool_fwd")
    qh, kh, vb = _qkv_prep(z, S["q_norm"], S["k_norm"], n_heads, d_pool, "qkv_prep")
    cum_t = _forget_fwd(z, bpad, f_block, "forget_fwd")[:n_heads]
    cum_c = cum_t.reshape(n_heads, Lp, 1)
    cum_r = cum_t.reshape(n_heads, 1, Lp)
    att, lse_c, lse_r = _attn_fwd(qh, kh, vb, cum_c, cum_r, "attn_fwd")
    mix = jnp.concatenate([pool_out, att], axis=1)
    h2 = _mm_nn_residual(mix, Wm["wout"], h1, 1.0, 512, 512, "out_proj")
    h3, ffn2 = _ffn_fwd(h2, S["ffn2_norm"], get_weights("ffn2_up", h2), get_weights, "ffn2")

    dh3, dh3b, loss = _loss_grad(h3, tgt, seq, "loss")
    dh2, dh2b, dg_ffn2 = _ffn_bwd(dh3, dh3b, h2, S["ffn2_norm"], ffn2, loss, put_grads, "ffn2")

    dmix = _mm_nt(dh2b, Wm["wout"], loss, 512, "out_proj_bwd")
    dwout = _mm_tn(mix, dh2b, loss, 1.0, 512, 512, "dw_out", stacked=False)
    dp, dpw, dpsc = _pool_bwd(dmix, pooled, Wm["pool_w"], S["pool_scale"], "pool_bwd")
    dob = dmix[:, d_pool:].astype(BF16)
    dqh, delta_r, dcq = _attn_bwd_q(qh, kh, vb, dob, cum_c, cum_r, lse_c, "attn_bwd_q")
    dkh, dv, dck = _attn_bwd_kv(qh, kh, vb, dob, cum_c, cum_r, lse_r, delta_r, "attn_bwd_kv")
    dq, dk, dgq, dgk = _qk_norm_bwd(dqh, dkh, z, S["q_norm"], S["k_norm"], n_heads, d_pool, "qk_norm_bwd")
    dfl, dbf = _forget_bwd(dcq, dck, z, bpad, f_block, "forget_bwd")
    dz = jnp.concatenate([dp, dq, dk, dv, dfl], axis=1)
    dwin = _mm_tn(u, dz, loss, 1.0, 512, 384, "dw_in", stacked=False)
    dep = put_grads("mix", dict(win=dwin, wout=dwout, pool_w=dpw))
    du = _mm_nt_sum([dz], [Wm["win"]], dep, 512, 384, "in_proj_bwd")
    dh1, dh1b, dg_mix = _rmsnorm_bwd(du, h1, r_mix, S["mix_norm"], dh2, "mix_norm_bwd")

    dh0, _, dg_ffn1 = _ffn_bwd(dh1, dh1b, h0, S["ffn1_norm"], ffn1, loss, put_grads, "ffn1")

    grads = dict(
        x=dh0[N_META:L], meta=dh0[:N_META],
        ffn1_norm=dg_ffn1, mix_norm=dg_mix, ffn2_norm=dg_ffn2, q_norm=dgq, k_norm=dgk,
        b_forget=dbf[:, :n_heads], pool_scale=dpsc,
    )
    return loss[0, 0], dh0, grads


HBM_SPEC = pl.BlockSpec(memory_space=pltpu.HBM)
N_CHIPS = 4


def _chip_peers():
    x, y, c = lax.axis_index("x"), lax.axis_index("y"), lax.axis_index("c")
    flips = [(1 - x, y), (x, 1 - y), (1 - x, 1 - y)]
    return 2 * x + y, [((px, py, c), 2 * px + py) for px, py in flips]


def _gathered_shape(shape, layout):
    if layout == "rows":
        return (N_CHIPS * shape[0],) + shape[1:]
    if layout == "cols":
        return (shape[0], N_CHIPS * shape[1])
    return (N_CHIPS,) + shape


def _cast_place(place, w, dep, layout, dtype, name):
    R, C = w.shape
    tr = _row_tile(R, C)
    nt = R // tr

    def body(place_ref, w_ref, dep_ref, o_ref):
        o_ref[...] = w_ref[...].astype(dtype)

    if layout == "rows":
        ospec = pl.BlockSpec((tr, C), lambda i, p: (p[1] * nt + i, 0))
    elif layout == "cols":
        ospec = pl.BlockSpec((tr, C), lambda i, p: (i, p[1]))
    else:
        ospec = pl.BlockSpec((None, tr, C), lambda i, p: (p[1], i, 0))
    return pl.pallas_call(
        body, name=name,
        grid_spec=pltpu.PrefetchScalarGridSpec(
            num_scalar_prefetch=1, grid=(nt,),
            in_specs=[pl.BlockSpec((tr, C), lambda i, p: (i, 0)), pl.BlockSpec(memory_space=pl.ANY)],
            out_specs=ospec),
        out_shape=jax.ShapeDtypeStruct(_gathered_shape((R, C), layout), dtype),
        compiler_params=_cparams("parallel"),
    )(place, w, dep)


SEM_SPEC = pl.BlockSpec(memory_space=pltpu.SEMAPHORE)
ANY_SPEC = pl.BlockSpec(memory_space=pl.ANY)
SPLIT_COPY = pltpu.CompilerParams(has_side_effects=pltpu.SideEffectType.DATAFLOW_SIDE_EFFECTING)


def _hbm(a):
    return pltpu.with_memory_space_constraint(a, pltpu.HBM)


def _gather_region(refs, shard_shapes, layouts, a, chip, half):
    rows_a = shard_shapes[a][0]
    h = rows_a // 2
    if layouts[a] == "rows":
        return refs[a].at[pl.ds(chip * rows_a + half * h, h)]
    if layouts[a] == "cols":
        cols_a = shard_shapes[a][1]
        return refs[a].at[pl.ds(half * h, h), pl.ds(chip * cols_a, cols_a)]
    return refs[a].at[chip, pl.ds(half * h, h)]


def _gather_start(bufs, after, shard_shapes, layouts, name):
    n = len(bufs)
    ns = 3 * n

    def body(*refs):
        in_refs = refs[:n]
        send_sems = refs[n + 1:n + 1 + ns]
        recv_sems = refs[n + 1 + ns:n + 1 + 2 * ns]
        token = refs[2 * n + 1 + 2 * ns]
        c = lax.axis_index("c")
        me, peers = _chip_peers()
        for a in range(n):
            mine = _gather_region(in_refs, shard_shapes, layouts, a, me, c)
            for k, (dev, _) in enumerate(peers):
                pltpu.make_async_remote_copy(
                    src_ref=mine, dst_ref=mine, send_sem=send_sems[3 * a + k], recv_sem=recv_sems[3 * a + k],
                    device_id=dev, device_id_type=MESH).start()
        token[...] = jnp.zeros_like(token)

    sem = pltpu.SemaphoreType.DMA(())
    out = pl.pallas_call(
        body, name=name,
        out_shape=(*[sem] * (2 * ns), *[pltpu.HBM(b.shape, b.dtype) for b in bufs],
                   jax.ShapeDtypeStruct((8, LANES), F32)),
        in_specs=[HBM_SPEC] * n + [ANY_SPEC],
        out_specs=(*[SEM_SPEC] * (2 * ns), *[HBM_SPEC] * n, pl.BlockSpec(memory_space=pltpu.VMEM)),
        input_output_aliases={a: 2 * ns + a for a in range(n)},
        compiler_params=SPLIT_COPY,
    )(*[_hbm(b) for b in bufs], after)
    return list(out[:ns]), list(out[ns:2 * ns]), list(out[2 * ns:2 * ns + n]), out[2 * ns + n]


def _gather_wait(bufs, send_sems, recv_sems, afters, shard_shapes, layouts, name):
    n = len(bufs)
    ns = 3 * n
    na = len(afters)

    def body(*refs):
        in_refs = refs[:n]
        send_sems = refs[n:n + ns]
        recv_sems = refs[n + ns:n + 2 * ns]
        token = refs[2 * n + 2 * ns + na]
        token[...] = jnp.zeros_like(token)
        c = lax.axis_index("c")
        me, peers = _chip_peers()
        for a in range(n):
            mine = _gather_region(in_refs, shard_shapes, layouts, a, me, c)
            for k, (dev, pidx) in enumerate(peers):
                landed = _gather_region(in_refs, shard_shapes, layouts, a, pidx, c)
                pltpu.make_async_remote_copy(
                    src_ref=mine, dst_ref=landed, send_sem=send_sems[3 * a + k], recv_sem=recv_sems[3 * a + k],
                    device_id=dev, device_id_type=MESH).wait_recv()
        for a in range(n):
            mine = _gather_region(in_refs, shard_shapes, layouts, a, me, c)
            for k, (dev, _) in enumerate(peers):
                pltpu.make_async_remote_copy(
                    src_ref=mine, dst_ref=mine, send_sem=send_sems[3 * a + k], recv_sem=recv_sems[3 * a + k],
                    device_id=dev, device_id_type=MESH).wait_send()

    out = pl.pallas_call(
        body, name=name,
        out_shape=(*[pltpu.HBM(b.shape, b.dtype) for b in bufs], jax.ShapeDtypeStruct((8, LANES), F32)),
        in_specs=[HBM_SPEC] * n + [SEM_SPEC] * (2 * ns) + [ANY_SPEC] * na,
        out_specs=(*[HBM_SPEC] * n, pl.BlockSpec(memory_space=pltpu.VMEM)),
        input_output_aliases={a: a for a in range(n)},
        compiler_params=SPLIT_COPY,
    )(*bufs, *send_sems, *recv_sems, *afters)
    return list(out[:n]), out[n]


def _gather_forward(bufs, dep, shard_shapes, layouts, name):
    n = len(bufs)

    def body(*refs):
        out_refs = refs[n + 1:2 * n + 1]
        send_sems, recv_sems = refs[2 * n + 1:]
        c = lax.axis_index("c")
        sib = (lax.axis_index("x"), lax.axis_index("y"), 1 - c)
        _, peers = _chip_peers()
        sends = []
        for a in range(n):
            for k, (_, pidx) in enumerate(peers):
                landed = _gather_region(out_refs, shard_shapes, layouts, a, pidx, c)
                cp = pltpu.make_async_remote_copy(
                    src_ref=landed, dst_ref=landed, send_sem=send_sems.at[a, k], recv_sem=recv_sems.at[a, k],
                    device_id=sib, device_id_type=MESH)
                cp.start()
                sends.append(cp)
        for a in range(n):
            for k, (_, pidx) in enumerate(peers):
                other = _gather_region(out_refs, shard_shapes, layouts, a, pidx, 1 - c)
                pltpu.make_async_remote_copy(
                    src_ref=other, dst_ref=other, send_sem=send_sems.at[a, k], recv_sem=recv_sems.at[a, k],
                    device_id=sib, device_id_type=MESH).wait_recv()
        for cp in sends:
            cp.wait_send()

    sem = pltpu.SemaphoreType.DMA((n, 3))
    return pl.pallas_call(
        body, name=name,
        in_specs=[HBM_SPEC] * n + [ANY_SPEC], out_specs=[HBM_SPEC] * n,
        out_shape=[jax.ShapeDtypeStruct(b.shape, b.dtype) for b in bufs],
        input_output_aliases={a: a for a in range(n)},
        scratch_shapes=[sem, sem],
    )(*bufs, dep)


def _send_sibling_halves(stacked, name):
    n = len(stacked)

    def body(*refs):
        in_refs = refs[:n]
        out_refs = refs[n:2 * n]
        send_sems, recv_sems = refs[2 * n:]
        c = lax.axis_index("c")
        sib = (lax.axis_index("x"), lax.axis_index("y"), 1 - c)
        sends = []
        for a in range(n):
            h = stacked[a].shape[1] // 2
            cp = pltpu.make_async_remote_copy(
                src_ref=in_refs[a].at[:, pl.ds((1 - c) * h, h)], dst_ref=out_refs[a], send_sem=send_sems.at[a],
                recv_sem=recv_sems.at[a], device_id=sib, device_id_type=MESH)
            cp.start()
            sends.append(cp)
        for cp in sends:
            cp.wait_recv()
        for cp in sends:
            cp.wait_send()

    return pl.pallas_call(
        body, name=name,
        in_specs=[HBM_SPEC] * n, out_specs=[HBM_SPEC] * n,
        out_shape=[jax.ShapeDtypeStruct((s.shape[0], s.shape[1] // 2, s.shape[2]), s.dtype) for s in stacked],
        scratch_shapes=[pltpu.SemaphoreType.DMA((n,)), pltpu.SemaphoreType.DMA((n,))],
    )(*stacked)


def _scatter_start(stacked, name):
    n = len(stacked)
    ns = 3 * n
    lands = [lax.empty((3,) + s.shape[1:], s.dtype) for s in stacked]

    def body(*refs):
        src_refs = refs[:n]
        land_refs = refs[n:2 * n]
        send_sems = refs[2 * n:2 * n + ns]
        recv_sems = refs[2 * n + ns:2 * n + 2 * ns]
        token = refs[4 * n + 2 * ns]
        _, peers = _chip_peers()
        for a in range(n):
            for k, (dev, pidx) in enumerate(peers):
                pltpu.make_async_remote_copy(
                    src_ref=src_refs[a].at[pidx], dst_ref=land_refs[a].at[k], send_sem=send_sems[3 * a + k],
                    recv_sem=recv_sems[3 * a + k], device_id=dev, device_id_type=MESH).start()
        token[...] = jnp.zeros_like(token)

    sem = pltpu.SemaphoreType.DMA(())
    out = pl.pallas_call(
        body, name=name,
        out_shape=(*[sem] * (2 * ns), *[pltpu.HBM(b.shape, b.dtype) for b in stacked],
                   *[pltpu.HBM(b.shape, b.dtype) for b in lands], jax.ShapeDtypeStruct((8, LANES), F32)),
        in_specs=[HBM_SPEC] * (2 * n),
        out_specs=(*[SEM_SPEC] * (2 * ns), *[HBM_SPEC] * (2 * n), pl.BlockSpec(memory_space=pltpu.VMEM)),
        input_output_aliases={a: 2 * ns + a for a in range(2 * n)},
        compiler_params=SPLIT_COPY,
    )(*[_hbm(b) for b in stacked], *[_hbm(b) for b in lands])
    o = 2 * ns
    return list(out[:ns]), list(out[ns:o]), list(out[o:o + n]), list(out[o + n:o + 2 * n]), out[o + 2 * n]


def _scatter_wait(srcs, lands, send_sems, recv_sems, after, name):
    n = len(srcs)
    ns = 3 * n

    def body(*refs):
        src_refs = refs[:n]
        land_refs = refs[n:2 * n]
        send_sems = refs[2 * n:2 * n + ns]
        recv_sems = refs[2 * n + ns:2 * n + 2 * ns]
        _, peers = _chip_peers()
        copies = [
            pltpu.make_async_remote_copy(
                src_ref=src_refs[a].at[pidx], dst_ref=land_refs[a].at[k], send_sem=send_sems[3 * a + k],
                recv_sem=recv_sems[3 * a + k], device_id=dev, device_id_type=MESH)
            for a in range(n) for k, (dev, pidx) in enumerate(peers)]
        for cp in copies:
            cp.wait_recv()
        for cp in copies:
            cp.wait_send()

    out = pl.pallas_call(
        body, name=name,
        out_shape=tuple(pltpu.HBM(b.shape, b.dtype) for b in list(srcs) + list(lands)),
        in_specs=[HBM_SPEC] * (2 * n) + [SEM_SPEC] * (2 * ns) + [ANY_SPEC],
        out_specs=tuple([HBM_SPEC] * (2 * n)),
        input_output_aliases={a: a for a in range(2 * n)},
        compiler_params=SPLIT_COPY,
    )(*srcs, *lands, *send_sems, *recv_sems, after)
    return list(out[n:])


def _swap_sibling(arrays, name):
    n = len(arrays)

    def body(*refs):
        in_refs = refs[:n]
        out_refs = refs[n:2 * n]
        send_sems, recv_sems = refs[2 * n:]
        sib = (lax.axis_index("x"), lax.axis_index("y"), 1 - lax.axis_index("c"))
        sends = []
        for a in range(n):
            cp = pltpu.make_async_remote_copy(
                src_ref=in_refs[a], dst_ref=out_refs[a], send_sem=send_sems.at[a], recv_sem=recv_sems.at[a],
                device_id=sib, device_id_type=MESH)
            cp.start()
            sends.append(cp)
        for cp in sends:
            cp.wait_recv()
        for cp in sends:
            cp.wait_send()

    return pl.pallas_call(
        body, name=name,
        in_specs=[HBM_SPEC] * n, out_specs=[HBM_SPEC] * n,
        out_shape=[jax.ShapeDtypeStruct(s.shape, s.dtype) for s in arrays],
        scratch_shapes=[pltpu.SemaphoreType.DMA((n,)), pltpu.SemaphoreType.DMA((n,))],
    )(*arrays)


def _all_reduce_small(v):
    R, C = v.shape
    n_dev = 8

    def body(v_ref, o_ref, buf, send_sems, recv_sems):
        x, y, c = lax.axis_index("x"), lax.axis_index("y"), lax.axis_index("c")
        me = 4 * x + 2 * y + c
        buf[me] = v_ref[...]
        sends = []
        for k in range(1, n_dev):
            px, py, pc = x ^ ((k >> 2) & 1), y ^ ((k >> 1) & 1), c ^ (k & 1)
            cp = pltpu.make_async_remote_copy(
                src_ref=v_ref, dst_ref=buf.at[me], send_sem=send_sems.at[k - 1], recv_sem=recv_sems.at[k - 1],
                device_id=(px, py, pc), device_id_type=MESH)
            cp.start()
            sends.append((cp, 4 * px + 2 * py + pc))
        for k in range(1, n_dev):
            cp, pidx = sends[k - 1]
            pltpu.make_async_remote_copy(
                src_ref=v_ref, dst_ref=buf.at[pidx], send_sem=send_sems.at[k - 1], recv_sem=recv_sems.at[k - 1],
                device_id=(x, y, c), device_id_type=MESH).wait_recv()
        for cp, _ in sends:
            cp.wait_send()
        acc = buf[0]
        for d in range(1, n_dev):
            acc = acc + buf[d]
        o_ref[...] = acc

    vm = pl.BlockSpec(memory_space=pltpu.VMEM)
    return pl.pallas_call(
        body, name="all_reduce_small",
        in_specs=[vm], out_specs=vm,
        out_shape=jax.ShapeDtypeStruct((R, C), F32),
        scratch_shapes=[pltpu.VMEM((n_dev, R, C), F32), pltpu.SemaphoreType.DMA((n_dev - 1,)),
                        pltpu.SemaphoreType.DMA((n_dev - 1,))],
    )(v)


def _pair_sum(place, own, sib, name):
    S, R, C = own.shape
    h = R // 2
    tr = _row_tile(h, C)
    nt = h // tr

    def body(place_ref, o_ref, s_ref, out_ref):
        out_ref[...] = (o_ref[...].astype(F32) + s_ref[...].astype(F32)).astype(BF16)

    return pl.pallas_call(
        body, name=name,
        grid_spec=pltpu.PrefetchScalarGridSpec(
            num_scalar_prefetch=1, grid=(S, nt),
            in_specs=[pl.BlockSpec((None, tr, C), lambda s, i, p: (s, p[0] * nt + i, 0)),
                      pl.BlockSpec((None, tr, C), lambda s, i, p: (s, i, 0))],
            out_specs=pl.BlockSpec((None, tr, C), lambda s, i, p: (s, i, 0))),
        out_shape=jax.ShapeDtypeStruct((S, h, C), BF16),
        compiler_params=_cparams("parallel", "parallel"),
    )(place, own, sib)


def _sum_slabs(place, own, sib, recv, name):
    S, R, C = own.shape
    h = R // 2
    tr = _row_tile(h, C)
    nt = h // tr

    def body(place_ref, o_ref, s_ref, r_ref, out_ref):
        acc = o_ref[...].astype(F32) + s_ref[...].astype(F32)
        for k in range(3):
            acc = acc + r_ref[k].astype(F32)
        out_ref[...] = acc

    return pl.pallas_call(
        body, name=name,
        grid_spec=pltpu.PrefetchScalarGridSpec(
            num_scalar_prefetch=1, grid=(nt,),
            in_specs=[pl.BlockSpec((None, tr, C), lambda i, p: (p[1], p[0] * nt + i, 0)),
                      pl.BlockSpec((None, tr, C), lambda i, p: (p[1], i, 0)),
                      pl.BlockSpec((3, tr, C), lambda i, p: (0, i, 0))],
            out_specs=pl.BlockSpec((tr, C), lambda i, p: (i, 0))),
        out_shape=jax.ShapeDtypeStruct((h, C), F32),
        compiler_params=_cparams("parallel"),
    )(place, own, sib, recv)


def _adamw(parts, w, m, v, name):
    R, C = w.shape
    tr = _row_tile(R, C)
    npart = len(parts)
    c1 = 1.0 - ADAM_B1 ** ADAM_STEP
    c2 = 1.0 - ADAM_B2 ** ADAM_STEP

    def body(*refs):
        p_refs = refs[:npart]
        w_ref, m_ref, v_ref, g_ref, d_ref, nm_ref, nv_ref = refs[npart:]
        g = p_refs[0][...]
        for p_ref in p_refs[1:]:
            g = g + p_ref[...]
        nm = ADAM_B1 * m_ref[...] + (1.0 - ADAM_B1) * g
        nv = ADAM_B2 * v_ref[...] + (1.0 - ADAM_B2) * (g * g)
        m_hat = nm / c1
        v_hat = nv / c2
        g_ref[...] = g
        d_ref[...] = -ADAM_LR * (m_hat / (jnp.sqrt(v_hat) + ADAM_EPS) + ADAM_WD * w_ref[...])
        nm_ref[...] = nm
        nv_ref[...] = nv

    blk = pl.BlockSpec((tr, C), lambda i: (i, 0))
    shape = jax.ShapeDtypeStruct((R, C), F32)
    return pl.pallas_call(
        body, name=name, grid=(R // tr,),
        in_specs=[blk] * (npart + 3), out_specs=[blk] * 4, out_shape=[shape] * 4,
        compiler_params=_cparams("parallel"),
    )(*parts, w, m, v)


def _adamw_halves(place, mine, other, w, m, v, name):
    R, C = w.shape
    h = R // 2
    tr = _row_tile(h, C)
    nt = h // tr
    c1 = 1.0 - ADAM_B1 ** ADAM_STEP
    c2 = 1.0 - ADAM_B2 ** ADAM_STEP

    def body(place_ref, mine_ref, other_ref, w_ref, m_ref, v_ref, g_ref, d_ref, nm_ref, nv_ref):
        is_mine = (pl.program_id(0) // nt) == place_ref[0]
        g = jnp.where(is_mine, mine_ref[...], other_ref[...])
        nm = ADAM_B1 * m_ref[...] + (1.0 - ADAM_B1) * g
        nv = ADAM_B2 * v_ref[...] + (1.0 - ADAM_B2) * (g * g)
        m_hat = nm / c1
        v_hat = nv / c2
        g_ref[...] = g
        d_ref[...] = -ADAM_LR * (m_hat / (jnp.sqrt(v_hat) + ADAM_EPS) + ADAM_WD * w_ref[...])
        nm_ref[...] = nm
        nv_ref[...] = nv

    def half_block(which):
        def index(i, p):
            first = p[0] if which == 0 else 1 - p[0]
            return jnp.clip(i - first * nt, 0, nt - 1), 0

        return pl.BlockSpec((tr, C), index)

    blk = pl.BlockSpec((tr, C), lambda i, p: (i, 0))
    shape = jax.ShapeDtypeStruct((R, C), F32)
    return pl.pallas_call(
        body, name=name,
        grid_spec=pltpu.PrefetchScalarGridSpec(
            num_scalar_prefetch=1, grid=(2 * nt,),
            in_specs=[half_block(0), half_block(1), blk, blk, blk], out_specs=[blk] * 4),
        out_shape=[shape] * 4,
        compiler_params=_cparams("parallel"),
    )(place, mine, other, w, m, v)


SMALL_NAMES = ("ffn1_norm", "mix_norm", "ffn2_norm", "pool_scale", "q_norm", "k_norm", "b_forget")
SMALL_COLS = 1024
LOSS_LANE = 512


def _pack_small(vals):
    rows = [vals[n].reshape(-1, SMALL_COLS) for n in ("ffn1_norm", "mix_norm", "ffn2_norm", "pool_scale")]
    tail = jnp.concatenate([vals["q_norm"].reshape(-1), vals["k_norm"].reshape(-1), vals["b_forget"].reshape(-1)])
    rows.append(jnp.pad(tail, (0, SMALL_COLS - tail.shape[0])).reshape(1, SMALL_COLS))
    return jnp.concatenate(rows, axis=0)


def _unpack_small(packed, like):
    out = {}
    r = 0
    for n in ("ffn1_norm", "mix_norm", "ffn2_norm", "pool_scale"):
        k = like[n].size // SMALL_COLS
        out[n] = packed[r:r + k].reshape(like[n].shape)
        r += k
    o = 0
    for n in ("q_norm", "k_norm", "b_forget"):
        k = like[n].size
        out[n] = packed[r, o:o + k].reshape(like[n].shape)
        o += k
    return out


def kernel(x, meta_tokens, ffn1_norm, ffn1_w_gate, ffn1_w_up, ffn1_w_down, mix_norm, w_in, b_forget, q_norm, k_norm, pool_w, pool_scale, w_out, ffn2_norm, ffn2_w_gate, ffn2_w_up, ffn2_w_down, loss_target, m_meta_tokens, m_ffn1_norm, m_ffn1_w_gate, m_ffn1_w_up, m_ffn1_w_down, m_mix_norm, m_w_in, m_b_forget, m_q_norm, m_k_norm, m_pool_w, m_pool_scale, m_w_out, m_ffn2_norm, m_ffn2_w_gate, m_ffn2_w_up, m_ffn2_w_down, v_meta_tokens, v_ffn1_norm, v_ffn1_w_gate, v_ffn1_w_up, v_ffn1_w_down, v_mix_norm, v_w_in, v_b_forget, v_q_norm, v_k_norm, v_pool_w, v_pool_scale, v_w_out, v_ffn2_norm, v_ffn2_w_gate, v_ffn2_w_up, v_ffn2_w_down):
    wts = dict(meta_tokens=meta_tokens, ffn1_norm=ffn1_norm, ffn1_w_gate=ffn1_w_gate, ffn1_w_up=ffn1_w_up,
               ffn1_w_down=ffn1_w_down, mix_norm=mix_norm, w_in=w_in, b_forget=b_forget, q_norm=q_norm,
               k_norm=k_norm, pool_w=pool_w, pool_scale=pool_scale, w_out=w_out, ffn2_norm=ffn2_norm,
               ffn2_w_gate=ffn2_w_gate, ffn2_w_up=ffn2_w_up, ffn2_w_down=ffn2_w_down)
    mom = dict(meta_tokens=m_meta_tokens, ffn1_norm=m_ffn1_norm, ffn1_w_gate=m_ffn1_w_gate, ffn1_w_up=m_ffn1_w_up,
               ffn1_w_down=m_ffn1_w_down, mix_norm=m_mix_norm, w_in=m_w_in, b_forget=m_b_forget, q_norm=m_q_norm,
               k_norm=m_k_norm, pool_w=m_pool_w, pool_scale=m_pool_scale, w_out=m_w_out, ffn2_norm=m_ffn2_norm,
               ffn2_w_gate=m_ffn2_w_gate, ffn2_w_up=m_ffn2_w_up, ffn2_w_down=m_ffn2_w_down)
    var = dict(meta_tokens=v_meta_tokens, ffn1_norm=v_ffn1_norm, ffn1_w_gate=v_ffn1_w_gate, ffn1_w_up=v_ffn1_w_up,
               ffn1_w_down=v_ffn1_w_down, mix_norm=v_mix_norm, w_in=v_w_in, b_forget=v_b_forget, q_norm=v_q_norm,
               k_norm=v_k_norm, pool_w=v_pool_w, pool_scale=v_pool_scale, w_out=v_w_out, ffn2_norm=v_ffn2_norm,
               ffn2_w_gate=v_ffn2_w_gate, ffn2_w_up=v_ffn2_w_up, ffn2_w_down=v_ffn2_w_down)
    order = list(wts)
    me = 2 * lax.axis_index("x") + lax.axis_index("y")

    D = x.shape[2]
    d_in_shard = w_in.shape[2]
    d_in = N_CHIPS * d_in_shard
    n_heads = b_forget.shape[1]
    d_in_pad = (d_in - n_heads) + LANES

    stages = dict(ffn1_up=("ffn1_w_gate", "ffn1_w_up", "meta_tokens"), ffn1_down=("ffn1_w_down",),
                  mix=("w_in", "w_out", "pool_w"),
                  ffn2_up=("ffn2_w_gate", "ffn2_w_up"), ffn2_down=("ffn2_w_down",))
    stage_order = list(stages)
    place = jnp.stack([lax.axis_index("c"), me]).astype(jnp.int32)
    layouts = dict(ffn1_w_gate="cols", ffn1_w_up="cols", ffn1_w_down="rows", w_in="stack", w_out="rows",
                   pool_w="stack", ffn2_w_gate="cols", ffn2_w_up="cols", ffn2_w_down="rows", meta_tokens="stack")
    shards2d = {n: wts[n].reshape(-1, wts[n].shape[-1]) for n in layouts}

    def place_stage(stage, dep):
        return [_cast_place(place, shards2d[n], dep, layouts[n], F32 if n == "meta_tokens" else BF16, f"place_{n}")
                for n in stages[stage]]

    def start_stage(stage, bufs, after):
        shapes = [shards2d[n].shape for n in stages[stage]]
        lays = [layouts[n] for n in stages[stage]]
        return _gather_start(bufs, after, shapes, lays, f"gather_start_{stage}") + (shapes, lays)

    flight = {stage_order[0]: start_stage(stage_order[0], place_stage(stage_order[0], place), place)}
    first_token = flight[stage_order[0]][3]
    placed = {stage: place_stage(stage, first_token) for stage in stage_order[1:]}

    def cols(st):
        return jnp.transpose(st, (1, 0, 2)).reshape(st.shape[1], -1)

    def get_weights(stage, after):
        send_sems, recv_sems, bufs, _, shapes, lays = flight.pop(stage)
        afters = [b for st in stage_order[1:] for b in placed[st]] if after is None else [after]
        landed, token = _gather_wait(bufs, send_sems, recv_sems, afters, shapes, lays, f"gather_wait_{stage}")
        nxt = stage_order.index(stage) + 1
        if nxt < len(stage_order):
            flight[stage_order[nxt]] = start_stage(stage_order[nxt], placed.pop(stage_order[nxt]), token)
            token = flight[stage_order[nxt]][3]
        G = dict(zip(stages[stage], _gather_forward(landed, token, shapes, lays, f"gather_forward_{stage}")))
        if stage == "ffn1_up":
            return dict(wg=G["ffn1_w_gate"], wu=G["ffn1_w_up"], meta=cols(G["meta_tokens"]))
        if stage == "ffn2_up":
            return dict(wg=G["ffn2_w_gate"], wu=G["ffn2_w_up"])
        if stage != "mix":
            return dict(wd=G[stages[stage][0]])
        return dict(
            win=jnp.pad(cols(G["w_in"]), ((0, 0), (0, d_in_pad - d_in))), wout=G["w_out"],
            pool_w=jnp.transpose(G["pool_w"].reshape((N_CHIPS,) + pool_w.shape[1:]), (1, 0, 2, 3)).reshape(
                N_POOL_GROUPS, pool_w.shape[3], pool_w.shape[3]))

    def split_rows(a):
        return a.reshape(N_CHIPS, -1, a.shape[1])

    def split_win(a):
        return jnp.transpose(a[:, :d_in].reshape(D, N_CHIPS, d_in_shard), (1, 0, 2))

    def split_pool(a):
        r, c = pool_w.shape[2], pool_w.shape[3]
        return jnp.transpose(a.reshape(N_POOL_GROUPS, N_CHIPS, r, c), (1, 0, 2, 3)).reshape(N_CHIPS, -1, c)

    scatter = {}

    def put_grads(name, g):
        if name == "mix":
            names = stages["mix"]
            own = [split_win(g["win"]), split_rows(g["wout"]), split_pool(g["pool_w"])]
        else:
            names = (name,)
            own = [split_rows(g) if name.endswith("_down") else g]
        from_sib = _send_sibling_halves(own, f"send_halves_{name}")
        pair = [_pair_sum(place, o, s, f"pair_sum_{n}") for n, o, s in zip(names, own, from_sib)]
        send_sems, recv_sems, srcs, lands, token = _scatter_start(pair, f"scatter_start_{name}")
        scatter[name] = (names, own, from_sib, send_sems, recv_sems, srcs, lands, token)
        return token

    small = dict(ffn1_norm=ffn1_norm, mix_norm=mix_norm, ffn2_norm=ffn2_norm, q_norm=q_norm, k_norm=k_norm,
                 b_forget=b_forget, pool_scale=pool_scale)
    loss_part, dh0, gr = _local_step(x[0], loss_target[0], small, get_weights, put_grads)

    out_g, out_d, out_m, out_v = {}, {}, {}, {}
    after = list(scatter.values())[-1][7]
    for stage in scatter:
        names, own, from_sib, send_sems, recv_sems, srcs, lands, _ = scatter[stage]
        received = _scatter_wait(srcs, lands, send_sems, recv_sems, after, f"scatter_wait_{stage}")
        halves = [_sum_slabs(place, o, s, r, f"sum_{n}") for n, o, s, r in zip(names, own, from_sib, received)]
        other_halves = _swap_sibling(halves, f"swap_{stage}")
        for n, mine, other in zip(names, halves, other_halves):
            shape = wts[n].shape
            res = _adamw_halves(place, mine, other, shards2d[n], mom[n].reshape(shards2d[n].shape),
                                var[n].reshape(shards2d[n].shape), f"adamw_{n}")
            out_g[n], out_d[n], out_m[n], out_v[n] = (a.reshape(shape) for a in res)
            after = res[3]

    small_g = _pack_small({n: gr[n] for n in SMALL_NAMES})
    n_small = small_g.shape[0]
    small_g = small_g.at[n_small - 1, LOSS_LANE].set(loss_part)
    meta_rows = gr["meta"].reshape(-1, SMALL_COLS)
    total = _all_reduce_small(jnp.concatenate([small_g, meta_rows], axis=0))
    loss = total[n_small - 1, LOSS_LANE]
    res = _adamw([total[:n_small]], _pack_small({n: wts[n] for n in SMALL_NAMES}),
                 _pack_small({n: mom[n] for n in SMALL_NAMES}), _pack_small({n: var[n] for n in SMALL_NAMES}),
                 "adamw_small")
    for dst, packed in zip((out_g, out_d, out_m, out_v), res):
        dst.update(_unpack_small(packed, wts))
    meta_cols = meta_tokens.shape[1]
    meta_g = lax.dynamic_slice_in_dim(total[n_small:].reshape(N_META, D), me * meta_cols, meta_cols, axis=1)
    res = _adamw([meta_g], meta_tokens, m_meta_tokens, v_meta_tokens, "adamw_meta")
    out_g["meta_tokens"], out_d["meta_tokens"], out_m["meta_tokens"], out_v["meta_tokens"] = res

    grad_x = gr["x"].reshape(x.shape)
    return (loss, grad_x, *[out_g[n] for n in order], *[out_d[n] for n in order], *[out_m[n] for n in order],
            *[out_v[n] for n in order])
```

```python
import functools
import math

import jax
import jax.numpy as jnp
from jax import lax
from jax.experimental import pallas as pl
from jax.experimental.pallas import tpu as pltpu

F32 = jnp.float32
BF16 = jnp.bfloat16

N_META = 16
EPS = 1e-6
HEAD_DIM = 128
N_POOL_GROUPS = 4
LANES = 128
SEQ_ALIGN = 128
TQ = 128
CAUSAL_STEP = 512
VMEM_LIMIT = 56 * 1024 * 1024
ELEMWISE_BLOCK_BYTES = 2304 * 1024

ADAM_LR = 0.001
ADAM_B1 = 0.9
ADAM_B2 = 0.999
ADAM_EPS = 1e-08
ADAM_WD = 0.01
ADAM_STEP = 10

NT_DIMS = (((1,), (1,)), ((), ()))
NEG = -1e30
MESH = pl.DeviceIdType.MESH


def _cparams(*sem):
    return pltpu.CompilerParams(dimension_semantics=sem, vmem_limit_bytes=VMEM_LIMIT)


def _sigmoid(a):
    return 1.0 / (1.0 + jnp.exp(-a))


def _row_tile(rows, cols, itemsize=4):
    best = None
    for t in range(16, rows + 1, 16):
        if rows % t == 0 and t * cols * itemsize <= ELEMWISE_BLOCK_BYTES:
            best = t
    return best if best is not None else rows


def _mm_nn(x, w, tn, name):
    M, K = x.shape
    N = w.shape[1]

    def body(x_ref, w_ref, o_ref):
        o_ref[...] = jnp.dot(x_ref[...], w_ref[...], preferred_element_type=F32)

    return pl.pallas_call(
        body, name=name, grid=(N // tn,),
        in_specs=[pl.BlockSpec((M, K), lambda j: (0, 0)), pl.BlockSpec((K, tn), lambda j: (0, j))],
        out_specs=pl.BlockSpec((M, tn), lambda j: (0, j)),
        out_shape=jax.ShapeDtypeStruct((M, N), F32),
        compiler_params=_cparams("parallel"),
    )(x, w)


def _ffn_up(n, wg, wu, tn, name):
    M, K = n.shape
    N = wg.shape[1]

    def body(n_ref, wg_ref, wu_ref, a_ref, b_ref, s_ref, st_ref):
        nv = n_ref[...]
        a = jnp.dot(nv, wg_ref[...], preferred_element_type=F32)
        b = jnp.dot(nv, wu_ref[...], preferred_element_type=F32)
        a_ref[...] = a
        b_ref[...] = b
        s = a * _sigmoid(a) * b
        s_ref[...] = s.astype(BF16)
        st_ref[...] = s.T.astype(BF16)

    wspec = pl.BlockSpec((K, tn), lambda j: (0, j))
    ospec = pl.BlockSpec((M, tn), lambda j: (0, j))
    return pl.pallas_call(
        body, name=name, grid=(N // tn,),
        in_specs=[pl.BlockSpec((M, K), lambda j: (0, 0)), wspec, wspec],
        out_specs=[ospec, ospec, ospec, pl.BlockSpec((tn, M), lambda j: (j, 0))],
        out_shape=[jax.ShapeDtypeStruct((M, N), F32), jax.ShapeDtypeStruct((M, N), F32),
                   jax.ShapeDtypeStruct((M, N), BF16), jax.ShapeDtypeStruct((N, M), BF16)],
        compiler_params=_cparams("parallel"),
    )(n, wg, wu)


def _mm_nn_residual(x, w, res, alpha, tn, tk, name):
    M, K = x.shape
    N = w.shape[1]
    nk = K // tk

    def body(x_ref, w_ref, r_ref, o_ref, acc):
        k = pl.program_id(1)

        @pl.when(k == 0)
        def _():
            acc[...] = jnp.zeros_like(acc)

        acc[...] += jnp.dot(x_ref[...], w_ref[...], preferred_element_type=F32)

        @pl.when(k == nk - 1)
        def _():
            o_ref[...] = r_ref[...] + alpha * acc[...]

    return pl.pallas_call(
        body, name=name, grid=(N // tn, nk),
        in_specs=[pl.BlockSpec((M, tk), lambda j, k: (0, k)), pl.BlockSpec((tk, tn), lambda j, k: (k, j)),
                  pl.BlockSpec((M, tn), lambda j, k: (0, j))],
        out_specs=pl.BlockSpec((M, tn), lambda j, k: (0, j)),
        out_shape=jax.ShapeDtypeStruct((M, N), F32),
        scratch_shapes=[pltpu.VMEM((M, tn), F32)],
        compiler_params=_cparams("parallel", "arbitrary"),
    )(x, w, res)


def _ffn_bwd_hidden(dhb, wd, a, b, dep, tn, name):
    M, K = dhb.shape
    N = wd.shape[0]

    def body(dh_ref, w_ref, a_ref, b_ref, dep_ref, da_ref, db_ref):
        ds = 0.5 * lax.dot_general(dh_ref[...], w_ref[...], NT_DIMS, preferred_element_type=F32)
        av = a_ref[...]
        sig = _sigmoid(av)
        da_ref[...] = (ds * b_ref[...] * (sig * (1.0 + av * (1.0 - sig)))).astype(BF16)
        db_ref[...] = (ds * (av * sig)).astype(BF16)

    ospec = pl.BlockSpec((M, tn), lambda j: (0, j))
    return pl.pallas_call(
        body, name=name, grid=(N // tn,),
        in_specs=[pl.BlockSpec((M, K), lambda j: (0, 0)), pl.BlockSpec((tn, K), lambda j: (j, 0)), ospec, ospec,
                  pl.BlockSpec(memory_space=pl.ANY)],
        out_specs=[ospec, ospec],
        out_shape=[jax.ShapeDtypeStruct((M, N), BF16), jax.ShapeDtypeStruct((M, N), BF16)],
        compiler_params=_cparams("parallel"),
    )(dhb, wd, a, b, dep)


def _mm_nt(x, w, dep, tn, name):
    M, K = x.shape
    N = w.shape[0]

    def body(x_ref, w_ref, dep_ref, o_ref):
        o_ref[...] = lax.dot_general(x_ref[...], w_ref[...], NT_DIMS, preferred_element_type=F32)

    return pl.pallas_call(
        body, name=name, grid=(N // tn,),
        in_specs=[pl.BlockSpec((M, K), lambda j: (0, 0)), pl.BlockSpec((tn, K), lambda j: (j, 0)),
                  pl.BlockSpec(memory_space=pl.ANY)],
        out_specs=pl.BlockSpec((M, tn), lambda j: (0, j)),
        out_shape=jax.ShapeDtypeStruct((M, N), F32),
        compiler_params=_cparams("parallel"),
    )(x, w, dep)


def _mm_nt_sum(xs, ws, dep, tn, tk, name):
    npair = len(xs)
    M, K = xs[0].shape
    N = ws[0].shape[0]
    nk = K // tk

    def body(*refs):
        x_refs = refs[:npair]
        w_refs = refs[npair:2 * npair]
        o_ref = refs[2 * npair + 1]
        acc = refs[2 * npair + 2]
        k = pl.program_id(1)

        @pl.when(k == 0)
        def _():
            acc[...] = jnp.zeros_like(acc)

        for x_ref, w_ref in zip(x_refs, w_refs):
            acc[...] += lax.dot_general(x_ref[...], w_ref[...], NT_DIMS, preferred_element_type=F32)

        @pl.when(k == nk - 1)
        def _():
            o_ref[...] = acc[...]

    return pl.pallas_call(
        body, name=name, grid=(N // tn, nk),
        in_specs=[pl.BlockSpec((M, tk), lambda j, k: (0, k))] * npair
        + [pl.BlockSpec((tn, tk), lambda j, k: (j, k))] * npair + [pl.BlockSpec(memory_space=pl.ANY)],
        out_specs=pl.BlockSpec((M, tn), lambda j, k: (0, j)),
        out_shape=jax.ShapeDtypeStruct((M, N), F32),
        scratch_shapes=[pltpu.VMEM((M, tn), F32)],
        compiler_params=_cparams("parallel", "arbitrary"),
    )(*xs, *ws, dep)


def _mm_tn(xt, dy, dep, alpha, ti, tn, name, stacked):
    Kin, M = xt.shape
    N = dy.shape[1]

    def body(xt_ref, dy_ref, dep_ref, ob_ref):
        r = jnp.dot(xt_ref[...], dy_ref[...], preferred_element_type=F32)
        if alpha != 1.0:
            r = alpha * r
        ob_ref[...] = r.astype(BF16)

    if stacked:
        ospec = pl.BlockSpec((None, ti, tn), lambda i, j: (j, i, 0))
        oshape = (N // tn, Kin, tn)
    else:
        ospec = pl.BlockSpec((ti, tn), lambda i, j: (i, j))
        oshape = (Kin, N)
    return pl.pallas_call(
        body, name=name, grid=(Kin // ti, N // tn),
        in_specs=[pl.BlockSpec((ti, M), lambda i, j: (i, 0)), pl.BlockSpec((M, tn), lambda i, j: (0, j)),
                  pl.BlockSpec(memory_space=pl.ANY)],
        out_specs=ospec,
        out_shape=jax.ShapeDtypeStruct(oshape, BF16),
        compiler_params=_cparams("parallel", "parallel"),
    )(xt, dy, dep)


def _rmsnorm_fwd(h, g, name):
    M, D = h.shape
    tr = LANES

    def body(h_ref, g_ref, n_ref, nt_ref, r_ref):
        hv = h_ref[...]
        r = lax.rsqrt(jnp.mean(hv * hv, axis=-1, keepdims=True) + EPS)
        n = hv * r * g_ref[...]
        n_ref[...] = n.astype(BF16)
        nt_ref[...] = n.T.astype(BF16)
        r_ref[...] = r

    return pl.pallas_call(
        body, name=name, grid=(M // tr,),
        in_specs=[pl.BlockSpec((tr, D), lambda i: (i, 0)), pl.BlockSpec((1, D), lambda i: (0, 0))],
        out_specs=[pl.BlockSpec((tr, D), lambda i: (i, 0)), pl.BlockSpec((D, tr), lambda i: (0, i)),
                   pl.BlockSpec((tr, 1), lambda i: (i, 0))],
        out_shape=[jax.ShapeDtypeStruct((M, D), BF16), jax.ShapeDtypeStruct((D, M), BF16),
                   jax.ShapeDtypeStruct((M, 1), F32)],
        compiler_params=_cparams("parallel"),
    )(h, g)


def _rmsnorm_bwd(dn, h, r, g, dh_prev, name):
    M, D = h.shape
    tr = _row_tile(M, D)

    def body(dn_ref, h_ref, r_ref, g_ref, dp_ref, dh_ref, dhb_ref, dg_ref):
        i = pl.program_id(0)
        dnv = dn_ref[...]
        hv = h_ref[...]
        rv = r_ref[...]
        w = dnv * g_ref[...]
        c = jnp.mean(w * hv, axis=-1, keepdims=True)
        dh = dp_ref[...] + rv * w - hv * (rv * rv * rv * c)
        dh_ref[...] = dh
        dhb_ref[...] = dh.astype(BF16)
        part = jnp.sum(dnv * (hv * rv), axis=0, keepdims=True)

        @pl.when(i == 0)
        def _():
            dg_ref[...] = part

        @pl.when(i > 0)
        def _():
            dg_ref[...] += part

    row = pl.BlockSpec((tr, D), lambda i: (i, 0))
    vec = pl.BlockSpec((1, D), lambda i: (0, 0))
    return pl.pallas_call(
        body, name=name, grid=(M // tr,),
        in_specs=[row, row, pl.BlockSpec((tr, 1), lambda i: (i, 0)), vec, row],
        out_specs=[row, row, vec],
        out_shape=[jax.ShapeDtypeStruct((M, D), F32), jax.ShapeDtypeStruct((M, D), BF16),
                   jax.ShapeDtypeStruct((1, D), F32)],
        compiler_params=_cparams("arbitrary"),
    )(dn, h, r, g, dh_prev)


def _loss_grad(h, tgt, seq, name):
    M, D = h.shape
    tr = _row_tile(M, D)

    def body(h_ref, t_ref, dh_ref, dhb_ref, loss_ref):
        i = pl.program_id(0)
        row = i * tr + lax.broadcasted_iota(jnp.int32, (tr, 1), 0)
        valid = (row >= N_META) & (row < N_META + seq)
        d = jnp.where(valid, h_ref[...] - t_ref[...], 0.0)
        dh = d * (1.0 / D)
        dh_ref[...] = dh
        dhb_ref[...] = dh.astype(BF16)
        part = (0.5 / D) * jnp.sum(jnp.sum(d * d, axis=1, keepdims=True), axis=0, keepdims=True)

        @pl.when(i == 0)
        def _():
            loss_ref[...] = part

        @pl.when(i > 0)
        def _():
            loss_ref[...] += part

    row = pl.BlockSpec((tr, D), lambda i: (i, 0))
    return pl.pallas_call(
        body, name=name, grid=(M // tr,),
        in_specs=[row, row],
        out_specs=[row, row, pl.BlockSpec((1, 1), lambda i: (0, 0))],
        out_shape=[jax.ShapeDtypeStruct((M, D), F32), jax.ShapeDtypeStruct((M, D), BF16),
                   jax.ShapeDtypeStruct((1, 1), F32)],
        compiler_params=_cparams("arbitrary"),
    )(h, tgt)


def _group_window(g):
    return jnp.where(g == 0, 2, jnp.where(g == 1, 4, jnp.where(g == 2, 8, 16)))


def _pool_fwd(z, pw, psc, name):
    M = z.shape[0]
    C = pw.shape[1]

    def body(p_ref, w_ref, sc_ref, pooled_ref, out_ref, outt_ref):
        g = pl.program_id(0)
        p = p_ref[...]
        t = lax.broadcasted_iota(jnp.int32, (M, 1), 0)
        s = p
        wsum = jnp.zeros_like(p)
        for step in range(N_POOL_GROUPS):
            sh = 1 << step
            s = s + jnp.where(t >= sh, pltpu.roll(s, sh, 0), 0.0)
            wsum = jnp.where(g == step, s, wsum)
        cnt = jnp.minimum(t + 1, _group_window(g)).astype(F32)
        pb = (wsum / cnt - p).astype(BF16)
        pooled_ref[...] = pb
        out = jnp.dot(pb, w_ref[...], preferred_element_type=F32) * sc_ref[...]
        out_ref[...] = out.astype(BF16)
        outt_ref[...] = out.T.astype(BF16)

    col = pl.BlockSpec((M, C), lambda g: (0, g))
    return pl.pallas_call(
        body, name=name, grid=(N_POOL_GROUPS,),
        in_specs=[col, pl.BlockSpec((None, C, C), lambda g: (g, 0, 0)), pl.BlockSpec((1, C), lambda g: (0, g))],
        out_specs=[col, col, pl.BlockSpec((C, M), lambda g: (g, 0))],
        out_shape=[jax.ShapeDtypeStruct((M, N_POOL_GROUPS * C), BF16),
                   jax.ShapeDtypeStruct((M, N_POOL_GROUPS * C), BF16),
                   jax.ShapeDtypeStruct((N_POOL_GROUPS * C, M), BF16)],
        compiler_params=_cparams("parallel"),
    )(z, pw, psc)


def _pool_bwd(dmix, pooled, pw, psc, name):
    M = dmix.shape[0]
    C = pw.shape[1]

    def body(dm_ref, pooled_ref, w_ref, sc_ref, dp_ref, dwb_ref, dsc_ref):
        g = pl.program_id(0)
        dmx = dm_ref[...]
        pb = pooled_ref[...]
        wv = w_ref[...]
        mixed = jnp.dot(pb, wv, preferred_element_type=F32)
        dsc_ref[...] = jnp.sum(dmx * mixed, axis=0, keepdims=True)
        dmixed = (dmx * sc_ref[...]).astype(BF16)
        dw = jnp.dot(pb.astype(F32).T.astype(BF16), dmixed, preferred_element_type=F32)
        dwb_ref[...] = dw.astype(BF16)
        dpooled = lax.dot_general(dmixed, wv, NT_DIMS, preferred_element_type=F32)
        t = lax.broadcasted_iota(jnp.int32, (M, 1), 0)
        cnt = jnp.minimum(t + 1, _group_window(g)).astype(F32)
        s = dpooled / cnt
        wsum = jnp.zeros_like(s)
        for step in range(N_POOL_GROUPS):
            sh = 1 << step
            s = s + jnp.where(t < M - sh, pltpu.roll(s, M - sh, 0), 0.0)
            wsum = jnp.where(g == step, s, wsum)
        dp_ref[...] = (wsum - dpooled).astype(BF16)

    col = pl.BlockSpec((M, C), lambda g: (0, g))
    wspec = pl.BlockSpec((None, C, C), lambda g: (g, 0, 0))
    vec = pl.BlockSpec((1, C), lambda g: (0, g))
    return pl.pallas_call(
        body, name=name, grid=(N_POOL_GROUPS,),
        in_specs=[col, col, wspec, vec],
        out_specs=[col, wspec, vec],
        out_shape=[jax.ShapeDtypeStruct((M, N_POOL_GROUPS * C), BF16),
                   jax.ShapeDtypeStruct((N_POOL_GROUPS, C, C), BF16),
                   jax.ShapeDtypeStruct((1, N_POOL_GROUPS * C), F32)],
        compiler_params=_cparams("parallel"),
    )(dmix, pooled, pw, psc)


def _qkv_prep(z, gq, gk, n_heads, q_col, name):
    M = z.shape[0]
    H = n_heads
    qb = q_col // HEAD_DIM

    def body(q_ref, k_ref, v_ref, gq_ref, gk_ref, qh_ref, kh_ref, vb_ref):
        def norm(xv, g):
            r = lax.rsqrt(jnp.mean(xv * xv, axis=-1, keepdims=True) + EPS)
            return (xv * r * g).astype(BF16)

        qh_ref[...] = norm(q_ref[...], gq_ref[...])
        kh_ref[...] = norm(k_ref[...], gk_ref[...])
        vb_ref[...] = v_ref[...].astype(BF16)

    vec = pl.BlockSpec((1, HEAD_DIM), lambda h: (0, 0))
    out = pl.BlockSpec((M, HEAD_DIM), lambda h: (0, h))
    oshape = jax.ShapeDtypeStruct((M, H * HEAD_DIM), BF16)
    return pl.pallas_call(
        body, name=name, grid=(H,),
        in_specs=[pl.BlockSpec((M, HEAD_DIM), lambda h: (0, qb + h)),
                  pl.BlockSpec((M, HEAD_DIM), lambda h: (0, qb + H + h)),
                  pl.BlockSpec((M, HEAD_DIM), lambda h: (0, qb + 2 * H + h)), vec, vec],
        out_specs=[out, out, out],
        out_shape=[oshape, oshape, oshape],
        compiler_params=_cparams("parallel"),
    )(z, z, z, gq, gk)


def _forget_fwd(z, bpad, f_block, name):
    M = z.shape[0]

    def body(f_ref, b_ref, cum_ref):
        xx = f_ref[...] + b_ref[...]
        c = jnp.minimum(xx, 0.0) - jnp.log(1.0 + jnp.exp(-jnp.abs(xx)))
        t = lax.broadcasted_iota(jnp.int32, (M, 1), 0)
        sh = 1
        while sh < M:
            c = c + jnp.where(t >= sh, pltpu.roll(c, sh, 0), 0.0)
            sh *= 2
        cum_ref[...] = c.T

    return pl.pallas_call(
        body, name=name, grid=(1,),
        in_specs=[pl.BlockSpec((M, LANES), lambda i: (0, f_block)), pl.BlockSpec((1, LANES), lambda i: (0, 0))],
        out_specs=pl.BlockSpec((LANES, M), lambda i: (0, 0)),
        out_shape=jax.ShapeDtypeStruct((LANES, M), F32),
        compiler_params=_cparams("arbitrary"),
    )(z, bpad)


def _col_to_row(col):
    n = col.shape[0]
    return jnp.transpose(jnp.broadcast_to(col, (n, LANES)))[0:1, :]


def _causal_extents(M):
    edges = list(range(0, M, CAUSAL_STEP)) + [M]
    return list(zip(edges[:-1], edges[1:]))


def _attn_fwd(qh, kh, vb, cum_c, cum_r, name):
    M = qh.shape[0]
    H = qh.shape[1] // HEAD_DIM
    scale = 1.0 / math.sqrt(HEAD_DIM)

    def body(q_ref, k_ref, v_ref, cq_ref, ck_ref, o_ref, ot_ref, lc_ref, lr_ref):
        i = pl.program_id(1)

        def compute(n):
            s = lax.dot_general(q_ref[...], k_ref[0:n, :], NT_DIMS, preferred_element_type=F32) * scale
            s = s + (cq_ref[...] - ck_ref[:, 0:n])
            row = i * TQ + lax.broadcasted_iota(jnp.int32, (TQ, 1), 0)
            col = lax.broadcasted_iota(jnp.int32, (1, n), 1)
            s = jnp.where(row >= col, s, NEG)
            m = jnp.max(s, axis=1, keepdims=True)
            p = jnp.exp(s - m)
            l = jnp.sum(p, axis=1, keepdims=True)
            pn = (p / l).astype(BF16)
            o = jnp.dot(pn, v_ref[0:n, :], preferred_element_type=F32)
            o_ref[...] = o.astype(BF16)
            ot_ref[...] = o.T.astype(BF16)
            lse = m + jnp.log(l)
            lc_ref[...] = lse
            lr_ref[...] = _col_to_row(lse)

        for lo, hi in _causal_extents(M):
            pl.when((i >= lo // TQ) & (i < hi // TQ))(functools.partial(compute, hi))

    full = pl.BlockSpec((M, HEAD_DIM), lambda h, i: (0, h))
    tile = pl.BlockSpec((TQ, HEAD_DIM), lambda h, i: (i, h))
    colv = pl.BlockSpec((None, TQ, 1), lambda h, i: (h, i, 0))
    rowv_full = pl.BlockSpec((None, 1, M), lambda h, i: (h, 0, 0))
    rowv = pl.BlockSpec((None, 1, TQ), lambda h, i: (h, 0, i))
    return pl.pallas_call(
        body, name=name, grid=(H, M // TQ),
        in_specs=[tile, full, full, colv, rowv_full],
        out_specs=[tile, pl.BlockSpec((HEAD_DIM, TQ), lambda h, i: (h, i)), colv, rowv],
        out_shape=[jax.ShapeDtypeStruct((M, H * HEAD_DIM), BF16), jax.ShapeDtypeStruct((H * HEAD_DIM, M), BF16),
                   jax.ShapeDtypeStruct((H, M, 1), F32), jax.ShapeDtypeStruct((H, 1, M), F32)],
        compiler_params=_cparams("parallel", "parallel"),
    )(qh, kh, vb, cum_c, cum_r)


def _attn_bwd_q(qh, kh, vb, dob, cum_c, cum_r, lse_c, name):
    M = qh.shape[0]
    H = qh.shape[1] // HEAD_DIM
    scale = 1.0 / math.sqrt(HEAD_DIM)

    def body(q_ref, k_ref, v_ref, do_ref, cq_ref, ck_ref, l_ref, dq_ref, dr_ref, dcq_ref):
        i = pl.program_id(1)

        def compute(n):
            k = k_ref[0:n, :]
            s = lax.dot_general(q_ref[...], k, NT_DIMS, preferred_element_type=F32) * scale
            s = s + (cq_ref[...] - ck_ref[:, 0:n])
            row = i * TQ + lax.broadcasted_iota(jnp.int32, (TQ, 1), 0)
            col = lax.broadcasted_iota(jnp.int32, (1, n), 1)
            p = jnp.exp(jnp.where(row >= col, s, NEG) - l_ref[...])
            dp = lax.dot_general(do_ref[...], v_ref[0:n, :], NT_DIMS, preferred_element_type=F32)
            delta = jnp.sum(p * dp, axis=1, keepdims=True)
            ds = p * (dp - delta)
            dq_ref[...] = jnp.dot((ds * scale).astype(BF16), k, preferred_element_type=F32)
            dr_ref[...] = _col_to_row(delta)
            dcq_ref[...] = jnp.sum(ds, axis=1, keepdims=True)

        for lo, hi in _causal_extents(M):
            pl.when((i >= lo // TQ) & (i < hi // TQ))(functools.partial(compute, hi))

    full = pl.BlockSpec((M, HEAD_DIM), lambda h, i: (0, h))
    tile = pl.BlockSpec((TQ, HEAD_DIM), lambda h, i: (i, h))
    colv = pl.BlockSpec((None, TQ, 1), lambda h, i: (h, i, 0))
    rowv_full = pl.BlockSpec((None, 1, M), lambda h, i: (h, 0, 0))
    rowv = pl.BlockSpec((None, 1, TQ), lambda h, i: (h, 0, i))
    return pl.pallas_call(
        body, name=name, grid=(H, M // TQ),
        in_specs=[tile, full, full, tile, colv, rowv_full, colv],
        out_specs=[tile, rowv, colv],
        out_shape=[jax.ShapeDtypeStruct((M, H * HEAD_DIM), F32), jax.ShapeDtypeStruct((H, 1, M), F32),
                   jax.ShapeDtypeStruct((H, M, 1), F32)],
        compiler_params=_cparams("parallel", "parallel"),
    )(qh, kh, vb, dob, cum_c, cum_r, lse_c)


def _attn_bwd_kv(qh, kh, vb, dob, cum_c, cum_r, lse_r, delta_r, name):
    M = qh.shape[0]
    H = qh.shape[1] // HEAD_DIM
    scale = 1.0 / math.sqrt(HEAD_DIM)

    def body(k_ref, v_ref, q_ref, do_ref, cq_ref, ck_ref, l_ref, d_ref, dk_ref, dv_ref, dck_ref):
        j = pl.program_id(1)

        def compute(q0):
            q = q_ref[q0:M, :]
            do = do_ref[q0:M, :]
            st = lax.dot_general(k_ref[...], q, NT_DIMS, preferred_element_type=F32) * scale
            st = st + (cq_ref[:, q0:M] - ck_ref[...])
            krow = j * TQ + lax.broadcasted_iota(jnp.int32, (TQ, 1), 0)
            qcol = q0 + lax.broadcasted_iota(jnp.int32, (1, M - q0), 1)
            pt = jnp.exp(jnp.where(qcol >= krow, st, NEG) - l_ref[:, q0:M])
            dpt = lax.dot_general(v_ref[...], do, NT_DIMS, preferred_element_type=F32)
            dst = pt * (dpt - d_ref[:, q0:M])
            dv_ref[...] = jnp.dot(pt.astype(BF16), do, preferred_element_type=F32).astype(BF16)
            dk_ref[...] = jnp.dot((dst * scale).astype(BF16), q, preferred_element_type=F32)
            dck_ref[...] = -jnp.sum(dst, axis=1, keepdims=True)

        for lo, hi in _causal_extents(M):
            pl.when((j >= lo // TQ) & (j < hi // TQ))(functools.partial(compute, lo))

    full = pl.BlockSpec((M, HEAD_DIM), lambda h, j: (0, h))
    tile = pl.BlockSpec((TQ, HEAD_DIM), lambda h, j: (j, h))
    colv = pl.BlockSpec((None, TQ, 1), lambda h, j: (h, j, 0))
    rowv_full = pl.BlockSpec((None, 1, M), lambda h, j: (h, 0, 0))
    return pl.pallas_call(
        body, name=name, grid=(H, M // TQ),
        in_specs=[tile, tile, full, full, rowv_full, colv, rowv_full, rowv_full],
        out_specs=[tile, tile, colv],
        out_shape=[jax.ShapeDtypeStruct((M, H * HEAD_DIM), F32), jax.ShapeDtypeStruct((M, H * HEAD_DIM), BF16),
                   jax.ShapeDtypeStruct((H, M, 1), F32)],
        compiler_params=_cparams("parallel", "parallel"),
    )(kh, vb, qh, dob, cum_r, cum_c, lse_r, delta_r)


def _qk_norm_bwd(dqh, dkh, z, gq, gk, n_heads, q_col, name):
    M = z.shape[0]
    H = n_heads
    qb = q_col // HEAD_DIM

    def body(dqh_ref, dkh_ref, q_ref, k_ref, gq_ref, gk_ref, dq_ref, dk_ref, dgq_ref, dgk_ref):
        h = pl.program_id(0)

        def one(dy, xv, g):
            r = lax.rsqrt(jnp.mean(xv * xv, axis=-1, keepdims=True) + EPS)
            w = dy * g
            c = jnp.mean(w * xv, axis=-1, keepdims=True)
            dx = r * w - xv * (r * r * r * c)
            return dx.astype(BF16), jnp.sum(dy * (xv * r), axis=0, keepdims=True)

        dq, dgq = one(dqh_ref[...], q_ref[...], gq_ref[...])
        dk, dgk = one(dkh_ref[...], k_ref[...], gk_ref[...])
        dq_ref[...] = dq
        dk_ref[...] = dk

        @pl.when(h == 0)
        def _():
            dgq_ref[...] = dgq
            dgk_ref[...] = dgk

        @pl.when(h > 0)
        def _():
            dgq_ref[...] += dgq
            dgk_ref[...] += dgk

    vec = pl.BlockSpec((1, HEAD_DIM), lambda h: (0, 0))
    head = pl.BlockSpec((M, HEAD_DIM), lambda h: (0, h))
    return pl.pallas_call(
        body, name=name, grid=(H,),
        in_specs=[head, head, pl.BlockSpec((M, HEAD_DIM), lambda h: (0, qb + h)),
                  pl.BlockSpec((M, HEAD_DIM), lambda h: (0, qb + H + h)), vec, vec],
        out_specs=[head, head, vec, vec],
        out_shape=[jax.ShapeDtypeStruct((M, H * HEAD_DIM), BF16), jax.ShapeDtypeStruct((M, H * HEAD_DIM), BF16),
                   jax.ShapeDtypeStruct((1, HEAD_DIM), F32), jax.ShapeDtypeStruct((1, HEAD_DIM), F32)],
        compiler_params=_cparams("arbitrary"),
    )(dqh, dkh, z, z, gq, gk)


def _forget_bwd(dcq, dck, z, bpad, f_block, name):
    H, M, _ = dcq.shape

    def body(dcq_ref, dck_ref, f_ref, b_ref, dfl_ref, db_ref):
        lane = lax.broadcasted_iota(jnp.int32, (1, LANES), 1)
        d = jnp.zeros((M, LANES), F32)
        for h in range(H):
            d = d + (dcq_ref[h] + dck_ref[h]) * (lane == h).astype(F32)
        t = lax.broadcasted_iota(jnp.int32, (M, 1), 0)
        sh = 1
        while sh < M:
            d = d + jnp.where(t < M - sh, pltpu.roll(d, M - sh, 0), 0.0)
            sh *= 2
        xx = f_ref[...] + b_ref[...]
        dfl = d * (1.0 / (1.0 + jnp.exp(xx)))
        dfl_ref[...] = dfl.astype(BF16)
        db_ref[...] = jnp.sum(dfl, axis=0, keepdims=True)

    colv = pl.BlockSpec((H, M, 1), lambda i: (0, 0, 0))
    return pl.pallas_call(
        body, name=name, grid=(1,),
        in_specs=[colv, colv, pl.BlockSpec((M, LANES), lambda i: (0, f_block)),
                  pl.BlockSpec((1, LANES), lambda i: (0, 0))],
        out_specs=[pl.BlockSpec((M, LANES), lambda i: (0, 0)), pl.BlockSpec((1, LANES), lambda i: (0, 0))],
        out_shape=[jax.ShapeDtypeStruct((M, LANES), BF16), jax.ShapeDtypeStruct((1, LANES), F32)],
        compiler_params=_cparams("arbitrary"),
    )(dcq, dck, z, bpad)


def _ffn_fwd(h, g, get_weights, tag):
    n, nt, r = _rmsnorm_fwd(h, g, f"{tag}_norm")
    up = get_weights(f"{tag}_up", n)
    a, b, s, st = _ffn_up(n, up["wg"], up["wu"], 256, f"{tag}_up")
    wd = get_weights(f"{tag}_down", s)["wd"]
    h_out = _mm_nn_residual(s, wd, h, 0.5, 512, wd.shape[0] // 4, f"{tag}_down")
    return h_out, (nt, r, a, b, st, up["wg"], up["wu"], wd)


def _ffn_bwd(dh, dhb, h, g, saved, dep, put_grads, tag):
    nt, r, a, b, st, wg, wu, wd = saved
    n_shards = 4
    da, db = _ffn_bwd_hidden(dhb, wd, a, b, dep, 256, f"{tag}_bwd_hidden")
    dwd = _mm_tn(st, dhb, dep, 0.5, st.shape[0] // n_shards, 1024, f"{tag}_dw_down", stacked=False)
    dep = put_grads(f"{tag}_w_down", dwd)
    dwg = _mm_tn(nt, da, dep, 1.0, 1024, wg.shape[1] // n_shards, f"{tag}_dw_gate", stacked=True)
    dep = put_grads(f"{tag}_w_gate", dwg)
    dwu = _mm_tn(nt, db, dep, 1.0, 1024, wu.shape[1] // n_shards, f"{tag}_dw_up", stacked=True)
    dep = put_grads(f"{tag}_w_up", dwu)
    dn = _mm_nt_sum([da, db], [wg, wu], dep, 512, wg.shape[1] // 4, f"{tag}_dn")
    dh_in, dhb_in, dg = _rmsnorm_bwd(dn, h, r, g, dh, f"{tag}_norm_bwd")
    return dh_in, dhb_in, dg


def _local_step(x, target, S, get_weights, put_grads):
    seq, D = x.shape
    L = N_META + seq
    Lp = -(-L // SEQ_ALIGN) * SEQ_ALIGN
    pad = jnp.zeros((Lp - L, D), F32)
    tgt = jnp.concatenate([jnp.zeros((N_META, D), F32), target, pad], axis=0)

    d_pool = S["pool_scale"].shape[1]
    n_heads = S["b_forget"].shape[1]
    d_att = n_heads * HEAD_DIM
    f_col = d_pool + 3 * d_att
    f_block = f_col // LANES
    bpad = jnp.pad(S["b_forget"], ((0, 0), (0, LANES - n_heads)))

    h0 = jnp.concatenate([get_weights("meta", None)["meta"], x, pad], axis=0)
    h1, ffn1 = _ffn_fwd(h0, S["ffn1_norm"], get_weights, "ffn1")
    Wm = get_weights("mix", h1)
    u, ut, r_mix = _rmsnorm_fwd(h1, S["mix_norm"], "mix_norm")
    z = _mm_nn(u, Wm["win"], 384, "in_proj")
    pooled, pool_out, pool_out_t = _pool_fwd(z, Wm["pool_w"], S["pool_scale"], "pool_fwd")
    qh, kh, vb = _qkv_prep(z, S["q_norm"], S["k_norm"], n_heads, d_pool, "qkv_prep")
    cum_t = _forget_fwd(z, bpad, f_block, "forget_fwd")[:n_heads]
    cum_c = cum_t.reshape(n_heads, Lp, 1)
    cum_r = cum_t.reshape(n_heads, 1, Lp)
    att, att_t, lse_c, lse_r = _attn_fwd(qh, kh, vb, cum_c, cum_r, "attn_fwd")
    mix = jnp.concatenate([pool_out, att], axis=1)
    mix_t = jnp.concatenate([pool_out_t, att_t], axis=0)
    h2 = _mm_nn_residual(mix, Wm["wout"], h1, 1.0, 512, 1024, "out_proj")
    h3, ffn2 = _ffn_fwd(h2, S["ffn2_norm"], get_weights, "ffn2")

    dh3, dh3b, loss = _loss_grad(h3, tgt, seq, "loss")
    dh2, dh2b, dg_ffn2 = _ffn_bwd(dh3, dh3b, h2, S["ffn2_norm"], ffn2, loss, put_grads, "ffn2")

    dmix = _mm_nt(dh2b, Wm["wout"], loss, 512, "out_proj_bwd")
    dwout = _mm_tn(mix_t, dh2b, loss, 1.0, 1024, 1024, "dw_out", stacked=False)
    dp, dpw, dpsc = _pool_bwd(dmix, pooled, Wm["pool_w"], S["pool_scale"], "pool_bwd")
    dob = dmix[:, d_pool:].astype(BF16)
    dqh, delta_r, dcq = _attn_bwd_q(qh, kh, vb, dob, cum_c, cum_r, lse_c, "attn_bwd_q")
    dkh, dv, dck = _attn_bwd_kv(qh, kh, vb, dob, cum_c, cum_r, lse_r, delta_r, "attn_bwd_kv")
    dq, dk, dgq, dgk = _qk_norm_bwd(dqh, dkh, z, S["q_norm"], S["k_norm"], n_heads, d_pool, "qk_norm_bwd")
    dfl, dbf = _forget_bwd(dcq, dck, z, bpad, f_block, "forget_bwd")
    dz = jnp.concatenate([dp, dq, dk, dv, dfl], axis=1)
    dwin = _mm_tn(ut, dz, loss, 1.0, 1024, dz.shape[1] // 3, "dw_in", stacked=False)
    dep = put_grads("mix", dict(win=dwin, wout=dwout, pool_w=dpw))
    du = _mm_nt_sum([dz], [Wm["win"]], dep, 512, Wm["win"].shape[1] // 3, "in_proj_bwd")
    dh1, dh1b, dg_mix = _rmsnorm_bwd(du, h1, r_mix, S["mix_norm"], dh2, "mix_norm_bwd")

    dh0, _, dg_ffn1 = _ffn_bwd(dh1, dh1b, h0, S["ffn1_norm"], ffn1, loss, put_grads, "ffn1")

    grads = dict(
        x=dh0[N_META:L], meta=dh0[:N_META],
        ffn1_norm=dg_ffn1, mix_norm=dg_mix, ffn2_norm=dg_ffn2, q_norm=dgq, k_norm=dgk,
        b_forget=dbf[:, :n_heads], pool_scale=dpsc,
    )
    return loss[0, 0], dh0, grads


HBM_SPEC = pl.BlockSpec(memory_space=pltpu.HBM)
N_CHIPS = 4


def _chip_peers():
    x, y, c = lax.axis_index("x"), lax.axis_index("y"), lax.axis_index("c")
    flips = [(1 - x, y), (x, 1 - y), (1 - x, 1 - y)]
    return 2 * x + y, [((px, py, c), 2 * px + py) for px, py in flips]


def _gathered_shape(shape, layout):
    if layout == "rows":
        return (N_CHIPS * shape[0],) + shape[1:]
    if layout == "cols":
        return (shape[0], N_CHIPS * shape[1])
    return (N_CHIPS,) + shape


def _cast_place(place, w, dep, layout, dtype, name):
    R, C = w.shape
    tr = _row_tile(R, C)
    nt = R // tr

    def body(place_ref, w_ref, dep_ref, o_ref):
        o_ref[...] = w_ref[...].astype(dtype)

    if layout == "rows":
        ospec = pl.BlockSpec((tr, C), lambda i, p: (p[1] * nt + i, 0))
    elif layout == "cols":
        ospec = pl.BlockSpec((tr, C), lambda i, p: (i, p[1]))
    else:
        ospec = pl.BlockSpec((None, tr, C), lambda i, p: (p[1], i, 0))
    return pl.pallas_call(
        body, name=name,
        grid_spec=pltpu.PrefetchScalarGridSpec(
            num_scalar_prefetch=1, grid=(nt,),
            in_specs=[pl.BlockSpec((tr, C), lambda i, p: (i, 0)), pl.BlockSpec(memory_space=pl.ANY)],
            out_specs=ospec),
        out_shape=jax.ShapeDtypeStruct(_gathered_shape((R, C), layout), dtype),
        compiler_params=_cparams("parallel"),
    )(place, w, dep)


SEM_SPEC = pl.BlockSpec(memory_space=pltpu.SEMAPHORE)
ANY_SPEC = pl.BlockSpec(memory_space=pl.ANY)
SPLIT_COPY = pltpu.CompilerParams(has_side_effects=pltpu.SideEffectType.DATAFLOW_SIDE_EFFECTING)


def _hbm(a):
    return pltpu.with_memory_space_constraint(a, pltpu.HBM)


def _gather_region(refs, shard_shapes, layouts, a, chip, half):
    rows_a = shard_shapes[a][0]
    h = rows_a // 2
    if layouts[a] == "rows":
        return refs[a].at[pl.ds(chip * rows_a + half * h, h)]
    if layouts[a] == "cols":
        cols_a = shard_shapes[a][1]
        return refs[a].at[pl.ds(half * h, h), pl.ds(chip * cols_a, cols_a)]
    return refs[a].at[chip, pl.ds(half * h, h)]


def _gather_start(bufs, after, shard_shapes, layouts, name):
    n = len(bufs)
    ns = 3 * n

    def body(*refs):
        in_refs = refs[:n]
        send_sems = refs[n + 1:n + 1 + ns]
        recv_sems = refs[n + 1 + ns:n + 1 + 2 * ns]
        token = refs[2 * n + 1 + 2 * ns]
        c = lax.axis_index("c")
        me, peers = _chip_peers()
        for a in range(n):
            mine = _gather_region(in_refs, shard_shapes, layouts, a, me, c)
            for k, (dev, _) in enumerate(peers):
                pltpu.make_async_remote_copy(
                    src_ref=mine, dst_ref=mine, send_sem=send_sems[3 * a + k], recv_sem=recv_sems[3 * a + k],
                    device_id=dev, device_id_type=MESH).start()
        token[...] = jnp.zeros_like(token)

    sem = pltpu.SemaphoreType.DMA(())
    out = pl.pallas_call(
        body, name=name,
        out_shape=(*[sem] * (2 * ns), *[pltpu.HBM(b.shape, b.dtype) for b in bufs],
                   jax.ShapeDtypeStruct((8, LANES), F32)),
        in_specs=[HBM_SPEC] * n + [ANY_SPEC],
        out_specs=(*[SEM_SPEC] * (2 * ns), *[HBM_SPEC] * n, pl.BlockSpec(memory_space=pltpu.VMEM)),
        input_output_aliases={a: 2 * ns + a for a in range(n)},
        compiler_params=SPLIT_COPY,
    )(*[_hbm(b) for b in bufs], after)
    return list(out[:ns]), list(out[ns:2 * ns]), list(out[2 * ns:2 * ns + n]), out[2 * ns + n]


def _gather_wait(bufs, send_sems, recv_sems, afters, shard_shapes, layouts, name):
    n = len(bufs)
    ns = 3 * n
    na = len(afters)

    def body(*refs):
        in_refs = refs[:n]
        send_sems = refs[n:n + ns]
        recv_sems = refs[n + ns:n + 2 * ns]
        token = refs[2 * n + 2 * ns + na]
        token[...] = jnp.zeros_like(token)
        c = lax.axis_index("c")
        me, peers = _chip_peers()
        for a in range(n):
            mine = _gather_region(in_refs, shard_shapes, layouts, a, me, c)
            for k, (dev, pidx) in enumerate(peers):
                landed = _gather_region(in_refs, shard_shapes, layouts, a, pidx, c)
                pltpu.make_async_remote_copy(
                    src_ref=mine, dst_ref=landed, send_sem=send_sems[3 * a + k], recv_sem=recv_sems[3 * a + k],
                    device_id=dev, device_id_type=MESH).wait_recv()
        for a in range(n):
            mine = _gather_region(in_refs, shard_shapes, layouts, a, me, c)
            for k, (dev, _) in enumerate(peers):
                pltpu.make_async_remote_copy(
                    src_ref=mine, dst_ref=mine, send_sem=send_sems[3 * a + k], recv_sem=recv_sems[3 * a + k],
                    device_id=dev, device_id_type=MESH).wait_send()

    out = pl.pallas_call(
        body, name=name,
        out_shape=(*[pltpu.HBM(b.shape, b.dtype) for b in bufs], jax.ShapeDtypeStruct((8, LANES), F32)),
        in_specs=[HBM_SPEC] * n + [SEM_SPEC] * (2 * ns) + [ANY_SPEC] * na,
        out_specs=(*[HBM_SPEC] * n, pl.BlockSpec(memory_space=pltpu.VMEM)),
        input_output_aliases={a: a for a in range(n)},
        compiler_params=SPLIT_COPY,
    )(*bufs, *send_sems, *recv_sems, *afters)
    return list(out[:n]), out[n]


def _gather_forward(bufs, dep, shard_shapes, layouts, name):
    n = len(bufs)

    def body(*refs):
        out_refs = refs[n + 1:2 * n + 1]
        send_sems, recv_sems = refs[2 * n + 1:]
        c = lax.axis_index("c")
        sib = (lax.axis_index("x"), lax.axis_index("y"), 1 - c)
        _, peers = _chip_peers()
        sends = []
        for a in range(n):
            for k, (_, pidx) in enumerate(peers):
                landed = _gather_region(out_refs, shard_shapes, layouts, a, pidx, c)
                cp = pltpu.make_async_remote_copy(
                    src_ref=landed, dst_ref=landed, send_sem=send_sems.at[a, k], recv_sem=recv_sems.at[a, k],
                    device_id=sib, device_id_type=MESH)
                cp.start()
                sends.append(cp)
        for a in range(n):
            for k, (_, pidx) in enumerate(peers):
                other = _gather_region(out_refs, shard_shapes, layouts, a, pidx, 1 - c)
                pltpu.make_async_remote_copy(
                    src_ref=other, dst_ref=other, send_sem=send_sems.at[a, k], recv_sem=recv_sems.at[a, k],
                    device_id=sib, device_id_type=MESH).wait_recv()
        for cp in sends:
            cp.wait_send()

    sem = pltpu.SemaphoreType.DMA((n, 3))
    return pl.pallas_call(
        body, name=name,
        in_specs=[HBM_SPEC] * n + [ANY_SPEC], out_specs=[HBM_SPEC] * n,
        out_shape=[jax.ShapeDtypeStruct(b.shape, b.dtype) for b in bufs],
        input_output_aliases={a: a for a in range(n)},
        scratch_shapes=[sem, sem],
    )(*bufs, dep)


def _send_sibling_halves(stacked, name):
    n = len(stacked)

    def body(*refs):
        in_refs = refs[:n]
        out_refs = refs[n:2 * n]
        send_sems, recv_sems = refs[2 * n:]
        c = lax.axis_index("c")
        sib = (lax.axis_index("x"), lax.axis_index("y"), 1 - c)
        sends = []
        for a in range(n):
            h = stacked[a].shape[1] // 2
            cp = pltpu.make_async_remote_copy(
                src_ref=in_refs[a].at[:, pl.ds((1 - c) * h, h)], dst_ref=out_refs[a], send_sem=send_sems.at[a],
                recv_sem=recv_sems.at[a], device_id=sib, device_id_type=MESH)
            cp.start()
            sends.append(cp)
        for cp in sends:
            cp.wait_recv()
        for cp in sends:
            cp.wait_send()

    return pl.pallas_call(
        body, name=name,
        in_specs=[HBM_SPEC] * n, out_specs=[HBM_SPEC] * n,
        out_shape=[jax.ShapeDtypeStruct((s.shape[0], s.shape[1] // 2, s.shape[2]), s.dtype) for s in stacked],
        scratch_shapes=[pltpu.SemaphoreType.DMA((n,)), pltpu.SemaphoreType.DMA((n,))],
    )(*stacked)


def _scatter_start(stacked, name):
    n = len(stacked)
    ns = 3 * n
    lands = [lax.empty((3,) + s.shape[1:], s.dtype) for s in stacked]

    def body(*refs):
        src_refs = refs[:n]
        land_refs = refs[n:2 * n]
        send_sems = refs[2 * n:2 * n + ns]
        recv_sems = refs[2 * n + ns:2 * n + 2 * ns]
        token = refs[4 * n + 2 * ns]
        _, peers = _chip_peers()
        for a in range(n):
            for k, (dev, pidx) in enumerate(peers):
                pltpu.make_async_remote_copy(
                    src_ref=src_refs[a].at[pidx], dst_ref=land_refs[a].at[k], send_sem=send_sems[3 * a + k],
                    recv_sem=recv_sems[3 * a + k], device_id=dev, device_id_type=MESH).start()
        token[...] = jnp.zeros_like(token)

    sem = pltpu.SemaphoreType.DMA(())
    out = pl.pallas_call(
        body, name=name,
        out_shape=(*[sem] * (2 * ns), *[pltpu.HBM(b.shape, b.dtype) for b in stacked],
                   *[pltpu.HBM(b.shape, b.dtype) for b in lands], jax.ShapeDtypeStruct((8, LANES), F32)),
        in_specs=[HBM_SPEC] * (2 * n),
        out_specs=(*[SEM_SPEC] * (2 * ns), *[HBM_SPEC] * (2 * n), pl.BlockSpec(memory_space=pltpu.VMEM)),
        input_output_aliases={a: 2 * ns + a for a in range(2 * n)},
        compiler_params=SPLIT_COPY,
    )(*[_hbm(b) for b in stacked], *[_hbm(b) for b in lands])
    o = 2 * ns
    return list(out[:ns]), list(out[ns:o]), list(out[o:o + n]), list(out[o + n:o + 2 * n]), out[o + 2 * n]


def _scatter_wait(srcs, lands, send_sems, recv_sems, after, name):
    n = len(srcs)
    ns = 3 * n

    def body(*refs):
        src_refs = refs[:n]
        land_refs = refs[n:2 * n]
        send_sems = refs[2 * n:2 * n + ns]
        recv_sems = refs[2 * n + ns:2 * n + 2 * ns]
        _, peers = _chip_peers()
        copies = [
            pltpu.make_async_remote_copy(
                src_ref=src_refs[a].at[pidx], dst_ref=land_refs[a].at[k], send_sem=send_sems[3 * a + k],
                recv_sem=recv_sems[3 * a + k], device_id=dev, device_id_type=MESH)
            for a in range(n) for k, (dev, pidx) in enumerate(peers)]
        for cp in copies:
            cp.wait_recv()
        for cp in copies:
            cp.wait_send()

    out = pl.pallas_call(
        body, name=name,
        out_shape=tuple(pltpu.HBM(b.shape, b.dtype) for b in list(srcs) + list(lands)),
        in_specs=[HBM_SPEC] * (2 * n) + [SEM_SPEC] * (2 * ns) + [ANY_SPEC],
        out_specs=tuple([HBM_SPEC] * (2 * n)),
        input_output_aliases={a: a for a in range(2 * n)},
        compiler_params=SPLIT_COPY,
    )(*srcs, *lands, *send_sems, *recv_sems, after)
    return list(out[n:])


def _swap_sibling(arrays, name):
    n = len(arrays)

    def body(*refs):
        in_refs = refs[:n]
        out_refs = refs[n:2 * n]
        send_sems, recv_sems = refs[2 * n:]
        sib = (lax.axis_index("x"), lax.axis_index("y"), 1 - lax.axis_index("c"))
        sends = []
        for a in range(n):
            cp = pltpu.make_async_remote_copy(
                src_ref=in_refs[a], dst_ref=out_refs[a], send_sem=send_sems.at[a], recv_sem=recv_sems.at[a],
                device_id=sib, device_id_type=MESH)
            cp.start()
            sends.append(cp)
        for cp in sends:
            cp.wait_recv()
        for cp in sends:
            cp.wait_send()

    return pl.pallas_call(
        body, name=name,
        in_specs=[HBM_SPEC] * n, out_specs=[HBM_SPEC] * n,
        out_shape=[jax.ShapeDtypeStruct(s.shape, s.dtype) for s in arrays],
        scratch_shapes=[pltpu.SemaphoreType.DMA((n,)), pltpu.SemaphoreType.DMA((n,))],
    )(*arrays)


def _all_reduce_small(v):
    R, C = v.shape
    n_dev = 8

    def body(v_ref, o_ref, buf, send_sems, recv_sems):
        x, y, c = lax.axis_index("x"), lax.axis_index("y"), lax.axis_index("c")
        me = 4 * x + 2 * y + c
        buf[me] = v_ref[...]
        sends = []
        for k in range(1, n_dev):
            px, py, pc = x ^ ((k >> 2) & 1), y ^ ((k >> 1) & 1), c ^ (k & 1)
            cp = pltpu.make_async_remote_copy(
                src_ref=v_ref, dst_ref=buf.at[me], send_sem=send_sems.at[k - 1], recv_sem=recv_sems.at[k - 1],
                device_id=(px, py, pc), device_id_type=MESH)
            cp.start()
            sends.append((cp, 4 * px + 2 * py + pc))
        for k in range(1, n_dev):
            cp, pidx = sends[k - 1]
            pltpu.make_async_remote_copy(
                src_ref=v_ref, dst_ref=buf.at[pidx], send_sem=send_sems.at[k - 1], recv_sem=recv_sems.at[k - 1],
                device_id=(x, y, c), device_id_type=MESH).wait_recv()
        for cp, _ in sends:
            cp.wait_send()
        acc = buf[0]
        for d in range(1, n_dev):
            acc = acc + buf[d]
        o_ref[...] = acc

    vm = pl.BlockSpec(memory_space=pltpu.VMEM)
    return pl.pallas_call(
        body, name="all_reduce_small",
        in_specs=[vm], out_specs=vm,
        out_shape=jax.ShapeDtypeStruct((R, C), F32),
        scratch_shapes=[pltpu.VMEM((n_dev, R, C), F32), pltpu.SemaphoreType.DMA((n_dev - 1,)),
                        pltpu.SemaphoreType.DMA((n_dev - 1,))],
    )(v)


def _pair_sum(place, own, sib, name):
    S, R, C = own.shape
    h = R // 2
    tr = _row_tile(h, C)
    nt = h // tr

    def body(place_ref, o_ref, s_ref, out_ref):
        out_ref[...] = (o_ref[...].astype(F32) + s_ref[...].astype(F32)).astype(BF16)

    return pl.pallas_call(
        body, name=name,
        grid_spec=pltpu.PrefetchScalarGridSpec(
            num_scalar_prefetch=1, grid=(S, nt),
            in_specs=[pl.BlockSpec((None, tr, C), lambda s, i, p: (s, p[0] * nt + i, 0)),
                      pl.BlockSpec((None, tr, C), lambda s, i, p: (s, i, 0))],
            out_specs=pl.BlockSpec((None, tr, C), lambda s, i, p: (s, i, 0))),
        out_shape=jax.ShapeDtypeStruct((S, h, C), BF16),
        compiler_params=_cparams("parallel", "parallel"),
    )(place, own, sib)


def _sum_slabs(place, own, sib, recv, name):
    S, R, C = own.shape
    h = R // 2
    tr = _row_tile(h, C)
    nt = h // tr

    def body(place_ref, o_ref, s_ref, r_ref, out_ref):
        acc = o_ref[...].astype(F32) + s_ref[...].astype(F32)
        for k in range(3):
            acc = acc + r_ref[k].astype(F32)
        out_ref[...] = acc

    return pl.pallas_call(
        body, name=name,
        grid_spec=pltpu.PrefetchScalarGridSpec(
            num_scalar_prefetch=1, grid=(nt,),
            in_specs=[pl.BlockSpec((None, tr, C), lambda i, p: (p[1], p[0] * nt + i, 0)),
                      pl.BlockSpec((None, tr, C), lambda i, p: (p[1], i, 0)),
                      pl.BlockSpec((3, tr, C), lambda i, p: (0, i, 0))],
            out_specs=pl.BlockSpec((tr, C), lambda i, p: (i, 0))),
        out_shape=jax.ShapeDtypeStruct((h, C), F32),
        compiler_params=_cparams("parallel"),
    )(place, own, sib, recv)


def _adamw(parts, w, m, v, name):
    R, C = w.shape
    tr = _row_tile(R, C)
    npart = len(parts)
    c1 = 1.0 - ADAM_B1 ** ADAM_STEP
    c2 = 1.0 - ADAM_B2 ** ADAM_STEP

    def body(*refs):
        p_refs = refs[:npart]
        w_ref, m_ref, v_ref, g_ref, d_ref, nm_ref, nv_ref = refs[npart:]
        g = p_refs[0][...]
        for p_ref in p_refs[1:]:
            g = g + p_ref[...]
        nm = ADAM_B1 * m_ref[...] + (1.0 - ADAM_B1) * g
        nv = ADAM_B2 * v_ref[...] + (1.0 - ADAM_B2) * (g * g)
        m_hat = nm / c1
        v_hat = nv / c2
        g_ref[...] = g
        d_ref[...] = -ADAM_LR * (m_hat / (jnp.sqrt(v_hat) + ADAM_EPS) + ADAM_WD * w_ref[...])
        nm_ref[...] = nm
        nv_ref[...] = nv

    blk = pl.BlockSpec((tr, C), lambda i: (i, 0))
    shape = jax.ShapeDtypeStruct((R, C), F32)
    return pl.pallas_call(
        body, name=name, grid=(R // tr,),
        in_specs=[blk] * (npart + 3), out_specs=[blk] * 4, out_shape=[shape] * 4,
        compiler_params=_cparams("parallel"),
    )(*parts, w, m, v)


def _adamw_halves(place, mine, other, w, m, v, name):
    R, C = w.shape
    h = R // 2
    tr = _row_tile(h, C)
    nt = h // tr
    c1 = 1.0 - ADAM_B1 ** ADAM_STEP
    c2 = 1.0 - ADAM_B2 ** ADAM_STEP

    def body(place_ref, mine_ref, other_ref, w_ref, m_ref, v_ref, g_ref, d_ref, nm_ref, nv_ref):
        is_mine = (pl.program_id(0) // nt) == place_ref[0]
        g = jnp.where(is_mine, mine_ref[...], other_ref[...])
        nm = ADAM_B1 * m_ref[...] + (1.0 - ADAM_B1) * g
        nv = ADAM_B2 * v_ref[...] + (1.0 - ADAM_B2) * (g * g)
        m_hat = nm / c1
        v_hat = nv / c2
        g_ref[...] = g
        d_ref[...] = -ADAM_LR * (m_hat / (jnp.sqrt(v_hat) + ADAM_EPS) + ADAM_WD * w_ref[...])
        nm_ref[...] = nm
        nv_ref[...] = nv

    def half_block(which):
        def index(i, p):
            first = p[0] if which == 0 else 1 - p[0]
            return jnp.clip(i - first * nt, 0, nt - 1), 0

        return pl.BlockSpec((tr, C), index)

    blk = pl.BlockSpec((tr, C), lambda i, p: (i, 0))
    shape = jax.ShapeDtypeStruct((R, C), F32)
    return pl.pallas_call(
        body, name=name,
        grid_spec=pltpu.PrefetchScalarGridSpec(
            num_scalar_prefetch=1, grid=(2 * nt,),
            in_specs=[half_block(0), half_block(1), blk, blk, blk], out_specs=[blk] * 4),
        out_shape=[shape] * 4,
        compiler_params=_cparams("parallel"),
    )(place, mine, other, w, m, v)


SMALL_NAMES = ("ffn1_norm", "mix_norm", "ffn2_norm", "pool_scale", "q_norm", "k_norm", "b_forget")
SMALL_COLS = 1024
LOSS_LANE = 512


def _pack_small(vals):
    rows = [vals[n].reshape(-1, SMALL_COLS) for n in ("ffn1_norm", "mix_norm", "ffn2_norm", "pool_scale")]
    tail = jnp.concatenate([vals["q_norm"].reshape(-1), vals["k_norm"].reshape(-1), vals["b_forget"].reshape(-1)])
    rows.append(jnp.pad(tail, (0, SMALL_COLS - tail.shape[0])).reshape(1, SMALL_COLS))
    return jnp.concatenate(rows, axis=0)


def _unpack_small(packed, like):
    out = {}
    r = 0
    for n in ("ffn1_norm", "mix_norm", "ffn2_norm", "pool_scale"):
        k = like[n].size // SMALL_COLS
        out[n] = packed[r:r + k].reshape(like[n].shape)
        r += k
    o = 0
    for n in ("q_norm", "k_norm", "b_forget"):
        k = like[n].size
        out[n] = packed[r, o:o + k].reshape(like[n].shape)
        o += k
    return out


def kernel(x, meta_tokens, ffn1_norm, ffn1_w_gate, ffn1_w_up, ffn1_w_down, mix_norm, w_in, b_forget, q_norm, k_norm, pool_w, pool_scale, w_out, ffn2_norm, ffn2_w_gate, ffn2_w_up, ffn2_w_down, loss_target, m_meta_tokens, m_ffn1_norm, m_ffn1_w_gate, m_ffn1_w_up, m_ffn1_w_down, m_mix_norm, m_w_in, m_b_forget, m_q_norm, m_k_norm, m_pool_w, m_pool_scale, m_w_out, m_ffn2_norm, m_ffn2_w_gate, m_ffn2_w_up, m_ffn2_w_down, v_meta_tokens, v_ffn1_norm, v_ffn1_w_gate, v_ffn1_w_up, v_ffn1_w_down, v_mix_norm, v_w_in, v_b_forget, v_q_norm, v_k_norm, v_pool_w, v_pool_scale, v_w_out, v_ffn2_norm, v_ffn2_w_gate, v_ffn2_w_up, v_ffn2_w_down):
    wts = dict(meta_tokens=meta_tokens, ffn1_norm=ffn1_norm, ffn1_w_gate=ffn1_w_gate, ffn1_w_up=ffn1_w_up,
               ffn1_w_down=ffn1_w_down, mix_norm=mix_norm, w_in=w_in, b_forget=b_forget, q_norm=q_norm,
               k_norm=k_norm, pool_w=pool_w, pool_scale=pool_scale, w_out=w_out, ffn2_norm=ffn2_norm,
               ffn2_w_gate=ffn2_w_gate, ffn2_w_up=ffn2_w_up, ffn2_w_down=ffn2_w_down)
    mom = dict(meta_tokens=m_meta_tokens, ffn1_norm=m_ffn1_norm, ffn1_w_gate=m_ffn1_w_gate, ffn1_w_up=m_ffn1_w_up,
               ffn1_w_down=m_ffn1_w_down, mix_norm=m_mix_norm, w_in=m_w_in, b_forget=m_b_forget, q_norm=m_q_norm,
               k_norm=m_k_norm, pool_w=m_pool_w, pool_scale=m_pool_scale, w_out=m_w_out, ffn2_norm=m_ffn2_norm,
               ffn2_w_gate=m_ffn2_w_gate, ffn2_w_up=m_ffn2_w_up, ffn2_w_down=m_ffn2_w_down)
    var = dict(meta_tokens=v_meta_tokens, ffn1_norm=v_ffn1_norm, ffn1_w_gate=v_ffn1_w_gate, ffn1_w_up=v_ffn1_w_up,
               ffn1_w_down=v_ffn1_w_down, mix_norm=v_mix_norm, w_in=v_w_in, b_forget=v_b_forget, q_norm=v_q_norm,
               k_norm=v_k_norm, pool_w=v_pool_w, pool_scale=v_pool_scale, w_out=v_w_out, ffn2_norm=v_ffn2_norm,
               ffn2_w_gate=v_ffn2_w_gate, ffn2_w_up=v_ffn2_w_up, ffn2_w_down=v_ffn2_w_down)
    order = list(wts)
    me = 2 * lax.axis_index("x") + lax.axis_index("y")

    D = x.shape[2]
    d_in_shard = w_in.shape[2]
    d_in = N_CHIPS * d_in_shard
    n_heads = b_forget.shape[1]
    d_in_pad = (d_in - n_heads) + LANES

    stages = dict(meta=("meta_tokens",), ffn1_up=("ffn1_w_gate", "ffn1_w_up"), ffn1_down=("ffn1_w_down",),
                  mix=("w_in", "w_out", "pool_w"),
                  ffn2_up=("ffn2_w_gate", "ffn2_w_up"), ffn2_down=("ffn2_w_down",))
    stage_order = list(stages)
    place = jnp.stack([lax.axis_index("c"), me]).astype(jnp.int32)
    layouts = dict(ffn1_w_gate="cols", ffn1_w_up="cols", ffn1_w_down="rows", w_in="stack", w_out="rows",
                   pool_w="stack", ffn2_w_gate="cols", ffn2_w_up="cols", ffn2_w_down="rows", meta_tokens="stack")
    shards2d = {n: wts[n].reshape(-1, wts[n].shape[-1]) for n in layouts}

    def place_stage(stage, dep):
        return [_cast_place(place, shards2d[n], dep, layouts[n], F32 if n == "meta_tokens" else BF16, f"place_{n}")
                for n in stages[stage]]

    def start_stage(stage, bufs, after):
        shapes = [shards2d[n].shape for n in stages[stage]]
        lays = [layouts[n] for n in stages[stage]]
        return _gather_start(bufs, after, shapes, lays, f"gather_start_{stage}") + (shapes, lays)

    flight = {stage_order[0]: start_stage(stage_order[0], place_stage(stage_order[0], place), place)}
    placed = {stage_order[1]: place_stage(stage_order[1], flight[stage_order[0]][3])}

    def cols(st):
        return jnp.transpose(st, (1, 0, 2)).reshape(st.shape[1], -1)

    def get_weights(stage, after):
        k = stage_order.index(stage)
        send_sems, recv_sems, bufs, _, shapes, lays = flight.pop(stage)
        afters = ([] if after is None else [after]) + [b for st in placed for b in placed[st]]
        landed, token = _gather_wait(bufs, send_sems, recv_sems, afters, shapes, lays, f"gather_wait_{stage}")
        if k + 1 < len(stage_order):
            flight[stage_order[k + 1]] = start_stage(stage_order[k + 1], placed.pop(stage_order[k + 1]), token)
            token = flight[stage_order[k + 1]][3]
        if k == 0:
            placed.update({st: place_stage(st, token) for st in stage_order[2:]})
        G = dict(zip(stages[stage], _gather_forward(landed, token, shapes, lays, f"gather_forward_{stage}")))
        if stage == "meta":
            return dict(meta=cols(G["meta_tokens"]))
        if stage == "ffn1_up":
            return dict(wg=G["ffn1_w_gate"], wu=G["ffn1_w_up"])
        if stage == "ffn2_up":
            return dict(wg=G["ffn2_w_gate"], wu=G["ffn2_w_up"])
        if stage != "mix":
            return dict(wd=G[stages[stage][0]])
        return dict(
            win=jnp.pad(cols(G["w_in"]), ((0, 0), (0, d_in_pad - d_in))), wout=G["w_out"],
            pool_w=jnp.transpose(G["pool_w"].reshape((N_CHIPS,) + pool_w.shape[1:]), (1, 0, 2, 3)).reshape(
                N_POOL_GROUPS, pool_w.shape[3], pool_w.shape[3]))

    def split_rows(a):
        return a.reshape(N_CHIPS, -1, a.shape[1])

    def split_win(a):
        return jnp.transpose(a[:, :d_in].reshape(D, N_CHIPS, d_in_shard), (1, 0, 2))

    def split_pool(a):
        r, c = pool_w.shape[2], pool_w.shape[3]
        return jnp.transpose(a.reshape(N_POOL_GROUPS, N_CHIPS, r, c), (1, 0, 2, 3)).reshape(N_CHIPS, -1, c)

    scatter = {}

    def put_grads(name, g):
        if name == "mix":
            names = stages["mix"]
            own = [split_win(g["win"]), split_rows(g["wout"]), split_pool(g["pool_w"])]
        else:
            names = (name,)
            own = [split_rows(g) if name.endswith("_down") else g]
        from_sib = _send_sibling_halves(own, f"send_halves_{name}")
        pair = [_pair_sum(place, o, s, f"pair_sum_{n}") for n, o, s in zip(names, own, from_sib)]
        send_sems, recv_sems, srcs, lands, token = _scatter_start(pair, f"scatter_start_{name}")
        scatter[name] = (names, own, from_sib, send_sems, recv_sems, srcs, lands, token)
        return token

    small = dict(ffn1_norm=ffn1_norm, mix_norm=mix_norm, ffn2_norm=ffn2_norm, q_norm=q_norm, k_norm=k_norm,
                 b_forget=b_forget, pool_scale=pool_scale)
    loss_part, dh0, gr = _local_step(x[0], loss_target[0], small, get_weights, put_grads)

    out_g, out_d, out_m, out_v = {}, {}, {}, {}
    after = list(scatter.values())[-1][7]
    for stage in scatter:
        names, own, from_sib, send_sems, recv_sems, srcs, lands, _ = scatter[stage]
        received = _scatter_wait(srcs, lands, send_sems, recv_sems, after, f"scatter_wait_{stage}")
        halves = [_sum_slabs(place, o, s, r, f"sum_{n}") for n, o, s, r in zip(names, own, from_sib, received)]
        other_halves = _swap_sibling(halves, f"swap_{stage}")
        for n, mine, other in zip(names, halves, other_halves):
            shape = wts[n].shape
            res = _adamw_halves(place, mine, other, shards2d[n], mom[n].reshape(shards2d[n].shape),
                                var[n].reshape(shards2d[n].shape), f"adamw_{n}")
            out_g[n], out_d[n], out_m[n], out_v[n] = (a.reshape(shape) for a in res)
            after = res[3]

    small_g = _pack_small({n: gr[n] for n in SMALL_NAMES})
    n_small = small_g.shape[0]
    small_g = small_g.at[n_small - 1, LOSS_LANE].set(loss_part)
    meta_rows = gr["meta"].reshape(-1, SMALL_COLS)
    total = _all_reduce_small(jnp.concatenate([small_g, meta_rows], axis=0))
    loss = total[n_small - 1, LOSS_LANE]
    res = _adamw([total[:n_small]], _pack_small({n: wts[n] for n in SMALL_NAMES}),
                 _pack_small({n: mom[n] for n in SMALL_NAMES}), _pack_small({n: var[n] for n in SMALL_NAMES}),
                 "adamw_small")
    for dst, packed in zip((out_g, out_d, out_m, out_v), res):
        dst.update(_unpack_small(packed, wts))
    meta_cols = meta_tokens.shape[1]
    meta_g = lax.dynamic_slice_in_dim(total[n_small:].reshape(N_META, D), me * meta_cols, meta_cols, axis=1)
    res = _adamw([meta_g], meta_tokens, m_meta_tokens, v_meta_tokens, "adamw_meta")
    out_g["meta_tokens"], out_d["meta_tokens"], out_m["meta_tokens"], out_v["meta_tokens"] = res

    grad_x = gr["x"].reshape(x.shape)
    return (loss, grad_x, *[out_g[n] for n in order], *[out_d[n] for n in order], *[out_m[n] for n in order],
            *[out_v[n] for n in order])
```

```python
import functools
import math

import jax
import jax.numpy as jnp
from jax import lax
from jax.experimental import pallas as pl
from jax.experimental.pallas import tpu as pltpu

F32 = jnp.float32
BF16 = jnp.bfloat16

N_META = 16
EPS = 1e-6
HEAD_DIM = 128
N_POOL_GROUPS = 4
LANES = 128
SEQ_ALIGN = 128
TQ = 128
CAUSAL_STEP = 512
VMEM_LIMIT = 56 * 1024 * 1024
ELEMWISE_BLOCK_BYTES = 2304 * 1024

ADAM_LR = 0.001
ADAM_B1 = 0.9
ADAM_B2 = 0.999
ADAM_EPS = 1e-08
ADAM_WD = 0.01
ADAM_STEP = 10

NT_DIMS = (((1,), (1,)), ((), ()))
NEG = -1e30
MESH = pl.DeviceIdType.MESH


def _cparams(*sem):
    return pltpu.CompilerParams(dimension_semantics=sem, vmem_limit_bytes=VMEM_LIMIT)


def _sigmoid(a):
    return 1.0 / (1.0 + jnp.exp(-a))


def _row_tile(rows, cols, itemsize=4):
    best = None
    for t in range(16, rows + 1, 16):
        if rows % t == 0 and t * cols * itemsize <= ELEMWISE_BLOCK_BYTES:
            best = t
    return best if best is not None else rows


def _mm_nn(x, w, tn, name):
    M, K = x.shape
    N = w.shape[1]

    def body(x_ref, w_ref, o_ref):
        o_ref[...] = jnp.dot(x_ref[...], w_ref[...], preferred_element_type=F32)

    return pl.pallas_call(
        body, name=name, grid=(N // tn,),
        in_specs=[pl.BlockSpec((M, K), lambda j: (0, 0)), pl.BlockSpec((K, tn), lambda j: (0, j))],
        out_specs=pl.BlockSpec((M, tn), lambda j: (0, j)),
        out_shape=jax.ShapeDtypeStruct((M, N), F32),
        compiler_params=_cparams("parallel"),
    )(x, w)


def _ffn_up(n, wg, wu, tn, name):
    M, K = n.shape
    N = wg.shape[1]

    def body(n_ref, wg_ref, wu_ref, a_ref, b_ref, s_ref, st_ref):
        nv = n_ref[...]
        a = jnp.dot(nv, wg_ref[...], preferred_element_type=F32)
        b = jnp.dot(nv, wu_ref[...], preferred_element_type=F32)
        a_ref[...] = a
        b_ref[...] = b
        s = a * _sigmoid(a) * b
        s_ref[...] = s.astype(BF16)
        st_ref[...] = s.T.astype(BF16)

    wspec = pl.BlockSpec((K, tn), lambda j: (0, j))
    ospec = pl.BlockSpec((M, tn), lambda j: (0, j))
    return pl.pallas_call(
        body, name=name, grid=(N // tn,),
        in_specs=[pl.BlockSpec((M, K), lambda j: (0, 0)), wspec, wspec],
        out_specs=[ospec, ospec, ospec, pl.BlockSpec((tn, M), lambda j: (j, 0))],
        out_shape=[jax.ShapeDtypeStruct((M, N), F32), jax.ShapeDtypeStruct((M, N), F32),
                   jax.ShapeDtypeStruct((M, N), BF16), jax.ShapeDtypeStruct((N, M), BF16)],
        compiler_params=_cparams("parallel"),
    )(n, wg, wu)


def _mm_nn_residual(x, w, res, alpha, tn, tk, name):
    M, K = x.shape
    N = w.shape[1]
    nk = K // tk

    def body(x_ref, w_ref, r_ref, o_ref, acc):
        k = pl.program_id(1)

        @pl.when(k == 0)
        def _():
            acc[...] = jnp.zeros_like(acc)

        acc[...] += jnp.dot(x_ref[...], w_ref[...], preferred_element_type=F32)

        @pl.when(k == nk - 1)
        def _():
            o_ref[...] = r_ref[...] + alpha * acc[...]

    return pl.pallas_call(
        body, name=name, grid=(N // tn, nk),
        in_specs=[pl.BlockSpec((M, tk), lambda j, k: (0, k)), pl.BlockSpec((tk, tn), lambda j, k: (k, j)),
                  pl.BlockSpec((M, tn), lambda j, k: (0, j))],
        out_specs=pl.BlockSpec((M, tn), lambda j, k: (0, j)),
        out_shape=jax.ShapeDtypeStruct((M, N), F32),
        scratch_shapes=[pltpu.VMEM((M, tn), F32)],
        compiler_params=_cparams("parallel", "arbitrary"),
    )(x, w, res)


def _ffn_bwd_hidden(dhb, wd, a, b, dep, tn, name):
    M, K = dhb.shape
    N = wd.shape[0]

    def body(dh_ref, w_ref, a_ref, b_ref, dep_ref, da_ref, db_ref):
        ds = 0.5 * lax.dot_general(dh_ref[...], w_ref[...], NT_DIMS, preferred_element_type=F32)
        av = a_ref[...]
        sig = _sigmoid(av)
        da_ref[...] = (ds * b_ref[...] * (sig * (1.0 + av * (1.0 - sig)))).astype(BF16)
        db_ref[...] = (ds * (av * sig)).astype(BF16)

    ospec = pl.BlockSpec((M, tn), lambda j: (0, j))
    return pl.pallas_call(
        body, name=name, grid=(N // tn,),
        in_specs=[pl.BlockSpec((M, K), lambda j: (0, 0)), pl.BlockSpec((tn, K), lambda j: (j, 0)), ospec, ospec,
                  pl.BlockSpec(memory_space=pl.ANY)],
        out_specs=[ospec, ospec],
        out_shape=[jax.ShapeDtypeStruct((M, N), BF16), jax.ShapeDtypeStruct((M, N), BF16)],
        compiler_params=_cparams("parallel"),
    )(dhb, wd, a, b, dep)


def _mm_nt(x, w, dep, tn, name):
    M, K = x.shape
    N = w.shape[0]

    def body(x_ref, w_ref, dep_ref, o_ref):
        o_ref[...] = lax.dot_general(x_ref[...], w_ref[...], NT_DIMS, preferred_element_type=F32)

    return pl.pallas_call(
        body, name=name, grid=(N // tn,),
        in_specs=[pl.BlockSpec((M, K), lambda j: (0, 0)), pl.BlockSpec((tn, K), lambda j: (j, 0)),
                  pl.BlockSpec(memory_space=pl.ANY)],
        out_specs=pl.BlockSpec((M, tn), lambda j: (0, j)),
        out_shape=jax.ShapeDtypeStruct((M, N), F32),
        compiler_params=_cparams("parallel"),
    )(x, w, dep)


def _mm_nt_sum(xs, ws, dep, tn, tk, name):
    npair = len(xs)
    M, K = xs[0].shape
    N = ws[0].shape[0]
    nk = K // tk

    def body(*refs):
        x_refs = refs[:npair]
        w_refs = refs[npair:2 * npair]
        o_ref = refs[2 * npair + 1]
        acc = refs[2 * npair + 2]
        k = pl.program_id(1)

        @pl.when(k == 0)
        def _():
            acc[...] = jnp.zeros_like(acc)

        for x_ref, w_ref in zip(x_refs, w_refs):
            acc[...] += lax.dot_general(x_ref[...], w_ref[...], NT_DIMS, preferred_element_type=F32)

        @pl.when(k == nk - 1)
        def _():
            o_ref[...] = acc[...]

    return pl.pallas_call(
        body, name=name, grid=(N // tn, nk),
        in_specs=[pl.BlockSpec((M, tk), lambda j, k: (0, k))] * npair
        + [pl.BlockSpec((tn, tk), lambda j, k: (j, k))] * npair + [pl.BlockSpec(memory_space=pl.ANY)],
        out_specs=pl.BlockSpec((M, tn), lambda j, k: (0, j)),
        out_shape=jax.ShapeDtypeStruct((M, N), F32),
        scratch_shapes=[pltpu.VMEM((M, tn), F32)],
        compiler_params=_cparams("parallel", "arbitrary"),
    )(*xs, *ws, dep)


def _mm_tn(xt, dy, dep, alpha, ti, tn, name, stacked):
    Kin, M = xt.shape
    N = dy.shape[1]

    def body(xt_ref, dy_ref, dep_ref, ob_ref):
        r = jnp.dot(xt_ref[...], dy_ref[...], preferred_element_type=F32)
        if alpha != 1.0:
            r = alpha * r
        ob_ref[...] = r.astype(BF16)

    if stacked:
        ospec = pl.BlockSpec((None, ti, tn), lambda i, j: (j, i, 0))
        oshape = (N // tn, Kin, tn)
    else:
        ospec = pl.BlockSpec((ti, tn), lambda i, j: (i, j))
        oshape = (Kin, N)
    return pl.pallas_call(
        body, name=name, grid=(Kin // ti, N // tn),
        in_specs=[pl.BlockSpec((ti, M), lambda i, j: (i, 0)), pl.BlockSpec((M, tn), lambda i, j: (0, j)),
                  pl.BlockSpec(memory_space=pl.ANY)],
        out_specs=ospec,
        out_shape=jax.ShapeDtypeStruct(oshape, BF16),
        compiler_params=_cparams("parallel", "parallel"),
    )(xt, dy, dep)


def _rmsnorm_fwd(h, g, name):
    M, D = h.shape
    tr = LANES

    def body(h_ref, g_ref, n_ref, nt_ref, r_ref):
        hv = h_ref[...]
        r = lax.rsqrt(jnp.mean(hv * hv, axis=-1, keepdims=True) + EPS)
        n = hv * r * g_ref[...]
        n_ref[...] = n.astype(BF16)
        nt_ref[...] = n.T.astype(BF16)
        r_ref[...] = r

    return pl.pallas_call(
        body, name=name, grid=(M // tr,),
        in_specs=[pl.BlockSpec((tr, D), lambda i: (i, 0)), pl.BlockSpec((1, D), lambda i: (0, 0))],
        out_specs=[pl.BlockSpec((tr, D), lambda i: (i, 0)), pl.BlockSpec((D, tr), lambda i: (0, i)),
                   pl.BlockSpec((tr, 1), lambda i: (i, 0))],
        out_shape=[jax.ShapeDtypeStruct((M, D), BF16), jax.ShapeDtypeStruct((D, M), BF16),
                   jax.ShapeDtypeStruct((M, 1), F32)],
        compiler_params=_cparams("parallel"),
    )(h, g)


def _rmsnorm_bwd(dn, h, r, g, dh_prev, name):
    M, D = h.shape
    tr = _row_tile(M, D)

    def body(dn_ref, h_ref, r_ref, g_ref, dp_ref, dh_ref, dhb_ref, dg_ref):
        i = pl.program_id(0)
        dnv = dn_ref[...]
        hv = h_ref[...]
        rv = r_ref[...]
        w = dnv * g_ref[...]
        c = jnp.mean(w * hv, axis=-1, keepdims=True)
        dh = dp_ref[...] + rv * w - hv * (rv * rv * rv * c)
        dh_ref[...] = dh
        dhb_ref[...] = dh.astype(BF16)
        part = jnp.sum(dnv * (hv * rv), axis=0, keepdims=True)

        @pl.when(i == 0)
        def _():
            dg_ref[...] = part

        @pl.when(i > 0)
        def _():
            dg_ref[...] += part

    row = pl.BlockSpec((tr, D), lambda i: (i, 0))
    vec = pl.BlockSpec((1, D), lambda i: (0, 0))
    return pl.pallas_call(
        body, name=name, grid=(M // tr,),
        in_specs=[row, row, pl.BlockSpec((tr, 1), lambda i: (i, 0)), vec, row],
        out_specs=[row, row, vec],
        out_shape=[jax.ShapeDtypeStruct((M, D), F32), jax.ShapeDtypeStruct((M, D), BF16),
                   jax.ShapeDtypeStruct((1, D), F32)],
        compiler_params=_cparams("arbitrary"),
    )(dn, h, r, g, dh_prev)


def _loss_grad(h, tgt, seq, name):
    M, D = h.shape
    tr = _row_tile(M, D)

    def body(h_ref, t_ref, dh_ref, dhb_ref, loss_ref):
        i = pl.program_id(0)
        row = i * tr + lax.broadcasted_iota(jnp.int32, (tr, 1), 0)
        valid = (row >= N_META) & (row < N_META + seq)
        d = jnp.where(valid, h_ref[...] - t_ref[...], 0.0)
        dh = d * (1.0 / D)
        dh_ref[...] = dh
        dhb_ref[...] = dh.astype(BF16)
        part = (0.5 / D) * jnp.sum(jnp.sum(d * d, axis=1, keepdims=True), axis=0, keepdims=True)

        @pl.when(i == 0)
        def _():
            loss_ref[...] = part

        @pl.when(i > 0)
        def _():
            loss_ref[...] += part

    row = pl.BlockSpec((tr, D), lambda i: (i, 0))
    return pl.pallas_call(
        body, name=name, grid=(M // tr,),
        in_specs=[row, row],
        out_specs=[row, row, pl.BlockSpec((1, 1), lambda i: (0, 0))],
        out_shape=[jax.ShapeDtypeStruct((M, D), F32), jax.ShapeDtypeStruct((M, D), BF16),
                   jax.ShapeDtypeStruct((1, 1), F32)],
        compiler_params=_cparams("arbitrary"),
    )(h, tgt)


def _group_window(g):
    return jnp.where(g == 0, 2, jnp.where(g == 1, 4, jnp.where(g == 2, 8, 16)))


def _pool_fwd(z, pw, psc, name):
    M = z.shape[0]
    C = pw.shape[1]

    def body(p_ref, w_ref, sc_ref, pooled_ref, out_ref, outt_ref):
        g = pl.program_id(0)
        p = p_ref[...]
        t = lax.broadcasted_iota(jnp.int32, (M, 1), 0)
        s = p
        wsum = jnp.zeros_like(p)
        for step in range(N_POOL_GROUPS):
            sh = 1 << step
            s = s + jnp.where(t >= sh, pltpu.roll(s, sh, 0), 0.0)
            wsum = jnp.where(g == step, s, wsum)
        cnt = jnp.minimum(t + 1, _group_window(g)).astype(F32)
        pb = (wsum / cnt - p).astype(BF16)
        pooled_ref[...] = pb
        out = jnp.dot(pb, w_ref[...], preferred_element_type=F32) * sc_ref[...]
        out_ref[...] = out.astype(BF16)
        outt_ref[...] = out.T.astype(BF16)

    col = pl.BlockSpec((M, C), lambda g: (0, g))
    return pl.pallas_call(
        body, name=name, grid=(N_POOL_GROUPS,),
        in_specs=[col, pl.BlockSpec((None, C, C), lambda g: (g, 0, 0)), pl.BlockSpec((1, C), lambda g: (0, g))],
        out_specs=[col, col, pl.BlockSpec((C, M), lambda g: (g, 0))],
        out_shape=[jax.ShapeDtypeStruct((M, N_POOL_GROUPS * C), BF16),
                   jax.ShapeDtypeStruct((M, N_POOL_GROUPS * C), BF16),
                   jax.ShapeDtypeStruct((N_POOL_GROUPS * C, M), BF16)],
        compiler_params=_cparams("parallel"),
    )(z, pw, psc)


def _pool_bwd(dmix, pooled, pw, psc, name):
    M = dmix.shape[0]
    C = pw.shape[1]

    def body(dm_ref, pooled_ref, w_ref, sc_ref, dp_ref, dwb_ref, dsc_ref):
        g = pl.program_id(0)
        dmx = dm_ref[...]
        pb = pooled_ref[...]
        wv = w_ref[...]
        mixed = jnp.dot(pb, wv, preferred_element_type=F32)
        dsc_ref[...] = jnp.sum(dmx * mixed, axis=0, keepdims=True)
        dmixed = (dmx * sc_ref[...]).astype(BF16)
        dw = jnp.dot(pb.astype(F32).T.astype(BF16), dmixed, preferred_element_type=F32)
        dwb_ref[...] = dw.astype(BF16)
        dpooled = lax.dot_general(dmixed, wv, NT_DIMS, preferred_element_type=F32)
        t = lax.broadcasted_iota(jnp.int32, (M, 1), 0)
        cnt = jnp.minimum(t + 1, _group_window(g)).astype(F32)
        s = dpooled / cnt
        wsum = jnp.zeros_like(s)
        for step in range(N_POOL_GROUPS):
            sh = 1 << step
            s = s + jnp.where(t < M - sh, pltpu.roll(s, M - sh, 0), 0.0)
            wsum = jnp.where(g == step, s, wsum)
        dp_ref[...] = (wsum - dpooled).astype(BF16)

    col = pl.BlockSpec((M, C), lambda g: (0, g))
    wspec = pl.BlockSpec((None, C, C), lambda g: (g, 0, 0))
    vec = pl.BlockSpec((1, C), lambda g: (0, g))
    return pl.pallas_call(
        body, name=name, grid=(N_POOL_GROUPS,),
        in_specs=[col, col, wspec, vec],
        out_specs=[col, wspec, vec],
        out_shape=[jax.ShapeDtypeStruct((M, N_POOL_GROUPS * C), BF16),
                   jax.ShapeDtypeStruct((N_POOL_GROUPS, C, C), BF16),
                   jax.ShapeDtypeStruct((1, N_POOL_GROUPS * C), F32)],
        compiler_params=_cparams("parallel"),
    )(dmix, pooled, pw, psc)


def _qkv_prep(z, gq, gk, n_heads, q_col, name):
    M = z.shape[0]
    H = n_heads
    qb = q_col // HEAD_DIM

    def body(q_ref, k_ref, v_ref, gq_ref, gk_ref, qh_ref, kh_ref, vb_ref):
        def norm(xv, g):
            r = lax.rsqrt(jnp.mean(xv * xv, axis=-1, keepdims=True) + EPS)
            return (xv * r * g).astype(BF16)

        qh_ref[...] = norm(q_ref[...], gq_ref[...])
        kh_ref[...] = norm(k_ref[...], gk_ref[...])
        vb_ref[...] = v_ref[...].astype(BF16)

    vec = pl.BlockSpec((1, HEAD_DIM), lambda h: (0, 0))
    out = pl.BlockSpec((M, HEAD_DIM), lambda h: (0, h))
    oshape = jax.ShapeDtypeStruct((M, H * HEAD_DIM), BF16)
    return pl.pallas_call(
        body, name=name, grid=(H,),
        in_specs=[pl.BlockSpec((M, HEAD_DIM), lambda h: (0, qb + h)),
                  pl.BlockSpec((M, HEAD_DIM), lambda h: (0, qb + H + h)),
                  pl.BlockSpec((M, HEAD_DIM), lambda h: (0, qb + 2 * H + h)), vec, vec],
        out_specs=[out, out, out],
        out_shape=[oshape, oshape, oshape],
        compiler_params=_cparams("parallel"),
    )(z, z, z, gq, gk)


def _forget_fwd(z, bpad, f_block, name):
    M = z.shape[0]

    def body(f_ref, b_ref, cum_ref):
        xx = f_ref[...] + b_ref[...]
        c = jnp.minimum(xx, 0.0) - jnp.log(1.0 + jnp.exp(-jnp.abs(xx)))
        t = lax.broadcasted_iota(jnp.int32, (M, 1), 0)
        sh = 1
        while sh < M:
            c = c + jnp.where(t >= sh, pltpu.roll(c, sh, 0), 0.0)
            sh *= 2
        cum_ref[...] = c.T

    return pl.pallas_call(
        body, name=name, grid=(1,),
        in_specs=[pl.BlockSpec((M, LANES), lambda i: (0, f_block)), pl.BlockSpec((1, LANES), lambda i: (0, 0))],
        out_specs=pl.BlockSpec((LANES, M), lambda i: (0, 0)),
        out_shape=jax.ShapeDtypeStruct((LANES, M), F32),
        compiler_params=_cparams("arbitrary"),
    )(z, bpad)


def _col_to_row(col):
    n = col.shape[0]
    return jnp.transpose(jnp.broadcast_to(col, (n, LANES)))[0:1, :]


def _causal_extents(M):
    edges = list(range(0, M, CAUSAL_STEP)) + [M]
    return list(zip(edges[:-1], edges[1:]))


def _attn_fwd(qh, kh, vb, cum_c, cum_r, name):
    M = qh.shape[0]
    H = qh.shape[1] // HEAD_DIM
    scale = 1.0 / math.sqrt(HEAD_DIM)

    def body(q_ref, k_ref, v_ref, cq_ref, ck_ref, o_ref, ot_ref, lc_ref, lr_ref):
        i = pl.program_id(1)

        def compute(n):
            s = lax.dot_general(q_ref[...], k_ref[0:n, :], NT_DIMS, preferred_element_type=F32) * scale
            s = s + (cq_ref[...] - ck_ref[:, 0:n])
            row = i * TQ + lax.broadcasted_iota(jnp.int32, (TQ, 1), 0)
            col = lax.broadcasted_iota(jnp.int32, (1, n), 1)
            s = jnp.where(row >= col, s, NEG)
            m = jnp.max(s, axis=1, keepdims=True)
            p = jnp.exp(s - m)
            l = jnp.sum(p, axis=1, keepdims=True)
            pn = (p / l).astype(BF16)
            o = jnp.dot(pn, v_ref[0:n, :], preferred_element_type=F32)
            o_ref[...] = o.astype(BF16)
            ot_ref[...] = o.T.astype(BF16)
            lse = m + jnp.log(l)
            lc_ref[...] = lse
            lr_ref[...] = _col_to_row(lse)

        for lo, hi in _causal_extents(M):
            pl.when((i >= lo // TQ) & (i < hi // TQ))(functools.partial(compute, hi))

    full = pl.BlockSpec((M, HEAD_DIM), lambda h, i: (0, h))
    tile = pl.BlockSpec((TQ, HEAD_DIM), lambda h, i: (i, h))
    colv = pl.BlockSpec((None, TQ, 1), lambda h, i: (h, i, 0))
    rowv_full = pl.BlockSpec((None, 1, M), lambda h, i: (h, 0, 0))
    rowv = pl.BlockSpec((None, 1, TQ), lambda h, i: (h, 0, i))
    return pl.pallas_call(
        body, name=name, grid=(H, M // TQ),
        in_specs=[tile, full, full, colv, rowv_full],
        out_specs=[tile, pl.BlockSpec((HEAD_DIM, TQ), lambda h, i: (h, i)), colv, rowv],
        out_shape=[jax.ShapeDtypeStruct((M, H * HEAD_DIM), BF16), jax.ShapeDtypeStruct((H * HEAD_DIM, M), BF16),
                   jax.ShapeDtypeStruct((H, M, 1), F32), jax.ShapeDtypeStruct((H, 1, M), F32)],
        compiler_params=_cparams("parallel", "parallel"),
    )(qh, kh, vb, cum_c, cum_r)


def _attn_bwd_q(qh, kh, vb, dob, cum_c, cum_r, lse_c, name):
    M = qh.shape[0]
    H = qh.shape[1] // HEAD_DIM
    scale = 1.0 / math.sqrt(HEAD_DIM)

    def body(q_ref, k_ref, v_ref, do_ref, cq_ref, ck_ref, l_ref, dq_ref, dr_ref, dcq_ref):
        i = pl.program_id(1)

        def compute(n):
            k = k_ref[0:n, :]
            s = lax.dot_general(q_ref[...], k, NT_DIMS, preferred_element_type=F32) * scale
            s = s + (cq_ref[...] - ck_ref[:, 0:n])
            row = i * TQ + lax.broadcasted_iota(jnp.int32, (TQ, 1), 0)
            col = lax.broadcasted_iota(jnp.int32, (1, n), 1)
            p = jnp.exp(jnp.where(row >= col, s, NEG) - l_ref[...])
            dp = lax.dot_general(do_ref[...], v_ref[0:n, :], NT_DIMS, preferred_element_type=F32)
            delta = jnp.sum(p * dp, axis=1, keepdims=True)
            ds = p * (dp - delta)
            dq_ref[...] = jnp.dot((ds * scale).astype(BF16), k, preferred_element_type=F32)
            dr_ref[...] = _col_to_row(delta)
            dcq_ref[...] = jnp.sum(ds, axis=1, keepdims=True)

        for lo, hi in _causal_extents(M):
            pl.when((i >= lo // TQ) & (i < hi // TQ))(functools.partial(compute, hi))

    full = pl.BlockSpec((M, HEAD_DIM), lambda h, i: (0, h))
    tile = pl.BlockSpec((TQ, HEAD_DIM), lambda h, i: (i, h))
    colv = pl.BlockSpec((None, TQ, 1), lambda h, i: (h, i, 0))
    rowv_full = pl.BlockSpec((None, 1, M), lambda h, i: (h, 0, 0))
    rowv = pl.BlockSpec((None, 1, TQ), lambda h, i: (h, 0, i))
    return pl.pallas_call(
        body, name=name, grid=(H, M // TQ),
        in_specs=[tile, full, full, tile, colv, rowv_full, colv],
        out_specs=[tile, rowv, colv],
        out_shape=[jax.ShapeDtypeStruct((M, H * HEAD_DIM), F32), jax.ShapeDtypeStruct((H, 1, M), F32),
                   jax.ShapeDtypeStruct((H, M, 1), F32)],
        compiler_params=_cparams("parallel", "parallel"),
    )(qh, kh, vb, dob, cum_c, cum_r, lse_c)


def _attn_bwd_kv(qh, kh, vb, dob, cum_c, cum_r, lse_r, delta_r, name):
    M = qh.shape[0]
    H = qh.shape[1] // HEAD_DIM
    scale = 1.0 / math.sqrt(HEAD_DIM)

    def body(k_ref, v_ref, q_ref, do_ref, cq_ref, ck_ref, l_ref, d_ref, dk_ref, dv_ref, dck_ref):
        j = pl.program_id(1)

        def compute(q0):
            q = q_ref[q0:M, :]
            do = do_ref[q0:M, :]
            st = lax.dot_general(k_ref[...], q, NT_DIMS, preferred_element_type=F32) * scale
            st = st + (cq_ref[:, q0:M] - ck_ref[...])
            krow = j * TQ + lax.broadcasted_iota(jnp.int32, (TQ, 1), 0)
            qcol = q0 + lax.broadcasted_iota(jnp.int32, (1, M - q0), 1)
            pt = jnp.exp(jnp.where(qcol >= krow, st, NEG) - l_ref[:, q0:M])
            dpt = lax.dot_general(v_ref[...], do, NT_DIMS, preferred_element_type=F32)
            dst = pt * (dpt - d_ref[:, q0:M])
            dv_ref[...] = jnp.dot(pt.astype(BF16), do, preferred_element_type=F32).astype(BF16)
            dk_ref[...] = jnp.dot((dst * scale).astype(BF16), q, preferred_element_type=F32)
            dck_ref[...] = -jnp.sum(dst, axis=1, keepdims=True)

        for lo, hi in _causal_extents(M):
            pl.when((j >= lo // TQ) & (j < hi // TQ))(functools.partial(compute, lo))

    full = pl.BlockSpec((M, HEAD_DIM), lambda h, j: (0, h))
    tile = pl.BlockSpec((TQ, HEAD_DIM), lambda h, j: (j, h))
    colv = pl.BlockSpec((None, TQ, 1), lambda h, j: (h, j, 0))
    rowv_full = pl.BlockSpec((None, 1, M), lambda h, j: (h, 0, 0))
    return pl.pallas_call(
        body, name=name, grid=(H, M // TQ),
        in_specs=[tile, tile, full, full, rowv_full, colv, rowv_full, rowv_full],
        out_specs=[tile, tile, colv],
        out_shape=[jax.ShapeDtypeStruct((M, H * HEAD_DIM), F32), jax.ShapeDtypeStruct((M, H * HEAD_DIM), BF16),
                   jax.ShapeDtypeStruct((H, M, 1), F32)],
        compiler_params=_cparams("parallel", "parallel"),
    )(kh, vb, qh, dob, cum_r, cum_c, lse_r, delta_r)


def _qk_norm_bwd(dqh, dkh, z, gq, gk, n_heads, q_col, name):
    M = z.shape[0]
    H = n_heads
    qb = q_col // HEAD_DIM

    def body(dqh_ref, dkh_ref, q_ref, k_ref, gq_ref, gk_ref, dq_ref, dk_ref, dgq_ref, dgk_ref):
        h = pl.program_id(0)

        def one(dy, xv, g):
            r = lax.rsqrt(jnp.mean(xv * xv, axis=-1, keepdims=True) + EPS)
            w = dy * g
            c = jnp.mean(w * xv, axis=-1, keepdims=True)
            dx = r * w - xv * (r * r * r * c)
            return dx.astype(BF16), jnp.sum(dy * (xv * r), axis=0, keepdims=True)

        dq, dgq = one(dqh_ref[...], q_ref[...], gq_ref[...])
        dk, dgk = one(dkh_ref[...], k_ref[...], gk_ref[...])
        dq_ref[...] = dq
        dk_ref[...] = dk

        @pl.when(h == 0)
        def _():
            dgq_ref[...] = dgq
            dgk_ref[...] = dgk

        @pl.when(h > 0)
        def _():
            dgq_ref[...] += dgq
            dgk_ref[...] += dgk

    vec = pl.BlockSpec((1, HEAD_DIM), lambda h: (0, 0))
    head = pl.BlockSpec((M, HEAD_DIM), lambda h: (0, h))
    return pl.pallas_call(
        body, name=name, grid=(H,),
        in_specs=[head, head, pl.BlockSpec((M, HEAD_DIM), lambda h: (0, qb + h)),
                  pl.BlockSpec((M, HEAD_DIM), lambda h: (0, qb + H + h)), vec, vec],
        out_specs=[head, head, vec, vec],
        out_shape=[jax.ShapeDtypeStruct((M, H * HEAD_DIM), BF16), jax.ShapeDtypeStruct((M, H * HEAD_DIM), BF16),
                   jax.ShapeDtypeStruct((1, HEAD_DIM), F32), jax.ShapeDtypeStruct((1, HEAD_DIM), F32)],
        compiler_params=_cparams("arbitrary"),
    )(dqh, dkh, z, z, gq, gk)


def _forget_bwd(dcq, dck, z, bpad, f_block, name):
    H, M, _ = dcq.shape

    def body(dcq_ref, dck_ref, f_ref, b_ref, dfl_ref, db_ref):
        lane = lax.broadcasted_iota(jnp.int32, (1, LANES), 1)
        d = jnp.zeros((M, LANES), F32)
        for h in range(H):
            d = d + (dcq_ref[h] + dck_ref[h]) * (lane == h).astype(F32)
        t = lax.broadcasted_iota(jnp.int32, (M, 1), 0)
        sh = 1
        while sh < M:
            d = d + jnp.where(t < M - sh, pltpu.roll(d, M - sh, 0), 0.0)
            sh *= 2
        xx = f_ref[...] + b_ref[...]
        dfl = d * (1.0 / (1.0 + jnp.exp(xx)))
        dfl_ref[...] = dfl.astype(BF16)
        db_ref[...] = jnp.sum(dfl, axis=0, keepdims=True)

    colv = pl.BlockSpec((H, M, 1), lambda i: (0, 0, 0))
    return pl.pallas_call(
        body, name=name, grid=(1,),
        in_specs=[colv, colv, pl.BlockSpec((M, LANES), lambda i: (0, f_block)),
                  pl.BlockSpec((1, LANES), lambda i: (0, 0))],
        out_specs=[pl.BlockSpec((M, LANES), lambda i: (0, 0)), pl.BlockSpec((1, LANES), lambda i: (0, 0))],
        out_shape=[jax.ShapeDtypeStruct((M, LANES), BF16), jax.ShapeDtypeStruct((1, LANES), F32)],
        compiler_params=_cparams("arbitrary"),
    )(dcq, dck, z, bpad)


def _ffn_fwd(h, g, get_weights, tag):
    n, nt, r = _rmsnorm_fwd(h, g, f"{tag}_norm")
    up = get_weights(f"{tag}_up", n)
    a, b, s, st = _ffn_up(n, up["wg"], up["wu"], 256, f"{tag}_up")
    wd = get_weights(f"{tag}_down", s)["wd"]
    h_out = _mm_nn_residual(s, wd, h, 0.5, 512, wd.shape[0] // 4, f"{tag}_down")
    return h_out, (nt, r, a, b, st, up["wg"], up["wu"], wd)


def _ffn_bwd(dh, dhb, h, g, saved, dep, put_grads, tag):
    nt, r, a, b, st, wg, wu, wd = saved
    n_shards = 4
    da, db = _ffn_bwd_hidden(dhb, wd, a, b, dep, 256, f"{tag}_bwd_hidden")
    dwd = _mm_tn(st, dhb, dep, 0.5, st.shape[0] // n_shards, 1024, f"{tag}_dw_down", stacked=False)
    dep = put_grads(f"{tag}_w_down", dwd)
    dwg = _mm_tn(nt, da, dep, 1.0, 1024, wg.shape[1] // n_shards, f"{tag}_dw_gate", stacked=True)
    dep = put_grads(f"{tag}_w_gate", dwg)
    dwu = _mm_tn(nt, db, dep, 1.0, 1024, wu.shape[1] // n_shards, f"{tag}_dw_up", stacked=True)
    dep = put_grads(f"{tag}_w_up", dwu)
    dn = _mm_nt_sum([da, db], [wg, wu], dep, 512, wg.shape[1] // 4, f"{tag}_dn")
    dh_in, dhb_in, dg = _rmsnorm_bwd(dn, h, r, g, dh, f"{tag}_norm_bwd")
    return dh_in, dhb_in, dg


def _local_step(x, target, S, get_weights, put_grads):
    seq, D = x.shape
    L = N_META + seq
    Lp = -(-L // SEQ_ALIGN) * SEQ_ALIGN
    pad = jnp.zeros((Lp - L, D), F32)
    tgt = jnp.concatenate([jnp.zeros((N_META, D), F32), target, pad], axis=0)

    d_pool = S["pool_scale"].shape[1]
    n_heads = S["b_forget"].shape[1]
    d_att = n_heads * HEAD_DIM
    f_col = d_pool + 3 * d_att
    f_block = f_col // LANES
    bpad = jnp.pad(S["b_forget"], ((0, 0), (0, LANES - n_heads)))

    h0 = jnp.concatenate([get_weights("meta", None)["meta"], x, pad], axis=0)
    h1, ffn1 = _ffn_fwd(h0, S["ffn1_norm"], get_weights, "ffn1")
    Wm = get_weights("mix", h1)
    u, ut, r_mix = _rmsnorm_fwd(h1, S["mix_norm"], "mix_norm")
    z = _mm_nn(u, Wm["win"], 384, "in_proj")
    pooled, pool_out, pool_out_t = _pool_fwd(z, Wm["pool_w"], S["pool_scale"], "pool_fwd")
    qh, kh, vb = _qkv_prep(z, S["q_norm"], S["k_norm"], n_heads, d_pool, "qkv_prep")
    cum_t = _forget_fwd(z, bpad, f_block, "forget_fwd")[:n_heads]
    cum_c = cum_t.reshape(n_heads, Lp, 1)
    cum_r = cum_t.reshape(n_heads, 1, Lp)
    att, att_t, lse_c, lse_r = _attn_fwd(qh, kh, vb, cum_c, cum_r, "attn_fwd")
    mix = jnp.concatenate([pool_out, att], axis=1)
    mix_t = jnp.concatenate([pool_out_t, att_t], axis=0)
    h2 = _mm_nn_residual(mix, Wm["wout"], h1, 1.0, 512, 1024, "out_proj")
    h3, ffn2 = _ffn_fwd(h2, S["ffn2_norm"], get_weights, "ffn2")

    dh3, dh3b, loss = _loss_grad(h3, tgt, seq, "loss")
    dh2, dh2b, dg_ffn2 = _ffn_bwd(dh3, dh3b, h2, S["ffn2_norm"], ffn2, loss, put_grads, "ffn2")

    dmix = _mm_nt(dh2b, Wm["wout"], loss, 512, "out_proj_bwd")
    dwout = _mm_tn(mix_t, dh2b, loss, 1.0, 1024, 1024, "dw_out", stacked=False)
    dp, dpw, dpsc = _pool_bwd(dmix, pooled, Wm["pool_w"], S["pool_scale"], "pool_bwd")
    dob = dmix[:, d_pool:].astype(BF16)
    dqh, delta_r, dcq = _attn_bwd_q(qh, kh, vb, dob, cum_c, cum_r, lse_c, "attn_bwd_q")
    dkh, dv, dck = _attn_bwd_kv(qh, kh, vb, dob, cum_c, cum_r, lse_r, delta_r, "attn_bwd_kv")
    dq, dk, dgq, dgk = _qk_norm_bwd(dqh, dkh, z, S["q_norm"], S["k_norm"], n_heads, d_pool, "qk_norm_bwd")
    dfl, dbf = _forget_bwd(dcq, dck, z, bpad, f_block, "forget_bwd")
    dz = jnp.concatenate([dp, dq, dk, dv, dfl], axis=1)
    dwin = _mm_tn(ut, dz, loss, 1.0, 1024, dz.shape[1] // 3, "dw_in", stacked=False)
    dep = put_grads("mix", dict(win=dwin, wout=dwout, pool_w=dpw))
    du = _mm_nt_sum([dz], [Wm["win"]], dep, 512, Wm["win"].shape[1] // 3, "in_proj_bwd")
    dh1, dh1b, dg_mix = _rmsnorm_bwd(du, h1, r_mix, S["mix_norm"], dh2, "mix_norm_bwd")

    dh0, _, dg_ffn1 = _ffn_bwd(dh1, dh1b, h0, S["ffn1_norm"], ffn1, loss, put_grads, "ffn1")

    grads = dict(
        x=dh0[N_META:L], meta=dh0[:N_META],
        ffn1_norm=dg_ffn1, mix_norm=dg_mix, ffn2_norm=dg_ffn2, q_norm=dgq, k_norm=dgk,
        b_forget=dbf[:, :n_heads], pool_scale=dpsc,
    )
    return loss[0, 0], dh0, grads


HBM_SPEC = pl.BlockSpec(memory_space=pltpu.HBM)
N_CHIPS = 4


def _chip_peers():
    x, y, c = lax.axis_index("x"), lax.axis_index("y"), lax.axis_index("c")
    flips = [(1 - x, y), (x, 1 - y), (1 - x, 1 - y)]
    return 2 * x + y, [((px, py, c), 2 * px + py) for px, py in flips]


def _gathered_shape(shape, layout):
    if layout == "rows":
        return (N_CHIPS * shape[0],) + shape[1:]
    if layout == "cols":
        return (shape[0], N_CHIPS * shape[1])
    return (N_CHIPS,) + shape


def _cast_place(place, w, dep, layout, dtype, name):
    R, C = w.shape
    tr = _row_tile(R, C)
    nt = R // tr

    def body(place_ref, w_ref, dep_ref, o_ref):
        o_ref[...] = w_ref[...].astype(dtype)

    if layout == "rows":
        ospec = pl.BlockSpec((tr, C), lambda i, p: (p[1] * nt + i, 0))
    elif layout == "cols":
        ospec = pl.BlockSpec((tr, C), lambda i, p: (i, p[1]))
    else:
        ospec = pl.BlockSpec((None, tr, C), lambda i, p: (p[1], i, 0))
    return pl.pallas_call(
        body, name=name,
        grid_spec=pltpu.PrefetchScalarGridSpec(
            num_scalar_prefetch=1, grid=(nt,),
            in_specs=[pl.BlockSpec((tr, C), lambda i, p: (i, 0)), pl.BlockSpec(memory_space=pl.ANY)],
            out_specs=ospec),
        out_shape=jax.ShapeDtypeStruct(_gathered_shape((R, C), layout), dtype),
        compiler_params=_cparams("parallel"),
    )(place, w, dep)


SEM_SPEC = pl.BlockSpec(memory_space=pltpu.SEMAPHORE)
ANY_SPEC = pl.BlockSpec(memory_space=pl.ANY)
SPLIT_COPY = pltpu.CompilerParams(has_side_effects=pltpu.SideEffectType.DATAFLOW_SIDE_EFFECTING)


def _hbm(a):
    return pltpu.with_memory_space_constraint(a, pltpu.HBM)


def _gather_region(refs, shard_shapes, layouts, a, chip, half):
    rows_a = shard_shapes[a][0]
    h = rows_a // 2
    if layouts[a] == "rows":
        return refs[a].at[pl.ds(chip * rows_a + half * h, h)]
    if layouts[a] == "cols":
        cols_a = shard_shapes[a][1]
        return refs[a].at[pl.ds(half * h, h), pl.ds(chip * cols_a, cols_a)]
    return refs[a].at[chip, pl.ds(half * h, h)]


def _gather_start(bufs, after, shard_shapes, layouts, name):
    n = len(bufs)
    ns = 3 * n

    def body(*refs):
        in_refs = refs[:n]
        send_sems = refs[n + 1:n + 1 + ns]
        recv_sems = refs[n + 1 + ns:n + 1 + 2 * ns]
        token = refs[2 * n + 1 + 2 * ns]
        c = lax.axis_index("c")
        me, peers = _chip_peers()
        for a in range(n):
            mine = _gather_region(in_refs, shard_shapes, layouts, a, me, c)
            for k, (dev, _) in enumerate(peers):
                pltpu.make_async_remote_copy(
                    src_ref=mine, dst_ref=mine, send_sem=send_sems[3 * a + k], recv_sem=recv_sems[3 * a + k],
                    device_id=dev, device_id_type=MESH).start()
        token[...] = jnp.zeros_like(token)

    sem = pltpu.SemaphoreType.DMA(())
    out = pl.pallas_call(
        body, name=name,
        out_shape=(*[sem] * (2 * ns), *[pltpu.HBM(b.shape, b.dtype) for b in bufs],
                   jax.ShapeDtypeStruct((8, LANES), F32)),
        in_specs=[HBM_SPEC] * n + [ANY_SPEC],
        out_specs=(*[SEM_SPEC] * (2 * ns), *[HBM_SPEC] * n, pl.BlockSpec(memory_space=pltpu.VMEM)),
        input_output_aliases={a: 2 * ns + a for a in range(n)},
        compiler_params=SPLIT_COPY,
    )(*[_hbm(b) for b in bufs], after)
    return list(out[:ns]), list(out[ns:2 * ns]), list(out[2 * ns:2 * ns + n]), out[2 * ns + n]


def _gather_wait(bufs, send_sems, recv_sems, afters, shard_shapes, layouts, name):
    n = len(bufs)
    ns = 3 * n
    na = len(afters)

    def body(*refs):
        in_refs = refs[:n]
        send_sems = refs[n:n + ns]
        recv_sems = refs[n + ns:n + 2 * ns]
        token = refs[2 * n + 2 * ns + na]
        token[...] = jnp.zeros_like(token)
        c = lax.axis_index("c")
        me, peers = _chip_peers()
        for a in range(n):
            mine = _gather_region(in_refs, shard_shapes, layouts, a, me, c)
            for k, (dev, pidx) in enumerate(peers):
                landed = _gather_region(in_refs, shard_shapes, layouts, a, pidx, c)
                pltpu.make_async_remote_copy(
                    src_ref=mine, dst_ref=landed, send_sem=send_sems[3 * a + k], recv_sem=recv_sems[3 * a + k],
                    device_id=dev, device_id_type=MESH).wait_recv()
        for a in range(n):
            mine = _gather_region(in_refs, shard_shapes, layouts, a, me, c)
            for k, (dev, _) in enumerate(peers):
                pltpu.make_async_remote_copy(
                    src_ref=mine, dst_ref=mine, send_sem=send_sems[3 * a + k], recv_sem=recv_sems[3 * a + k],
                    device_id=dev, device_id_type=MESH).wait_send()

    out = pl.pallas_call(
        body, name=name,
        out_shape=(*[pltpu.HBM(b.shape, b.dtype) for b in bufs], jax.ShapeDtypeStruct((8, LANES), F32)),
        in_specs=[HBM_SPEC] * n + [SEM_SPEC] * (2 * ns) + [ANY_SPEC] * na,
        out_specs=(*[HBM_SPEC] * n, pl.BlockSpec(memory_space=pltpu.VMEM)),
        input_output_aliases={a: a for a in range(n)},
        compiler_params=SPLIT_COPY,
    )(*bufs, *send_sems, *recv_sems, *afters)
    return list(out[:n]), out[n]


def _gather_forward(bufs, dep, shard_shapes, layouts, name):
    n = len(bufs)

    def body(*refs):
        out_refs = refs[n + 1:2 * n + 1]
        send_sems, recv_sems = refs[2 * n + 1:]
        c = lax.axis_index("c")
        sib = (lax.axis_index("x"), lax.axis_index("y"), 1 - c)
        _, peers = _chip_peers()
        sends = []
        for a in range(n):
            for k, (_, pidx) in enumerate(peers):
                landed = _gather_region(out_refs, shard_shapes, layouts, a, pidx, c)
                cp = pltpu.make_async_remote_copy(
                    src_ref=landed, dst_ref=landed, send_sem=send_sems.at[a, k], recv_sem=recv_sems.at[a, k],
                    device_id=sib, device_id_type=MESH)
                cp.start()
                sends.append(cp)
        for a in range(n):
            for k, (_, pidx) in enumerate(peers):
                other = _gather_region(out_refs, shard_shapes, layouts, a, pidx, 1 - c)
                pltpu.make_async_remote_copy(
                    src_ref=other, dst_ref=other, send_sem=send_sems.at[a, k], recv_sem=recv_sems.at[a, k],
                    device_id=sib, device_id_type=MESH).wait_recv()
        for cp in sends:
            cp.wait_send()

    sem = pltpu.SemaphoreType.DMA((n, 3))
    return pl.pallas_call(
        body, name=name,
        in_specs=[HBM_SPEC] * n + [ANY_SPEC], out_specs=[HBM_SPEC] * n,
        out_shape=[jax.ShapeDtypeStruct(b.shape, b.dtype) for b in bufs],
        input_output_aliases={a: a for a in range(n)},
        scratch_shapes=[sem, sem],
    )(*bufs, dep)


def _halves_copies(src_refs, land_refs, send_sems, recv_sems):
    c = lax.axis_index("c")
    sib = (lax.axis_index("x"), lax.axis_index("y"), 1 - c)
    copies = []
    for a, (src, land) in enumerate(zip(src_refs, land_refs)):
        h = src.shape[1] // 2
        copies.append(pltpu.make_async_remote_copy(
            src_ref=src.at[:, pl.ds((1 - c) * h, h)], dst_ref=land, send_sem=send_sems[a], recv_sem=recv_sems[a],
            device_id=sib, device_id_type=MESH))
    return copies


def _whole_copies(src_refs, land_refs, send_sems, recv_sems):
    sib = (lax.axis_index("x"), lax.axis_index("y"), 1 - lax.axis_index("c"))
    return [pltpu.make_async_remote_copy(src_ref=src, dst_ref=land, send_sem=send_sems[a], recv_sem=recv_sems[a],
                                         device_id=sib, device_id_type=MESH)
            for a, (src, land) in enumerate(zip(src_refs, land_refs))]


def _halves_land_shape(shape):
    return (shape[0], shape[1] // 2, shape[2])


def _sibling_start(stacked, copies, land_shape, name):
    n = len(stacked)
    lands = [lax.empty(land_shape(s.shape), s.dtype) for s in stacked]

    def body(*refs):
        for cp in copies(refs[:n], refs[n:2 * n], refs[2 * n:3 * n], refs[3 * n:4 * n]):
            cp.start()
        token = refs[6 * n]
        token[...] = jnp.zeros_like(token)

    sem = pltpu.SemaphoreType.DMA(())
    out = pl.pallas_call(
        body, name=name,
        out_shape=(*[sem] * (2 * n), *[pltpu.HBM(b.shape, b.dtype) for b in stacked],
                   *[pltpu.HBM(b.shape, b.dtype) for b in lands], jax.ShapeDtypeStruct((8, LANES), F32)),
        in_specs=[HBM_SPEC] * (2 * n),
        out_specs=(*[SEM_SPEC] * (2 * n), *[HBM_SPEC] * (2 * n), pl.BlockSpec(memory_space=pltpu.VMEM)),
        input_output_aliases={a: 2 * n + a for a in range(2 * n)},
        compiler_params=SPLIT_COPY,
    )(*[_hbm(b) for b in stacked], *[_hbm(b) for b in lands])
    return list(out[:n]), list(out[n:2 * n]), list(out[2 * n:3 * n]), list(out[3 * n:4 * n]), out[4 * n]


def _sibling_wait(srcs, lands, send_sems, recv_sems, after, copies_of, name):
    n = len(srcs)

    def body(*refs):
        copies = copies_of(refs[:n], refs[n:2 * n], refs[2 * n:3 * n], refs[3 * n:4 * n])
        for cp in copies:
            cp.wait_recv()
        for cp in copies:
            cp.wait_send()

    out = pl.pallas_call(
        body, name=name,
        out_shape=tuple(pltpu.HBM(b.shape, b.dtype) for b in list(srcs) + list(lands)),
        in_specs=[HBM_SPEC] * (2 * n) + [SEM_SPEC] * (2 * n) + [ANY_SPEC],
        out_specs=tuple([HBM_SPEC] * (2 * n)),
        input_output_aliases={a: a for a in range(2 * n)},
        compiler_params=SPLIT_COPY,
    )(*srcs, *lands, *send_sems, *recv_sems, after)
    return list(out[:n]), list(out[n:])


def _scatter_start(stacked, name):
    n = len(stacked)
    ns = 3 * n
    lands = [lax.empty((3,) + s.shape[1:], s.dtype) for s in stacked]

    def body(*refs):
        src_refs = refs[:n]
        land_refs = refs[n:2 * n]
        send_sems = refs[2 * n:2 * n + ns]
        recv_sems = refs[2 * n + ns:2 * n + 2 * ns]
        token = refs[4 * n + 2 * ns]
        _, peers = _chip_peers()
        for a in range(n):
            for k, (dev, pidx) in enumerate(peers):
                pltpu.make_async_remote_copy(
                    src_ref=src_refs[a].at[k], dst_ref=land_refs[a].at[k], send_sem=send_sems[3 * a + k],
                    recv_sem=recv_sems[3 * a + k], device_id=dev, device_id_type=MESH).start()
        token[...] = jnp.zeros_like(token)

    sem = pltpu.SemaphoreType.DMA(())
    out = pl.pallas_call(
        body, name=name,
        out_shape=(*[sem] * (2 * ns), *[pltpu.HBM(b.shape, b.dtype) for b in stacked],
                   *[pltpu.HBM(b.shape, b.dtype) for b in lands], jax.ShapeDtypeStruct((8, LANES), F32)),
        in_specs=[HBM_SPEC] * (2 * n),
        out_specs=(*[SEM_SPEC] * (2 * ns), *[HBM_SPEC] * (2 * n), pl.BlockSpec(memory_space=pltpu.VMEM)),
        input_output_aliases={a: 2 * ns + a for a in range(2 * n)},
        compiler_params=SPLIT_COPY,
    )(*[_hbm(b) for b in stacked], *[_hbm(b) for b in lands])
    o = 2 * ns
    return list(out[:ns]), list(out[ns:o]), list(out[o:o + n]), list(out[o + n:o + 2 * n]), out[o + 2 * n]


def _scatter_wait(srcs, lands, send_sems, recv_sems, after, name):
    n = len(srcs)
    ns = 3 * n

    def body(*refs):
        src_refs = refs[:n]
        land_refs = refs[n:2 * n]
        send_sems = refs[2 * n:2 * n + ns]
        recv_sems = refs[2 * n + ns:2 * n + 2 * ns]
        _, peers = _chip_peers()
        copies = [
            pltpu.make_async_remote_copy(
                src_ref=src_refs[a].at[k], dst_ref=land_refs[a].at[k], send_sem=send_sems[3 * a + k],
                recv_sem=recv_sems[3 * a + k], device_id=dev, device_id_type=MESH)
            for a in range(n) for k, (dev, pidx) in enumerate(peers)]
        for cp in copies:
            cp.wait_recv()
        for cp in copies:
            cp.wait_send()

    out = pl.pallas_call(
        body, name=name,
        out_shape=tuple(pltpu.HBM(b.shape, b.dtype) for b in list(srcs) + list(lands)),
        in_specs=[HBM_SPEC] * (2 * n) + [SEM_SPEC] * (2 * ns) + [ANY_SPEC],
        out_specs=tuple([HBM_SPEC] * (2 * n)),
        input_output_aliases={a: a for a in range(2 * n)},
        compiler_params=SPLIT_COPY,
    )(*srcs, *lands, *send_sems, *recv_sems, after)
    return list(out[n:])


def _all_reduce_small(v):
    R, C = v.shape
    n_dev = 8

    def body(v_ref, o_ref, buf, send_sems, recv_sems):
        x, y, c = lax.axis_index("x"), lax.axis_index("y"), lax.axis_index("c")
        me = 4 * x + 2 * y + c
        buf[me] = v_ref[...]
        sends = []
        for k in range(1, n_dev):
            px, py, pc = x ^ ((k >> 2) & 1), y ^ ((k >> 1) & 1), c ^ (k & 1)
            cp = pltpu.make_async_remote_copy(
                src_ref=v_ref, dst_ref=buf.at[me], send_sem=send_sems.at[k - 1], recv_sem=recv_sems.at[k - 1],
                device_id=(px, py, pc), device_id_type=MESH)
            cp.start()
            sends.append((cp, 4 * px + 2 * py + pc))
        for k in range(1, n_dev):
            cp, pidx = sends[k - 1]
            pltpu.make_async_remote_copy(
                src_ref=v_ref, dst_ref=buf.at[pidx], send_sem=send_sems.at[k - 1], recv_sem=recv_sems.at[k - 1],
                device_id=(x, y, c), device_id_type=MESH).wait_recv()
        for cp, _ in sends:
            cp.wait_send()
        acc = buf[0]
        for d in range(1, n_dev):
            acc = acc + buf[d]
        o_ref[...] = acc

    vm = pl.BlockSpec(memory_space=pltpu.VMEM)
    return pl.pallas_call(
        body, name="all_reduce_small",
        in_specs=[vm], out_specs=vm,
        out_shape=jax.ShapeDtypeStruct((R, C), F32),
        scratch_shapes=[pltpu.VMEM((n_dev, R, C), F32), pltpu.SemaphoreType.DMA((n_dev - 1,)),
                        pltpu.SemaphoreType.DMA((n_dev - 1,))],
    )(v)


def _pair_sum(place, own, sib, name):
    S, R, C = own.shape
    h = R // 2
    tr = _row_tile(h, C)
    nt = h // tr

    def body(place_ref, o_ref, s_ref, out_ref):
        out_ref[...] = (o_ref[...].astype(F32) + s_ref[...].astype(F32)).astype(BF16)

    return pl.pallas_call(
        body, name=name,
        grid_spec=pltpu.PrefetchScalarGridSpec(
            num_scalar_prefetch=1, grid=(3, nt),
            in_specs=[pl.BlockSpec((None, tr, C), lambda k, i, p: (p[2 + k], p[0] * nt + i, 0)),
                      pl.BlockSpec((None, tr, C), lambda k, i, p: (p[2 + k], i, 0))],
            out_specs=pl.BlockSpec((None, tr, C), lambda k, i, p: (k, i, 0))),
        out_shape=jax.ShapeDtypeStruct((3, h, C), BF16),
        compiler_params=_cparams("parallel", "parallel"),
    )(place, own, sib)


def _sum_slabs(place, own, sib, recv, name):
    S, R, C = own.shape
    h = R // 2
    tr = _row_tile(h, C)
    nt = h // tr

    def body(place_ref, o_ref, s_ref, r_ref, out_ref):
        acc = o_ref[...].astype(F32) + s_ref[...].astype(F32)
        for k in range(3):
            acc = acc + r_ref[k].astype(F32)
        out_ref[...] = acc

    return pl.pallas_call(
        body, name=name,
        grid_spec=pltpu.PrefetchScalarGridSpec(
            num_scalar_prefetch=1, grid=(nt,),
            in_specs=[pl.BlockSpec((None, tr, C), lambda i, p: (p[1], p[0] * nt + i, 0)),
                      pl.BlockSpec((None, tr, C), lambda i, p: (p[1], i, 0)),
                      pl.BlockSpec((3, tr, C), lambda i, p: (0, i, 0))],
            out_specs=pl.BlockSpec((tr, C), lambda i, p: (i, 0))),
        out_shape=jax.ShapeDtypeStruct((h, C), F32),
        compiler_params=_cparams("parallel"),
    )(place, own, sib, recv)


def _adamw(parts, w, m, v, name):
    R, C = w.shape
    tr = _row_tile(R, C)
    npart = len(parts)
    c1 = 1.0 - ADAM_B1 ** ADAM_STEP
    c2 = 1.0 - ADAM_B2 ** ADAM_STEP

    def body(*refs):
        p_refs = refs[:npart]
        w_ref, m_ref, v_ref, g_ref, d_ref, nm_ref, nv_ref = refs[npart:]
        g = p_refs[0][...]
        for p_ref in p_refs[1:]:
            g = g + p_ref[...]
        nm = ADAM_B1 * m_ref[...] + (1.0 - ADAM_B1) * g
        nv = ADAM_B2 * v_ref[...] + (1.0 - ADAM_B2) * (g * g)
        m_hat = nm / c1
        v_hat = nv / c2
        g_ref[...] = g
        d_ref[...] = -ADAM_LR * (m_hat / (jnp.sqrt(v_hat) + ADAM_EPS) + ADAM_WD * w_ref[...])
        nm_ref[...] = nm
        nv_ref[...] = nv

    blk = pl.BlockSpec((tr, C), lambda i: (i, 0))
    shape = jax.ShapeDtypeStruct((R, C), F32)
    return pl.pallas_call(
        body, name=name, grid=(R // tr,),
        in_specs=[blk] * (npart + 3), out_specs=[blk] * 4, out_shape=[shape] * 4,
        compiler_params=_cparams("parallel"),
    )(*parts, w, m, v)


def _adamw_halves(place, mine, other, w, m, v, name):
    R, C = w.shape
    h = R // 2
    tr = _row_tile(h, C)
    nt = h // tr
    c1 = 1.0 - ADAM_B1 ** ADAM_STEP
    c2 = 1.0 - ADAM_B2 ** ADAM_STEP

    def body(place_ref, mine_ref, other_ref, w_ref, m_ref, v_ref, g_ref, d_ref, nm_ref, nv_ref):
        is_mine = (pl.program_id(0) // nt) == place_ref[0]
        g = jnp.where(is_mine, mine_ref[...], other_ref[...])
        nm = ADAM_B1 * m_ref[...] + (1.0 - ADAM_B1) * g
        nv = ADAM_B2 * v_ref[...] + (1.0 - ADAM_B2) * (g * g)
        m_hat = nm / c1
        v_hat = nv / c2
        g_ref[...] = g
        d_ref[...] = -ADAM_LR * (m_hat / (jnp.sqrt(v_hat) + ADAM_EPS) + ADAM_WD * w_ref[...])
        nm_ref[...] = nm
        nv_ref[...] = nv

    def half_block(which):
        def index(i, p):
            first = p[0] if which == 0 else 1 - p[0]
            return jnp.clip(i - first * nt, 0, nt - 1), 0

        return pl.BlockSpec((tr, C), index)

    blk = pl.BlockSpec((tr, C), lambda i, p: (i, 0))
    shape = jax.ShapeDtypeStruct((R, C), F32)
    return pl.pallas_call(
        body, name=name,
        grid_spec=pltpu.PrefetchScalarGridSpec(
            num_scalar_prefetch=1, grid=(2 * nt,),
            in_specs=[half_block(0), half_block(1), blk, blk, blk], out_specs=[blk] * 4),
        out_shape=[shape] * 4,
        compiler_params=_cparams("parallel"),
    )(place, mine, other, w, m, v)


SMALL_NAMES = ("ffn1_norm", "mix_norm", "ffn2_norm", "pool_scale", "q_norm", "k_norm", "b_forget")
SMALL_COLS = 1024
LOSS_LANE = 512


def _pack_small(vals):
    rows = [vals[n].reshape(-1, SMALL_COLS) for n in ("ffn1_norm", "mix_norm", "ffn2_norm", "pool_scale")]
    tail = jnp.concatenate([vals["q_norm"].reshape(-1), vals["k_norm"].reshape(-1), vals["b_forget"].reshape(-1)])
    rows.append(jnp.pad(tail, (0, SMALL_COLS - tail.shape[0])).reshape(1, SMALL_COLS))
    return jnp.concatenate(rows, axis=0)


def _unpack_small(packed, like):
    out = {}
    r = 0
    for n in ("ffn1_norm", "mix_norm", "ffn2_norm", "pool_scale"):
        k = like[n].size // SMALL_COLS
        out[n] = packed[r:r + k].reshape(like[n].shape)
        r += k
    o = 0
    for n in ("q_norm", "k_norm", "b_forget"):
        k = like[n].size
        out[n] = packed[r, o:o + k].reshape(like[n].shape)
        o += k
    return out


def kernel(x, meta_tokens, ffn1_norm, ffn1_w_gate, ffn1_w_up, ffn1_w_down, mix_norm, w_in, b_forget, q_norm, k_norm, pool_w, pool_scale, w_out, ffn2_norm, ffn2_w_gate, ffn2_w_up, ffn2_w_down, loss_target, m_meta_tokens, m_ffn1_norm, m_ffn1_w_gate, m_ffn1_w_up, m_ffn1_w_down, m_mix_norm, m_w_in, m_b_forget, m_q_norm, m_k_norm, m_pool_w, m_pool_scale, m_w_out, m_ffn2_norm, m_ffn2_w_gate, m_ffn2_w_up, m_ffn2_w_down, v_meta_tokens, v_ffn1_norm, v_ffn1_w_gate, v_ffn1_w_up, v_ffn1_w_down, v_mix_norm, v_w_in, v_b_forget, v_q_norm, v_k_norm, v_pool_w, v_pool_scale, v_w_out, v_ffn2_norm, v_ffn2_w_gate, v_ffn2_w_up, v_ffn2_w_down):
    wts = dict(meta_tokens=meta_tokens, ffn1_norm=ffn1_norm, ffn1_w_gate=ffn1_w_gate, ffn1_w_up=ffn1_w_up,
               ffn1_w_down=ffn1_w_down, mix_norm=mix_norm, w_in=w_in, b_forget=b_forget, q_norm=q_norm,
               k_norm=k_norm, pool_w=pool_w, pool_scale=pool_scale, w_out=w_out, ffn2_norm=ffn2_norm,
               ffn2_w_gate=ffn2_w_gate, ffn2_w_up=ffn2_w_up, ffn2_w_down=ffn2_w_down)
    mom = dict(meta_tokens=m_meta_tokens, ffn1_norm=m_ffn1_norm, ffn1_w_gate=m_ffn1_w_gate, ffn1_w_up=m_ffn1_w_up,
               ffn1_w_down=m_ffn1_w_down, mix_norm=m_mix_norm, w_in=m_w_in, b_forget=m_b_forget, q_norm=m_q_norm,
               k_norm=m_k_norm, pool_w=m_pool_w, pool_scale=m_pool_scale, w_out=m_w_out, ffn2_norm=m_ffn2_norm,
               ffn2_w_gate=m_ffn2_w_gate, ffn2_w_up=m_ffn2_w_up, ffn2_w_down=m_ffn2_w_down)
    var = dict(meta_tokens=v_meta_tokens, ffn1_norm=v_ffn1_norm, ffn1_w_gate=v_ffn1_w_gate, ffn1_w_up=v_ffn1_w_up,
               ffn1_w_down=v_ffn1_w_down, mix_norm=v_mix_norm, w_in=v_w_in, b_forget=v_b_forget, q_norm=v_q_norm,
               k_norm=v_k_norm, pool_w=v_pool_w, pool_scale=v_pool_scale, w_out=v_w_out, ffn2_norm=v_ffn2_norm,
               ffn2_w_gate=v_ffn2_w_gate, ffn2_w_up=v_ffn2_w_up, ffn2_w_down=v_ffn2_w_down)
    order = list(wts)
    me = 2 * lax.axis_index("x") + lax.axis_index("y")

    D = x.shape[2]
    d_in_shard = w_in.shape[2]
    d_in = N_CHIPS * d_in_shard
    n_heads = b_forget.shape[1]
    d_in_pad = (d_in - n_heads) + LANES

    stages = dict(meta=("meta_tokens",), ffn1_up=("ffn1_w_gate", "ffn1_w_up"), ffn1_down=("ffn1_w_down",),
                  mix=("w_in", "w_out", "pool_w"),
                  ffn2_up=("ffn2_w_gate", "ffn2_w_up"), ffn2_down=("ffn2_w_down",))
    stage_order = list(stages)
    xi, yi = lax.axis_index("x"), lax.axis_index("y")
    place = jnp.stack([lax.axis_index("c"), me, 2 * (1 - xi) + yi, 2 * xi + 1 - yi, 2 * (1 - xi) + 1 - yi]).astype(
        jnp.int32)
    layouts = dict(ffn1_w_gate="cols", ffn1_w_up="cols", ffn1_w_down="rows", w_in="stack", w_out="rows",
                   pool_w="stack", ffn2_w_gate="cols", ffn2_w_up="cols", ffn2_w_down="rows", meta_tokens="stack")
    shards2d = {n: wts[n].reshape(-1, wts[n].shape[-1]) for n in layouts}

    def place_stage(stage, dep):
        return [_cast_place(place, shards2d[n], dep, layouts[n], F32 if n == "meta_tokens" else BF16, f"place_{n}")
                for n in stages[stage]]

    def start_stage(stage, bufs, after):
        shapes = [shards2d[n].shape for n in stages[stage]]
        lays = [layouts[n] for n in stages[stage]]
        return _gather_start(bufs, after, shapes, lays, f"gather_start_{stage}") + (shapes, lays)

    flight = {stage_order[0]: start_stage(stage_order[0], place_stage(stage_order[0], place), place)}
    placed = {stage_order[1]: place_stage(stage_order[1], flight[stage_order[0]][3])}

    def cols(st):
        return jnp.transpose(st, (1, 0, 2)).reshape(st.shape[1], -1)

    def get_weights(stage, after):
        k = stage_order.index(stage)
        send_sems, recv_sems, bufs, _, shapes, lays = flight.pop(stage)
        afters = ([] if after is None else [after]) + [b for st in placed for b in placed[st]]
        if k == 1:
            afters += [mom["w_in"].reshape(shards2d["w_in"].shape), var["w_in"].reshape(shards2d["w_in"].shape)]
        landed, token = _gather_wait(bufs, send_sems, recv_sems, afters, shapes, lays, f"gather_wait_{stage}")
        if k + 1 < len(stage_order):
            flight[stage_order[k + 1]] = start_stage(stage_order[k + 1], placed.pop(stage_order[k + 1]), token)
            token = flight[stage_order[k + 1]][3]
        if k == 0:
            placed.update({st: place_stage(st, token) for st in stage_order[2:]})
        G = dict(zip(stages[stage], _gather_forward(landed, token, shapes, lays, f"gather_forward_{stage}")))
        if stage == "meta":
            return dict(meta=cols(G["meta_tokens"]))
        if stage == "ffn1_up":
            return dict(wg=G["ffn1_w_gate"], wu=G["ffn1_w_up"])
        if stage == "ffn2_up":
            return dict(wg=G["ffn2_w_gate"], wu=G["ffn2_w_up"])
        if stage != "mix":
            return dict(wd=G[stages[stage][0]])
        return dict(
            win=jnp.pad(cols(G["w_in"]), ((0, 0), (0, d_in_pad - d_in))), wout=G["w_out"],
            pool_w=jnp.transpose(G["pool_w"].reshape((N_CHIPS,) + pool_w.shape[1:]), (1, 0, 2, 3)).reshape(
                N_POOL_GROUPS, pool_w.shape[3], pool_w.shape[3]))

    def split_rows(a):
        return a.reshape(N_CHIPS, -1, a.shape[1])

    def split_win(a):
        return jnp.transpose(a[:, :d_in].reshape(D, N_CHIPS, d_in_shard), (1, 0, 2))

    def split_pool(a):
        r, c = pool_w.shape[2], pool_w.shape[3]
        return jnp.transpose(a.reshape(N_POOL_GROUPS, N_CHIPS, r, c), (1, 0, 2, 3)).reshape(N_CHIPS, -1, c)

    scatter = {}
    pending = []

    def finish_pending(after):
        name, names, (send_sems, recv_sems, srcs, lands) = pending.pop()
        own, from_sib = _sibling_wait(srcs, lands, send_sems, recv_sems, after, _halves_copies,
                                      f"halves_wait_{name}")
        pair = [_pair_sum(place, o, s, f"pair_sum_{n}") for n, o, s in zip(names, own, from_sib)]
        send_sems, recv_sems, srcs, lands, token = _scatter_start(pair, f"scatter_start_{name}")
        scatter[name] = (names, own, from_sib, send_sems, recv_sems, srcs, lands, token)
        return token

    def put_grads(name, g):
        if name == "mix":
            names = stages["mix"]
            own = [split_win(g["win"]), split_rows(g["wout"]), split_pool(g["pool_w"])]
        else:
            names = (name,)
            own = [split_rows(g) if name.endswith("_down") else g]
        *flying, token = _sibling_start(own, _halves_copies, _halves_land_shape, f"halves_start_{name}")
        if pending:
            token = finish_pending(token)
        pending.append((name, names, flying))
        return token

    small = dict(ffn1_norm=ffn1_norm, mix_norm=mix_norm, ffn2_norm=ffn2_norm, q_norm=q_norm, k_norm=k_norm,
                 b_forget=b_forget, pool_scale=pool_scale)
    loss_part, dh0, gr = _local_step(x[0], loss_target[0], small, get_weights, put_grads)

    out_g, out_d, out_m, out_v = {}, {}, {}, {}

    def update(stage, names, flying, after):
        send_sems, recv_sems, srcs, lands = flying
        halves, other_halves = _sibling_wait(srcs, lands, send_sems, recv_sems, after, _whole_copies,
                                             f"swap_wait_{stage}")
        for n, mine, other in zip(names, halves, other_halves):
            shape = wts[n].shape
            res = _adamw_halves(place, mine, other, shards2d[n], mom[n].reshape(shards2d[n].shape),
                                var[n].reshape(shards2d[n].shape), f"adamw_{n}")
            out_g[n], out_d[n], out_m[n], out_v[n] = (a.reshape(shape) for a in res)
        return res[3]

    after = finish_pending(dh0)
    swapping = None
    for stage in scatter:
        names, own, from_sib, send_sems, recv_sems, srcs, lands, _ = scatter[stage]
        received = _scatter_wait(srcs, lands, send_sems, recv_sems, after, f"scatter_wait_{stage}")
        halves = [_sum_slabs(place, o, s, r, f"sum_{n}") for n, o, s, r in zip(names, own, from_sib, received)]
        *flying, after = _sibling_start(halves, _whole_copies, lambda shape: shape, f"swap_start_{stage}")
        if swapping is not None:
            after = update(*swapping, after)
        swapping = (stage, names, flying)
    update(*swapping, after)

    small_g = _pack_small({n: gr[n] for n in SMALL_NAMES})
    n_small = small_g.shape[0]
    small_g = small_g.at[n_small - 1, LOSS_LANE].set(loss_part)
    meta_rows = gr["meta"].reshape(-1, SMALL_COLS)
    total = _all_reduce_small(jnp.concatenate([small_g, meta_rows], axis=0))
    loss = total[n_small - 1, LOSS_LANE]
    res = _adamw([total[:n_small]], _pack_small({n: wts[n] for n in SMALL_NAMES}),
                 _pack_small({n: mom[n] for n in SMALL_NAMES}), _pack_small({n: var[n] for n in SMALL_NAMES}),
                 "adamw_small")
    for dst, packed in zip((out_g, out_d, out_m, out_v), res):
        dst.update(_unpack_small(packed, wts))
    meta_cols = meta_tokens.shape[1]
    meta_g = lax.dynamic_slice_in_dim(total[n_small:].reshape(N_META, D), me * meta_cols, meta_cols, axis=1)
    res = _adamw([meta_g], meta_tokens, m_meta_tokens, v_meta_tokens, "adamw_meta")
    out_g["meta_tokens"], out_d["meta_tokens"], out_m["meta_tokens"], out_v["meta_tokens"] = res

    grad_x = gr["x"].reshape(x.shape)
    return (loss, grad_x, *[out_g[n] for n in order], *[out_d[n] for n in order], *[out_m[n] for n in order],
            *[out_v[n] for n in order])
```

```python
import functools
import math

import jax
import jax.numpy as jnp
from jax import lax
from jax.experimental import pallas as pl
from jax.experimental.pallas import tpu as pltpu

F32 = jnp.float32
BF16 = jnp.bfloat16

N_META = 16
EPS = 1e-6
HEAD_DIM = 128
N_POOL_GROUPS = 4
LANES = 128
SEQ_ALIGN = 128
TQ = 128
CAUSAL_STEP = 512
VMEM_LIMIT = 56 * 1024 * 1024
ELEMWISE_BLOCK_BYTES = 2304 * 1024

ADAM_LR = 0.001
ADAM_B1 = 0.9
ADAM_B2 = 0.999
ADAM_EPS = 1e-08
ADAM_WD = 0.01
ADAM_STEP = 10

NT_DIMS = (((1,), (1,)), ((), ()))
NEG = -1e30
MESH = pl.DeviceIdType.MESH


def _cparams(*sem):
    return pltpu.CompilerParams(dimension_semantics=sem, vmem_limit_bytes=VMEM_LIMIT)


def _sigmoid(a):
    return 1.0 / (1.0 + jnp.exp(-a))


def _row_tile(rows, cols, itemsize=4):
    best = None
    for t in range(16, rows + 1, 16):
        if rows % t == 0 and t * cols * itemsize <= ELEMWISE_BLOCK_BYTES:
            best = t
    return best if best is not None else rows


def _mm_nn(x, w, tn, name):
    M, K = x.shape
    N = w.shape[1]

    def body(x_ref, w_ref, o_ref):
        o_ref[...] = jnp.dot(x_ref[...], w_ref[...], preferred_element_type=F32)

    return pl.pallas_call(
        body, name=name, grid=(N // tn,),
        in_specs=[pl.BlockSpec((M, K), lambda j: (0, 0)), pl.BlockSpec((K, tn), lambda j: (0, j))],
        out_specs=pl.BlockSpec((M, tn), lambda j: (0, j)),
        out_shape=jax.ShapeDtypeStruct((M, N), F32),
        compiler_params=_cparams("parallel"),
    )(x, w)


def _ffn_up(n, wg, wu, dep, tn, name):
    M, K = n.shape
    N = wg.shape[1]

    def body(n_ref, wg_ref, wu_ref, dep_ref, a_ref, b_ref, s_ref, st_ref):
        nv = n_ref[...]
        a = jnp.dot(nv, wg_ref[...], preferred_element_type=F32)
        b = jnp.dot(nv, wu_ref[...], preferred_element_type=F32)
        a_ref[...] = a
        b_ref[...] = b
        s = a * _sigmoid(a) * b
        s_ref[...] = s.astype(BF16)
        st_ref[...] = s.T.astype(BF16)

    wspec = pl.BlockSpec((K, tn), lambda j: (0, j))
    ospec = pl.BlockSpec((M, tn), lambda j: (0, j))
    return pl.pallas_call(
        body, name=name, grid=(N // tn,),
        in_specs=[pl.BlockSpec((M, K), lambda j: (0, 0)), wspec, wspec, pl.BlockSpec(memory_space=pl.ANY)],
        out_specs=[ospec, ospec, ospec, pl.BlockSpec((tn, M), lambda j: (j, 0))],
        out_shape=[jax.ShapeDtypeStruct((M, N), F32), jax.ShapeDtypeStruct((M, N), F32),
                   jax.ShapeDtypeStruct((M, N), BF16), jax.ShapeDtypeStruct((N, M), BF16)],
        compiler_params=_cparams("parallel"),
    )(n, wg, wu, dep)


def _mm_nn_residual(x, w, res, dep, alpha, tn, tk, name):
    M, K = x.shape
    N = w.shape[1]
    nk = K // tk

    def body(x_ref, w_ref, r_ref, dep_ref, o_ref, acc):
        k = pl.program_id(1)

        @pl.when(k == 0)
        def _():
            acc[...] = jnp.zeros_like(acc)

        acc[...] += jnp.dot(x_ref[...], w_ref[...], preferred_element_type=F32)

        @pl.when(k == nk - 1)
        def _():
            o_ref[...] = r_ref[...] + alpha * acc[...]

    return pl.pallas_call(
        body, name=name, grid=(N // tn, nk),
        in_specs=[pl.BlockSpec((M, tk), lambda j, k: (0, k)), pl.BlockSpec((tk, tn), lambda j, k: (k, j)),
                  pl.BlockSpec((M, tn), lambda j, k: (0, j)), pl.BlockSpec(memory_space=pl.ANY)],
        out_specs=pl.BlockSpec((M, tn), lambda j, k: (0, j)),
        out_shape=jax.ShapeDtypeStruct((M, N), F32),
        scratch_shapes=[pltpu.VMEM((M, tn), F32)],
        compiler_params=_cparams("parallel", "arbitrary"),
    )(x, w, res, dep)


def _ffn_bwd_hidden(dhb, wd, a, b, dep, tn, name):
    M, K = dhb.shape
    N = wd.shape[0]

    def body(dh_ref, w_ref, a_ref, b_ref, dep_ref, da_ref, db_ref):
        ds = 0.5 * lax.dot_general(dh_ref[...], w_ref[...], NT_DIMS, preferred_element_type=F32)
        av = a_ref[...]
        sig = _sigmoid(av)
        da_ref[...] = (ds * b_ref[...] * (sig * (1.0 + av * (1.0 - sig)))).astype(BF16)
        db_ref[...] = (ds * (av * sig)).astype(BF16)

    ospec = pl.BlockSpec((M, tn), lambda j: (0, j))
    return pl.pallas_call(
        body, name=name, grid=(N // tn,),
        in_specs=[pl.BlockSpec((M, K), lambda j: (0, 0)), pl.BlockSpec((tn, K), lambda j: (j, 0)), ospec, ospec,
                  pl.BlockSpec(memory_space=pl.ANY)],
        out_specs=[ospec, ospec],
        out_shape=[jax.ShapeDtypeStruct((M, N), BF16), jax.ShapeDtypeStruct((M, N), BF16)],
        compiler_params=_cparams("parallel"),
    )(dhb, wd, a, b, dep)


def _mm_nt(x, w, dep, tn, name):
    M, K = x.shape
    N = w.shape[0]

    def body(x_ref, w_ref, dep_ref, o_ref):
        o_ref[...] = lax.dot_general(x_ref[...], w_ref[...], NT_DIMS, preferred_element_type=F32)

    return pl.pallas_call(
        body, name=name, grid=(N // tn,),
        in_specs=[pl.BlockSpec((M, K), lambda j: (0, 0)), pl.BlockSpec((tn, K), lambda j: (j, 0)),
                  pl.BlockSpec(memory_space=pl.ANY)],
        out_specs=pl.BlockSpec((M, tn), lambda j: (0, j)),
        out_shape=jax.ShapeDtypeStruct((M, N), F32),
        compiler_params=_cparams("parallel"),
    )(x, w, dep)


def _mm_nt_sum(xs, ws, dep, tn, tk, name):
    npair = len(xs)
    M, K = xs[0].shape
    N = ws[0].shape[0]
    nk = K // tk

    def body(*refs):
        x_refs = refs[:npair]
        w_refs = refs[npair:2 * npair]
        o_ref = refs[2 * npair + 1]
        acc = refs[2 * npair + 2]
        k = pl.program_id(1)

        @pl.when(k == 0)
        def _():
            acc[...] = jnp.zeros_like(acc)

        for x_ref, w_ref in zip(x_refs, w_refs):
            acc[...] += lax.dot_general(x_ref[...], w_ref[...], NT_DIMS, preferred_element_type=F32)

        @pl.when(k == nk - 1)
        def _():
            o_ref[...] = acc[...]

    return pl.pallas_call(
        body, name=name, grid=(N // tn, nk),
        in_specs=[pl.BlockSpec((M, tk), lambda j, k: (0, k))] * npair
        + [pl.BlockSpec((tn, tk), lambda j, k: (j, k))] * npair + [pl.BlockSpec(memory_space=pl.ANY)],
        out_specs=pl.BlockSpec((M, tn), lambda j, k: (0, j)),
        out_shape=jax.ShapeDtypeStruct((M, N), F32),
        scratch_shapes=[pltpu.VMEM((M, tn), F32)],
        compiler_params=_cparams("parallel", "arbitrary"),
    )(*xs, *ws, dep)


def _mm_tn(xt, dy, dep, alpha, ti, tn, name, stacked):
    Kin, M = xt.shape
    N = dy.shape[1]

    def body(xt_ref, dy_ref, dep_ref, ob_ref):
        r = jnp.dot(xt_ref[...], dy_ref[...], preferred_element_type=F32)
        if alpha != 1.0:
            r = alpha * r
        ob_ref[...] = r.astype(BF16)

    if stacked:
        ospec = pl.BlockSpec((None, ti, tn), lambda i, j: (j, i, 0))
        oshape = (N // tn, Kin, tn)
    else:
        ospec = pl.BlockSpec((ti, tn), lambda i, j: (i, j))
        oshape = (Kin, N)
    return pl.pallas_call(
        body, name=name, grid=(Kin // ti, N // tn),
        in_specs=[pl.BlockSpec((ti, M), lambda i, j: (i, 0)), pl.BlockSpec((M, tn), lambda i, j: (0, j)),
                  pl.BlockSpec(memory_space=pl.ANY)],
        out_specs=ospec,
        out_shape=jax.ShapeDtypeStruct(oshape, BF16),
        compiler_params=_cparams("parallel", "parallel"),
    )(xt, dy, dep)


def _rmsnorm_fwd(h, g, dep, name):
    M, D = h.shape
    tr = LANES

    def body(h_ref, g_ref, dep_ref, n_ref, nt_ref, r_ref):
        hv = h_ref[...]
        r = lax.rsqrt(jnp.mean(hv * hv, axis=-1, keepdims=True) + EPS)
        n = hv * r * g_ref[...]
        n_ref[...] = n.astype(BF16)
        nt_ref[...] = n.T.astype(BF16)
        r_ref[...] = r

    return pl.pallas_call(
        body, name=name, grid=(M // tr,),
        in_specs=[pl.BlockSpec((tr, D), lambda i: (i, 0)), pl.BlockSpec((1, D), lambda i: (0, 0)),
                  pl.BlockSpec(memory_space=pl.ANY)],
        out_specs=[pl.BlockSpec((tr, D), lambda i: (i, 0)), pl.BlockSpec((D, tr), lambda i: (0, i)),
                   pl.BlockSpec((tr, 1), lambda i: (i, 0))],
        out_shape=[jax.ShapeDtypeStruct((M, D), BF16), jax.ShapeDtypeStruct((D, M), BF16),
                   jax.ShapeDtypeStruct((M, 1), F32)],
        compiler_params=_cparams("parallel"),
    )(h, g, dep)


def _rmsnorm_bwd(dn, h, r, g, dh_prev, name):
    M, D = h.shape
    tr = _row_tile(M, D)

    def body(dn_ref, h_ref, r_ref, g_ref, dp_ref, dh_ref, dhb_ref, dg_ref):
        i = pl.program_id(0)
        dnv = dn_ref[...]
        hv = h_ref[...]
        rv = r_ref[...]
        w = dnv * g_ref[...]
        c = jnp.mean(w * hv, axis=-1, keepdims=True)
        dh = dp_ref[...] + rv * w - hv * (rv * rv * rv * c)
        dh_ref[...] = dh
        dhb_ref[...] = dh.astype(BF16)
        part = jnp.sum(dnv * (hv * rv), axis=0, keepdims=True)

        @pl.when(i == 0)
        def _():
            dg_ref[...] = part

        @pl.when(i > 0)
        def _():
            dg_ref[...] += part

    row = pl.BlockSpec((tr, D), lambda i: (i, 0))
    vec = pl.BlockSpec((1, D), lambda i: (0, 0))
    return pl.pallas_call(
        body, name=name, grid=(M // tr,),
        in_specs=[row, row, pl.BlockSpec((tr, 1), lambda i: (i, 0)), vec, row],
        out_specs=[row, row, vec],
        out_shape=[jax.ShapeDtypeStruct((M, D), F32), jax.ShapeDtypeStruct((M, D), BF16),
                   jax.ShapeDtypeStruct((1, D), F32)],
        compiler_params=_cparams("arbitrary"),
    )(dn, h, r, g, dh_prev)


def _loss_grad(h, tgt, seq, name):
    M, D = h.shape
    tr = _row_tile(M, D)

    def body(h_ref, t_ref, dh_ref, dhb_ref, loss_ref):
        i = pl.program_id(0)
        row = i * tr + lax.broadcasted_iota(jnp.int32, (tr, 1), 0)
        valid = (row >= N_META) & (row < N_META + seq)
        d = jnp.where(valid, h_ref[...] - t_ref[...], 0.0)
        dh = d * (1.0 / D)
        dh_ref[...] = dh
        dhb_ref[...] = dh.astype(BF16)
        part = (0.5 / D) * jnp.sum(jnp.sum(d * d, axis=1, keepdims=True), axis=0, keepdims=True)

        @pl.when(i == 0)
        def _():
            loss_ref[...] = part

        @pl.when(i > 0)
        def _():
            loss_ref[...] += part

    row = pl.BlockSpec((tr, D), lambda i: (i, 0))
    return pl.pallas_call(
        body, name=name, grid=(M // tr,),
        in_specs=[row, row],
        out_specs=[row, row, pl.BlockSpec((1, 1), lambda i: (0, 0))],
        out_shape=[jax.ShapeDtypeStruct((M, D), F32), jax.ShapeDtypeStruct((M, D), BF16),
                   jax.ShapeDtypeStruct((1, 1), F32)],
        compiler_params=_cparams("arbitrary"),
    )(h, tgt)


def _group_window(g):
    return jnp.where(g == 0, 2, jnp.where(g == 1, 4, jnp.where(g == 2, 8, 16)))


def _pool_fwd(z, pw, psc, name):
    M = z.shape[0]
    C = pw.shape[1]

    def body(p_ref, w_ref, sc_ref, pooled_ref, out_ref, outt_ref):
        g = pl.program_id(0)
        p = p_ref[...]
        t = lax.broadcasted_iota(jnp.int32, (M, 1), 0)
        s = p
        wsum = jnp.zeros_like(p)
        for step in range(N_POOL_GROUPS):
            sh = 1 << step
            s = s + jnp.where(t >= sh, pltpu.roll(s, sh, 0), 0.0)
            wsum = jnp.where(g == step, s, wsum)
        cnt = jnp.minimum(t + 1, _group_window(g)).astype(F32)
        pb = (wsum / cnt - p).astype(BF16)
        pooled_ref[...] = pb
        out = jnp.dot(pb, w_ref[...], preferred_element_type=F32) * sc_ref[...]
        out_ref[...] = out.astype(BF16)
        outt_ref[...] = out.T.astype(BF16)

    col = pl.BlockSpec((M, C), lambda g: (0, g))
    return pl.pallas_call(
        body, name=name, grid=(N_POOL_GROUPS,),
        in_specs=[col, pl.BlockSpec((None, C, C), lambda g: (g, 0, 0)), pl.BlockSpec((1, C), lambda g: (0, g))],
        out_specs=[col, col, pl.BlockSpec((C, M), lambda g: (g, 0))],
        out_shape=[jax.ShapeDtypeStruct((M, N_POOL_GROUPS * C), BF16),
                   jax.ShapeDtypeStruct((M, N_POOL_GROUPS * C), BF16),
                   jax.ShapeDtypeStruct((N_POOL_GROUPS * C, M), BF16)],
        compiler_params=_cparams("parallel"),
    )(z, pw, psc)


def _pool_bwd(dmix, pooled, pw, psc, name):
    M = dmix.shape[0]
    C = pw.shape[1]

    def body(dm_ref, pooled_ref, w_ref, sc_ref, dp_ref, dwb_ref, dsc_ref):
        g = pl.program_id(0)
        dmx = dm_ref[...]
        pb = pooled_ref[...]
        wv = w_ref[...]
        mixed = jnp.dot(pb, wv, preferred_element_type=F32)
        dsc_ref[...] = jnp.sum(dmx * mixed, axis=0, keepdims=True)
        dmixed = (dmx * sc_ref[...]).astype(BF16)
        dw = jnp.dot(pb.astype(F32).T.astype(BF16), dmixed, preferred_element_type=F32)
        dwb_ref[...] = dw.astype(BF16)
        dpooled = lax.dot_general(dmixed, wv, NT_DIMS, preferred_element_type=F32)
        t = lax.broadcasted_iota(jnp.int32, (M, 1), 0)
        cnt = jnp.minimum(t + 1, _group_window(g)).astype(F32)
        s = dpooled / cnt
        wsum = jnp.zeros_like(s)
        for step in range(N_POOL_GROUPS):
            sh = 1 << step
            s = s + jnp.where(t < M - sh, pltpu.roll(s, M - sh, 0), 0.0)
            wsum = jnp.where(g == step, s, wsum)
        dp_ref[...] = (wsum - dpooled).astype(BF16)

    col = pl.BlockSpec((M, C), lambda g: (0, g))
    wspec = pl.BlockSpec((None, C, C), lambda g: (g, 0, 0))
    vec = pl.BlockSpec((1, C), lambda g: (0, g))
    return pl.pallas_call(
        body, name=name, grid=(N_POOL_GROUPS,),
        in_specs=[col, col, wspec, vec],
        out_specs=[col, wspec, vec],
        out_shape=[jax.ShapeDtypeStruct((M, N_POOL_GROUPS * C), BF16),
                   jax.ShapeDtypeStruct((N_POOL_GROUPS, C, C), BF16),
                   jax.ShapeDtypeStruct((1, N_POOL_GROUPS * C), F32)],
        compiler_params=_cparams("parallel"),
    )(dmix, pooled, pw, psc)


def _qkv_prep(z, gq, gk, n_heads, q_col, name):
    M = z.shape[0]
    H = n_heads
    qb = q_col // HEAD_DIM

    def body(q_ref, k_ref, v_ref, gq_ref, gk_ref, qh_ref, kh_ref, vb_ref):
        def norm(xv, g):
            r = lax.rsqrt(jnp.mean(xv * xv, axis=-1, keepdims=True) + EPS)
            return (xv * r * g).astype(BF16)

        qh_ref[...] = norm(q_ref[...], gq_ref[...])
        kh_ref[...] = norm(k_ref[...], gk_ref[...])
        vb_ref[...] = v_ref[...].astype(BF16)

    vec = pl.BlockSpec((1, HEAD_DIM), lambda h: (0, 0))
    out = pl.BlockSpec((M, HEAD_DIM), lambda h: (0, h))
    oshape = jax.ShapeDtypeStruct((M, H * HEAD_DIM), BF16)
    return pl.pallas_call(
        body, name=name, grid=(H,),
        in_specs=[pl.BlockSpec((M, HEAD_DIM), lambda h: (0, qb + h)),
                  pl.BlockSpec((M, HEAD_DIM), lambda h: (0, qb + H + h)),
                  pl.BlockSpec((M, HEAD_DIM), lambda h: (0, qb + 2 * H + h)), vec, vec],
        out_specs=[out, out, out],
        out_shape=[oshape, oshape, oshape],
        compiler_params=_cparams("parallel"),
    )(z, z, z, gq, gk)


def _forget_fwd(z, bpad, f_block, name):
    M = z.shape[0]

    def body(f_ref, b_ref, cum_ref):
        xx = f_ref[...] + b_ref[...]
        c = jnp.minimum(xx, 0.0) - jnp.log(1.0 + jnp.exp(-jnp.abs(xx)))
        t = lax.broadcasted_iota(jnp.int32, (M, 1), 0)
        sh = 1
        while sh < M:
            c = c + jnp.where(t >= sh, pltpu.roll(c, sh, 0), 0.0)
            sh *= 2
        cum_ref[...] = c.T

    return pl.pallas_call(
        body, name=name, grid=(1,),
        in_specs=[pl.BlockSpec((M, LANES), lambda i: (0, f_block)), pl.BlockSpec((1, LANES), lambda i: (0, 0))],
        out_specs=pl.BlockSpec((LANES, M), lambda i: (0, 0)),
        out_shape=jax.ShapeDtypeStruct((LANES, M), F32),
        compiler_params=_cparams("arbitrary"),
    )(z, bpad)


def _col_to_row(col):
    n = col.shape[0]
    return jnp.transpose(jnp.broadcast_to(col, (n, LANES)))[0:1, :]


def _causal_extents(M):
    edges = list(range(0, M, CAUSAL_STEP)) + [M]
    return list(zip(edges[:-1], edges[1:]))


def _heads_per_step(n_heads):
    return 2 if n_heads % 2 == 0 else 1


def _attn_fwd(qh, kh, vb, cum_c, cum_r, name):
    M = qh.shape[0]
    H = qh.shape[1] // HEAD_DIM
    hp = _heads_per_step(H)
    scale = 1.0 / math.sqrt(HEAD_DIM)

    def body(q_ref, k_ref, v_ref, cq_ref, ck_ref, o_ref, ot_ref, lc_ref, lr_ref):
        i = pl.program_id(1)

        def compute(n):
            row = i * TQ + lax.broadcasted_iota(jnp.int32, (TQ, 1), 0)
            col = lax.broadcasted_iota(jnp.int32, (1, n), 1)
            for hh in range(hp):
                d0, d1 = hh * HEAD_DIM, (hh + 1) * HEAD_DIM
                s = lax.dot_general(q_ref[:, d0:d1], k_ref[0:n, d0:d1], NT_DIMS, preferred_element_type=F32) * scale
                s = s + (cq_ref[hh] - ck_ref[hh, :, 0:n])
                s = jnp.where(row >= col, s, NEG)
                m = jnp.max(s, axis=1, keepdims=True)
                p = jnp.exp(s - m)
                l = jnp.sum(p, axis=1, keepdims=True)
                pn = (p / l).astype(BF16)
                o = jnp.dot(pn, v_ref[0:n, d0:d1], preferred_element_type=F32)
                o_ref[:, d0:d1] = o.astype(BF16)
                ot_ref[d0:d1, :] = o.T.astype(BF16)
                lse = m + jnp.log(l)
                lc_ref[hh] = lse
                lr_ref[hh] = _col_to_row(lse)

        for lo, hi in _causal_extents(M):
            pl.when((i >= lo // TQ) & (i < hi // TQ))(functools.partial(compute, hi))

    full = pl.BlockSpec((M, hp * HEAD_DIM), lambda h, i: (0, h))
    tile = pl.BlockSpec((TQ, hp * HEAD_DIM), lambda h, i: (i, h))
    colv = pl.BlockSpec((hp, TQ, 1), lambda h, i: (h, i, 0))
    rowv_full = pl.BlockSpec((hp, 1, M), lambda h, i: (h, 0, 0))
    rowv = pl.BlockSpec((hp, 1, TQ), lambda h, i: (h, 0, i))
    return pl.pallas_call(
        body, name=name, grid=(H // hp, M // TQ),
        in_specs=[tile, full, full, colv, rowv_full],
        out_specs=[tile, pl.BlockSpec((hp * HEAD_DIM, TQ), lambda h, i: (h, i)), colv, rowv],
        out_shape=[jax.ShapeDtypeStruct((M, H * HEAD_DIM), BF16), jax.ShapeDtypeStruct((H * HEAD_DIM, M), BF16),
                   jax.ShapeDtypeStruct((H, M, 1), F32), jax.ShapeDtypeStruct((H, 1, M), F32)],
        compiler_params=_cparams("parallel", "parallel"),
    )(qh, kh, vb, cum_c, cum_r)


def _attn_bwd_q(qh, kh, vb, dob, cum_c, cum_r, lse_c, name):
    M = qh.shape[0]
    H = qh.shape[1] // HEAD_DIM
    hp = _heads_per_step(H)
    scale = 1.0 / math.sqrt(HEAD_DIM)

    def body(q_ref, k_ref, v_ref, do_ref, cq_ref, ck_ref, l_ref, dq_ref, dr_ref, dcq_ref):
        i = pl.program_id(1)

        def compute(n):
            row = i * TQ + lax.broadcasted_iota(jnp.int32, (TQ, 1), 0)
            col = lax.broadcasted_iota(jnp.int32, (1, n), 1)
            for hh in range(hp):
                d0, d1 = hh * HEAD_DIM, (hh + 1) * HEAD_DIM
                k = k_ref[0:n, d0:d1]
                s = lax.dot_general(q_ref[:, d0:d1], k, NT_DIMS, preferred_element_type=F32) * scale
                s = s + (cq_ref[hh] - ck_ref[hh, :, 0:n])
                p = jnp.exp(jnp.where(row >= col, s, NEG) - l_ref[hh])
                dp = lax.dot_general(do_ref[:, d0:d1], v_ref[0:n, d0:d1], NT_DIMS, preferred_element_type=F32)
                delta = jnp.sum(p * dp, axis=1, keepdims=True)
                ds = p * (dp - delta)
                dq_ref[:, d0:d1] = jnp.dot((ds * scale).astype(BF16), k, preferred_element_type=F32)
                dr_ref[hh] = _col_to_row(delta)
                dcq_ref[hh] = jnp.sum(ds, axis=1, keepdims=True)

        for lo, hi in _causal_extents(M):
            pl.when((i >= lo // TQ) & (i < hi // TQ))(functools.partial(compute, hi))

    full = pl.BlockSpec((M, hp * HEAD_DIM), lambda h, i: (0, h))
    tile = pl.BlockSpec((TQ, hp * HEAD_DIM), lambda h, i: (i, h))
    colv = pl.BlockSpec((hp, TQ, 1), lambda h, i: (h, i, 0))
    rowv_full = pl.BlockSpec((hp, 1, M), lambda h, i: (h, 0, 0))
    rowv = pl.BlockSpec((hp, 1, TQ), lambda h, i: (h, 0, i))
    return pl.pallas_call(
        body, name=name, grid=(H // hp, M // TQ),
        in_specs=[tile, full, full, tile, colv, rowv_full, colv],
        out_specs=[tile, rowv, colv],
        out_shape=[jax.ShapeDtypeStruct((M, H * HEAD_DIM), F32), jax.ShapeDtypeStruct((H, 1, M), F32),
                   jax.ShapeDtypeStruct((H, M, 1), F32)],
        compiler_params=_cparams("parallel", "parallel"),
    )(qh, kh, vb, dob, cum_c, cum_r, lse_c)


def _attn_bwd_kv(qh, kh, vb, dob, cum_c, cum_r, lse_r, delta_r, name):
    M = qh.shape[0]
    H = qh.shape[1] // HEAD_DIM
    hp = _heads_per_step(H)
    scale = 1.0 / math.sqrt(HEAD_DIM)

    def body(k_ref, v_ref, q_ref, do_ref, cq_ref, ck_ref, l_ref, d_ref, dk_ref, dv_ref, dck_ref):
        j = pl.program_id(1)

        def compute(q0):
            krow = j * TQ + lax.broadcasted_iota(jnp.int32, (TQ, 1), 0)
            qcol = q0 + lax.broadcasted_iota(jnp.int32, (1, M - q0), 1)
            for hh in range(hp):
                d0, d1 = hh * HEAD_DIM, (hh + 1) * HEAD_DIM
                q = q_ref[q0:M, d0:d1]
                do = do_ref[q0:M, d0:d1]
                st = lax.dot_general(k_ref[:, d0:d1], q, NT_DIMS, preferred_element_type=F32) * scale
                st = st + (cq_ref[hh, :, q0:M] - ck_ref[hh])
                pt = jnp.exp(jnp.where(qcol >= krow, st, NEG) - l_ref[hh, :, q0:M])
                dpt = lax.dot_general(v_ref[:, d0:d1], do, NT_DIMS, preferred_element_type=F32)
                dst = pt * (dpt - d_ref[hh, :, q0:M])
                dv_ref[:, d0:d1] = jnp.dot(pt.astype(BF16), do, preferred_element_type=F32).astype(BF16)
                dk_ref[:, d0:d1] = jnp.dot((dst * scale).astype(BF16), q, preferred_element_type=F32)
                dck_ref[hh] = -jnp.sum(dst, axis=1, keepdims=True)

        for lo, hi in _causal_extents(M):
            pl.when((j >= lo // TQ) & (j < hi // TQ))(functools.partial(compute, lo))

    full = pl.BlockSpec((M, hp * HEAD_DIM), lambda h, j: (0, h))
    tile = pl.BlockSpec((TQ, hp * HEAD_DIM), lambda h, j: (j, h))
    colv = pl.BlockSpec((hp, TQ, 1), lambda h, j: (h, j, 0))
    rowv_full = pl.BlockSpec((hp, 1, M), lambda h, j: (h, 0, 0))
    return pl.pallas_call(
        body, name=name, grid=(H // hp, M // TQ),
        in_specs=[tile, tile, full, full, rowv_full, colv, rowv_full, rowv_full],
        out_specs=[tile, tile, colv],
        out_shape=[jax.ShapeDtypeStruct((M, H * HEAD_DIM), F32), jax.ShapeDtypeStruct((M, H * HEAD_DIM), BF16),
                   jax.ShapeDtypeStruct((H, M, 1), F32)],
        compiler_params=_cparams("parallel", "parallel"),
    )(kh, vb, qh, dob, cum_r, cum_c, lse_r, delta_r)


def _qk_norm_bwd(dqh, dkh, z, gq, gk, n_heads, q_col, name):
    M = z.shape[0]
    H = n_heads
    qb = q_col // HEAD_DIM

    def body(dqh_ref, dkh_ref, q_ref, k_ref, gq_ref, gk_ref, dq_ref, dk_ref, dgq_ref, dgk_ref):
        h = pl.program_id(0)

        def one(dy, xv, g):
            r = lax.rsqrt(jnp.mean(xv * xv, axis=-1, keepdims=True) + EPS)
            w = dy * g
            c = jnp.mean(w * xv, axis=-1, keepdims=True)
            dx = r * w - xv * (r * r * r * c)
            return dx.astype(BF16), jnp.sum(dy * (xv * r), axis=0, keepdims=True)

        dq, dgq = one(dqh_ref[...], q_ref[...], gq_ref[...])
        dk, dgk = one(dkh_ref[...], k_ref[...], gk_ref[...])
        dq_ref[...] = dq
        dk_ref[...] = dk

        @pl.when(h == 0)
        def _():
            dgq_ref[...] = dgq
            dgk_ref[...] = dgk

        @pl.when(h > 0)
        def _():
            dgq_ref[...] += dgq
            dgk_ref[...] += dgk

    vec = pl.BlockSpec((1, HEAD_DIM), lambda h: (0, 0))
    head = pl.BlockSpec((M, HEAD_DIM), lambda h: (0, h))
    return pl.pallas_call(
        body, name=name, grid=(H,),
        in_specs=[head, head, pl.BlockSpec((M, HEAD_DIM), lambda h: (0, qb + h)),
                  pl.BlockSpec((M, HEAD_DIM), lambda h: (0, qb + H + h)), vec, vec],
        out_specs=[head, head, vec, vec],
        out_shape=[jax.ShapeDtypeStruct((M, H * HEAD_DIM), BF16), jax.ShapeDtypeStruct((M, H * HEAD_DIM), BF16),
                   jax.ShapeDtypeStruct((1, HEAD_DIM), F32), jax.ShapeDtypeStruct((1, HEAD_DIM), F32)],
        compiler_params=_cparams("arbitrary"),
    )(dqh, dkh, z, z, gq, gk)


def _forget_bwd(dcq, dck, z, bpad, f_block, name):
    H, M, _ = dcq.shape

    def body(dcq_ref, dck_ref, f_ref, b_ref, dfl_ref, db_ref):
        lane = lax.broadcasted_iota(jnp.int32, (1, LANES), 1)
        d = jnp.zeros((M, LANES), F32)
        for h in range(H):
            d = d + (dcq_ref[h] + dck_ref[h]) * (lane == h).astype(F32)
        t = lax.broadcasted_iota(jnp.int32, (M, 1), 0)
        sh = 1
        while sh < M:
            d = d + jnp.where(t < M - sh, pltpu.roll(d, M - sh, 0), 0.0)
            sh *= 2
        xx = f_ref[...] + b_ref[...]
        dfl = d * (1.0 / (1.0 + jnp.exp(xx)))
        dfl_ref[...] = dfl.astype(BF16)
        db_ref[...] = jnp.sum(dfl, axis=0, keepdims=True)

    colv = pl.BlockSpec((H, M, 1), lambda i: (0, 0, 0))
    return pl.pallas_call(
        body, name=name, grid=(1,),
        in_specs=[colv, colv, pl.BlockSpec((M, LANES), lambda i: (0, f_block)),
                  pl.BlockSpec((1, LANES), lambda i: (0, 0))],
        out_specs=[pl.BlockSpec((M, LANES), lambda i: (0, 0)), pl.BlockSpec((1, LANES), lambda i: (0, 0))],
        out_shape=[jax.ShapeDtypeStruct((M, LANES), BF16), jax.ShapeDtypeStruct((1, LANES), F32)],
        compiler_params=_cparams("arbitrary"),
    )(dcq, dck, z, bpad)


def _ffn_fwd(h, g, dep, get_weights, request, down_is_early, tag):
    n, nt, r = _rmsnorm_fwd(h, g, dep, f"{tag}_norm")
    up = get_weights(f"{tag}_up", n)
    dep = request(f"{tag}_down", n) if down_is_early else n
    a, b, s, st = _ffn_up(n, up["wg"], up["wu"], dep, 256, f"{tag}_up")
    wd = get_weights(f"{tag}_down", s)["wd"]
    h_out = _mm_nn_residual(s, wd, h, s, 0.5, 512, wd.shape[0] // 4, f"{tag}_down")
    return h_out, (nt, r, a, b, st, up["wg"], up["wu"], wd)


def _ffn_bwd(dh, dhb, h, g, saved, dep, put_grads, tag):
    nt, r, a, b, st, wg, wu, wd = saved
    n_shards = 4
    da, db = _ffn_bwd_hidden(dhb, wd, a, b, dep, 256, f"{tag}_bwd_hidden")
    dwd = _mm_tn(st, dhb, dep, 0.5, st.shape[0] // n_shards, 1024, f"{tag}_dw_down", stacked=False)
    dep = put_grads(f"{tag}_w_down", dwd)
    dwg = _mm_tn(nt, da, dep, 1.0, 1024, wg.shape[1] // n_shards, f"{tag}_dw_gate", stacked=True)
    dep = put_grads(f"{tag}_w_gate", dwg)
    dwu = _mm_tn(nt, db, dep, 1.0, 1024, wu.shape[1] // n_shards, f"{tag}_dw_up", stacked=True)
    dep = put_grads(f"{tag}_w_up", dwu)
    dn = _mm_nt_sum([da, db], [wg, wu], dep, 512, wg.shape[1] // 4, f"{tag}_dn")
    dh_in, dhb_in, dg = _rmsnorm_bwd(dn, h, r, g, dh, f"{tag}_norm_bwd")
    return dh_in, dhb_in, dg


def _local_step(x, target, S, get_weights, request, put_grads):
    seq, D = x.shape
    L = N_META + seq
    Lp = -(-L // SEQ_ALIGN) * SEQ_ALIGN
    pad = jnp.zeros((Lp - L, D), F32)
    tgt = jnp.concatenate([jnp.zeros((N_META, D), F32), target, pad], axis=0)

    d_pool = S["pool_scale"].shape[1]
    n_heads = S["b_forget"].shape[1]
    d_att = n_heads * HEAD_DIM
    f_col = d_pool + 3 * d_att
    f_block = f_col // LANES
    bpad = jnp.pad(S["b_forget"], ((0, 0), (0, LANES - n_heads)))

    h0 = jnp.concatenate([get_weights("meta", None)["meta"], x, pad], axis=0)
    h1, ffn1 = _ffn_fwd(h0, S["ffn1_norm"], h0, get_weights, request, False, "ffn1")
    u, ut, r_mix = _rmsnorm_fwd(h1, S["mix_norm"], request("mix", h1), "mix_norm")
    Wm = get_weights("mix", u)
    z = _mm_nn(u, Wm["win"], 384, "in_proj")
    pooled, pool_out, pool_out_t = _pool_fwd(z, Wm["pool_w"], S["pool_scale"], "pool_fwd")
    qh, kh, vb = _qkv_prep(z, S["q_norm"], S["k_norm"], n_heads, d_pool, "qkv_prep")
    cum_t = _forget_fwd(z, bpad, f_block, "forget_fwd")[:n_heads]
    cum_c = cum_t.reshape(n_heads, Lp, 1)
    cum_r = cum_t.reshape(n_heads, 1, Lp)
    att, att_t, lse_c, lse_r = _attn_fwd(qh, kh, vb, cum_c, cum_r, "attn_fwd")
    mix = jnp.concatenate([pool_out, att], axis=1)
    mix_t = jnp.concatenate([pool_out_t, att_t], axis=0)
    dep = request("ffn2_up", att)
    h2 = _mm_nn_residual(mix, Wm["wout"], h1, dep, 1.0, 512, 1024, "out_proj")
    h3, ffn2 = _ffn_fwd(h2, S["ffn2_norm"], h2, get_weights, request, False, "ffn2")

    dh3, dh3b, loss = _loss_grad(h3, tgt, seq, "loss")
    dh2, dh2b, dg_ffn2 = _ffn_bwd(dh3, dh3b, h2, S["ffn2_norm"], ffn2, loss, put_grads, "ffn2")

    dmix = _mm_nt(dh2b, Wm["wout"], loss, 512, "out_proj_bwd")
    dwout = _mm_tn(mix_t, dh2b, loss, 1.0, 1024, 1024, "dw_out", stacked=False)
    dp, dpw, dpsc = _pool_bwd(dmix, pooled, Wm["pool_w"], S["pool_scale"], "pool_bwd")
    dob = dmix[:, d_pool:].astype(BF16)
    dqh, delta_r, dcq = _attn_bwd_q(qh, kh, vb, dob, cum_c, cum_r, lse_c, "attn_bwd_q")
    dkh, dv, dck = _attn_bwd_kv(qh, kh, vb, dob, cum_c, cum_r, lse_r, delta_r, "attn_bwd_kv")
    dq, dk, dgq, dgk = _qk_norm_bwd(dqh, dkh, z, S["q_norm"], S["k_norm"], n_heads, d_pool, "qk_norm_bwd")
    dfl, dbf = _forget_bwd(dcq, dck, z, bpad, f_block, "forget_bwd")
    dz = jnp.concatenate([dp, dq, dk, dv, dfl], axis=1)
    dwin = _mm_tn(ut, dz, loss, 1.0, 1024, dz.shape[1] // 3, "dw_in", stacked=False)
    dep = put_grads("mix", dict(win=dwin, wout=dwout, pool_w=dpw))
    du = _mm_nt_sum([dz], [Wm["win"]], dep, 512, Wm["win"].shape[1] // 3, "in_proj_bwd")
    dh1, dh1b, dg_mix = _rmsnorm_bwd(du, h1, r_mix, S["mix_norm"], dh2, "mix_norm_bwd")

    dh0, _, dg_ffn1 = _ffn_bwd(dh1, dh1b, h0, S["ffn1_norm"], ffn1, loss, put_grads, "ffn1")

    grads = dict(
        x=dh0[N_META:L], meta=dh0[:N_META],
        ffn1_norm=dg_ffn1, mix_norm=dg_mix, ffn2_norm=dg_ffn2, q_norm=dgq, k_norm=dgk,
        b_forget=dbf[:, :n_heads], pool_scale=dpsc,
    )
    return loss[0, 0], dh0, grads


HBM_SPEC = pl.BlockSpec(memory_space=pltpu.HBM)
N_CHIPS = 4


def _chip_peers():
    x, y, c = lax.axis_index("x"), lax.axis_index("y"), lax.axis_index("c")
    flips = [(1 - x, y), (x, 1 - y), (1 - x, 1 - y)]
    return 2 * x + y, [((px, py, c), 2 * px + py) for px, py in flips]


def _gathered_shape(shape, layout):
    if layout == "rows":
        return (N_CHIPS * shape[0],) + shape[1:]
    if layout == "cols":
        return (shape[0], N_CHIPS * shape[1])
    return (N_CHIPS,) + shape


def _cast_place(place, w, dep, layout, dtype, name):
    R, C = w.shape
    tr = _row_tile(R, C)
    nt = R // tr

    def body(place_ref, w_ref, dep_ref, o_ref):
        o_ref[...] = w_ref[...].astype(dtype)

    if layout == "rows":
        ospec = pl.BlockSpec((tr, C), lambda i, p: (p[1] * nt + i, 0))
    elif layout == "cols":
        ospec = pl.BlockSpec((tr, C), lambda i, p: (i, p[1]))
    else:
        ospec = pl.BlockSpec((None, tr, C), lambda i, p: (p[1], i, 0))
    return pl.pallas_call(
        body, name=name,
        grid_spec=pltpu.PrefetchScalarGridSpec(
            num_scalar_prefetch=1, grid=(nt,),
            in_specs=[pl.BlockSpec((tr, C), lambda i, p: (i, 0)), pl.BlockSpec(memory_space=pl.ANY)],
            out_specs=ospec),
        out_shape=jax.ShapeDtypeStruct(_gathered_shape((R, C), layout), dtype),
        compiler_params=_cparams("parallel"),
    )(place, w, dep)


SEM_SPEC = pl.BlockSpec(memory_space=pltpu.SEMAPHORE)
ANY_SPEC = pl.BlockSpec(memory_space=pl.ANY)
SPLIT_COPY = pltpu.CompilerParams(has_side_effects=pltpu.SideEffectType.DATAFLOW_SIDE_EFFECTING)


def _hbm(a):
    return pltpu.with_memory_space_constraint(a, pltpu.HBM)


def _gather_region(refs, shard_shapes, layouts, a, chip, half):
    rows_a = shard_shapes[a][0]
    h = rows_a // 2
    if layouts[a] == "rows":
        return refs[a].at[pl.ds(chip * rows_a + half * h, h)]
    if layouts[a] == "cols":
        cols_a = shard_shapes[a][1]
        return refs[a].at[pl.ds(half * h, h), pl.ds(chip * cols_a, cols_a)]
    return refs[a].at[chip, pl.ds(half * h, h)]


def _gather_start(bufs, after, shard_shapes, layouts, name):
    n = len(bufs)
    ns = 3 * n

    def body(*refs):
        in_refs = refs[:n]
        send_sems = refs[n + 1:n + 1 + ns]
        recv_sems = refs[n + 1 + ns:n + 1 + 2 * ns]
        token = refs[2 * n + 1 + 2 * ns]
        c = lax.axis_index("c")
        me, peers = _chip_peers()
        for a in range(n):
            mine = _gather_region(in_refs, shard_shapes, layouts, a, me, c)
            for k, (dev, _) in enumerate(peers):
                pltpu.make_async_remote_copy(
                    src_ref=mine, dst_ref=mine, send_sem=send_sems[3 * a + k], recv_sem=recv_sems[3 * a + k],
                    device_id=dev, device_id_type=MESH).start()
        token[...] = jnp.zeros_like(token)

    sem = pltpu.SemaphoreType.DMA(())
    out = pl.pallas_call(
        body, name=name,
        out_shape=(*[sem] * (2 * ns), *[pltpu.HBM(b.shape, b.dtype) for b in bufs],
                   jax.ShapeDtypeStruct((8, LANES), F32)),
        in_specs=[HBM_SPEC] * n + [ANY_SPEC],
        out_specs=(*[SEM_SPEC] * (2 * ns), *[HBM_SPEC] * n, pl.BlockSpec(memory_space=pltpu.VMEM)),
        input_output_aliases={a: 2 * ns + a for a in range(n)},
        compiler_params=SPLIT_COPY,
    )(*[_hbm(b) for b in bufs], after)
    return list(out[:ns]), list(out[ns:2 * ns]), list(out[2 * ns:2 * ns + n]), out[2 * ns + n]


def _gather_wait(bufs, send_sems, recv_sems, afters, shard_shapes, layouts, name):
    n = len(bufs)
    ns = 3 * n
    na = len(afters)

    def body(*refs):
        in_refs = refs[:n]
        send_sems = refs[n:n + ns]
        recv_sems = refs[n + ns:n + 2 * ns]
        token = refs[2 * n + 2 * ns + na]
        token[...] = jnp.zeros_like(token)
        c = lax.axis_index("c")
        me, peers = _chip_peers()
        for a in range(n):
            mine = _gather_region(in_refs, shard_shapes, layouts, a, me, c)
            for k, (dev, pidx) in enumerate(peers):
                landed = _gather_region(in_refs, shard_shapes, layouts, a, pidx, c)
                pltpu.make_async_remote_copy(
                    src_ref=mine, dst_ref=landed, send_sem=send_sems[3 * a + k], recv_sem=recv_sems[3 * a + k],
                    device_id=dev, device_id_type=MESH).wait_recv()
        for a in range(n):
            mine = _gather_region(in_refs, shard_shapes, layouts, a, me, c)
            for k, (dev, _) in enumerate(peers):
                pltpu.make_async_remote_copy(
                    src_ref=mine, dst_ref=mine, send_sem=send_sems[3 * a + k], recv_sem=recv_sems[3 * a + k],
                    device_id=dev, device_id_type=MESH).wait_send()

    out = pl.pallas_call(
        body, name=name,
        out_shape=(*[pltpu.HBM(b.shape, b.dtype) for b in bufs], jax.ShapeDtypeStruct((8, LANES), F32)),
        in_specs=[HBM_SPEC] * n + [SEM_SPEC] * (2 * ns) + [ANY_SPEC] * na,
        out_specs=(*[HBM_SPEC] * n, pl.BlockSpec(memory_space=pltpu.VMEM)),
        input_output_aliases={a: a for a in range(n)},
        compiler_params=SPLIT_COPY,
    )(*bufs, *send_sems, *recv_sems, *afters)
    return list(out[:n]), out[n]


def _forward_start(bufs, after, shard_shapes, layouts, name):
    n = len(bufs)
    ns = 3 * n

    def body(*refs):
        in_refs = refs[:n]
        send_sems = refs[n + 1:n + 1 + ns]
        recv_sems = refs[n + 1 + ns:n + 1 + 2 * ns]
        token = refs[2 * n + 1 + 2 * ns]
        c = lax.axis_index("c")
        sib = (lax.axis_index("x"), lax.axis_index("y"), 1 - c)
        _, peers = _chip_peers()
        for a in range(n):
            for k, (_, pidx) in enumerate(peers):
                landed = _gather_region(in_refs, shard_shapes, layouts, a, pidx, c)
                pltpu.make_async_remote_copy(
                    src_ref=landed, dst_ref=landed, send_sem=send_sems[3 * a + k], recv_sem=recv_sems[3 * a + k],
                    device_id=sib, device_id_type=MESH).start()
        token[...] = jnp.zeros_like(token)

    sem = pltpu.SemaphoreType.DMA(())
    out = pl.pallas_call(
        body, name=name,
        out_shape=(*[sem] * (2 * ns), *[pltpu.HBM(b.shape, b.dtype) for b in bufs],
                   jax.ShapeDtypeStruct((8, LANES), F32)),
        in_specs=[HBM_SPEC] * n + [ANY_SPEC],
        out_specs=(*[SEM_SPEC] * (2 * ns), *[HBM_SPEC] * n, pl.BlockSpec(memory_space=pltpu.VMEM)),
        input_output_aliases={a: 2 * ns + a for a in range(n)},
        compiler_params=SPLIT_COPY,
    )(*[_hbm(b) for b in bufs], after)
    return list(out[:ns]), list(out[ns:2 * ns]), list(out[2 * ns:2 * ns + n]), out[2 * ns + n]


def _forward_wait(bufs, send_sems, recv_sems, after, shard_shapes, layouts, name):
    n = len(bufs)
    ns = 3 * n

    def body(*refs):
        in_refs = refs[:n]
        send_sems = refs[n:n + ns]
        recv_sems = refs[n + ns:n + 2 * ns]
        c = lax.axis_index("c")
        sib = (lax.axis_index("x"), lax.axis_index("y"), 1 - c)
        _, peers = _chip_peers()
        for a in range(n):
            for k, (_, pidx) in enumerate(peers):
                landed = _gather_region(in_refs, shard_shapes, layouts, a, pidx, c)
                other = _gather_region(in_refs, shard_shapes, layouts, a, pidx, 1 - c)
                cp = pltpu.make_async_remote_copy(
                    src_ref=landed, dst_ref=other, send_sem=send_sems[3 * a + k], recv_sem=recv_sems[3 * a + k],
                    device_id=sib, device_id_type=MESH)
                cp.wait_recv()
                cp.wait_send()

    return list(pl.pallas_call(
        body, name=name,
        out_shape=tuple(pltpu.HBM(b.shape, b.dtype) for b in bufs),
        in_specs=[HBM_SPEC] * n + [SEM_SPEC] * (2 * ns) + [ANY_SPEC],
        out_specs=tuple([HBM_SPEC] * n),
        input_output_aliases={a: a for a in range(n)},
        compiler_params=SPLIT_COPY,
    )(*bufs, *send_sems, *recv_sems, after))


def _halves_copies(src_refs, land_refs, send_sems, recv_sems):
    c = lax.axis_index("c")
    sib = (lax.axis_index("x"), lax.axis_index("y"), 1 - c)
    copies = []
    for a, (src, land) in enumerate(zip(src_refs, land_refs)):
        h = src.shape[1] // 2
        copies.append(pltpu.make_async_remote_copy(
            src_ref=src.at[:, pl.ds((1 - c) * h, h)], dst_ref=land, send_sem=send_sems[a], recv_sem=recv_sems[a],
            device_id=sib, device_id_type=MESH))
    return copies


def _whole_copies(src_refs, land_refs, send_sems, recv_sems):
    sib = (lax.axis_index("x"), lax.axis_index("y"), 1 - lax.axis_index("c"))
    return [pltpu.make_async_remote_copy(src_ref=src, dst_ref=land, send_sem=send_sems[a], recv_sem=recv_sems[a],
                                         device_id=sib, device_id_type=MESH)
            for a, (src, land) in enumerate(zip(src_refs, land_refs))]


def _halves_land_shape(shape):
    return (shape[0], shape[1] // 2, shape[2])


def _sibling_start(stacked, copies, land_shape, name):
    n = len(stacked)
    lands = [lax.empty(land_shape(s.shape), s.dtype) for s in stacked]

    def body(*refs):
        for cp in copies(refs[:n], refs[n:2 * n], refs[2 * n:3 * n], refs[3 * n:4 * n]):
            cp.start()
        token = refs[6 * n]
        token[...] = jnp.zeros_like(token)

    sem = pltpu.SemaphoreType.DMA(())
    out = pl.pallas_call(
        body, name=name,
        out_shape=(*[sem] * (2 * n), *[pltpu.HBM(b.shape, b.dtype) for b in stacked],
                   *[pltpu.HBM(b.shape, b.dtype) for b in lands], jax.ShapeDtypeStruct((8, LANES), F32)),
        in_specs=[HBM_SPEC] * (2 * n),
        out_specs=(*[SEM_SPEC] * (2 * n), *[HBM_SPEC] * (2 * n), pl.BlockSpec(memory_space=pltpu.VMEM)),
        input_output_aliases={a: 2 * n + a for a in range(2 * n)},
        compiler_params=SPLIT_COPY,
    )(*[_hbm(b) for b in stacked], *[_hbm(b) for b in lands])
    return list(out[:n]), list(out[n:2 * n]), list(out[2 * n:3 * n]), list(out[3 * n:4 * n]), out[4 * n]


def _sibling_wait(srcs, lands, send_sems, recv_sems, after, copies_of, name):
    n = len(srcs)

    def body(*refs):
        copies = copies_of(refs[:n], refs[n:2 * n], refs[2 * n:3 * n], refs[3 * n:4 * n])
        for cp in copies:
            cp.wait_recv()
        for cp in copies:
            cp.wait_send()

    out = pl.pallas_call(
        body, name=name,
        out_shape=tuple(pltpu.HBM(b.shape, b.dtype) for b in list(srcs) + list(lands)),
        in_specs=[HBM_SPEC] * (2 * n) + [SEM_SPEC] * (2 * n) + [ANY_SPEC],
        out_specs=tuple([HBM_SPEC] * (2 * n)),
        input_output_aliases={a: a for a in range(2 * n)},
        compiler_params=SPLIT_COPY,
    )(*srcs, *lands, *send_sems, *recv_sems, after)
    return list(out[:n]), list(out[n:])


def _scatter_start(stacked, name):
    n = len(stacked)
    ns = 3 * n
    lands = [lax.empty((3,) + s.shape[1:], s.dtype) for s in stacked]

    def body(*refs):
        src_refs = refs[:n]
        land_refs = refs[n:2 * n]
        send_sems = refs[2 * n:2 * n + ns]
        recv_sems = refs[2 * n + ns:2 * n + 2 * ns]
        token = refs[4 * n + 2 * ns]
        _, peers = _chip_peers()
        for a in range(n):
            for k, (dev, pidx) in enumerate(peers):
                pltpu.make_async_remote_copy(
                    src_ref=src_refs[a].at[k], dst_ref=land_refs[a].at[k], send_sem=send_sems[3 * a + k],
                    recv_sem=recv_sems[3 * a + k], device_id=dev, device_id_type=MESH).start()
        token[...] = jnp.zeros_like(token)

    sem = pltpu.SemaphoreType.DMA(())
    out = pl.pallas_call(
        body, name=name,
        out_shape=(*[sem] * (2 * ns), *[pltpu.HBM(b.shape, b.dtype) for b in stacked],
                   *[pltpu.HBM(b.shape, b.dtype) for b in lands], jax.ShapeDtypeStruct((8, LANES), F32)),
        in_specs=[HBM_SPEC] * (2 * n),
        out_specs=(*[SEM_SPEC] * (2 * ns), *[HBM_SPEC] * (2 * n), pl.BlockSpec(memory_space=pltpu.VMEM)),
        input_output_aliases={a: 2 * ns + a for a in range(2 * n)},
        compiler_params=SPLIT_COPY,
    )(*[_hbm(b) for b in stacked], *[_hbm(b) for b in lands])
    o = 2 * ns
    return list(out[:ns]), list(out[ns:o]), list(out[o:o + n]), list(out[o + n:o + 2 * n]), out[o + 2 * n]


def _scatter_wait(srcs, lands, send_sems, recv_sems, after, name):
    n = len(srcs)
    ns = 3 * n

    def body(*refs):
        src_refs = refs[:n]
        land_refs = refs[n:2 * n]
        send_sems = refs[2 * n:2 * n + ns]
        recv_sems = refs[2 * n + ns:2 * n + 2 * ns]
        _, peers = _chip_peers()
        copies = [
            pltpu.make_async_remote_copy(
                src_ref=src_refs[a].at[k], dst_ref=land_refs[a].at[k], send_sem=send_sems[3 * a + k],
                recv_sem=recv_sems[3 * a + k], device_id=dev, device_id_type=MESH)
            for a in range(n) for k, (dev, pidx) in enumerate(peers)]
        for cp in copies:
            cp.wait_recv()
        for cp in copies:
            cp.wait_send()

    out = pl.pallas_call(
        body, name=name,
        out_shape=tuple(pltpu.HBM(b.shape, b.dtype) for b in list(srcs) + list(lands)),
        in_specs=[HBM_SPEC] * (2 * n) + [SEM_SPEC] * (2 * ns) + [ANY_SPEC],
        out_specs=tuple([HBM_SPEC] * (2 * n)),
        input_output_aliases={a: a for a in range(2 * n)},
        compiler_params=SPLIT_COPY,
    )(*srcs, *lands, *send_sems, *recv_sems, after)
    return list(out[n:])


def _all_reduce_small(v):
    R, C = v.shape
    n_dev = 8

    def body(v_ref, o_ref, buf, send_sems, recv_sems):
        x, y, c = lax.axis_index("x"), lax.axis_index("y"), lax.axis_index("c")
        me = 4 * x + 2 * y + c
        buf[me] = v_ref[...]
        sends = []
        for k in range(1, n_dev):
            px, py, pc = x ^ ((k >> 2) & 1), y ^ ((k >> 1) & 1), c ^ (k & 1)
            cp = pltpu.make_async_remote_copy(
                src_ref=v_ref, dst_ref=buf.at[me], send_sem=send_sems.at[k - 1], recv_sem=recv_sems.at[k - 1],
                device_id=(px, py, pc), device_id_type=MESH)
            cp.start()
            sends.append((cp, 4 * px + 2 * py + pc))
        for k in range(1, n_dev):
            cp, pidx = sends[k - 1]
            pltpu.make_async_remote_copy(
                src_ref=v_ref, dst_ref=buf.at[pidx], send_sem=send_sems.at[k - 1], recv_sem=recv_sems.at[k - 1],
                device_id=(x, y, c), device_id_type=MESH).wait_recv()
        for cp, _ in sends:
            cp.wait_send()
        acc = buf[0]
        for d in range(1, n_dev):
            acc = acc + buf[d]
        o_ref[...] = acc

    vm = pl.BlockSpec(memory_space=pltpu.VMEM)
    return pl.pallas_call(
        body, name="all_reduce_small",
        in_specs=[vm], out_specs=vm,
        out_shape=jax.ShapeDtypeStruct((R, C), F32),
        scratch_shapes=[pltpu.VMEM((n_dev, R, C), F32), pltpu.SemaphoreType.DMA((n_dev - 1,)),
                        pltpu.SemaphoreType.DMA((n_dev - 1,))],
    )(v)


def _pair_sum(place, own, sib, name):
    S, R, C = own.shape
    h = R // 2
    tr = _row_tile(h, C)
    nt = h // tr

    def body(place_ref, o_ref, s_ref, out_ref):
        out_ref[...] = (o_ref[...].astype(F32) + s_ref[...].astype(F32)).astype(BF16)

    return pl.pallas_call(
        body, name=name,
        grid_spec=pltpu.PrefetchScalarGridSpec(
            num_scalar_prefetch=1, grid=(3, nt),
            in_specs=[pl.BlockSpec((None, tr, C), lambda k, i, p: (p[2 + k], p[0] * nt + i, 0)),
                      pl.BlockSpec((None, tr, C), lambda k, i, p: (p[2 + k], i, 0))],
            out_specs=pl.BlockSpec((None, tr, C), lambda k, i, p: (k, i, 0))),
        out_shape=jax.ShapeDtypeStruct((3, h, C), BF16),
        compiler_params=_cparams("parallel", "parallel"),
    )(place, own, sib)


def _sum_slabs(place, own, sib, recv, name):
    S, R, C = own.shape
    h = R // 2
    tr = _row_tile(h, C)
    nt = h // tr

    def body(place_ref, o_ref, s_ref, r_ref, out_ref):
        acc = o_ref[...].astype(F32) + s_ref[...].astype(F32)
        for k in range(3):
            acc = acc + r_ref[k].astype(F32)
        out_ref[...] = acc

    return pl.pallas_call(
        body, name=name,
        grid_spec=pltpu.PrefetchScalarGridSpec(
            num_scalar_prefetch=1, grid=(nt,),
            in_specs=[pl.BlockSpec((None, tr, C), lambda i, p: (p[1], p[0] * nt + i, 0)),
                      pl.BlockSpec((None, tr, C), lambda i, p: (p[1], i, 0)),
                      pl.BlockSpec((3, tr, C), lambda i, p: (0, i, 0))],
            out_specs=pl.BlockSpec((tr, C), lambda i, p: (i, 0))),
        out_shape=jax.ShapeDtypeStruct((h, C), F32),
        compiler_params=_cparams("parallel"),
    )(place, own, sib, recv)


def _adamw(parts, w, m, v, name):
    R, C = w.shape
    tr = _row_tile(R, C)
    npart = len(parts)
    c1 = 1.0 - ADAM_B1 ** ADAM_STEP
    c2 = 1.0 - ADAM_B2 ** ADAM_STEP

    def body(*refs):
        p_refs = refs[:npart]
        w_ref, m_ref, v_ref, g_ref, d_ref, nm_ref, nv_ref = refs[npart:]
        g = p_refs[0][...]
        for p_ref in p_refs[1:]:
            g = g + p_ref[...]
        nm = ADAM_B1 * m_ref[...] + (1.0 - ADAM_B1) * g
        nv = ADAM_B2 * v_ref[...] + (1.0 - ADAM_B2) * (g * g)
        m_hat = nm / c1
        v_hat = nv / c2
        g_ref[...] = g
        d_ref[...] = -ADAM_LR * (m_hat / (jnp.sqrt(v_hat) + ADAM_EPS) + ADAM_WD * w_ref[...])
        nm_ref[...] = nm
        nv_ref[...] = nv

    blk = pl.BlockSpec((tr, C), lambda i: (i, 0))
    shape = jax.ShapeDtypeStruct((R, C), F32)
    return pl.pallas_call(
        body, name=name, grid=(R // tr,),
        in_specs=[blk] * (npart + 3), out_specs=[blk] * 4, out_shape=[shape] * 4,
        compiler_params=_cparams("parallel"),
    )(*parts, w, m, v)


def _adamw_halves(place, mine, other, w, m, v, name):
    R, C = w.shape
    h = R // 2
    tr = _row_tile(h, C)
    nt = h // tr
    c1 = 1.0 - ADAM_B1 ** ADAM_STEP
    c2 = 1.0 - ADAM_B2 ** ADAM_STEP

    def body(place_ref, mine_ref, other_ref, w_ref, m_ref, v_ref, g_ref, d_ref, nm_ref, nv_ref):
        is_mine = (pl.program_id(0) // nt) == place_ref[0]
        g = jnp.where(is_mine, mine_ref[...], other_ref[...])
        nm = ADAM_B1 * m_ref[...] + (1.0 - ADAM_B1) * g
        nv = ADAM_B2 * v_ref[...] + (1.0 - ADAM_B2) * (g * g)
        m_hat = nm / c1
        v_hat = nv / c2
        g_ref[...] = g
        d_ref[...] = -ADAM_LR * (m_hat / (jnp.sqrt(v_hat) + ADAM_EPS) + ADAM_WD * w_ref[...])
        nm_ref[...] = nm
        nv_ref[...] = nv

    def half_block(which):
        def index(i, p):
            first = p[0] if which == 0 else 1 - p[0]
            return jnp.clip(i - first * nt, 0, nt - 1), 0

        return pl.BlockSpec((tr, C), index)

    blk = pl.BlockSpec((tr, C), lambda i, p: (i, 0))
    shape = jax.ShapeDtypeStruct((R, C), F32)
    return pl.pallas_call(
        body, name=name,
        grid_spec=pltpu.PrefetchScalarGridSpec(
            num_scalar_prefetch=1, grid=(2 * nt,),
            in_specs=[half_block(0), half_block(1), blk, blk, blk], out_specs=[blk] * 4),
        out_shape=[shape] * 4,
        compiler_params=_cparams("parallel"),
    )(place, mine, other, w, m, v)


SMALL_NAMES = ("ffn1_norm", "mix_norm", "ffn2_norm", "pool_scale", "q_norm", "k_norm", "b_forget")
SMALL_COLS = 1024
LOSS_LANE = 512


def _pack_small(vals):
    rows = [vals[n].reshape(-1, SMALL_COLS) for n in ("ffn1_norm", "mix_norm", "ffn2_norm", "pool_scale")]
    tail = jnp.concatenate([vals["q_norm"].reshape(-1), vals["k_norm"].reshape(-1), vals["b_forget"].reshape(-1)])
    rows.append(jnp.pad(tail, (0, SMALL_COLS - tail.shape[0])).reshape(1, SMALL_COLS))
    return jnp.concatenate(rows, axis=0)


def _unpack_small(packed, like):
    out = {}
    r = 0
    for n in ("ffn1_norm", "mix_norm", "ffn2_norm", "pool_scale"):
        k = like[n].size // SMALL_COLS
        out[n] = packed[r:r + k].reshape(like[n].shape)
        r += k
    o = 0
    for n in ("q_norm", "k_norm", "b_forget"):
        k = like[n].size
        out[n] = packed[r, o:o + k].reshape(like[n].shape)
        o += k
    return out


def kernel(x, meta_tokens, ffn1_norm, ffn1_w_gate, ffn1_w_up, ffn1_w_down, mix_norm, w_in, b_forget, q_norm, k_norm, pool_w, pool_scale, w_out, ffn2_norm, ffn2_w_gate, ffn2_w_up, ffn2_w_down, loss_target, m_meta_tokens, m_ffn1_norm, m_ffn1_w_gate, m_ffn1_w_up, m_ffn1_w_down, m_mix_norm, m_w_in, m_b_forget, m_q_norm, m_k_norm, m_pool_w, m_pool_scale, m_w_out, m_ffn2_norm, m_ffn2_w_gate, m_ffn2_w_up, m_ffn2_w_down, v_meta_tokens, v_ffn1_norm, v_ffn1_w_gate, v_ffn1_w_up, v_ffn1_w_down, v_mix_norm, v_w_in, v_b_forget, v_q_norm, v_k_norm, v_pool_w, v_pool_scale, v_w_out, v_ffn2_norm, v_ffn2_w_gate, v_ffn2_w_up, v_ffn2_w_down):
    wts = dict(meta_tokens=meta_tokens, ffn1_norm=ffn1_norm, ffn1_w_gate=ffn1_w_gate, ffn1_w_up=ffn1_w_up,
               ffn1_w_down=ffn1_w_down, mix_norm=mix_norm, w_in=w_in, b_forget=b_forget, q_norm=q_norm,
               k_norm=k_norm, pool_w=pool_w, pool_scale=pool_scale, w_out=w_out, ffn2_norm=ffn2_norm,
               ffn2_w_gate=ffn2_w_gate, ffn2_w_up=ffn2_w_up, ffn2_w_down=ffn2_w_down)
    mom = dict(meta_tokens=m_meta_tokens, ffn1_norm=m_ffn1_norm, ffn1_w_gate=m_ffn1_w_gate, ffn1_w_up=m_ffn1_w_up,
               ffn1_w_down=m_ffn1_w_down, mix_norm=m_mix_norm, w_in=m_w_in, b_forget=m_b_forget, q_norm=m_q_norm,
               k_norm=m_k_norm, pool_w=m_pool_w, pool_scale=m_pool_scale, w_out=m_w_out, ffn2_norm=m_ffn2_norm,
               ffn2_w_gate=m_ffn2_w_gate, ffn2_w_up=m_ffn2_w_up, ffn2_w_down=m_ffn2_w_down)
    var = dict(meta_tokens=v_meta_tokens, ffn1_norm=v_ffn1_norm, ffn1_w_gate=v_ffn1_w_gate, ffn1_w_up=v_ffn1_w_up,
               ffn1_w_down=v_ffn1_w_down, mix_norm=v_mix_norm, w_in=v_w_in, b_forget=v_b_forget, q_norm=v_q_norm,
               k_norm=v_k_norm, pool_w=v_pool_w, pool_scale=v_pool_scale, w_out=v_w_out, ffn2_norm=v_ffn2_norm,
               ffn2_w_gate=v_ffn2_w_gate, ffn2_w_up=v_ffn2_w_up, ffn2_w_down=v_ffn2_w_down)
    order = list(wts)
    me = 2 * lax.axis_index("x") + lax.axis_index("y")

    D = x.shape[2]
    d_in_shard = w_in.shape[2]
    d_in = N_CHIPS * d_in_shard
    n_heads = b_forget.shape[1]
    d_in_pad = (d_in - n_heads) + LANES

    stages = dict(meta=("meta_tokens",), ffn1_up=("ffn1_w_gate", "ffn1_w_up"), ffn1_down=("ffn1_w_down",),
                  mix=("w_in", "w_out", "pool_w"),
                  ffn2_up=("ffn2_w_gate", "ffn2_w_up"), ffn2_down=("ffn2_w_down",))
    stage_order = list(stages)
    xi, yi = lax.axis_index("x"), lax.axis_index("y")
    place = jnp.stack([lax.axis_index("c"), me, 2 * (1 - xi) + yi, 2 * xi + 1 - yi, 2 * (1 - xi) + 1 - yi]).astype(
        jnp.int32)
    layouts = dict(ffn1_w_gate="cols", ffn1_w_up="cols", ffn1_w_down="rows", w_in="stack", w_out="rows",
                   pool_w="stack", ffn2_w_gate="cols", ffn2_w_up="cols", ffn2_w_down="rows", meta_tokens="stack")
    shards2d = {n: wts[n].reshape(-1, wts[n].shape[-1]) for n in layouts}

    def place_stage(stage, dep):
        return [_cast_place(place, shards2d[n], dep, layouts[n], F32 if n == "meta_tokens" else BF16, f"place_{n}")
                for n in stages[stage]]

    def start_stage(stage, bufs, after):
        shapes = [shards2d[n].shape for n in stages[stage]]
        lays = [layouts[n] for n in stages[stage]]
        return _gather_start(bufs, after, shapes, lays, f"gather_start_{stage}") + (shapes, lays)

    flight = {stage_order[0]: start_stage(stage_order[0], place_stage(stage_order[0], place), place)}
    placed = {stage_order[1]: place_stage(stage_order[1], flight[stage_order[0]][3])}

    def cols(st):
        return jnp.transpose(st, (1, 0, 2)).reshape(st.shape[1], -1)

    forwarding = {}

    def request(stage, after):
        k = stage_order.index(stage)
        send_sems, recv_sems, bufs, _, shapes, lays = flight.pop(stage)
        afters = ([] if after is None else [after]) + [b for st in placed for b in placed[st]]
        if k == 1:
            afters += [mom["w_in"].reshape(shards2d["w_in"].shape), var["w_in"].reshape(shards2d["w_in"].shape)]
        landed, token = _gather_wait(bufs, send_sems, recv_sems, afters, shapes, lays, f"gather_wait_{stage}")
        if k + 1 < len(stage_order):
            flight[stage_order[k + 1]] = start_stage(stage_order[k + 1], placed.pop(stage_order[k + 1]), token)
            token = flight[stage_order[k + 1]][3]
        if k == 0:
            placed.update({st: place_stage(st, token) for st in stage_order[2:]})
        forwarding[stage] = _forward_start(landed, token, shapes, lays, f"forward_start_{stage}") + (shapes, lays)
        return forwarding[stage][3]

    def get_weights(stage, after):
        if stage not in forwarding:
            request(stage, after)
        send_sems, recv_sems, bufs, token, shapes, lays = forwarding.pop(stage)
        full = _forward_wait(bufs, send_sems, recv_sems, token if after is None else after, shapes, lays,
                             f"forward_wait_{stage}")
        G = dict(zip(stages[stage], full))
        if stage == "meta":
            return dict(meta=cols(G["meta_tokens"]))
        if stage == "ffn1_up":
            return dict(wg=G["ffn1_w_gate"], wu=G["ffn1_w_up"])
        if stage == "ffn2_up":
            return dict(wg=G["ffn2_w_gate"], wu=G["ffn2_w_up"])
        if stage != "mix":
            return dict(wd=G[stages[stage][0]])
        return dict(
            win=jnp.pad(cols(G["w_in"]), ((0, 0), (0, d_in_pad - d_in))), wout=G["w_out"],
            pool_w=jnp.transpose(G["pool_w"].reshape((N_CHIPS,) + pool_w.shape[1:]), (1, 0, 2, 3)).reshape(
                N_POOL_GROUPS, pool_w.shape[3], pool_w.shape[3]))

    def split_rows(a):
        return a.reshape(N_CHIPS, -1, a.shape[1])

    def split_win(a):
        return jnp.transpose(a[:, :d_in].reshape(D, N_CHIPS, d_in_shard), (1, 0, 2))

    def split_pool(a):
        r, c = pool_w.shape[2], pool_w.shape[3]
        return jnp.transpose(a.reshape(N_POOL_GROUPS, N_CHIPS, r, c), (1, 0, 2, 3)).reshape(N_CHIPS, -1, c)

    scatter = {}
    pending = []

    def finish_pending(after):
        name, names, (send_sems, recv_sems, srcs, lands) = pending.pop()
        own, from_sib = _sibling_wait(srcs, lands, send_sems, recv_sems, after, _halves_copies,
                                      f"halves_wait_{name}")
        pair = [_pair_sum(place, o, s, f"pair_sum_{n}") for n, o, s in zip(names, own, from_sib)]
        send_sems, recv_sems, srcs, lands, token = _scatter_start(pair, f"scatter_start_{name}")
        scatter[name] = (names, own, from_sib, send_sems, recv_sems, srcs, lands, token)
        return token

    def put_grads(name, g):
        if name == "mix":
            names = stages["mix"]
            own = [split_win(g["win"]), split_rows(g["wout"]), split_pool(g["pool_w"])]
        else:
            names = (name,)
            own = [split_rows(g) if name.endswith("_down") else g]
        *flying, token = _sibling_start(own, _halves_copies, _halves_land_shape, f"halves_start_{name}")
        if pending:
            token = finish_pending(token)
        pending.append((name, names, flying))
        return token

    small = dict(ffn1_norm=ffn1_norm, mix_norm=mix_norm, ffn2_norm=ffn2_norm, q_norm=q_norm, k_norm=k_norm,
                 b_forget=b_forget, pool_scale=pool_scale)
    loss_part, dh0, gr = _local_step(x[0], loss_target[0], small, get_weights, request, put_grads)

    out_g, out_d, out_m, out_v = {}, {}, {}, {}

    def update(stage, names, flying, after):
        send_sems, recv_sems, srcs, lands = flying
        halves, other_halves = _sibling_wait(srcs, lands, send_sems, recv_sems, after, _whole_copies,
                                             f"swap_wait_{stage}")
        for n, mine, other in zip(names, halves, other_halves):
            shape = wts[n].shape
            res = _adamw_halves(place, mine, other, shards2d[n], mom[n].reshape(shards2d[n].shape),
                                var[n].reshape(shards2d[n].shape), f"adamw_{n}")
            out_g[n], out_d[n], out_m[n], out_v[n] = (a.reshape(shape) for a in res)
        return res[3]

    after = finish_pending(dh0)
    swapping = None
    for stage in scatter:
        names, own, from_sib, send_sems, recv_sems, srcs, lands, _ = scatter[stage]
        received = _scatter_wait(srcs, lands, send_sems, recv_sems, after, f"scatter_wait_{stage}")
        halves = [_sum_slabs(place, o, s, r, f"sum_{n}") for n, o, s, r in zip(names, own, from_sib, received)]
        *flying, after = _sibling_start(halves, _whole_copies, lambda shape: shape, f"swap_start_{stage}")
        if swapping is not None:
            after = update(*swapping, after)
        swapping = (stage, names, flying)
    update(*swapping, after)

    small_g = _pack_small({n: gr[n] for n in SMALL_NAMES})
    n_small = small_g.shape[0]
    small_g = small_g.at[n_small - 1, LOSS_LANE].set(loss_part)
    meta_rows = gr["meta"].reshape(-1, SMALL_COLS)
    total = _all_reduce_small(jnp.concatenate([small_g, meta_rows], axis=0))
    loss = total[n_small - 1, LOSS_LANE]
    res = _adamw([total[:n_small]], _pack_small({n: wts[n] for n in SMALL_NAMES}),
                 _pack_small({n: mom[n] for n in SMALL_NAMES}), _pack_small({n: var[n] for n in SMALL_NAMES}),
                 "adamw_small")
    for dst, packed in zip((out_g, out_d, out_m, out_v), res):
        dst.update(_unpack_small(packed, wts))
    meta_cols = meta_tokens.shape[1]
    meta_g = lax.dynamic_slice_in_dim(total[n_small:].reshape(N_META, D), me * meta_cols, meta_cols, axis=1)
    res = _adamw([meta_g], meta_tokens, m_meta_tokens, v_meta_tokens, "adamw_meta")
    out_g["meta_tokens"], out_d["meta_tokens"], out_m["meta_tokens"], out_v["meta_tokens"] = res

    grad_x = gr["x"].reshape(x.shape)
    return (loss, grad_x, *[out_g[n] for n in order], *[out_d[n] for n in order], *[out_m[n] for n in order],
            *[out_v[n] for n in order])
```

```python
import functools
import math

import jax
import jax.numpy as jnp
from jax import lax
from jax.experimental import pallas as pl
from jax.experimental.pallas import tpu as pltpu

F32 = jnp.float32
BF16 = jnp.bfloat16

N_META = 16
EPS = 1e-6
HEAD_DIM = 128
N_POOL_GROUPS = 4
LANES = 128
SEQ_ALIGN = 128
TQ = 128
CAUSAL_STEP = 512
VMEM_LIMIT = 56 * 1024 * 1024
ELEMWISE_BLOCK_BYTES = 2304 * 1024

ADAM_LR = 0.001
ADAM_B1 = 0.9
ADAM_B2 = 0.999
ADAM_EPS = 1e-08
ADAM_WD = 0.01
ADAM_STEP = 10

NT_DIMS = (((1,), (1,)), ((), ()))
NEG = -1e30
MESH = pl.DeviceIdType.MESH


def _cparams(*sem):
    return pltpu.CompilerParams(dimension_semantics=sem, vmem_limit_bytes=VMEM_LIMIT)


def _sigmoid(a):
    return 1.0 / (1.0 + jnp.exp(-a))


def _row_tile(rows, cols, itemsize=4):
    best = None
    for t in range(16, rows + 1, 16):
        if rows % t == 0 and t * cols * itemsize <= ELEMWISE_BLOCK_BYTES:
            best = t
    return best if best is not None else rows


def _mm_nn(x, w, tn, name):
    M, K = x.shape
    N = w.shape[1]

    def body(x_ref, w_ref, o_ref):
        o_ref[...] = jnp.dot(x_ref[...], w_ref[...], preferred_element_type=F32)

    return pl.pallas_call(
        body, name=name, grid=(N // tn,),
        in_specs=[pl.BlockSpec((M, K), lambda j: (0, 0)), pl.BlockSpec((K, tn), lambda j: (0, j))],
        out_specs=pl.BlockSpec((M, tn), lambda j: (0, j)),
        out_shape=jax.ShapeDtypeStruct((M, N), F32),
        compiler_params=_cparams("parallel"),
    )(x, w)


def _ffn_up(n, wg, wu, dep, tn, name):
    M, K = n.shape
    N = wg.shape[1]

    def body(n_ref, wg_ref, wu_ref, dep_ref, p_ref, q_ref, s_ref, st_ref):
        nv = n_ref[...]
        a = jnp.dot(nv, wg_ref[...], preferred_element_type=F32)
        b = jnp.dot(nv, wu_ref[...], preferred_element_type=F32)
        sig = _sigmoid(a)
        silu = a * sig
        p_ref[...] = (b * (sig * (1.0 + a * (1.0 - sig)))).astype(BF16)
        q_ref[...] = silu.astype(BF16)
        s = silu * b
        s_ref[...] = s.astype(BF16)
        st_ref[...] = s.T.astype(BF16)

    wspec = pl.BlockSpec((K, tn), lambda j: (0, j))
    ospec = pl.BlockSpec((M, tn), lambda j: (0, j))
    return pl.pallas_call(
        body, name=name, grid=(N // tn,),
        in_specs=[pl.BlockSpec((M, K), lambda j: (0, 0)), wspec, wspec, pl.BlockSpec(memory_space=pl.ANY)],
        out_specs=[ospec, ospec, ospec, pl.BlockSpec((tn, M), lambda j: (j, 0))],
        out_shape=[jax.ShapeDtypeStruct((M, N), BF16), jax.ShapeDtypeStruct((M, N), BF16),
                   jax.ShapeDtypeStruct((M, N), BF16), jax.ShapeDtypeStruct((N, M), BF16)],
        compiler_params=_cparams("parallel"),
    )(n, wg, wu, dep)


def _mm_nn_residual(x, w, res, dep, alpha, tn, name):
    M, K = x.shape
    N = w.shape[1]

    def body(x_ref, w_ref, r_ref, dep_ref, o_ref):
        o_ref[...] = r_ref[...] + alpha * jnp.dot(x_ref[...], w_ref[...], preferred_element_type=F32)

    return pl.pallas_call(
        body, name=name, grid=(N // tn,),
        in_specs=[pl.BlockSpec((M, K), lambda j: (0, 0), pipeline_mode=pl.Buffered(1)),
                  pl.BlockSpec((K, tn), lambda j: (0, j)), pl.BlockSpec((M, tn), lambda j: (0, j)),
                  pl.BlockSpec(memory_space=pl.ANY)],
        out_specs=pl.BlockSpec((M, tn), lambda j: (0, j)),
        out_shape=jax.ShapeDtypeStruct((M, N), F32),
        compiler_params=_cparams("parallel"),
    )(x, w, res, dep)


def _ffn_bwd_hidden(dhb, wd, p, q, dep, tn, name):
    M, K = dhb.shape
    N = wd.shape[0]

    def body(dh_ref, w_ref, p_ref, q_ref, dep_ref, da_ref, db_ref):
        ds = 0.5 * lax.dot_general(dh_ref[...], w_ref[...], NT_DIMS, preferred_element_type=F32)
        da_ref[...] = (ds * p_ref[...].astype(F32)).astype(BF16)
        db_ref[...] = (ds * q_ref[...].astype(F32)).astype(BF16)

    ospec = pl.BlockSpec((M, tn), lambda j: (0, j))
    return pl.pallas_call(
        body, name=name, grid=(N // tn,),
        in_specs=[pl.BlockSpec((M, K), lambda j: (0, 0)), pl.BlockSpec((tn, K), lambda j: (j, 0)), ospec, ospec,
                  pl.BlockSpec(memory_space=pl.ANY)],
        out_specs=[ospec, ospec],
        out_shape=[jax.ShapeDtypeStruct((M, N), BF16), jax.ShapeDtypeStruct((M, N), BF16)],
        compiler_params=_cparams("parallel"),
    )(dhb, wd, p, q, dep)


def _mm_nt(x, w, dep, tn, name):
    M, K = x.shape
    N = w.shape[0]

    def body(x_ref, w_ref, dep_ref, o_ref):
        o_ref[...] = lax.dot_general(x_ref[...], w_ref[...], NT_DIMS, preferred_element_type=F32)

    return pl.pallas_call(
        body, name=name, grid=(N // tn,),
        in_specs=[pl.BlockSpec((M, K), lambda j: (0, 0)), pl.BlockSpec((tn, K), lambda j: (j, 0)),
                  pl.BlockSpec(memory_space=pl.ANY)],
        out_specs=pl.BlockSpec((M, tn), lambda j: (0, j)),
        out_shape=jax.ShapeDtypeStruct((M, N), F32),
        compiler_params=_cparams("parallel"),
    )(x, w, dep)


def _mm_nt_sum(xs, ws, dep, tn, tk, name):
    npair = len(xs)
    M, K = xs[0].shape
    N = ws[0].shape[0]
    nk = K // tk

    def body(*refs):
        x_refs = refs[:npair]
        w_refs = refs[npair:2 * npair]
        o_ref = refs[2 * npair + 1]
        acc = refs[2 * npair + 2]
        k = pl.program_id(1)

        @pl.when(k == 0)
        def _():
            acc[...] = jnp.zeros_like(acc)

        for x_ref, w_ref in zip(x_refs, w_refs):
            acc[...] += lax.dot_general(x_ref[...], w_ref[...], NT_DIMS, preferred_element_type=F32)

        @pl.when(k == nk - 1)
        def _():
            o_ref[...] = acc[...]

    return pl.pallas_call(
        body, name=name, grid=(N // tn, nk),
        in_specs=[pl.BlockSpec((M, tk), lambda j, k: (0, k))] * npair
        + [pl.BlockSpec((tn, tk), lambda j, k: (j, k))] * npair + [pl.BlockSpec(memory_space=pl.ANY)],
        out_specs=pl.BlockSpec((M, tn), lambda j, k: (0, j)),
        out_shape=jax.ShapeDtypeStruct((M, N), F32),
        scratch_shapes=[pltpu.VMEM((M, tn), F32)],
        compiler_params=_cparams("parallel", "arbitrary"),
    )(*xs, *ws, dep)


def _mm_tn(xt, dy, dep, alpha, ti, tn, name, stacked):
    Kin, M = xt.shape
    N = dy.shape[1]

    def body(xt_ref, dy_ref, dep_ref, ob_ref):
        r = jnp.dot(xt_ref[...], dy_ref[...], preferred_element_type=F32)
        if alpha != 1.0:
            r = alpha * r
        ob_ref[...] = r.astype(BF16)

    if stacked:
        ospec = pl.BlockSpec((None, ti, tn), lambda i, j: (j, i, 0))
        oshape = (N // tn, Kin, tn)
    else:
        ospec = pl.BlockSpec((ti, tn), lambda i, j: (i, j))
        oshape = (Kin, N)
    return pl.pallas_call(
        body, name=name, grid=(Kin // ti, N // tn),
        in_specs=[pl.BlockSpec((ti, M), lambda i, j: (i, 0)), pl.BlockSpec((M, tn), lambda i, j: (0, j)),
                  pl.BlockSpec(memory_space=pl.ANY)],
        out_specs=ospec,
        out_shape=jax.ShapeDtypeStruct(oshape, BF16),
        compiler_params=_cparams("parallel", "parallel"),
    )(xt, dy, dep)


def _rmsnorm_fwd(h, g, dep, name):
    M, D = h.shape
    tr = LANES

    def body(h_ref, g_ref, dep_ref, n_ref, nt_ref, r_ref):
        hv = h_ref[...]
        r = lax.rsqrt(jnp.mean(hv * hv, axis=-1, keepdims=True) + EPS)
        n = hv * r * g_ref[...]
        n_ref[...] = n.astype(BF16)
        nt_ref[...] = n.T.astype(BF16)
        r_ref[...] = r

    return pl.pallas_call(
        body, name=name, grid=(M // tr,),
        in_specs=[pl.BlockSpec((tr, D), lambda i: (i, 0)), pl.BlockSpec((1, D), lambda i: (0, 0)),
                  pl.BlockSpec(memory_space=pl.ANY)],
        out_specs=[pl.BlockSpec((tr, D), lambda i: (i, 0)), pl.BlockSpec((D, tr), lambda i: (0, i)),
                   pl.BlockSpec((tr, 1), lambda i: (i, 0))],
        out_shape=[jax.ShapeDtypeStruct((M, D), BF16), jax.ShapeDtypeStruct((D, M), BF16),
                   jax.ShapeDtypeStruct((M, 1), F32)],
        compiler_params=_cparams("parallel"),
    )(h, g, dep)


def _rmsnorm_bwd(dn, h, r, g, dh_prev, name):
    M, D = h.shape
    tr = _row_tile(M, D)

    def body(dn_ref, h_ref, r_ref, g_ref, dp_ref, dh_ref, dhb_ref, dg_ref):
        i = pl.program_id(0)
        dnv = dn_ref[...]
        hv = h_ref[...]
        rv = r_ref[...]
        w = dnv * g_ref[...]
        c = jnp.mean(w * hv, axis=-1, keepdims=True)
        dh = dp_ref[...] + rv * w - hv * (rv * rv * rv * c)
        dh_ref[...] = dh
        dhb_ref[...] = dh.astype(BF16)
        part = jnp.sum(dnv * (hv * rv), axis=0, keepdims=True)

        @pl.when(i == 0)
        def _():
            dg_ref[...] = part

        @pl.when(i > 0)
        def _():
            dg_ref[...] += part

    row = pl.BlockSpec((tr, D), lambda i: (i, 0))
    vec = pl.BlockSpec((1, D), lambda i: (0, 0))
    return pl.pallas_call(
        body, name=name, grid=(M // tr,),
        in_specs=[row, row, pl.BlockSpec((tr, 1), lambda i: (i, 0)), vec, row],
        out_specs=[row, row, vec],
        out_shape=[jax.ShapeDtypeStruct((M, D), F32), jax.ShapeDtypeStruct((M, D), BF16),
                   jax.ShapeDtypeStruct((1, D), F32)],
        compiler_params=_cparams("arbitrary"),
    )(dn, h, r, g, dh_prev)


def _loss_grad(h, tgt, seq, name):
    M, D = h.shape
    tr = _row_tile(M, D)

    def body(h_ref, t_ref, dh_ref, dhb_ref, loss_ref):
        i = pl.program_id(0)
        row = i * tr + lax.broadcasted_iota(jnp.int32, (tr, 1), 0)
        valid = (row >= N_META) & (row < N_META + seq)
        d = jnp.where(valid, h_ref[...] - t_ref[...], 0.0)
        dh = d * (1.0 / D)
        dh_ref[...] = dh
        dhb_ref[...] = dh.astype(BF16)
        part = (0.5 / D) * jnp.sum(jnp.sum(d * d, axis=1, keepdims=True), axis=0, keepdims=True)

        @pl.when(i == 0)
        def _():
            loss_ref[...] = part

        @pl.when(i > 0)
        def _():
            loss_ref[...] += part

    row = pl.BlockSpec((tr, D), lambda i: (i, 0))
    return pl.pallas_call(
        body, name=name, grid=(M // tr,),
        in_specs=[row, row],
        out_specs=[row, row, pl.BlockSpec((1, 1), lambda i: (0, 0))],
        out_shape=[jax.ShapeDtypeStruct((M, D), F32), jax.ShapeDtypeStruct((M, D), BF16),
                   jax.ShapeDtypeStruct((1, 1), F32)],
        compiler_params=_cparams("arbitrary"),
    )(h, tgt)


def _group_window(g):
    return jnp.where(g == 0, 2, jnp.where(g == 1, 4, jnp.where(g == 2, 8, 16)))


def _pool_fwd(z, pw, psc, name):
    M = z.shape[0]
    C = pw.shape[1]

    def body(p_ref, w_ref, sc_ref, pooled_ref, out_ref, outt_ref):
        g = pl.program_id(0)
        p = p_ref[...]
        t = lax.broadcasted_iota(jnp.int32, (M, 1), 0)
        s = p
        wsum = jnp.zeros_like(p)
        for step in range(N_POOL_GROUPS):
            sh = 1 << step
            s = s + jnp.where(t >= sh, pltpu.roll(s, sh, 0), 0.0)
            wsum = jnp.where(g == step, s, wsum)
        cnt = jnp.minimum(t + 1, _group_window(g)).astype(F32)
        pb = (wsum / cnt - p).astype(BF16)
        pooled_ref[...] = pb
        out = jnp.dot(pb, w_ref[...], preferred_element_type=F32) * sc_ref[...]
        out_ref[...] = out.astype(BF16)
        outt_ref[...] = out.T.astype(BF16)

    col = pl.BlockSpec((M, C), lambda g: (0, g))
    return pl.pallas_call(
        body, name=name, grid=(N_POOL_GROUPS,),
        in_specs=[col, pl.BlockSpec((None, C, C), lambda g: (g, 0, 0)), pl.BlockSpec((1, C), lambda g: (0, g))],
        out_specs=[col, col, pl.BlockSpec((C, M), lambda g: (g, 0))],
        out_shape=[jax.ShapeDtypeStruct((M, N_POOL_GROUPS * C), BF16),
                   jax.ShapeDtypeStruct((M, N_POOL_GROUPS * C), BF16),
                   jax.ShapeDtypeStruct((N_POOL_GROUPS * C, M), BF16)],
        compiler_params=_cparams("parallel"),
    )(z, pw, psc)


def _pool_bwd(dmix, pooled, pw, psc, name):
    M = dmix.shape[0]
    C = pw.shape[1]

    def body(dm_ref, pooled_ref, w_ref, sc_ref, dp_ref, dwb_ref, dsc_ref):
        g = pl.program_id(0)
        dmx = dm_ref[...]
        pb = pooled_ref[...]
        wv = w_ref[...]
        mixed = jnp.dot(pb, wv, preferred_element_type=F32)
        dsc_ref[...] = jnp.sum(dmx * mixed, axis=0, keepdims=True)
        dmixed = (dmx * sc_ref[...]).astype(BF16)
        dw = jnp.dot(pb.astype(F32).T.astype(BF16), dmixed, preferred_element_type=F32)
        dwb_ref[...] = dw.astype(BF16)
        dpooled = lax.dot_general(dmixed, wv, NT_DIMS, preferred_element_type=F32)
        t = lax.broadcasted_iota(jnp.int32, (M, 1), 0)
        cnt = jnp.minimum(t + 1, _group_window(g)).astype(F32)
        s = dpooled / cnt
        wsum = jnp.zeros_like(s)
        for step in range(N_POOL_GROUPS):
            sh = 1 << step
            s = s + jnp.where(t < M - sh, pltpu.roll(s, M - sh, 0), 0.0)
            wsum = jnp.where(g == step, s, wsum)
        dp_ref[...] = (wsum - dpooled).astype(BF16)

    col = pl.BlockSpec((M, C), lambda g: (0, g))
    wspec = pl.BlockSpec((None, C, C), lambda g: (g, 0, 0))
    vec = pl.BlockSpec((1, C), lambda g: (0, g))
    return pl.pallas_call(
        body, name=name, grid=(N_POOL_GROUPS,),
        in_specs=[col, col, wspec, vec],
        out_specs=[col, wspec, vec],
        out_shape=[jax.ShapeDtypeStruct((M, N_POOL_GROUPS * C), BF16),
                   jax.ShapeDtypeStruct((N_POOL_GROUPS, C, C), BF16),
                   jax.ShapeDtypeStruct((1, N_POOL_GROUPS * C), F32)],
        compiler_params=_cparams("parallel"),
    )(dmix, pooled, pw, psc)


def _qkv_prep(z, gq, gk, n_heads, q_col, name):
    M = z.shape[0]
    H = n_heads
    qb = q_col // HEAD_DIM

    def body(q_ref, k_ref, v_ref, gq_ref, gk_ref, qh_ref, kh_ref, vb_ref):
        def norm(xv, g):
            r = lax.rsqrt(jnp.mean(xv * xv, axis=-1, keepdims=True) + EPS)
            return (xv * r * g).astype(BF16)

        qh_ref[...] = norm(q_ref[...], gq_ref[...])
        kh_ref[...] = norm(k_ref[...], gk_ref[...])
        vb_ref[...] = v_ref[...].astype(BF16)

    vec = pl.BlockSpec((1, HEAD_DIM), lambda h: (0, 0))
    out = pl.BlockSpec((M, HEAD_DIM), lambda h: (0, h))
    oshape = jax.ShapeDtypeStruct((M, H * HEAD_DIM), BF16)
    return pl.pallas_call(
        body, name=name, grid=(H,),
        in_specs=[pl.BlockSpec((M, HEAD_DIM), lambda h: (0, qb + h)),
                  pl.BlockSpec((M, HEAD_DIM), lambda h: (0, qb + H + h)),
                  pl.BlockSpec((M, HEAD_DIM), lambda h: (0, qb + 2 * H + h)), vec, vec],
        out_specs=[out, out, out],
        out_shape=[oshape, oshape, oshape],
        compiler_params=_cparams("parallel"),
    )(z, z, z, gq, gk)


def _forget_fwd(z, bpad, f_block, name):
    M = z.shape[0]

    def body(f_ref, b_ref, cum_ref):
        xx = f_ref[...] + b_ref[...]
        c = jnp.minimum(xx, 0.0) - jnp.log(1.0 + jnp.exp(-jnp.abs(xx)))
        t = lax.broadcasted_iota(jnp.int32, (M, 1), 0)
        sh = 1
        while sh < M:
            c = c + jnp.where(t >= sh, pltpu.roll(c, sh, 0), 0.0)
            sh *= 2
        cum_ref[...] = c.T

    return pl.pallas_call(
        body, name=name, grid=(1,),
        in_specs=[pl.BlockSpec((M, LANES), lambda i: (0, f_block)), pl.BlockSpec((1, LANES), lambda i: (0, 0))],
        out_specs=pl.BlockSpec((LANES, M), lambda i: (0, 0)),
        out_shape=jax.ShapeDtypeStruct((LANES, M), F32),
        compiler_params=_cparams("arbitrary"),
    )(z, bpad)


def _col_to_row(col):
    n = col.shape[0]
    return jnp.transpose(jnp.broadcast_to(col, (n, LANES)))[0:1, :]


def _causal_extents(M):
    edges = list(range(0, M, CAUSAL_STEP)) + [M]
    return list(zip(edges[:-1], edges[1:]))


def _heads_per_step(n_heads):
    return 2 if n_heads % 2 == 0 else 1


def _attn_fwd(qh, kh, vb, cum_c, cum_r, name):
    M = qh.shape[0]
    H = qh.shape[1] // HEAD_DIM
    hp = _heads_per_step(H)
    scale = 1.0 / math.sqrt(HEAD_DIM)

    def body(q_ref, k_ref, v_ref, cq_ref, ck_ref, o_ref, ot_ref, lc_ref, lr_ref):
        i = pl.program_id(1)

        def compute(n):
            row = i * TQ + lax.broadcasted_iota(jnp.int32, (TQ, 1), 0)
            col = lax.broadcasted_iota(jnp.int32, (1, n), 1)
            for hh in range(hp):
                d0, d1 = hh * HEAD_DIM, (hh + 1) * HEAD_DIM
                s = lax.dot_general(q_ref[:, d0:d1], k_ref[0:n, d0:d1], NT_DIMS, preferred_element_type=F32) * scale
                s = s + (cq_ref[hh] - ck_ref[hh, :, 0:n])
                s = jnp.where(row >= col, s, NEG)
                m = jnp.max(s, axis=1, keepdims=True)
                p = jnp.exp(s - m)
                l = jnp.sum(p, axis=1, keepdims=True)
                pn = (p / l).astype(BF16)
                o = jnp.dot(pn, v_ref[0:n, d0:d1], preferred_element_type=F32)
                o_ref[:, d0:d1] = o.astype(BF16)
                ot_ref[d0:d1, :] = o.T.astype(BF16)
                lse = m + jnp.log(l)
                lc_ref[hh] = lse
                lr_ref[hh] = _col_to_row(lse)

        for lo, hi in _causal_extents(M):
            pl.when((i >= lo // TQ) & (i < hi // TQ))(functools.partial(compute, hi))

    full = pl.BlockSpec((M, hp * HEAD_DIM), lambda h, i: (0, h))
    tile = pl.BlockSpec((TQ, hp * HEAD_DIM), lambda h, i: (i, h))
    colv = pl.BlockSpec((hp, TQ, 1), lambda h, i: (h, i, 0))
    rowv_full = pl.BlockSpec((hp, 1, M), lambda h, i: (h, 0, 0))
    rowv = pl.BlockSpec((hp, 1, TQ), lambda h, i: (h, 0, i))
    return pl.pallas_call(
        body, name=name, grid=(H // hp, M // TQ),
        in_specs=[tile, full, full, colv, rowv_full],
        out_specs=[tile, pl.BlockSpec((hp * HEAD_DIM, TQ), lambda h, i: (h, i)), colv, rowv],
        out_shape=[jax.ShapeDtypeStruct((M, H * HEAD_DIM), BF16), jax.ShapeDtypeStruct((H * HEAD_DIM, M), BF16),
                   jax.ShapeDtypeStruct((H, M, 1), F32), jax.ShapeDtypeStruct((H, 1, M), F32)],
        compiler_params=_cparams("parallel", "parallel"),
    )(qh, kh, vb, cum_c, cum_r)


def _attn_bwd_q(qh, kh, vb, dob, cum_c, cum_r, lse_c, name):
    M = qh.shape[0]
    H = qh.shape[1] // HEAD_DIM
    hp = _heads_per_step(H)
    scale = 1.0 / math.sqrt(HEAD_DIM)

    def body(q_ref, k_ref, v_ref, do_ref, cq_ref, ck_ref, l_ref, dq_ref, dr_ref, dcq_ref):
        i = pl.program_id(1)

        def compute(n):
            row = i * TQ + lax.broadcasted_iota(jnp.int32, (TQ, 1), 0)
            col = lax.broadcasted_iota(jnp.int32, (1, n), 1)
            for hh in range(hp):
                d0, d1 = hh * HEAD_DIM, (hh + 1) * HEAD_DIM
                k = k_ref[0:n, d0:d1]
                s = lax.dot_general(q_ref[:, d0:d1], k, NT_DIMS, preferred_element_type=F32) * scale
                s = s + (cq_ref[hh] - ck_ref[hh, :, 0:n])
                p = jnp.exp(jnp.where(row >= col, s, NEG) - l_ref[hh])
                dp = lax.dot_general(do_ref[:, d0:d1], v_ref[0:n, d0:d1], NT_DIMS, preferred_element_type=F32)
                delta = jnp.sum(p * dp, axis=1, keepdims=True)
                ds = p * (dp - delta)
                dq_ref[:, d0:d1] = jnp.dot((ds * scale).astype(BF16), k, preferred_element_type=F32)
                dr_ref[hh] = _col_to_row(delta)
                dcq_ref[hh] = jnp.sum(ds, axis=1, keepdims=True)

        for lo, hi in _causal_extents(M):
            pl.when((i >= lo // TQ) & (i < hi // TQ))(functools.partial(compute, hi))

    full = pl.BlockSpec((M, hp * HEAD_DIM), lambda h, i: (0, h))
    tile = pl.BlockSpec((TQ, hp * HEAD_DIM), lambda h, i: (i, h))
    colv = pl.BlockSpec((hp, TQ, 1), lambda h, i: (h, i, 0))
    rowv_full = pl.BlockSpec((hp, 1, M), lambda h, i: (h, 0, 0))
    rowv = pl.BlockSpec((hp, 1, TQ), lambda h, i: (h, 0, i))
    return pl.pallas_call(
        body, name=name, grid=(H // hp, M // TQ),
        in_specs=[tile, full, full, tile, colv, rowv_full, colv],
        out_specs=[tile, rowv, colv],
        out_shape=[jax.ShapeDtypeStruct((M, H * HEAD_DIM), F32), jax.ShapeDtypeStruct((H, 1, M), F32),
                   jax.ShapeDtypeStruct((H, M, 1), F32)],
        compiler_params=_cparams("parallel", "parallel"),
    )(qh, kh, vb, dob, cum_c, cum_r, lse_c)


def _attn_bwd_kv(qh, kh, vb, dob, cum_c, cum_r, lse_r, delta_r, name):
    M = qh.shape[0]
    H = qh.shape[1] // HEAD_DIM
    hp = _heads_per_step(H)
    scale = 1.0 / math.sqrt(HEAD_DIM)

    def body(k_ref, v_ref, q_ref, do_ref, cq_ref, ck_ref, l_ref, d_ref, dk_ref, dv_ref, dck_ref):
        j = pl.program_id(1)

        def compute(q0):
            krow = j * TQ + lax.broadcasted_iota(jnp.int32, (TQ, 1), 0)
            qcol = q0 + lax.broadcasted_iota(jnp.int32, (1, M - q0), 1)
            for hh in range(hp):
                d0, d1 = hh * HEAD_DIM, (hh + 1) * HEAD_DIM
                q = q_ref[q0:M, d0:d1]
                do = do_ref[q0:M, d0:d1]
                st = lax.dot_general(k_ref[:, d0:d1], q, NT_DIMS, preferred_element_type=F32) * scale
                st = st + (cq_ref[hh, :, q0:M] - ck_ref[hh])
                pt = jnp.exp(jnp.where(qcol >= krow, st, NEG) - l_ref[hh, :, q0:M])
                dpt = lax.dot_general(v_ref[:, d0:d1], do, NT_DIMS, preferred_element_type=F32)
                dst = pt * (dpt - d_ref[hh, :, q0:M])
                dv_ref[:, d0:d1] = jnp.dot(pt.astype(BF16), do, preferred_element_type=F32).astype(BF16)
                dk_ref[:, d0:d1] = jnp.dot((dst * scale).astype(BF16), q, preferred_element_type=F32)
                dck_ref[hh] = -jnp.sum(dst, axis=1, keepdims=True)

        for lo, hi in _causal_extents(M):
            pl.when((j >= lo // TQ) & (j < hi // TQ))(functools.partial(compute, lo))

    full = pl.BlockSpec((M, hp * HEAD_DIM), lambda h, j: (0, h))
    tile = pl.BlockSpec((TQ, hp * HEAD_DIM), lambda h, j: (j, h))
    colv = pl.BlockSpec((hp, TQ, 1), lambda h, j: (h, j, 0))
    rowv_full = pl.BlockSpec((hp, 1, M), lambda h, j: (h, 0, 0))
    return pl.pallas_call(
        body, name=name, grid=(H // hp, M // TQ),
        in_specs=[tile, tile, full, full, rowv_full, colv, rowv_full, rowv_full],
        out_specs=[tile, tile, colv],
        out_shape=[jax.ShapeDtypeStruct((M, H * HEAD_DIM), F32), jax.ShapeDtypeStruct((M, H * HEAD_DIM), BF16),
                   jax.ShapeDtypeStruct((H, M, 1), F32)],
        compiler_params=_cparams("parallel", "parallel"),
    )(kh, vb, qh, dob, cum_r, cum_c, lse_r, delta_r)


def _qk_norm_bwd(dqh, dkh, z, gq, gk, n_heads, q_col, name):
    M = z.shape[0]
    H = n_heads
    qb = q_col // HEAD_DIM

    def body(dqh_ref, dkh_ref, q_ref, k_ref, gq_ref, gk_ref, dq_ref, dk_ref, dgq_ref, dgk_ref):
        h = pl.program_id(0)

        def one(dy, xv, g):
            r = lax.rsqrt(jnp.mean(xv * xv, axis=-1, keepdims=True) + EPS)
            w = dy * g
            c = jnp.mean(w * xv, axis=-1, keepdims=True)
            dx = r * w - xv * (r * r * r * c)
            return dx.astype(BF16), jnp.sum(dy * (xv * r), axis=0, keepdims=True)

        dq, dgq = one(dqh_ref[...], q_ref[...], gq_ref[...])
        dk, dgk = one(dkh_ref[...], k_ref[...], gk_ref[...])
        dq_ref[...] = dq
        dk_ref[...] = dk

        @pl.when(h == 0)
        def _():
            dgq_ref[...] = dgq
            dgk_ref[...] = dgk

        @pl.when(h > 0)
        def _():
            dgq_ref[...] += dgq
            dgk_ref[...] += dgk

    vec = pl.BlockSpec((1, HEAD_DIM), lambda h: (0, 0))
    head = pl.BlockSpec((M, HEAD_DIM), lambda h: (0, h))
    return pl.pallas_call(
        body, name=name, grid=(H,),
        in_specs=[head, head, pl.BlockSpec((M, HEAD_DIM), lambda h: (0, qb + h)),
                  pl.BlockSpec((M, HEAD_DIM), lambda h: (0, qb + H + h)), vec, vec],
        out_specs=[head, head, vec, vec],
        out_shape=[jax.ShapeDtypeStruct((M, H * HEAD_DIM), BF16), jax.ShapeDtypeStruct((M, H * HEAD_DIM), BF16),
                   jax.ShapeDtypeStruct((1, HEAD_DIM), F32), jax.ShapeDtypeStruct((1, HEAD_DIM), F32)],
        compiler_params=_cparams("arbitrary"),
    )(dqh, dkh, z, z, gq, gk)


def _forget_bwd(dcq, dck, z, bpad, f_block, name):
    H, M, _ = dcq.shape

    def body(dcq_ref, dck_ref, f_ref, b_ref, dfl_ref, db_ref):
        lane = lax.broadcasted_iota(jnp.int32, (1, LANES), 1)
        d = jnp.zeros((M, LANES), F32)
        for h in range(H):
            d = d + (dcq_ref[h] + dck_ref[h]) * (lane == h).astype(F32)
        t = lax.broadcasted_iota(jnp.int32, (M, 1), 0)
        sh = 1
        while sh < M:
            d = d + jnp.where(t < M - sh, pltpu.roll(d, M - sh, 0), 0.0)
            sh *= 2
        xx = f_ref[...] + b_ref[...]
        dfl = d * (1.0 / (1.0 + jnp.exp(xx)))
        dfl_ref[...] = dfl.astype(BF16)
        db_ref[...] = jnp.sum(dfl, axis=0, keepdims=True)

    colv = pl.BlockSpec((H, M, 1), lambda i: (0, 0, 0))
    return pl.pallas_call(
        body, name=name, grid=(1,),
        in_specs=[colv, colv, pl.BlockSpec((M, LANES), lambda i: (0, f_block)),
                  pl.BlockSpec((1, LANES), lambda i: (0, 0))],
        out_specs=[pl.BlockSpec((M, LANES), lambda i: (0, 0)), pl.BlockSpec((1, LANES), lambda i: (0, 0))],
        out_shape=[jax.ShapeDtypeStruct((M, LANES), BF16), jax.ShapeDtypeStruct((1, LANES), F32)],
        compiler_params=_cparams("arbitrary"),
    )(dcq, dck, z, bpad)


def _ffn_fwd(h, g, dep, get_weights, request, down_is_early, tag):
    n, nt, r = _rmsnorm_fwd(h, g, dep, f"{tag}_norm")
    up = get_weights(f"{tag}_up", n)
    dep = request(f"{tag}_down", n) if down_is_early else n
    p, q, s, st = _ffn_up(n, up["wg"], up["wu"], dep, 256, f"{tag}_up")
    wd = get_weights(f"{tag}_down", s)["wd"]
    h_out = _mm_nn_residual(s, wd, h, s, 0.5, 256, f"{tag}_down")
    return h_out, (nt, r, p, q, st, up["wg"], up["wu"], wd)


def _ffn_bwd(dh, dhb, h, g, saved, dep, put_grads, tag):
    nt, r, p, q, st, wg, wu, wd = saved
    n_shards = 4
    da, db = _ffn_bwd_hidden(dhb, wd, p, q, dep, 256, f"{tag}_bwd_hidden")
    dwd = _mm_tn(st, dhb, dep, 0.5, st.shape[0] // n_shards, 1024, f"{tag}_dw_down", stacked=False)
    dep = put_grads(f"{tag}_w_down", dwd)
    dwg = _mm_tn(nt, da, dep, 1.0, 1024, wg.shape[1] // n_shards, f"{tag}_dw_gate", stacked=True)
    dep = put_grads(f"{tag}_w_gate", dwg)
    dwu = _mm_tn(nt, db, dep, 1.0, 1024, wu.shape[1] // n_shards, f"{tag}_dw_up", stacked=True)
    dep = put_grads(f"{tag}_w_up", dwu)
    dn = _mm_nt_sum([da, db], [wg, wu], dep, 512, wg.shape[1] // 4, f"{tag}_dn")
    dh_in, dhb_in, dg = _rmsnorm_bwd(dn, h, r, g, dh, f"{tag}_norm_bwd")
    return dh_in, dhb_in, dg


def _local_step(x, target, S, get_weights, request, put_grads):
    seq, D = x.shape
    L = N_META + seq
    Lp = -(-L // SEQ_ALIGN) * SEQ_ALIGN
    pad = jnp.zeros((Lp - L, D), F32)
    tgt = jnp.concatenate([jnp.zeros((N_META, D), F32), target, pad], axis=0)

    d_pool = S["pool_scale"].shape[1]
    n_heads = S["b_forget"].shape[1]
    d_att = n_heads * HEAD_DIM
    f_col = d_pool + 3 * d_att
    f_block = f_col // LANES
    bpad = jnp.pad(S["b_forget"], ((0, 0), (0, LANES - n_heads)))

    h0 = jnp.concatenate([get_weights("meta", None)["meta"], x, pad], axis=0)
    h1, ffn1 = _ffn_fwd(h0, S["ffn1_norm"], h0, get_weights, request, False, "ffn1")
    u, ut, r_mix = _rmsnorm_fwd(h1, S["mix_norm"], request("mix", h1), "mix_norm")
    Wm = get_weights("mix", u)
    z = _mm_nn(u, Wm["win"], 384, "in_proj")
    pooled, pool_out, pool_out_t = _pool_fwd(z, Wm["pool_w"], S["pool_scale"], "pool_fwd")
    qh, kh, vb = _qkv_prep(z, S["q_norm"], S["k_norm"], n_heads, d_pool, "qkv_prep")
    cum_t = _forget_fwd(z, bpad, f_block, "forget_fwd")[:n_heads]
    cum_c = cum_t.reshape(n_heads, Lp, 1)
    cum_r = cum_t.reshape(n_heads, 1, Lp)
    att, att_t, lse_c, lse_r = _attn_fwd(qh, kh, vb, cum_c, cum_r, "attn_fwd")
    mix = jnp.concatenate([pool_out, att], axis=1)
    mix_t = jnp.concatenate([pool_out_t, att_t], axis=0)
    dep = request("ffn2_up", att)
    h2 = _mm_nn_residual(mix, Wm["wout"], h1, dep, 1.0, 512, "out_proj")
    h3, ffn2 = _ffn_fwd(h2, S["ffn2_norm"], h2, get_weights, request, False, "ffn2")

    dh3, dh3b, loss = _loss_grad(h3, tgt, seq, "loss")
    dh2, dh2b, dg_ffn2 = _ffn_bwd(dh3, dh3b, h2, S["ffn2_norm"], ffn2, loss, put_grads, "ffn2")

    dmix = _mm_nt(dh2b, Wm["wout"], loss, 512, "out_proj_bwd")
    dwout = _mm_tn(mix_t, dh2b, loss, 1.0, 1024, 1024, "dw_out", stacked=False)
    dp, dpw, dpsc = _pool_bwd(dmix, pooled, Wm["pool_w"], S["pool_scale"], "pool_bwd")
    dob = dmix[:, d_pool:].astype(BF16)
    dqh, delta_r, dcq = _attn_bwd_q(qh, kh, vb, dob, cum_c, cum_r, lse_c, "attn_bwd_q")
    dkh, dv, dck = _attn_bwd_kv(qh, kh, vb, dob, cum_c, cum_r, lse_r, delta_r, "attn_bwd_kv")
    dq, dk, dgq, dgk = _qk_norm_bwd(dqh, dkh, z, S["q_norm"], S["k_norm"], n_heads, d_pool, "qk_norm_bwd")
    dfl, dbf = _forget_bwd(dcq, dck, z, bpad, f_block, "forget_bwd")
    dz = jnp.concatenate([dp, dq, dk, dv, dfl], axis=1)
    dwin = _mm_tn(ut, dz, loss, 1.0, 1024, dz.shape[1] // 3, "dw_in", stacked=False)
    dep = put_grads("mix", dict(win=dwin, wout=dwout, pool_w=dpw))
    du = _mm_nt_sum([dz], [Wm["win"]], dep, 512, Wm["win"].shape[1] // 3, "in_proj_bwd")
    dh1, dh1b, dg_mix = _rmsnorm_bwd(du, h1, r_mix, S["mix_norm"], dh2, "mix_norm_bwd")

    dh0, _, dg_ffn1 = _ffn_bwd(dh1, dh1b, h0, S["ffn1_norm"], ffn1, loss, put_grads, "ffn1")

    grads = dict(
        x=dh0[N_META:L], meta=dh0[:N_META],
        ffn1_norm=dg_ffn1, mix_norm=dg_mix, ffn2_norm=dg_ffn2, q_norm=dgq, k_norm=dgk,
        b_forget=dbf[:, :n_heads], pool_scale=dpsc,
    )
    return loss[0, 0], dh0, grads


HBM_SPEC = pl.BlockSpec(memory_space=pltpu.HBM)
N_CHIPS = 4


def _chip_peers():
    x, y, c = lax.axis_index("x"), lax.axis_index("y"), lax.axis_index("c")
    flips = [(1 - x, y), (x, 1 - y), (1 - x, 1 - y)]
    return 2 * x + y, [((px, py, c), 2 * px + py) for px, py in flips]


def _gathered_shape(shape, layout):
    if layout == "rows":
        return (N_CHIPS * shape[0],) + shape[1:]
    if layout == "cols":
        return (shape[0], N_CHIPS * shape[1])
    return (N_CHIPS,) + shape


def _cast_place(place, w, dep, layout, dtype, name):
    R, C = w.shape
    tr = _row_tile(R, C)
    nt = R // tr

    def body(place_ref, w_ref, dep_ref, o_ref):
        o_ref[...] = w_ref[...].astype(dtype)

    if layout == "rows":
        ospec = pl.BlockSpec((tr, C), lambda i, p: (p[1] * nt + i, 0))
    elif layout == "cols":
        ospec = pl.BlockSpec((tr, C), lambda i, p: (i, p[1]))
    else:
        ospec = pl.BlockSpec((None, tr, C), lambda i, p: (p[1], i, 0))
    return pl.pallas_call(
        body, name=name,
        grid_spec=pltpu.PrefetchScalarGridSpec(
            num_scalar_prefetch=1, grid=(nt,),
            in_specs=[pl.BlockSpec((tr, C), lambda i, p: (i, 0)), pl.BlockSpec(memory_space=pl.ANY)],
            out_specs=ospec),
        out_shape=jax.ShapeDtypeStruct(_gathered_shape((R, C), layout), dtype),
        compiler_params=_cparams("parallel"),
    )(place, w, dep)


SEM_SPEC = pl.BlockSpec(memory_space=pltpu.SEMAPHORE)
ANY_SPEC = pl.BlockSpec(memory_space=pl.ANY)
SPLIT_COPY = pltpu.CompilerParams(has_side_effects=pltpu.SideEffectType.DATAFLOW_SIDE_EFFECTING)


def _hbm(a):
    return pltpu.with_memory_space_constraint(a, pltpu.HBM)


def _gather_region(refs, shard_shapes, layouts, a, chip, half):
    rows_a = shard_shapes[a][0]
    h = rows_a // 2
    if layouts[a] == "rows":
        return refs[a].at[pl.ds(chip * rows_a + half * h, h)]
    if layouts[a] == "cols":
        cols_a = shard_shapes[a][1]
        return refs[a].at[pl.ds(half * h, h), pl.ds(chip * cols_a, cols_a)]
    return refs[a].at[chip, pl.ds(half * h, h)]


def _gather_start(bufs, after, shard_shapes, layouts, name):
    n = len(bufs)
    ns = 3 * n

    def body(*refs):
        in_refs = refs[:n]
        send_sems = refs[n + 1:n + 1 + ns]
        recv_sems = refs[n + 1 + ns:n + 1 + 2 * ns]
        token = refs[2 * n + 1 + 2 * ns]
        c = lax.axis_index("c")
        me, peers = _chip_peers()
        for a in range(n):
            mine = _gather_region(in_refs, shard_shapes, layouts, a, me, c)
            for k, (dev, _) in enumerate(peers):
                pltpu.make_async_remote_copy(
                    src_ref=mine, dst_ref=mine, send_sem=send_sems[3 * a + k], recv_sem=recv_sems[3 * a + k],
                    device_id=dev, device_id_type=MESH).start()
        token[...] = jnp.zeros_like(token)

    sem = pltpu.SemaphoreType.DMA(())
    out = pl.pallas_call(
        body, name=name,
        out_shape=(*[sem] * (2 * ns), *[pltpu.HBM(b.shape, b.dtype) for b in bufs],
                   jax.ShapeDtypeStruct((8, LANES), F32)),
        in_specs=[HBM_SPEC] * n + [ANY_SPEC],
        out_specs=(*[SEM_SPEC] * (2 * ns), *[HBM_SPEC] * n, pl.BlockSpec(memory_space=pltpu.VMEM)),
        input_output_aliases={a: 2 * ns + a for a in range(n)},
        compiler_params=SPLIT_COPY,
    )(*[_hbm(b) for b in bufs], after)
    return list(out[:ns]), list(out[ns:2 * ns]), list(out[2 * ns:2 * ns + n]), out[2 * ns + n]


def _gather_wait(bufs, send_sems, recv_sems, afters, shard_shapes, layouts, name):
    n = len(bufs)
    ns = 3 * n
    na = len(afters)

    def body(*refs):
        in_refs = refs[:n]
        send_sems = refs[n:n + ns]
        recv_sems = refs[n + ns:n + 2 * ns]
        token = refs[2 * n + 2 * ns + na]
        token[...] = jnp.zeros_like(token)
        c = lax.axis_index("c")
        me, peers = _chip_peers()
        for a in range(n):
            mine = _gather_region(in_refs, shard_shapes, layouts, a, me, c)
            for k, (dev, pidx) in enumerate(peers):
                landed = _gather_region(in_refs, shard_shapes, layouts, a, pidx, c)
                pltpu.make_async_remote_copy(
                    src_ref=mine, dst_ref=landed, send_sem=send_sems[3 * a + k], recv_sem=recv_sems[3 * a + k],
                    device_id=dev, device_id_type=MESH).wait_recv()
        for a in range(n):
            mine = _gather_region(in_refs, shard_shapes, layouts, a, me, c)
            for k, (dev, _) in enumerate(peers):
                pltpu.make_async_remote_copy(
                    src_ref=mine, dst_ref=mine, send_sem=send_sems[3 * a + k], recv_sem=recv_sems[3 * a + k],
                    device_id=dev, device_id_type=MESH).wait_send()

    out = pl.pallas_call(
        body, name=name,
        out_shape=(*[pltpu.HBM(b.shape, b.dtype) for b in bufs], jax.ShapeDtypeStruct((8, LANES), F32)),
        in_specs=[HBM_SPEC] * n + [SEM_SPEC] * (2 * ns) + [ANY_SPEC] * na,
        out_specs=(*[HBM_SPEC] * n, pl.BlockSpec(memory_space=pltpu.VMEM)),
        input_output_aliases={a: a for a in range(n)},
        compiler_params=SPLIT_COPY,
    )(*bufs, *send_sems, *recv_sems, *afters)
    return list(out[:n]), out[n]


def _forward_start(bufs, after, shard_shapes, layouts, name):
    n = len(bufs)
    ns = 3 * n

    def body(*refs):
        in_refs = refs[:n]
        send_sems = refs[n + 1:n + 1 + ns]
        recv_sems = refs[n + 1 + ns:n + 1 + 2 * ns]
        token = refs[2 * n + 1 + 2 * ns]
        c = lax.axis_index("c")
        sib = (lax.axis_index("x"), lax.axis_index("y"), 1 - c)
        _, peers = _chip_peers()
        for a in range(n):
            for k, (_, pidx) in enumerate(peers):
                landed = _gather_region(in_refs, shard_shapes, layouts, a, pidx, c)
                pltpu.make_async_remote_copy(
                    src_ref=landed, dst_ref=landed, send_sem=send_sems[3 * a + k], recv_sem=recv_sems[3 * a + k],
                    device_id=sib, device_id_type=MESH).start()
        token[...] = jnp.zeros_like(token)

    sem = pltpu.SemaphoreType.DMA(())
    out = pl.pallas_call(
        body, name=name,
        out_shape=(*[sem] * (2 * ns), *[pltpu.HBM(b.shape, b.dtype) for b in bufs],
                   jax.ShapeDtypeStruct((8, LANES), F32)),
        in_specs=[HBM_SPEC] * n + [ANY_SPEC],
        out_specs=(*[SEM_SPEC] * (2 * ns), *[HBM_SPEC] * n, pl.BlockSpec(memory_space=pltpu.VMEM)),
        input_output_aliases={a: 2 * ns + a for a in range(n)},
        compiler_params=SPLIT_COPY,
    )(*[_hbm(b) for b in bufs], after)
    return list(out[:ns]), list(out[ns:2 * ns]), list(out[2 * ns:2 * ns + n]), out[2 * ns + n]


def _forward_wait(bufs, send_sems, recv_sems, after, shard_shapes, layouts, name):
    n = len(bufs)
    ns = 3 * n

    def body(*refs):
        in_refs = refs[:n]
        send_sems = refs[n:n + ns]
        recv_sems = refs[n + ns:n + 2 * ns]
        c = lax.axis_index("c")
        sib = (lax.axis_index("x"), lax.axis_index("y"), 1 - c)
        _, peers = _chip_peers()
        for a in range(n):
            for k, (_, pidx) in enumerate(peers):
                landed = _gather_region(in_refs, shard_shapes, layouts, a, pidx, c)
                other = _gather_region(in_refs, shard_shapes, layouts, a, pidx, 1 - c)
                cp = pltpu.make_async_remote_copy(
                    src_ref=landed, dst_ref=other, send_sem=send_sems[3 * a + k], recv_sem=recv_sems[3 * a + k],
                    device_id=sib, device_id_type=MESH)
                cp.wait_recv()
                cp.wait_send()

    return list(pl.pallas_call(
        body, name=name,
        out_shape=tuple(pltpu.HBM(b.shape, b.dtype) for b in bufs),
        in_specs=[HBM_SPEC] * n + [SEM_SPEC] * (2 * ns) + [ANY_SPEC],
        out_specs=tuple([HBM_SPEC] * n),
        input_output_aliases={a: a for a in range(n)},
        compiler_params=SPLIT_COPY,
    )(*bufs, *send_sems, *recv_sems, after))


def _halves_copies(src_refs, land_refs, send_sems, recv_sems):
    c = lax.axis_index("c")
    sib = (lax.axis_index("x"), lax.axis_index("y"), 1 - c)
    copies = []
    for a, (src, land) in enumerate(zip(src_refs, land_refs)):
        h = src.shape[1] // 2
        copies.append(pltpu.make_async_remote_copy(
            src_ref=src.at[:, pl.ds((1 - c) * h, h)], dst_ref=land, send_sem=send_sems[a], recv_sem=recv_sems[a],
            device_id=sib, device_id_type=MESH))
    return copies


def _whole_copies(src_refs, land_refs, send_sems, recv_sems):
    sib = (lax.axis_index("x"), lax.axis_index("y"), 1 - lax.axis_index("c"))
    return [pltpu.make_async_remote_copy(src_ref=src, dst_ref=land, send_sem=send_sems[a], recv_sem=recv_sems[a],
                                         device_id=sib, device_id_type=MESH)
            for a, (src, land) in enumerate(zip(src_refs, land_refs))]


def _halves_land_shape(shape):
    return (shape[0], shape[1] // 2, shape[2])


def _sibling_start(stacked, copies, land_shape, name):
    n = len(stacked)
    lands = [lax.empty(land_shape(s.shape), s.dtype) for s in stacked]

    def body(*refs):
        for cp in copies(refs[:n], refs[n:2 * n], refs[2 * n:3 * n], refs[3 * n:4 * n]):
            cp.start()
        token = refs[6 * n]
        token[...] = jnp.zeros_like(token)

    sem = pltpu.SemaphoreType.DMA(())
    out = pl.pallas_call(
        body, name=name,
        out_shape=(*[sem] * (2 * n), *[pltpu.HBM(b.shape, b.dtype) for b in stacked],
                   *[pltpu.HBM(b.shape, b.dtype) for b in lands], jax.ShapeDtypeStruct((8, LANES), F32)),
        in_specs=[HBM_SPEC] * (2 * n),
        out_specs=(*[SEM_SPEC] * (2 * n), *[HBM_SPEC] * (2 * n), pl.BlockSpec(memory_space=pltpu.VMEM)),
        input_output_aliases={a: 2 * n + a for a in range(2 * n)},
        compiler_params=SPLIT_COPY,
    )(*[_hbm(b) for b in stacked], *[_hbm(b) for b in lands])
    return list(out[:n]), list(out[n:2 * n]), list(out[2 * n:3 * n]), list(out[3 * n:4 * n]), out[4 * n]


def _sibling_wait(srcs, lands, send_sems, recv_sems, after, copies_of, name):
    n = len(srcs)

    def body(*refs):
        copies = copies_of(refs[:n], refs[n:2 * n], refs[2 * n:3 * n], refs[3 * n:4 * n])
        for cp in copies:
            cp.wait_recv()
        for cp in copies:
            cp.wait_send()

    out = pl.pallas_call(
        body, name=name,
        out_shape=tuple(pltpu.HBM(b.shape, b.dtype) for b in list(srcs) + list(lands)),
        in_specs=[HBM_SPEC] * (2 * n) + [SEM_SPEC] * (2 * n) + [ANY_SPEC],
        out_specs=tuple([HBM_SPEC] * (2 * n)),
        input_output_aliases={a: a for a in range(2 * n)},
        compiler_params=SPLIT_COPY,
    )(*srcs, *lands, *send_sems, *recv_sems, after)
    return list(out[:n]), list(out[n:])


def _scatter_start(stacked, name):
    n = len(stacked)
    ns = 3 * n
    lands = [lax.empty((3,) + s.shape[1:], s.dtype) for s in stacked]

    def body(*refs):
        src_refs = refs[:n]
        land_refs = refs[n:2 * n]
        send_sems = refs[2 * n:2 * n + ns]
        recv_sems = refs[2 * n + ns:2 * n + 2 * ns]
        token = refs[4 * n + 2 * ns]
        _, peers = _chip_peers()
        for a in range(n):
            for k, (dev, pidx) in enumerate(peers):
                pltpu.make_async_remote_copy(
                    src_ref=src_refs[a].at[k], dst_ref=land_refs[a].at[k], send_sem=send_sems[3 * a + k],
                    recv_sem=recv_sems[3 * a + k], device_id=dev, device_id_type=MESH).start()
        token[...] = jnp.zeros_like(token)

    sem = pltpu.SemaphoreType.DMA(())
    out = pl.pallas_call(
        body, name=name,
        out_shape=(*[sem] * (2 * ns), *[pltpu.HBM(b.shape, b.dtype) for b in stacked],
                   *[pltpu.HBM(b.shape, b.dtype) for b in lands], jax.ShapeDtypeStruct((8, LANES), F32)),
        in_specs=[HBM_SPEC] * (2 * n),
        out_specs=(*[SEM_SPEC] * (2 * ns), *[HBM_SPEC] * (2 * n), pl.BlockSpec(memory_space=pltpu.VMEM)),
        input_output_aliases={a: 2 * ns + a for a in range(2 * n)},
        compiler_params=SPLIT_COPY,
    )(*[_hbm(b) for b in stacked], *[_hbm(b) for b in lands])
    o = 2 * ns
    return list(out[:ns]), list(out[ns:o]), list(out[o:o + n]), list(out[o + n:o + 2 * n]), out[o + 2 * n]


def _scatter_wait(srcs, lands, send_sems, recv_sems, after, name):
    n = len(srcs)
    ns = 3 * n

    def body(*refs):
        src_refs = refs[:n]
        land_refs = refs[n:2 * n]
        send_sems = refs[2 * n:2 * n + ns]
        recv_sems = refs[2 * n + ns:2 * n + 2 * ns]
        _, peers = _chip_peers()
        copies = [
            pltpu.make_async_remote_copy(
                src_ref=src_refs[a].at[k], dst_ref=land_refs[a].at[k], send_sem=send_sems[3 * a + k],
                recv_sem=recv_sems[3 * a + k], device_id=dev, device_id_type=MESH)
            for a in range(n) for k, (dev, pidx) in enumerate(peers)]
        for cp in copies:
            cp.wait_recv()
        for cp in copies:
            cp.wait_send()

    out = pl.pallas_call(
        body, name=name,
        out_shape=tuple(pltpu.HBM(b.shape, b.dtype) for b in list(srcs) + list(lands)),
        in_specs=[HBM_SPEC] * (2 * n) + [SEM_SPEC] * (2 * ns) + [ANY_SPEC],
        out_specs=tuple([HBM_SPEC] * (2 * n)),
        input_output_aliases={a: a for a in range(2 * n)},
        compiler_params=SPLIT_COPY,
    )(*srcs, *lands, *send_sems, *recv_sems, after)
    return list(out[n:])


def _all_reduce_small(v):
    R, C = v.shape
    n_dev = 8

    def body(v_ref, o_ref, buf, send_sems, recv_sems):
        x, y, c = lax.axis_index("x"), lax.axis_index("y"), lax.axis_index("c")
        me = 4 * x + 2 * y + c
        buf[me] = v_ref[...]
        sends = []
        for k in range(1, n_dev):
            px, py, pc = x ^ ((k >> 2) & 1), y ^ ((k >> 1) & 1), c ^ (k & 1)
            cp = pltpu.make_async_remote_copy(
                src_ref=v_ref, dst_ref=buf.at[me], send_sem=send_sems.at[k - 1], recv_sem=recv_sems.at[k - 1],
                device_id=(px, py, pc), device_id_type=MESH)
            cp.start()
            sends.append((cp, 4 * px + 2 * py + pc))
        for k in range(1, n_dev):
            cp, pidx = sends[k - 1]
            pltpu.make_async_remote_copy(
                src_ref=v_ref, dst_ref=buf.at[pidx], send_sem=send_sems.at[k - 1], recv_sem=recv_sems.at[k - 1],
                device_id=(x, y, c), device_id_type=MESH).wait_recv()
        for cp, _ in sends:
            cp.wait_send()
        acc = buf[0]
        for d in range(1, n_dev):
            acc = acc + buf[d]
        o_ref[...] = acc

    vm = pl.BlockSpec(memory_space=pltpu.VMEM)
    return pl.pallas_call(
        body, name="all_reduce_small",
        in_specs=[vm], out_specs=vm,
        out_shape=jax.ShapeDtypeStruct((R, C), F32),
        scratch_shapes=[pltpu.VMEM((n_dev, R, C), F32), pltpu.SemaphoreType.DMA((n_dev - 1,)),
                        pltpu.SemaphoreType.DMA((n_dev - 1,))],
    )(v)


def _pair_sum(place, own, sib, name):
    S, R, C = own.shape
    h = R // 2
    tr = _row_tile(h, C)
    nt = h // tr

    def body(place_ref, o_ref, s_ref, out_ref):
        out_ref[...] = (o_ref[...].astype(F32) + s_ref[...].astype(F32)).astype(BF16)

    return pl.pallas_call(
        body, name=name,
        grid_spec=pltpu.PrefetchScalarGridSpec(
            num_scalar_prefetch=1, grid=(3, nt),
            in_specs=[pl.BlockSpec((None, tr, C), lambda k, i, p: (p[2 + k], p[0] * nt + i, 0)),
                      pl.BlockSpec((None, tr, C), lambda k, i, p: (p[2 + k], i, 0))],
            out_specs=pl.BlockSpec((None, tr, C), lambda k, i, p: (k, i, 0))),
        out_shape=jax.ShapeDtypeStruct((3, h, C), BF16),
        compiler_params=_cparams("parallel", "parallel"),
    )(place, own, sib)


def _sum_slabs(place, own, sib, recv, name):
    S, R, C = own.shape
    h = R // 2
    tr = _row_tile(h, C)
    nt = h // tr

    def body(place_ref, o_ref, s_ref, r_ref, out_ref):
        acc = o_ref[...].astype(F32) + s_ref[...].astype(F32)
        for k in range(3):
            acc = acc + r_ref[k].astype(F32)
        out_ref[...] = acc

    return pl.pallas_call(
        body, name=name,
        grid_spec=pltpu.PrefetchScalarGridSpec(
            num_scalar_prefetch=1, grid=(nt,),
            in_specs=[pl.BlockSpec((None, tr, C), lambda i, p: (p[1], p[0] * nt + i, 0)),
                      pl.BlockSpec((None, tr, C), lambda i, p: (p[1], i, 0)),
                      pl.BlockSpec((3, tr, C), lambda i, p: (0, i, 0))],
            out_specs=pl.BlockSpec((tr, C), lambda i, p: (i, 0))),
        out_shape=jax.ShapeDtypeStruct((h, C), F32),
        compiler_params=_cparams("parallel"),
    )(place, own, sib, recv)


def _adamw(parts, w, m, v, name):
    R, C = w.shape
    tr = _row_tile(R, C)
    npart = len(parts)
    c1 = 1.0 - ADAM_B1 ** ADAM_STEP
    c2 = 1.0 - ADAM_B2 ** ADAM_STEP

    def body(*refs):
        p_refs = refs[:npart]
        w_ref, m_ref, v_ref, g_ref, d_ref, nm_ref, nv_ref = refs[npart:]
        g = p_refs[0][...]
        for p_ref in p_refs[1:]:
            g = g + p_ref[...]
        nm = ADAM_B1 * m_ref[...] + (1.0 - ADAM_B1) * g
        nv = ADAM_B2 * v_ref[...] + (1.0 - ADAM_B2) * (g * g)
        m_hat = nm / c1
        v_hat = nv / c2
        g_ref[...] = g
        d_ref[...] = -ADAM_LR * (m_hat / (jnp.sqrt(v_hat) + ADAM_EPS) + ADAM_WD * w_ref[...])
        nm_ref[...] = nm
        nv_ref[...] = nv

    blk = pl.BlockSpec((tr, C), lambda i: (i, 0))
    shape = jax.ShapeDtypeStruct((R, C), F32)
    return pl.pallas_call(
        body, name=name, grid=(R // tr,),
        in_specs=[blk] * (npart + 3), out_specs=[blk] * 4, out_shape=[shape] * 4,
        compiler_params=_cparams("parallel"),
    )(*parts, w, m, v)


def _adamw_halves(place, mine, other, w, m, v, name):
    R, C = w.shape
    h = R // 2
    tr = _row_tile(h, C)
    nt = h // tr
    c1 = 1.0 - ADAM_B1 ** ADAM_STEP
    c2 = 1.0 - ADAM_B2 ** ADAM_STEP

    def body(place_ref, mine_ref, other_ref, w_ref, m_ref, v_ref, g_ref, d_ref, nm_ref, nv_ref):
        is_mine = (pl.program_id(0) // nt) == place_ref[0]
        g = jnp.where(is_mine, mine_ref[...], other_ref[...])
        nm = ADAM_B1 * m_ref[...] + (1.0 - ADAM_B1) * g
        nv = ADAM_B2 * v_ref[...] + (1.0 - ADAM_B2) * (g * g)
        m_hat = nm / c1
        v_hat = nv / c2
        g_ref[...] = g
        d_ref[...] = -ADAM_LR * (m_hat / (jnp.sqrt(v_hat) + ADAM_EPS) + ADAM_WD * w_ref[...])
        nm_ref[...] = nm
        nv_ref[...] = nv

    def half_block(which):
        def index(i, p):
            first = p[0] if which == 0 else 1 - p[0]
            return jnp.clip(i - first * nt, 0, nt - 1), 0

        return pl.BlockSpec((tr, C), index)

    blk = pl.BlockSpec((tr, C), lambda i, p: (i, 0))
    shape = jax.ShapeDtypeStruct((R, C), F32)
    return pl.pallas_call(
        body, name=name,
        grid_spec=pltpu.PrefetchScalarGridSpec(
            num_scalar_prefetch=1, grid=(2 * nt,),
            in_specs=[half_block(0), half_block(1), blk, blk, blk], out_specs=[blk] * 4),
        out_shape=[shape] * 4,
        compiler_params=_cparams("parallel"),
    )(place, mine, other, w, m, v)


SMALL_NAMES = ("ffn1_norm", "mix_norm", "ffn2_norm", "pool_scale", "q_norm", "k_norm", "b_forget")
SMALL_COLS = 1024
LOSS_LANE = 512


def _pack_small(vals):
    rows = [vals[n].reshape(-1, SMALL_COLS) for n in ("ffn1_norm", "mix_norm", "ffn2_norm", "pool_scale")]
    tail = jnp.concatenate([vals["q_norm"].reshape(-1), vals["k_norm"].reshape(-1), vals["b_forget"].reshape(-1)])
    rows.append(jnp.pad(tail, (0, SMALL_COLS - tail.shape[0])).reshape(1, SMALL_COLS))
    return jnp.concatenate(rows, axis=0)


def _unpack_small(packed, like):
    out = {}
    r = 0
    for n in ("ffn1_norm", "mix_norm", "ffn2_norm", "pool_scale"):
        k = like[n].size // SMALL_COLS
        out[n] = packed[r:r + k].reshape(like[n].shape)
        r += k
    o = 0
    for n in ("q_norm", "k_norm", "b_forget"):
        k = like[n].size
        out[n] = packed[r, o:o + k].reshape(like[n].shape)
        o += k
    return out


def kernel(x, meta_tokens, ffn1_norm, ffn1_w_gate, ffn1_w_up, ffn1_w_down, mix_norm, w_in, b_forget, q_norm, k_norm, pool_w, pool_scale, w_out, ffn2_norm, ffn2_w_gate, ffn2_w_up, ffn2_w_down, loss_target, m_meta_tokens, m_ffn1_norm, m_ffn1_w_gate, m_ffn1_w_up, m_ffn1_w_down, m_mix_norm, m_w_in, m_b_forget, m_q_norm, m_k_norm, m_pool_w, m_pool_scale, m_w_out, m_ffn2_norm, m_ffn2_w_gate, m_ffn2_w_up, m_ffn2_w_down, v_meta_tokens, v_ffn1_norm, v_ffn1_w_gate, v_ffn1_w_up, v_ffn1_w_down, v_mix_norm, v_w_in, v_b_forget, v_q_norm, v_k_norm, v_pool_w, v_pool_scale, v_w_out, v_ffn2_norm, v_ffn2_w_gate, v_ffn2_w_up, v_ffn2_w_down):
    wts = dict(meta_tokens=meta_tokens, ffn1_norm=ffn1_norm, ffn1_w_gate=ffn1_w_gate, ffn1_w_up=ffn1_w_up,
               ffn1_w_down=ffn1_w_down, mix_norm=mix_norm, w_in=w_in, b_forget=b_forget, q_norm=q_norm,
               k_norm=k_norm, pool_w=pool_w, pool_scale=pool_scale, w_out=w_out, ffn2_norm=ffn2_norm,
               ffn2_w_gate=ffn2_w_gate, ffn2_w_up=ffn2_w_up, ffn2_w_down=ffn2_w_down)
    mom = dict(meta_tokens=m_meta_tokens, ffn1_norm=m_ffn1_norm, ffn1_w_gate=m_ffn1_w_gate, ffn1_w_up=m_ffn1_w_up,
               ffn1_w_down=m_ffn1_w_down, mix_norm=m_mix_norm, w_in=m_w_in, b_forget=m_b_forget, q_norm=m_q_norm,
               k_norm=m_k_norm, pool_w=m_pool_w, pool_scale=m_pool_scale, w_out=m_w_out, ffn2_norm=m_ffn2_norm,
               ffn2_w_gate=m_ffn2_w_gate, ffn2_w_up=m_ffn2_w_up, ffn2_w_down=m_ffn2_w_down)
    var = dict(meta_tokens=v_meta_tokens, ffn1_norm=v_ffn1_norm, ffn1_w_gate=v_ffn1_w_gate, ffn1_w_up=v_ffn1_w_up,
               ffn1_w_down=v_ffn1_w_down, mix_norm=v_mix_norm, w_in=v_w_in, b_forget=v_b_forget, q_norm=v_q_norm,
               k_norm=v_k_norm, pool_w=v_pool_w, pool_scale=v_pool_scale, w_out=v_w_out, ffn2_norm=v_ffn2_norm,
               ffn2_w_gate=v_ffn2_w_gate, ffn2_w_up=v_ffn2_w_up, ffn2_w_down=v_ffn2_w_down)
    order = list(wts)
    me = 2 * lax.axis_index("x") + lax.axis_index("y")

    D = x.shape[2]
    d_in_shard = w_in.shape[2]
    d_in = N_CHIPS * d_in_shard
    n_heads = b_forget.shape[1]
    d_in_pad = (d_in - n_heads) + LANES

    stages = dict(meta=("meta_tokens",), ffn1_up=("ffn1_w_gate", "ffn1_w_up"), ffn1_down=("ffn1_w_down",),
                  mix=("w_in", "w_out", "pool_w"),
                  ffn2_up=("ffn2_w_gate", "ffn2_w_up"), ffn2_down=("ffn2_w_down",))
    stage_order = list(stages)
    xi, yi = lax.axis_index("x"), lax.axis_index("y")
    place = jnp.stack([lax.axis_index("c"), me, 2 * (1 - xi) + yi, 2 * xi + 1 - yi, 2 * (1 - xi) + 1 - yi]).astype(
        jnp.int32)
    layouts = dict(ffn1_w_gate="cols", ffn1_w_up="cols", ffn1_w_down="rows", w_in="stack", w_out="rows",
                   pool_w="stack", ffn2_w_gate="cols", ffn2_w_up="cols", ffn2_w_down="rows", meta_tokens="stack")
    shards2d = {n: wts[n].reshape(-1, wts[n].shape[-1]) for n in layouts}

    def place_stage(stage, dep):
        return [_cast_place(place, shards2d[n], dep, layouts[n], F32 if n == "meta_tokens" else BF16, f"place_{n}")
                for n in stages[stage]]

    def start_stage(stage, bufs, after):
        shapes = [shards2d[n].shape for n in stages[stage]]
        lays = [layouts[n] for n in stages[stage]]
        return _gather_start(bufs, after, shapes, lays, f"gather_start_{stage}") + (shapes, lays)

    flight = {stage_order[0]: start_stage(stage_order[0], place_stage(stage_order[0], place), place)}
    placed = {stage_order[1]: place_stage(stage_order[1], flight[stage_order[0]][3])}

    def cols(st):
        return jnp.transpose(st, (1, 0, 2)).reshape(st.shape[1], -1)

    forwarding = {}

    def request(stage, after):
        k = stage_order.index(stage)
        send_sems, recv_sems, bufs, _, shapes, lays = flight.pop(stage)
        afters = ([] if after is None else [after]) + [b for st in placed for b in placed[st]]
        if k == 1:
            afters += [mom["w_in"].reshape(shards2d["w_in"].shape), var["w_in"].reshape(shards2d["w_in"].shape)]
        landed, token = _gather_wait(bufs, send_sems, recv_sems, afters, shapes, lays, f"gather_wait_{stage}")
        if k + 1 < len(stage_order):
            flight[stage_order[k + 1]] = start_stage(stage_order[k + 1], placed.pop(stage_order[k + 1]), token)
            token = flight[stage_order[k + 1]][3]
        if k == 0:
            placed.update({st: place_stage(st, token) for st in stage_order[2:]})
        forwarding[stage] = _forward_start(landed, token, shapes, lays, f"forward_start_{stage}") + (shapes, lays)
        return forwarding[stage][3]

    def get_weights(stage, after):
        if stage not in forwarding:
            request(stage, after)
        send_sems, recv_sems, bufs, token, shapes, lays = forwarding.pop(stage)
        full = _forward_wait(bufs, send_sems, recv_sems, token if after is None else after, shapes, lays,
                             f"forward_wait_{stage}")
        G = dict(zip(stages[stage], full))
        if stage == "meta":
            return dict(meta=cols(G["meta_tokens"]))
        if stage == "ffn1_up":
            return dict(wg=G["ffn1_w_gate"], wu=G["ffn1_w_up"])
        if stage == "ffn2_up":
            return dict(wg=G["ffn2_w_gate"], wu=G["ffn2_w_up"])
        if stage != "mix":
            return dict(wd=G[stages[stage][0]])
        return dict(
            win=jnp.concatenate([G["w_in"][s] for s in range(N_CHIPS)] + [jnp.zeros((D, d_in_pad - d_in), BF16)],
                                axis=1), wout=G["w_out"],
            pool_w=jnp.transpose(G["pool_w"].reshape((N_CHIPS,) + pool_w.shape[1:]), (1, 0, 2, 3)).reshape(
                N_POOL_GROUPS, pool_w.shape[3], pool_w.shape[3]))

    def split_rows(a):
        return a.reshape(N_CHIPS, -1, a.shape[1])

    def split_win(a):
        return jnp.stack([a[:, s * d_in_shard:(s + 1) * d_in_shard] for s in range(N_CHIPS)])

    def split_pool(a):
        r, c = pool_w.shape[2], pool_w.shape[3]
        return jnp.transpose(a.reshape(N_POOL_GROUPS, N_CHIPS, r, c), (1, 0, 2, 3)).reshape(N_CHIPS, -1, c)

    scatter = {}
    pending = []

    def finish_pending(after):
        name, names, (send_sems, recv_sems, srcs, lands) = pending.pop()
        own, from_sib = _sibling_wait(srcs, lands, send_sems, recv_sems, after, _halves_copies,
                                      f"halves_wait_{name}")
        pair = [_pair_sum(place, o, s, f"pair_sum_{n}") for n, o, s in zip(names, own, from_sib)]
        send_sems, recv_sems, srcs, lands, token = _scatter_start(pair, f"scatter_start_{name}")
        scatter[name] = (names, own, from_sib, send_sems, recv_sems, srcs, lands, token)
        return token

    def put_grads(name, g):
        if name == "mix":
            names = stages["mix"]
            own = [split_win(g["win"]), split_rows(g["wout"]), split_pool(g["pool_w"])]
        else:
            names = (name,)
            own = [split_rows(g) if name.endswith("_down") else g]
        *flying, token = _sibling_start(own, _halves_copies, _halves_land_shape, f"halves_start_{name}")
        if pending:
            token = finish_pending(token)
        pending.append((name, names, flying))
        return token

    small = dict(ffn1_norm=ffn1_norm, mix_norm=mix_norm, ffn2_norm=ffn2_norm, q_norm=q_norm, k_norm=k_norm,
                 b_forget=b_forget, pool_scale=pool_scale)
    loss_part, dh0, gr = _local_step(x[0], loss_target[0], small, get_weights, request, put_grads)

    out_g, out_d, out_m, out_v = {}, {}, {}, {}

    def update(stage, names, flying, after):
        send_sems, recv_sems, srcs, lands = flying
        halves, other_halves = _sibling_wait(srcs, lands, send_sems, recv_sems, after, _whole_copies,
                                             f"swap_wait_{stage}")
        for n, mine, other in zip(names, halves, other_halves):
            shape = wts[n].shape
            res = _adamw_halves(place, mine, other, shards2d[n], mom[n].reshape(shards2d[n].shape),
                                var[n].reshape(shards2d[n].shape), f"adamw_{n}")
            out_g[n], out_d[n], out_m[n], out_v[n] = (a.reshape(shape) for a in res)
        return res[3]

    after = finish_pending(dh0)
    swapping = None
    for stage in scatter:
        names, own, from_sib, send_sems, recv_sems, srcs, lands, _ = scatter[stage]
        received = _scatter_wait(srcs, lands, send_sems, recv_sems, after, f"scatter_wait_{stage}")
        halves = [_sum_slabs(place, o, s, r, f"sum_{n}") for n, o, s, r in zip(names, own, from_sib, received)]
        *flying, after = _sibling_start(halves, _whole_copies, lambda shape: shape, f"swap_start_{stage}")
        if swapping is not None:
            after = update(*swapping, after)
        swapping = (stage, names, flying)
    update(*swapping, after)

    small_g = _pack_small({n: gr[n] for n in SMALL_NAMES})
    n_small = small_g.shape[0]
    small_g = small_g.at[n_small - 1, LOSS_LANE].set(loss_part)
    meta_rows = gr["meta"].reshape(-1, SMALL_COLS)
    total = _all_reduce_small(jnp.concatenate([small_g, meta_rows], axis=0))
    loss = total[n_small - 1, LOSS_LANE]
    res = _adamw([total[:n_small]], _pack_small({n: wts[n] for n in SMALL_NAMES}),
                 _pack_small({n: mom[n] for n in SMALL_NAMES}), _pack_small({n: var[n] for n in SMALL_NAMES}),
                 "adamw_small")
    for dst, packed in zip((out_g, out_d, out_m, out_v), res):
        dst.update(_unpack_small(packed, wts))
    meta_cols = meta_tokens.shape[1]
    meta_g = lax.dynamic_slice_in_dim(total[n_small:].reshape(N_META, D), me * meta_cols, meta_cols, axis=1)
    res = _adamw([meta_g], meta_tokens, m_meta_tokens, v_meta_tokens, "adamw_meta")
    out_g["meta_tokens"], out_d["meta_tokens"], out_m["meta_tokens"], out_v["meta_tokens"] = res

    grad_x = gr["x"].reshape(x.shape)
    return (loss, grad_x, *[out_g[n] for n in order], *[out_d[n] for n in order], *[out_m[n] for n in order],
            *[out_v[n] for n in order])
```

```python
import functools
import math

import jax
import jax.numpy as jnp
from jax import lax
from jax.experimental import pallas as pl
from jax.experimental.pallas import tpu as pltpu

F32 = jnp.float32
BF16 = jnp.bfloat16

N_META = 16
EPS = 1e-6
HEAD_DIM = 128
N_POOL_GROUPS = 4
LANES = 128
SEQ_ALIGN = 128
TQ = 128
CAUSAL_STEP = 512
VMEM_LIMIT = 56 * 1024 * 1024
ELEMWISE_BLOCK_BYTES = 2304 * 1024

ADAM_LR = 0.001
ADAM_B1 = 0.9
ADAM_B2 = 0.999
ADAM_EPS = 1e-08
ADAM_WD = 0.01
ADAM_STEP = 10

NT_DIMS = (((1,), (1,)), ((), ()))
NEG = -1e30
MESH = pl.DeviceIdType.MESH


def _cparams(*sem):
    return pltpu.CompilerParams(dimension_semantics=sem, vmem_limit_bytes=VMEM_LIMIT)


def _sigmoid(a):
    return 1.0 / (1.0 + jnp.exp(-a))


def _row_tile(rows, cols, itemsize=4):
    best = None
    for t in range(16, rows + 1, 16):
        if rows % t == 0 and t * cols * itemsize <= ELEMWISE_BLOCK_BYTES:
            best = t
    return best if best is not None else rows


def _mm_nn(x, w, tn, name):
    M, K = x.shape
    N = w.shape[1]

    def body(x_ref, w_ref, o_ref):
        o_ref[...] = jnp.dot(x_ref[...], w_ref[...], preferred_element_type=F32)

    return pl.pallas_call(
        body, name=name, grid=(N // tn,),
        in_specs=[pl.BlockSpec((M, K), lambda j: (0, 0)), pl.BlockSpec((K, tn), lambda j: (0, j))],
        out_specs=pl.BlockSpec((M, tn), lambda j: (0, j)),
        out_shape=jax.ShapeDtypeStruct((M, N), F32),
        compiler_params=_cparams("parallel"),
    )(x, w)


def _ffn_up(n, wg, wu, dep, tn, name):
    M, K = n.shape
    N = wg.shape[1]

    def body(n_ref, wg_ref, wu_ref, dep_ref, p_ref, q_ref, s_ref, st_ref):
        nv = n_ref[...]
        a = jnp.dot(nv, wg_ref[...], preferred_element_type=F32)
        b = jnp.dot(nv, wu_ref[...], preferred_element_type=F32)
        sig = _sigmoid(a)
        silu = a * sig
        p_ref[...] = (b * (sig * (1.0 + a * (1.0 - sig)))).astype(BF16)
        q_ref[...] = silu.astype(BF16)
        s = silu * b
        s_ref[...] = s.astype(BF16)
        st_ref[...] = s.T.astype(BF16)

    wspec = pl.BlockSpec((K, tn), lambda j: (0, j))
    ospec = pl.BlockSpec((M, tn), lambda j: (0, j))
    return pl.pallas_call(
        body, name=name, grid=(N // tn,),
        in_specs=[pl.BlockSpec((M, K), lambda j: (0, 0)), wspec, wspec, pl.BlockSpec(memory_space=pl.ANY)],
        out_specs=[ospec, ospec, ospec, pl.BlockSpec((tn, M), lambda j: (j, 0))],
        out_shape=[jax.ShapeDtypeStruct((M, N), BF16), jax.ShapeDtypeStruct((M, N), BF16),
                   jax.ShapeDtypeStruct((M, N), BF16), jax.ShapeDtypeStruct((N, M), BF16)],
        compiler_params=_cparams("parallel"),
    )(n, wg, wu, dep)


def _ffn_up_from_gate(n, a, wu, tn, name):
    M, K = n.shape
    N = wu.shape[1]

    def body(n_ref, a_ref, wu_ref, p_ref, q_ref, s_ref, st_ref):
        a = a_ref[...]
        b = jnp.dot(n_ref[...], wu_ref[...], preferred_element_type=F32)
        sig = _sigmoid(a)
        silu = a * sig
        p_ref[...] = (b * (sig * (1.0 + a * (1.0 - sig)))).astype(BF16)
        q_ref[...] = silu.astype(BF16)
        s = silu * b
        s_ref[...] = s.astype(BF16)
        st_ref[...] = s.T.astype(BF16)

    ospec = pl.BlockSpec((M, tn), lambda j: (0, j))
    return pl.pallas_call(
        body, name=name, grid=(N // tn,),
        in_specs=[pl.BlockSpec((M, K), lambda j: (0, 0)), ospec, pl.BlockSpec((K, tn), lambda j: (0, j))],
        out_specs=[ospec, ospec, ospec, pl.BlockSpec((tn, M), lambda j: (j, 0))],
        out_shape=[jax.ShapeDtypeStruct((M, N), BF16), jax.ShapeDtypeStruct((M, N), BF16),
                   jax.ShapeDtypeStruct((M, N), BF16), jax.ShapeDtypeStruct((N, M), BF16)],
        compiler_params=_cparams("parallel"),
    )(n, a, wu)


def _mm_nn_residual(x, w, res, dep, alpha, tn, name):
    M, K = x.shape
    N = w.shape[1]

    def body(x_ref, w_ref, r_ref, dep_ref, o_ref):
        o_ref[...] = r_ref[...] + alpha * jnp.dot(x_ref[...], w_ref[...], preferred_element_type=F32)

    return pl.pallas_call(
        body, name=name, grid=(N // tn,),
        in_specs=[pl.BlockSpec((M, K), lambda j: (0, 0), pipeline_mode=pl.Buffered(1)),
                  pl.BlockSpec((K, tn), lambda j: (0, j)), pl.BlockSpec((M, tn), lambda j: (0, j)),
                  pl.BlockSpec(memory_space=pl.ANY)],
        out_specs=pl.BlockSpec((M, tn), lambda j: (0, j)),
        out_shape=jax.ShapeDtypeStruct((M, N), F32),
        compiler_params=_cparams("parallel"),
    )(x, w, res, dep)


def _ffn_bwd_hidden(dhb, wd, p, q, dep, tn, name):
    M, K = dhb.shape
    N = wd.shape[0]

    def body(dh_ref, w_ref, p_ref, q_ref, dep_ref, da_ref, db_ref):
        ds = 0.5 * lax.dot_general(dh_ref[...], w_ref[...], NT_DIMS, preferred_element_type=F32)
        da_ref[...] = (ds * p_ref[...].astype(F32)).astype(BF16)
        db_ref[...] = (ds * q_ref[...].astype(F32)).astype(BF16)

    ospec = pl.BlockSpec((M, tn), lambda j: (0, j))
    return pl.pallas_call(
        body, name=name, grid=(N // tn,),
        in_specs=[pl.BlockSpec((M, K), lambda j: (0, 0)), pl.BlockSpec((tn, K), lambda j: (j, 0)), ospec, ospec,
                  pl.BlockSpec(memory_space=pl.ANY)],
        out_specs=[ospec, ospec],
        out_shape=[jax.ShapeDtypeStruct((M, N), BF16), jax.ShapeDtypeStruct((M, N), BF16)],
        compiler_params=_cparams("parallel"),
    )(dhb, wd, p, q, dep)


def _mm_nt(x, w, dep, tn, name):
    M, K = x.shape
    N = w.shape[0]

    def body(x_ref, w_ref, dep_ref, o_ref):
        o_ref[...] = lax.dot_general(x_ref[...], w_ref[...], NT_DIMS, preferred_element_type=F32)

    return pl.pallas_call(
        body, name=name, grid=(N // tn,),
        in_specs=[pl.BlockSpec((M, K), lambda j: (0, 0)), pl.BlockSpec((tn, K), lambda j: (j, 0)),
                  pl.BlockSpec(memory_space=pl.ANY)],
        out_specs=pl.BlockSpec((M, tn), lambda j: (0, j)),
        out_shape=jax.ShapeDtypeStruct((M, N), F32),
        compiler_params=_cparams("parallel"),
    )(x, w, dep)


def _mm_nt_sum(xs, ws, dep, tn, tk, name):
    npair = len(xs)
    M, K = xs[0].shape
    N = ws[0].shape[0]
    nk = K // tk

    def body(*refs):
        x_refs = refs[:npair]
        w_refs = refs[npair:2 * npair]
        o_ref = refs[2 * npair + 1]
        acc = refs[2 * npair + 2]
        k = pl.program_id(1)

        @pl.when(k == 0)
        def _():
            acc[...] = jnp.zeros_like(acc)

        for x_ref, w_ref in zip(x_refs, w_refs):
            acc[...] += lax.dot_general(x_ref[...], w_ref[...], NT_DIMS, preferred_element_type=F32)

        @pl.when(k == nk - 1)
        def _():
            o_ref[...] = acc[...]

    return pl.pallas_call(
        body, name=name, grid=(N // tn, nk),
        in_specs=[pl.BlockSpec((M, tk), lambda j, k: (0, k))] * npair
        + [pl.BlockSpec((tn, tk), lambda j, k: (j, k))] * npair + [pl.BlockSpec(memory_space=pl.ANY)],
        out_specs=pl.BlockSpec((M, tn), lambda j, k: (0, j)),
        out_shape=jax.ShapeDtypeStruct((M, N), F32),
        scratch_shapes=[pltpu.VMEM((M, tn), F32)],
        compiler_params=_cparams("parallel", "arbitrary"),
    )(*xs, *ws, dep)


def _mm_tn(xt, dy, dep, alpha, ti, tn, name, stacked):
    Kin, M = xt.shape
    N = dy.shape[1]

    def body(xt_ref, dy_ref, dep_ref, ob_ref):
        r = jnp.dot(xt_ref[...], dy_ref[...], preferred_element_type=F32)
        if alpha != 1.0:
            r = alpha * r
        ob_ref[...] = r.astype(BF16)

    if stacked:
        ospec = pl.BlockSpec((None, ti, tn), lambda i, j: (j, i, 0))
        oshape = (N // tn, Kin, tn)
    else:
        ospec = pl.BlockSpec((ti, tn), lambda i, j: (i, j))
        oshape = (Kin, N)
    return pl.pallas_call(
        body, name=name, grid=(Kin // ti, N // tn),
        in_specs=[pl.BlockSpec((ti, M), lambda i, j: (i, 0)), pl.BlockSpec((M, tn), lambda i, j: (0, j)),
                  pl.BlockSpec(memory_space=pl.ANY)],
        out_specs=ospec,
        out_shape=jax.ShapeDtypeStruct(oshape, BF16),
        compiler_params=_cparams("parallel", "parallel"),
    )(xt, dy, dep)


def _rmsnorm_fwd(h, g, dep, name):
    M, D = h.shape
    tr = LANES

    def body(h_ref, g_ref, dep_ref, n_ref, nt_ref, r_ref):
        hv = h_ref[...]
        r = lax.rsqrt(jnp.mean(hv * hv, axis=-1, keepdims=True) + EPS)
        n = hv * r * g_ref[...]
        n_ref[...] = n.astype(BF16)
        nt_ref[...] = n.T.astype(BF16)
        r_ref[...] = r

    return pl.pallas_call(
        body, name=name, grid=(M // tr,),
        in_specs=[pl.BlockSpec((tr, D), lambda i: (i, 0)), pl.BlockSpec((1, D), lambda i: (0, 0)),
                  pl.BlockSpec(memory_space=pl.ANY)],
        out_specs=[pl.BlockSpec((tr, D), lambda i: (i, 0)), pl.BlockSpec((D, tr), lambda i: (0, i)),
                   pl.BlockSpec((tr, 1), lambda i: (i, 0))],
        out_shape=[jax.ShapeDtypeStruct((M, D), BF16), jax.ShapeDtypeStruct((D, M), BF16),
                   jax.ShapeDtypeStruct((M, 1), F32)],
        compiler_params=_cparams("parallel"),
    )(h, g, dep)


def _rmsnorm_bwd(dn, h, r, g, dh_prev, name):
    M, D = h.shape
    tr = _row_tile(M, D)

    def body(dn_ref, h_ref, r_ref, g_ref, dp_ref, dh_ref, dhb_ref, dg_ref):
        i = pl.program_id(0)
        dnv = dn_ref[...]
        hv = h_ref[...]
        rv = r_ref[...]
        w = dnv * g_ref[...]
        c = jnp.mean(w * hv, axis=-1, keepdims=True)
        dh = dp_ref[...] + rv * w - hv * (rv * rv * rv * c)
        dh_ref[...] = dh
        dhb_ref[...] = dh.astype(BF16)
        part = jnp.sum(dnv * (hv * rv), axis=0, keepdims=True)

        @pl.when(i == 0)
        def _():
            dg_ref[...] = part

        @pl.when(i > 0)
        def _():
            dg_ref[...] += part

    row = pl.BlockSpec((tr, D), lambda i: (i, 0))
    vec = pl.BlockSpec((1, D), lambda i: (0, 0))
    return pl.pallas_call(
        body, name=name, grid=(M // tr,),
        in_specs=[row, row, pl.BlockSpec((tr, 1), lambda i: (i, 0)), vec, row],
        out_specs=[row, row, vec],
        out_shape=[jax.ShapeDtypeStruct((M, D), F32), jax.ShapeDtypeStruct((M, D), BF16),
                   jax.ShapeDtypeStruct((1, D), F32)],
        compiler_params=_cparams("arbitrary"),
    )(dn, h, r, g, dh_prev)


def _loss_grad(h, tgt, seq, name):
    M, D = h.shape
    tr = _row_tile(M, D)

    def body(h_ref, t_ref, dh_ref, dhb_ref, loss_ref):
        i = pl.program_id(0)
        row = i * tr + lax.broadcasted_iota(jnp.int32, (tr, 1), 0)
        valid = (row >= N_META) & (row < N_META + seq)
        d = jnp.where(valid, h_ref[...] - t_ref[...], 0.0)
        dh = d * (1.0 / D)
        dh_ref[...] = dh
        dhb_ref[...] = dh.astype(BF16)
        part = (0.5 / D) * jnp.sum(jnp.sum(d * d, axis=1, keepdims=True), axis=0, keepdims=True)

        @pl.when(i == 0)
        def _():
            loss_ref[...] = part

        @pl.when(i > 0)
        def _():
            loss_ref[...] += part

    row = pl.BlockSpec((tr, D), lambda i: (i, 0))
    return pl.pallas_call(
        body, name=name, grid=(M // tr,),
        in_specs=[row, row],
        out_specs=[row, row, pl.BlockSpec((1, 1), lambda i: (0, 0))],
        out_shape=[jax.ShapeDtypeStruct((M, D), F32), jax.ShapeDtypeStruct((M, D), BF16),
                   jax.ShapeDtypeStruct((1, 1), F32)],
        compiler_params=_cparams("arbitrary"),
    )(h, tgt)


def _group_window(g):
    return jnp.where(g == 0, 2, jnp.where(g == 1, 4, jnp.where(g == 2, 8, 16)))


def _pool_fwd(z, pw, psc, name):
    M = z.shape[0]
    C = pw.shape[1]

    def body(p_ref, w_ref, sc_ref, pooled_ref, out_ref, outt_ref):
        g = pl.program_id(0)
        p = p_ref[...]
        t = lax.broadcasted_iota(jnp.int32, (M, 1), 0)
        s = p
        wsum = jnp.zeros_like(p)
        for step in range(N_POOL_GROUPS):
            sh = 1 << step
            s = s + jnp.where(t >= sh, pltpu.roll(s, sh, 0), 0.0)
            wsum = jnp.where(g == step, s, wsum)
        cnt = jnp.minimum(t + 1, _group_window(g)).astype(F32)
        pb = (wsum / cnt - p).astype(BF16)
        pooled_ref[...] = pb
        out = jnp.dot(pb, w_ref[...], preferred_element_type=F32) * sc_ref[...]
        out_ref[...] = out.astype(BF16)
        outt_ref[...] = out.T.astype(BF16)

    col = pl.BlockSpec((M, C), lambda g: (0, g))
    return pl.pallas_call(
        body, name=name, grid=(N_POOL_GROUPS,),
        in_specs=[col, pl.BlockSpec((None, C, C), lambda g: (g, 0, 0)), pl.BlockSpec((1, C), lambda g: (0, g))],
        out_specs=[col, col, pl.BlockSpec((C, M), lambda g: (g, 0))],
        out_shape=[jax.ShapeDtypeStruct((M, N_POOL_GROUPS * C), BF16),
                   jax.ShapeDtypeStruct((M, N_POOL_GROUPS * C), BF16),
                   jax.ShapeDtypeStruct((N_POOL_GROUPS * C, M), BF16)],
        compiler_params=_cparams("parallel"),
    )(z, pw, psc)


def _pool_bwd(dmix, pooled, pw, psc, name):
    M = dmix.shape[0]
    C = pw.shape[1]

    def body(dm_ref, pooled_ref, w_ref, sc_ref, dp_ref, dwb_ref, dsc_ref):
        g = pl.program_id(0)
        dmx = dm_ref[...]
        pb = pooled_ref[...]
        wv = w_ref[...]
        mixed = jnp.dot(pb, wv, preferred_element_type=F32)
        dsc_ref[...] = jnp.sum(dmx * mixed, axis=0, keepdims=True)
        dmixed = (dmx * sc_ref[...]).astype(BF16)
        dw = jnp.dot(pb.astype(F32).T.astype(BF16), dmixed, preferred_element_type=F32)
        dwb_ref[...] = dw.astype(BF16)
        dpooled = lax.dot_general(dmixed, wv, NT_DIMS, preferred_element_type=F32)
        t = lax.broadcasted_iota(jnp.int32, (M, 1), 0)
        cnt = jnp.minimum(t + 1, _group_window(g)).astype(F32)
        s = dpooled / cnt
        wsum = jnp.zeros_like(s)
        for step in range(N_POOL_GROUPS):
            sh = 1 << step
            s = s + jnp.where(t < M - sh, pltpu.roll(s, M - sh, 0), 0.0)
            wsum = jnp.where(g == step, s, wsum)
        dp_ref[...] = (wsum - dpooled).astype(BF16)

    col = pl.BlockSpec((M, C), lambda g: (0, g))
    wspec = pl.BlockSpec((None, C, C), lambda g: (g, 0, 0))
    vec = pl.BlockSpec((1, C), lambda g: (0, g))
    return pl.pallas_call(
        body, name=name, grid=(N_POOL_GROUPS,),
        in_specs=[col, col, wspec, vec],
        out_specs=[col, wspec, vec],
        out_shape=[jax.ShapeDtypeStruct((M, N_POOL_GROUPS * C), BF16),
                   jax.ShapeDtypeStruct((N_POOL_GROUPS, C, C), BF16),
                   jax.ShapeDtypeStruct((1, N_POOL_GROUPS * C), F32)],
        compiler_params=_cparams("parallel"),
    )(dmix, pooled, pw, psc)


def _qkv_prep(z, gq, gk, n_heads, q_col, name):
    M = z.shape[0]
    H = n_heads
    qb = q_col // HEAD_DIM

    def body(q_ref, k_ref, v_ref, gq_ref, gk_ref, qh_ref, kh_ref, vb_ref):
        def norm(xv, g):
            r = lax.rsqrt(jnp.mean(xv * xv, axis=-1, keepdims=True) + EPS)
            return (xv * r * g).astype(BF16)

        qh_ref[...] = norm(q_ref[...], gq_ref[...])
        kh_ref[...] = norm(k_ref[...], gk_ref[...])
        vb_ref[...] = v_ref[...].astype(BF16)

    vec = pl.BlockSpec((1, HEAD_DIM), lambda h: (0, 0))
    out = pl.BlockSpec((M, HEAD_DIM), lambda h: (0, h))
    oshape = jax.ShapeDtypeStruct((M, H * HEAD_DIM), BF16)
    return pl.pallas_call(
        body, name=name, grid=(H,),
        in_specs=[pl.BlockSpec((M, HEAD_DIM), lambda h: (0, qb + h)),
                  pl.BlockSpec((M, HEAD_DIM), lambda h: (0, qb + H + h)),
                  pl.BlockSpec((M, HEAD_DIM), lambda h: (0, qb + 2 * H + h)), vec, vec],
        out_specs=[out, out, out],
        out_shape=[oshape, oshape, oshape],
        compiler_params=_cparams("parallel"),
    )(z, z, z, gq, gk)


def _forget_fwd(z, bpad, f_block, name):
    M = z.shape[0]

    def body(f_ref, b_ref, cum_ref):
        xx = f_ref[...] + b_ref[...]
        c = jnp.minimum(xx, 0.0) - jnp.log(1.0 + jnp.exp(-jnp.abs(xx)))
        t = lax.broadcasted_iota(jnp.int32, (M, 1), 0)
        sh = 1
        while sh < M:
            c = c + jnp.where(t >= sh, pltpu.roll(c, sh, 0), 0.0)
            sh *= 2
        cum_ref[...] = c.T

    return pl.pallas_call(
        body, name=name, grid=(1,),
        in_specs=[pl.BlockSpec((M, LANES), lambda i: (0, f_block)), pl.BlockSpec((1, LANES), lambda i: (0, 0))],
        out_specs=pl.BlockSpec((LANES, M), lambda i: (0, 0)),
        out_shape=jax.ShapeDtypeStruct((LANES, M), F32),
        compiler_params=_cparams("arbitrary"),
    )(z, bpad)


def _col_to_row(col):
    n = col.shape[0]
    return jnp.transpose(jnp.broadcast_to(col, (n, LANES)))[0:1, :]


def _causal_extents(M):
    edges = list(range(0, M, CAUSAL_STEP)) + [M]
    return list(zip(edges[:-1], edges[1:]))


def _heads_per_step(n_heads):
    return 2 if n_heads % 2 == 0 else 1


def _attn_fwd(qh, kh, vb, cum_c, cum_r, name):
    M = qh.shape[0]
    H = qh.shape[1] // HEAD_DIM
    hp = _heads_per_step(H)
    scale = 1.0 / math.sqrt(HEAD_DIM)

    def body(q_ref, k_ref, v_ref, cq_ref, ck_ref, o_ref, ot_ref, lc_ref, lr_ref):
        i = pl.program_id(1)

        def compute(n):
            row = i * TQ + lax.broadcasted_iota(jnp.int32, (TQ, 1), 0)
            col = lax.broadcasted_iota(jnp.int32, (1, n), 1)
            for hh in range(hp):
                d0, d1 = hh * HEAD_DIM, (hh + 1) * HEAD_DIM
                s = lax.dot_general(q_ref[:, d0:d1], k_ref[0:n, d0:d1], NT_DIMS, preferred_element_type=F32) * scale
                s = s + (cq_ref[hh] - ck_ref[hh, :, 0:n])
                s = jnp.where(row >= col, s, NEG)
                m = jnp.max(s, axis=1, keepdims=True)
                p = jnp.exp(s - m)
                l = jnp.sum(p, axis=1, keepdims=True)
                pn = (p / l).astype(BF16)
                o = jnp.dot(pn, v_ref[0:n, d0:d1], preferred_element_type=F32)
                o_ref[:, d0:d1] = o.astype(BF16)
                ot_ref[d0:d1, :] = o.T.astype(BF16)
                lse = m + jnp.log(l)
                lc_ref[hh] = lse
                lr_ref[hh] = _col_to_row(lse)

        for lo, hi in _causal_extents(M):
            pl.when((i >= lo // TQ) & (i < hi // TQ))(functools.partial(compute, hi))

    full = pl.BlockSpec((M, hp * HEAD_DIM), lambda h, i: (0, h))
    tile = pl.BlockSpec((TQ, hp * HEAD_DIM), lambda h, i: (i, h))
    colv = pl.BlockSpec((hp, TQ, 1), lambda h, i: (h, i, 0))
    rowv_full = pl.BlockSpec((hp, 1, M), lambda h, i: (h, 0, 0))
    rowv = pl.BlockSpec((hp, 1, TQ), lambda h, i: (h, 0, i))
    return pl.pallas_call(
        body, name=name, grid=(H // hp, M // TQ),
        in_specs=[tile, full, full, colv, rowv_full],
        out_specs=[tile, pl.BlockSpec((hp * HEAD_DIM, TQ), lambda h, i: (h, i)), colv, rowv],
        out_shape=[jax.ShapeDtypeStruct((M, H * HEAD_DIM), BF16), jax.ShapeDtypeStruct((H * HEAD_DIM, M), BF16),
                   jax.ShapeDtypeStruct((H, M, 1), F32), jax.ShapeDtypeStruct((H, 1, M), F32)],
        compiler_params=_cparams("parallel", "parallel"),
    )(qh, kh, vb, cum_c, cum_r)


def _attn_bwd_q(qh, kh, vb, dob, cum_c, cum_r, lse_c, name):
    M = qh.shape[0]
    H = qh.shape[1] // HEAD_DIM
    hp = _heads_per_step(H)
    scale = 1.0 / math.sqrt(HEAD_DIM)

    def body(q_ref, k_ref, v_ref, do_ref, cq_ref, ck_ref, l_ref, dq_ref, dr_ref, dcq_ref):
        i = pl.program_id(1)

        def compute(n):
            row = i * TQ + lax.broadcasted_iota(jnp.int32, (TQ, 1), 0)
            col = lax.broadcasted_iota(jnp.int32, (1, n), 1)
            for hh in range(hp):
                d0, d1 = hh * HEAD_DIM, (hh + 1) * HEAD_DIM
                k = k_ref[0:n, d0:d1]
                s = lax.dot_general(q_ref[:, d0:d1], k, NT_DIMS, preferred_element_type=F32) * scale
                s = s + (cq_ref[hh] - ck_ref[hh, :, 0:n])
                p = jnp.exp(jnp.where(row >= col, s, NEG) - l_ref[hh])
                dp = lax.dot_general(do_ref[:, d0:d1], v_ref[0:n, d0:d1], NT_DIMS, preferred_element_type=F32)
                delta = jnp.sum(p * dp, axis=1, keepdims=True)
                ds = p * (dp - delta)
                dq_ref[:, d0:d1] = jnp.dot((ds * scale).astype(BF16), k, preferred_element_type=F32)
                dr_ref[hh] = _col_to_row(delta)
                dcq_ref[hh] = jnp.sum(ds, axis=1, keepdims=True)

        for lo, hi in _causal_extents(M):
            pl.when((i >= lo // TQ) & (i < hi // TQ))(functools.partial(compute, hi))

    full = pl.BlockSpec((M, hp * HEAD_DIM), lambda h, i: (0, h))
    tile = pl.BlockSpec((TQ, hp * HEAD_DIM), lambda h, i: (i, h))
    colv = pl.BlockSpec((hp, TQ, 1), lambda h, i: (h, i, 0))
    rowv_full = pl.BlockSpec((hp, 1, M), lambda h, i: (h, 0, 0))
    rowv = pl.BlockSpec((hp, 1, TQ), lambda h, i: (h, 0, i))
    return pl.pallas_call(
        body, name=name, grid=(H // hp, M // TQ),
        in_specs=[tile, full, full, tile, colv, rowv_full, colv],
        out_specs=[tile, rowv, colv],
        out_shape=[jax.ShapeDtypeStruct((M, H * HEAD_DIM), F32), jax.ShapeDtypeStruct((H, 1, M), F32),
                   jax.ShapeDtypeStruct((H, M, 1), F32)],
        compiler_params=_cparams("parallel", "parallel"),
    )(qh, kh, vb, dob, cum_c, cum_r, lse_c)


def _attn_bwd_kv(qh, kh, vb, dob, cum_c, cum_r, lse_r, delta_r, name):
    M = qh.shape[0]
    H = qh.shape[1] // HEAD_DIM
    hp = _heads_per_step(H)
    scale = 1.0 / math.sqrt(HEAD_DIM)

    def body(k_ref, v_ref, q_ref, do_ref, cq_ref, ck_ref, l_ref, d_ref, dk_ref, dv_ref, dck_ref):
        j = pl.program_id(1)

        def compute(q0):
            krow = j * TQ + lax.broadcasted_iota(jnp.int32, (TQ, 1), 0)
            qcol = q0 + lax.broadcasted_iota(jnp.int32, (1, M - q0), 1)
            for hh in range(hp):
                d0, d1 = hh * HEAD_DIM, (hh + 1) * HEAD_DIM
                q = q_ref[q0:M, d0:d1]
                do = do_ref[q0:M, d0:d1]
                st = lax.dot_general(k_ref[:, d0:d1], q, NT_DIMS, preferred_element_type=F32) * scale
                st = st + (cq_ref[hh, :, q0:M] - ck_ref[hh])
                pt = jnp.exp(jnp.where(qcol >= krow, st, NEG) - l_ref[hh, :, q0:M])
                dpt = lax.dot_general(v_ref[:, d0:d1], do, NT_DIMS, preferred_element_type=F32)
                dst = pt * (dpt - d_ref[hh, :, q0:M])
                dv_ref[:, d0:d1] = jnp.dot(pt.astype(BF16), do, preferred_element_type=F32).astype(BF16)
                dk_ref[:, d0:d1] = jnp.dot((dst * scale).astype(BF16), q, preferred_element_type=F32)
                dck_ref[hh] = -jnp.sum(dst, axis=1, keepdims=True)

        for lo, hi in _causal_extents(M):
            pl.when((j >= lo // TQ) & (j < hi // TQ))(functools.partial(compute, lo))

    full = pl.BlockSpec((M, hp * HEAD_DIM), lambda h, j: (0, h))
    tile = pl.BlockSpec((TQ, hp * HEAD_DIM), lambda h, j: (j, h))
    colv = pl.BlockSpec((hp, TQ, 1), lambda h, j: (h, j, 0))
    rowv_full = pl.BlockSpec((hp, 1, M), lambda h, j: (h, 0, 0))
    return pl.pallas_call(
        body, name=name, grid=(H // hp, M // TQ),
        in_specs=[tile, tile, full, full, rowv_full, colv, rowv_full, rowv_full],
        out_specs=[tile, tile, colv],
        out_shape=[jax.ShapeDtypeStruct((M, H * HEAD_DIM), F32), jax.ShapeDtypeStruct((M, H * HEAD_DIM), BF16),
                   jax.ShapeDtypeStruct((H, M, 1), F32)],
        compiler_params=_cparams("parallel", "parallel"),
    )(kh, vb, qh, dob, cum_r, cum_c, lse_r, delta_r)


def _qk_norm_bwd(dqh, dkh, z, gq, gk, n_heads, q_col, name):
    M = z.shape[0]
    H = n_heads
    qb = q_col // HEAD_DIM

    def body(dqh_ref, dkh_ref, q_ref, k_ref, gq_ref, gk_ref, dq_ref, dk_ref, dgq_ref, dgk_ref):
        h = pl.program_id(0)

        def one(dy, xv, g):
            r = lax.rsqrt(jnp.mean(xv * xv, axis=-1, keepdims=True) + EPS)
            w = dy * g
            c = jnp.mean(w * xv, axis=-1, keepdims=True)
            dx = r * w - xv * (r * r * r * c)
            return dx.astype(BF16), jnp.sum(dy * (xv * r), axis=0, keepdims=True)

        dq, dgq = one(dqh_ref[...], q_ref[...], gq_ref[...])
        dk, dgk = one(dkh_ref[...], k_ref[...], gk_ref[...])
        dq_ref[...] = dq
        dk_ref[...] = dk

        @pl.when(h == 0)
        def _():
            dgq_ref[...] = dgq
            dgk_ref[...] = dgk

        @pl.when(h > 0)
        def _():
            dgq_ref[...] += dgq
            dgk_ref[...] += dgk

    vec = pl.BlockSpec((1, HEAD_DIM), lambda h: (0, 0))
    head = pl.BlockSpec((M, HEAD_DIM), lambda h: (0, h))
    return pl.pallas_call(
        body, name=name, grid=(H,),
        in_specs=[head, head, pl.BlockSpec((M, HEAD_DIM), lambda h: (0, qb + h)),
                  pl.BlockSpec((M, HEAD_DIM), lambda h: (0, qb + H + h)), vec, vec],
        out_specs=[head, head, vec, vec],
        out_shape=[jax.ShapeDtypeStruct((M, H * HEAD_DIM), BF16), jax.ShapeDtypeStruct((M, H * HEAD_DIM), BF16),
                   jax.ShapeDtypeStruct((1, HEAD_DIM), F32), jax.ShapeDtypeStruct((1, HEAD_DIM), F32)],
        compiler_params=_cparams("arbitrary"),
    )(dqh, dkh, z, z, gq, gk)


def _forget_bwd(dcq, dck, z, bpad, f_block, name):
    H, M, _ = dcq.shape

    def body(dcq_ref, dck_ref, f_ref, b_ref, dfl_ref, db_ref):
        lane = lax.broadcasted_iota(jnp.int32, (1, LANES), 1)
        d = jnp.zeros((M, LANES), F32)
        for h in range(H):
            d = d + (dcq_ref[h] + dck_ref[h]) * (lane == h).astype(F32)
        t = lax.broadcasted_iota(jnp.int32, (M, 1), 0)
        sh = 1
        while sh < M:
            d = d + jnp.where(t < M - sh, pltpu.roll(d, M - sh, 0), 0.0)
            sh *= 2
        xx = f_ref[...] + b_ref[...]
        dfl = d * (1.0 / (1.0 + jnp.exp(xx)))
        dfl_ref[...] = dfl.astype(BF16)
        db_ref[...] = jnp.sum(dfl, axis=0, keepdims=True)

    colv = pl.BlockSpec((H, M, 1), lambda i: (0, 0, 0))
    return pl.pallas_call(
        body, name=name, grid=(1,),
        in_specs=[colv, colv, pl.BlockSpec((M, LANES), lambda i: (0, f_block)),
                  pl.BlockSpec((1, LANES), lambda i: (0, 0))],
        out_specs=[pl.BlockSpec((M, LANES), lambda i: (0, 0)), pl.BlockSpec((1, LANES), lambda i: (0, 0))],
        out_shape=[jax.ShapeDtypeStruct((M, LANES), BF16), jax.ShapeDtypeStruct((1, LANES), F32)],
        compiler_params=_cparams("arbitrary"),
    )(dcq, dck, z, bpad)


def _ffn_fwd(h, g, dep, get_weights, gate_first, tag):
    n, nt, r = _rmsnorm_fwd(h, g, dep, f"{tag}_norm")
    if gate_first:
        wg = get_weights(f"{tag}_gate", n)["wg"]
        a = _mm_nn(n, wg, 256, f"{tag}_gate")
        wu = get_weights(f"{tag}_up", a)["wu"]
        p, q, s, st = _ffn_up_from_gate(n, a, wu, 256, f"{tag}_up")
    else:
        up = get_weights(f"{tag}_up", n)
        wg, wu = up["wg"], up["wu"]
        p, q, s, st = _ffn_up(n, wg, wu, n, 256, f"{tag}_up")
    wd = get_weights(f"{tag}_down", s)["wd"]
    h_out = _mm_nn_residual(s, wd, h, s, 0.5, 256, f"{tag}_down")
    return h_out, (nt, r, p, q, st, wg, wu, wd)


def _ffn_bwd(dh, dhb, h, g, saved, dep, put_grads, tag):
    nt, r, p, q, st, wg, wu, wd = saved
    n_shards = 4
    da, db = _ffn_bwd_hidden(dhb, wd, p, q, dep, 256, f"{tag}_bwd_hidden")
    dwd = _mm_tn(st, dhb, dep, 0.5, st.shape[0] // n_shards, 1024, f"{tag}_dw_down", stacked=False)
    dep = put_grads(f"{tag}_w_down", dwd)
    dwg = _mm_tn(nt, da, dep, 1.0, 1024, wg.shape[1] // n_shards, f"{tag}_dw_gate", stacked=True)
    dep = put_grads(f"{tag}_w_gate", dwg)
    dwu = _mm_tn(nt, db, dep, 1.0, 1024, wu.shape[1] // n_shards, f"{tag}_dw_up", stacked=True)
    dep = put_grads(f"{tag}_w_up", dwu)
    dn = _mm_nt_sum([da, db], [wg, wu], dep, 512, wg.shape[1] // 4, f"{tag}_dn")
    dh_in, dhb_in, dg = _rmsnorm_bwd(dn, h, r, g, dh, f"{tag}_norm_bwd")
    return dh_in, dhb_in, dg


def _local_step(x, target, S, get_weights, request, put_grads):
    seq, D = x.shape
    L = N_META + seq
    Lp = -(-L // SEQ_ALIGN) * SEQ_ALIGN
    pad = jnp.zeros((Lp - L, D), F32)
    tgt = jnp.concatenate([jnp.zeros((N_META, D), F32), target, pad], axis=0)

    d_pool = S["pool_scale"].shape[1]
    n_heads = S["b_forget"].shape[1]
    d_att = n_heads * HEAD_DIM
    f_col = d_pool + 3 * d_att
    f_block = f_col // LANES
    bpad = jnp.pad(S["b_forget"], ((0, 0), (0, LANES - n_heads)))

    h0 = jnp.concatenate([get_weights("meta", None)["meta"], x, pad], axis=0)
    h1, ffn1 = _ffn_fwd(h0, S["ffn1_norm"], h0, get_weights, True, "ffn1")
    u, ut, r_mix = _rmsnorm_fwd(h1, S["mix_norm"], request("mix", h1), "mix_norm")
    Wm = get_weights("mix", u)
    z = _mm_nn(u, Wm["win"], 384, "in_proj")
    pooled, pool_out, pool_out_t = _pool_fwd(z, Wm["pool_w"], S["pool_scale"], "pool_fwd")
    qh, kh, vb = _qkv_prep(z, S["q_norm"], S["k_norm"], n_heads, d_pool, "qkv_prep")
    cum_t = _forget_fwd(z, bpad, f_block, "forget_fwd")[:n_heads]
    cum_c = cum_t.reshape(n_heads, Lp, 1)
    cum_r = cum_t.reshape(n_heads, 1, Lp)
    att, att_t, lse_c, lse_r = _attn_fwd(qh, kh, vb, cum_c, cum_r, "attn_fwd")
    mix = jnp.concatenate([pool_out, att], axis=1)
    mix_t = jnp.concatenate([pool_out_t, att_t], axis=0)
    dep = request("ffn2_up", att)
    h2 = _mm_nn_residual(mix, Wm["wout"], h1, dep, 1.0, 512, "out_proj")
    h3, ffn2 = _ffn_fwd(h2, S["ffn2_norm"], h2, get_weights, False, "ffn2")

    dh3, dh3b, loss = _loss_grad(h3, tgt, seq, "loss")
    dh2, dh2b, dg_ffn2 = _ffn_bwd(dh3, dh3b, h2, S["ffn2_norm"], ffn2, loss, put_grads, "ffn2")

    dmix = _mm_nt(dh2b, Wm["wout"], loss, 512, "out_proj_bwd")
    dwout = _mm_tn(mix_t, dh2b, loss, 1.0, 1024, 1024, "dw_out", stacked=False)
    dp, dpw, dpsc = _pool_bwd(dmix, pooled, Wm["pool_w"], S["pool_scale"], "pool_bwd")
    dob = dmix[:, d_pool:].astype(BF16)
    dqh, delta_r, dcq = _attn_bwd_q(qh, kh, vb, dob, cum_c, cum_r, lse_c, "attn_bwd_q")
    dkh, dv, dck = _attn_bwd_kv(qh, kh, vb, dob, cum_c, cum_r, lse_r, delta_r, "attn_bwd_kv")
    dq, dk, dgq, dgk = _qk_norm_bwd(dqh, dkh, z, S["q_norm"], S["k_norm"], n_heads, d_pool, "qk_norm_bwd")
    dfl, dbf = _forget_bwd(dcq, dck, z, bpad, f_block, "forget_bwd")
    dz = jnp.concatenate([dp, dq, dk, dv, dfl], axis=1)
    dwin = _mm_tn(ut, dz, loss, 1.0, 1024, dz.shape[1] // 3, "dw_in", stacked=False)
    dep = put_grads("mix", dict(win=dwin, wout=dwout, pool_w=dpw))
    du = _mm_nt_sum([dz], [Wm["win"]], dep, 512, Wm["win"].shape[1] // 3, "in_proj_bwd")
    dh1, dh1b, dg_mix = _rmsnorm_bwd(du, h1, r_mix, S["mix_norm"], dh2, "mix_norm_bwd")

    dh0, _, dg_ffn1 = _ffn_bwd(dh1, dh1b, h0, S["ffn1_norm"], ffn1, loss, put_grads, "ffn1")

    grads = dict(
        x=dh0[N_META:L], meta=dh0[:N_META],
        ffn1_norm=dg_ffn1, mix_norm=dg_mix, ffn2_norm=dg_ffn2, q_norm=dgq, k_norm=dgk,
        b_forget=dbf[:, :n_heads], pool_scale=dpsc,
    )
    return loss[0, 0], dh0, grads


HBM_SPEC = pl.BlockSpec(memory_space=pltpu.HBM)
N_CHIPS = 4


def _chip_peers():
    x, y, c = lax.axis_index("x"), lax.axis_index("y"), lax.axis_index("c")
    flips = [(1 - x, y), (x, 1 - y), (1 - x, 1 - y)]
    return 2 * x + y, [((px, py, c), 2 * px + py) for px, py in flips]


def _gathered_shape(shape, layout):
    if layout == "rows":
        return (N_CHIPS * shape[0],) + shape[1:]
    if layout == "cols":
        return (shape[0], N_CHIPS * shape[1])
    return (N_CHIPS,) + shape


def _cast_place(place, w, dep, layout, dtype, name):
    R, C = w.shape
    tr = _row_tile(R, C)
    nt = R // tr

    def body(place_ref, w_ref, dep_ref, o_ref):
        o_ref[...] = w_ref[...].astype(dtype)

    if layout == "rows":
        ospec = pl.BlockSpec((tr, C), lambda i, p: (p[1] * nt + i, 0))
    elif layout == "cols":
        ospec = pl.BlockSpec((tr, C), lambda i, p: (i, p[1]))
    else:
        ospec = pl.BlockSpec((None, tr, C), lambda i, p: (p[1], i, 0))
    return pl.pallas_call(
        body, name=name,
        grid_spec=pltpu.PrefetchScalarGridSpec(
            num_scalar_prefetch=1, grid=(nt,),
            in_specs=[pl.BlockSpec((tr, C), lambda i, p: (i, 0)), pl.BlockSpec(memory_space=pl.ANY)],
            out_specs=ospec),
        out_shape=jax.ShapeDtypeStruct(_gathered_shape((R, C), layout), dtype),
        compiler_params=_cparams("parallel"),
    )(place, w, dep)


SEM_SPEC = pl.BlockSpec(memory_space=pltpu.SEMAPHORE)
ANY_SPEC = pl.BlockSpec(memory_space=pl.ANY)
SPLIT_COPY = pltpu.CompilerParams(has_side_effects=pltpu.SideEffectType.DATAFLOW_SIDE_EFFECTING)


def _hbm(a):
    return pltpu.with_memory_space_constraint(a, pltpu.HBM)


def _gather_region(refs, shard_shapes, layouts, a, chip, half):
    rows_a = shard_shapes[a][0]
    h = rows_a // 2
    if layouts[a] == "rows":
        return refs[a].at[pl.ds(chip * rows_a + half * h, h)]
    if layouts[a] == "cols":
        cols_a = shard_shapes[a][1]
        return refs[a].at[pl.ds(half * h, h), pl.ds(chip * cols_a, cols_a)]
    return refs[a].at[chip, pl.ds(half * h, h)]


def _gather_start(bufs, after, shard_shapes, layouts, name):
    n = len(bufs)
    ns = 3 * n

    def body(*refs):
        in_refs = refs[:n]
        send_sems = refs[n + 1:n + 1 + ns]
        recv_sems = refs[n + 1 + ns:n + 1 + 2 * ns]
        token = refs[2 * n + 1 + 2 * ns]
        c = lax.axis_index("c")
        me, peers = _chip_peers()
        for a in range(n):
            mine = _gather_region(in_refs, shard_shapes, layouts, a, me, c)
            for k, (dev, _) in enumerate(peers):
                pltpu.make_async_remote_copy(
                    src_ref=mine, dst_ref=mine, send_sem=send_sems[3 * a + k], recv_sem=recv_sems[3 * a + k],
                    device_id=dev, device_id_type=MESH).start()
        token[...] = jnp.zeros_like(token)

    sem = pltpu.SemaphoreType.DMA(())
    out = pl.pallas_call(
        body, name=name,
        out_shape=(*[sem] * (2 * ns), *[pltpu.HBM(b.shape, b.dtype) for b in bufs],
                   jax.ShapeDtypeStruct((8, LANES), F32)),
        in_specs=[HBM_SPEC] * n + [ANY_SPEC],
        out_specs=(*[SEM_SPEC] * (2 * ns), *[HBM_SPEC] * n, pl.BlockSpec(memory_space=pltpu.VMEM)),
        input_output_aliases={a: 2 * ns + a for a in range(n)},
        compiler_params=SPLIT_COPY,
    )(*[_hbm(b) for b in bufs], after)
    return list(out[:ns]), list(out[ns:2 * ns]), list(out[2 * ns:2 * ns + n]), out[2 * ns + n]


def _gather_wait(bufs, send_sems, recv_sems, afters, shard_shapes, layouts, name):
    n = len(bufs)
    ns = 3 * n
    na = len(afters)

    def body(*refs):
        in_refs = refs[:n]
        send_sems = refs[n:n + ns]
        recv_sems = refs[n + ns:n + 2 * ns]
        token = refs[2 * n + 2 * ns + na]
        token[...] = jnp.zeros_like(token)
        c = lax.axis_index("c")
        me, peers = _chip_peers()
        for a in range(n):
            mine = _gather_region(in_refs, shard_shapes, layouts, a, me, c)
            for k, (dev, pidx) in enumerate(peers):
                landed = _gather_region(in_refs, shard_shapes, layouts, a, pidx, c)
                pltpu.make_async_remote_copy(
                    src_ref=mine, dst_ref=landed, send_sem=send_sems[3 * a + k], recv_sem=recv_sems[3 * a + k],
                    device_id=dev, device_id_type=MESH).wait_recv()
        for a in range(n):
            mine = _gather_region(in_refs, shard_shapes, layouts, a, me, c)
            for k, (dev, _) in enumerate(peers):
                pltpu.make_async_remote_copy(
                    src_ref=mine, dst_ref=mine, send_sem=send_sems[3 * a + k], recv_sem=recv_sems[3 * a + k],
                    device_id=dev, device_id_type=MESH).wait_send()

    out = pl.pallas_call(
        body, name=name,
        out_shape=(*[pltpu.HBM(b.shape, b.dtype) for b in bufs], jax.ShapeDtypeStruct((8, LANES), F32)),
        in_specs=[HBM_SPEC] * n + [SEM_SPEC] * (2 * ns) + [ANY_SPEC] * na,
        out_specs=(*[HBM_SPEC] * n, pl.BlockSpec(memory_space=pltpu.VMEM)),
        input_output_aliases={a: a for a in range(n)},
        compiler_params=SPLIT_COPY,
    )(*bufs, *send_sems, *recv_sems, *afters)
    return list(out[:n]), out[n]


def _forward_start(bufs, after, shard_shapes, layouts, name):
    n = len(bufs)
    ns = 3 * n

    def body(*refs):
        in_refs = refs[:n]
        send_sems = refs[n + 1:n + 1 + ns]
        recv_sems = refs[n + 1 + ns:n + 1 + 2 * ns]
        token = refs[2 * n + 1 + 2 * ns]
        c = lax.axis_index("c")
        sib = (lax.axis_index("x"), lax.axis_index("y"), 1 - c)
        _, peers = _chip_peers()
        for a in range(n):
            for k, (_, pidx) in enumerate(peers):
                landed = _gather_region(in_refs, shard_shapes, layouts, a, pidx, c)
                pltpu.make_async_remote_copy(
                    src_ref=landed, dst_ref=landed, send_sem=send_sems[3 * a + k], recv_sem=recv_sems[3 * a + k],
                    device_id=sib, device_id_type=MESH).start()
        token[...] = jnp.zeros_like(token)

    sem = pltpu.SemaphoreType.DMA(())
    out = pl.pallas_call(
        body, name=name,
        out_shape=(*[sem] * (2 * ns), *[pltpu.HBM(b.shape, b.dtype) for b in bufs],
                   jax.ShapeDtypeStruct((8, LANES), F32)),
        in_specs=[HBM_SPEC] * n + [ANY_SPEC],
        out_specs=(*[SEM_SPEC] * (2 * ns), *[HBM_SPEC] * n, pl.BlockSpec(memory_space=pltpu.VMEM)),
        input_output_aliases={a: 2 * ns + a for a in range(n)},
        compiler_params=SPLIT_COPY,
    )(*[_hbm(b) for b in bufs], after)
    return list(out[:ns]), list(out[ns:2 * ns]), list(out[2 * ns:2 * ns + n]), out[2 * ns + n]


def _forward_wait(bufs, send_sems, recv_sems, after, shard_shapes, layouts, name):
    n = len(bufs)
    ns = 3 * n

    def body(*refs):
        in_refs = refs[:n]
        send_sems = refs[n:n + ns]
        recv_sems = refs[n + ns:n + 2 * ns]
        c = lax.axis_index("c")
        sib = (lax.axis_index("x"), lax.axis_index("y"), 1 - c)
        _, peers = _chip_peers()
        for a in range(n):
            for k, (_, pidx) in enumerate(peers):
                landed = _gather_region(in_refs, shard_shapes, layouts, a, pidx, c)
                other = _gather_region(in_refs, shard_shapes, layouts, a, pidx, 1 - c)
                cp = pltpu.make_async_remote_copy(
                    src_ref=landed, dst_ref=other, send_sem=send_sems[3 * a + k], recv_sem=recv_sems[3 * a + k],
                    device_id=sib, device_id_type=MESH)
                cp.wait_recv()
                cp.wait_send()

    return list(pl.pallas_call(
        body, name=name,
        out_shape=tuple(pltpu.HBM(b.shape, b.dtype) for b in bufs),
        in_specs=[HBM_SPEC] * n + [SEM_SPEC] * (2 * ns) + [ANY_SPEC],
        out_specs=tuple([HBM_SPEC] * n),
        input_output_aliases={a: a for a in range(n)},
        compiler_params=SPLIT_COPY,
    )(*bufs, *send_sems, *recv_sems, after))


def _halves_copies(src_refs, land_refs, send_sems, recv_sems):
    c = lax.axis_index("c")
    sib = (lax.axis_index("x"), lax.axis_index("y"), 1 - c)
    copies = []
    for a, (src, land) in enumerate(zip(src_refs, land_refs)):
        h = src.shape[1] // 2
        copies.append(pltpu.make_async_remote_copy(
            src_ref=src.at[:, pl.ds((1 - c) * h, h)], dst_ref=land, send_sem=send_sems[a], recv_sem=recv_sems[a],
            device_id=sib, device_id_type=MESH))
    return copies


def _whole_copies(src_refs, land_refs, send_sems, recv_sems):
    sib = (lax.axis_index("x"), lax.axis_index("y"), 1 - lax.axis_index("c"))
    return [pltpu.make_async_remote_copy(src_ref=src, dst_ref=land, send_sem=send_sems[a], recv_sem=recv_sems[a],
                                         device_id=sib, device_id_type=MESH)
            for a, (src, land) in enumerate(zip(src_refs, land_refs))]


def _halves_land_shape(shape):
    return (shape[0], shape[1] // 2, shape[2])


def _sibling_start(stacked, copies, land_shape, name):
    n = len(stacked)
    lands = [lax.empty(land_shape(s.shape), s.dtype) for s in stacked]

    def body(*refs):
        for cp in copies(refs[:n], refs[n:2 * n], refs[2 * n:3 * n], refs[3 * n:4 * n]):
            cp.start()
        token = refs[6 * n]
        token[...] = jnp.zeros_like(token)

    sem = pltpu.SemaphoreType.DMA(())
    out = pl.pallas_call(
        body, name=name,
        out_shape=(*[sem] * (2 * n), *[pltpu.HBM(b.shape, b.dtype) for b in stacked],
                   *[pltpu.HBM(b.shape, b.dtype) for b in lands], jax.ShapeDtypeStruct((8, LANES), F32)),
        in_specs=[HBM_SPEC] * (2 * n),
        out_specs=(*[SEM_SPEC] * (2 * n), *[HBM_SPEC] * (2 * n), pl.BlockSpec(memory_space=pltpu.VMEM)),
        input_output_aliases={a: 2 * n + a for a in range(2 * n)},
        compiler_params=SPLIT_COPY,
    )(*[_hbm(b) for b in stacked], *[_hbm(b) for b in lands])
    return list(out[:n]), list(out[n:2 * n]), list(out[2 * n:3 * n]), list(out[3 * n:4 * n]), out[4 * n]


def _sibling_wait(srcs, lands, send_sems, recv_sems, after, copies_of, name):
    n = len(srcs)

    def body(*refs):
        copies = copies_of(refs[:n], refs[n:2 * n], refs[2 * n:3 * n], refs[3 * n:4 * n])
        for cp in copies:
            cp.wait_recv()
        for cp in copies:
            cp.wait_send()

    out = pl.pallas_call(
        body, name=name,
        out_shape=tuple(pltpu.HBM(b.shape, b.dtype) for b in list(srcs) + list(lands)),
        in_specs=[HBM_SPEC] * (2 * n) + [SEM_SPEC] * (2 * n) + [ANY_SPEC],
        out_specs=tuple([HBM_SPEC] * (2 * n)),
        input_output_aliases={a: a for a in range(2 * n)},
        compiler_params=SPLIT_COPY,
    )(*srcs, *lands, *send_sems, *recv_sems, after)
    return list(out[:n]), list(out[n:])


def _scatter_start(stacked, name):
    n = len(stacked)
    ns = 3 * n
    lands = [lax.empty((3,) + s.shape[1:], s.dtype) for s in stacked]

    def body(*refs):
        src_refs = refs[:n]
        land_refs = refs[n:2 * n]
        send_sems = refs[2 * n:2 * n + ns]
        recv_sems = refs[2 * n + ns:2 * n + 2 * ns]
        token = refs[4 * n + 2 * ns]
        _, peers = _chip_peers()
        for a in range(n):
            for k, (dev, pidx) in enumerate(peers):
                pltpu.make_async_remote_copy(
                    src_ref=src_refs[a].at[k], dst_ref=land_refs[a].at[k], send_sem=send_sems[3 * a + k],
                    recv_sem=recv_sems[3 * a + k], device_id=dev, device_id_type=MESH).start()
        token[...] = jnp.zeros_like(token)

    sem = pltpu.SemaphoreType.DMA(())
    out = pl.pallas_call(
        body, name=name,
        out_shape=(*[sem] * (2 * ns), *[pltpu.HBM(b.shape, b.dtype) for b in stacked],
                   *[pltpu.HBM(b.shape, b.dtype) for b in lands], jax.ShapeDtypeStruct((8, LANES), F32)),
        in_specs=[HBM_SPEC] * (2 * n),
        out_specs=(*[SEM_SPEC] * (2 * ns), *[HBM_SPEC] * (2 * n), pl.BlockSpec(memory_space=pltpu.VMEM)),
        input_output_aliases={a: 2 * ns + a for a in range(2 * n)},
        compiler_params=SPLIT_COPY,
    )(*[_hbm(b) for b in stacked], *[_hbm(b) for b in lands])
    o = 2 * ns
    return list(out[:ns]), list(out[ns:o]), list(out[o:o + n]), list(out[o + n:o + 2 * n]), out[o + 2 * n]


def _scatter_wait(srcs, lands, send_sems, recv_sems, after, name):
    n = len(srcs)
    ns = 3 * n

    def body(*refs):
        src_refs = refs[:n]
        land_refs = refs[n:2 * n]
        send_sems = refs[2 * n:2 * n + ns]
        recv_sems = refs[2 * n + ns:2 * n + 2 * ns]
        _, peers = _chip_peers()
        copies = [
            pltpu.make_async_remote_copy(
                src_ref=src_refs[a].at[k], dst_ref=land_refs[a].at[k], send_sem=send_sems[3 * a + k],
                recv_sem=recv_sems[3 * a + k], device_id=dev, device_id_type=MESH)
            for a in range(n) for k, (dev, pidx) in enumerate(peers)]
        for cp in copies:
            cp.wait_recv()
        for cp in copies:
            cp.wait_send()

    out = pl.pallas_call(
        body, name=name,
        out_shape=tuple(pltpu.HBM(b.shape, b.dtype) for b in list(srcs) + list(lands)),
        in_specs=[HBM_SPEC] * (2 * n) + [SEM_SPEC] * (2 * ns) + [ANY_SPEC],
        out_specs=tuple([HBM_SPEC] * (2 * n)),
        input_output_aliases={a: a for a in range(2 * n)},
        compiler_params=SPLIT_COPY,
    )(*srcs, *lands, *send_sems, *recv_sems, after)
    return list(out[n:])


def _all_reduce_small(v):
    R, C = v.shape
    n_dev = 8

    def body(v_ref, o_ref, buf, send_sems, recv_sems):
        x, y, c = lax.axis_index("x"), lax.axis_index("y"), lax.axis_index("c")
        me = 4 * x + 2 * y + c
        buf[me] = v_ref[...]
        sends = []
        for k in range(1, n_dev):
            px, py, pc = x ^ ((k >> 2) & 1), y ^ ((k >> 1) & 1), c ^ (k & 1)
            cp = pltpu.make_async_remote_copy(
                src_ref=v_ref, dst_ref=buf.at[me], send_sem=send_sems.at[k - 1], recv_sem=recv_sems.at[k - 1],
                device_id=(px, py, pc), device_id_type=MESH)
            cp.start()
            sends.append((cp, 4 * px + 2 * py + pc))
        for k in range(1, n_dev):
            cp, pidx = sends[k - 1]
            pltpu.make_async_remote_copy(
                src_ref=v_ref, dst_ref=buf.at[pidx], send_sem=send_sems.at[k - 1], recv_sem=recv_sems.at[k - 1],
                device_id=(x, y, c), device_id_type=MESH).wait_recv()
        for cp, _ in sends:
            cp.wait_send()
        acc = buf[0]
        for d in range(1, n_dev):
            acc = acc + buf[d]
        o_ref[...] = acc

    vm = pl.BlockSpec(memory_space=pltpu.VMEM)
    return pl.pallas_call(
        body, name="all_reduce_small",
        in_specs=[vm], out_specs=vm,
        out_shape=jax.ShapeDtypeStruct((R, C), F32),
        scratch_shapes=[pltpu.VMEM((n_dev, R, C), F32), pltpu.SemaphoreType.DMA((n_dev - 1,)),
                        pltpu.SemaphoreType.DMA((n_dev - 1,))],
    )(v)


def _pair_sum(place, own, sib, name):
    S, R, C = own.shape
    h = R // 2
    tr = _row_tile(h, C, own.dtype.itemsize)
    nt = h // tr

    def body(place_ref, o_ref, s_ref, out_ref):
        out_ref[...] = (o_ref[...].astype(F32) + s_ref[...].astype(F32)).astype(BF16)

    return pl.pallas_call(
        body, name=name,
        grid_spec=pltpu.PrefetchScalarGridSpec(
            num_scalar_prefetch=1, grid=(3, nt),
            in_specs=[pl.BlockSpec((None, tr, C), lambda k, i, p: (p[2 + k], p[0] * nt + i, 0)),
                      pl.BlockSpec((None, tr, C), lambda k, i, p: (p[2 + k], i, 0))],
            out_specs=pl.BlockSpec((None, tr, C), lambda k, i, p: (k, i, 0))),
        out_shape=jax.ShapeDtypeStruct((3, h, C), BF16),
        compiler_params=_cparams("parallel", "parallel"),
    )(place, own, sib)


def _sum_slabs(place, own, sib, recv, name):
    S, R, C = own.shape
    h = R // 2
    tr = _row_tile(h, C)
    nt = h // tr

    def body(place_ref, o_ref, s_ref, r_ref, out_ref):
        acc = o_ref[...].astype(F32) + s_ref[...].astype(F32)
        for k in range(3):
            acc = acc + r_ref[k].astype(F32)
        out_ref[...] = acc

    return pl.pallas_call(
        body, name=name,
        grid_spec=pltpu.PrefetchScalarGridSpec(
            num_scalar_prefetch=1, grid=(nt,),
            in_specs=[pl.BlockSpec((None, tr, C), lambda i, p: (p[1], p[0] * nt + i, 0)),
                      pl.BlockSpec((None, tr, C), lambda i, p: (p[1], i, 0)),
                      pl.BlockSpec((3, tr, C), lambda i, p: (0, i, 0))],
            out_specs=pl.BlockSpec((tr, C), lambda i, p: (i, 0))),
        out_shape=jax.ShapeDtypeStruct((h, C), F32),
        compiler_params=_cparams("parallel"),
    )(place, own, sib, recv)


def _adamw(parts, w, m, v, name):
    R, C = w.shape
    tr = _row_tile(R, C)
    npart = len(parts)
    c1 = 1.0 - ADAM_B1 ** ADAM_STEP
    c2 = 1.0 - ADAM_B2 ** ADAM_STEP

    def body(*refs):
        p_refs = refs[:npart]
        w_ref, m_ref, v_ref, g_ref, d_ref, nm_ref, nv_ref = refs[npart:]
        g = p_refs[0][...]
        for p_ref in p_refs[1:]:
            g = g + p_ref[...]
        nm = ADAM_B1 * m_ref[...] + (1.0 - ADAM_B1) * g
        nv = ADAM_B2 * v_ref[...] + (1.0 - ADAM_B2) * (g * g)
        m_hat = nm / c1
        v_hat = nv / c2
        g_ref[...] = g
        d_ref[...] = -ADAM_LR * (m_hat / (jnp.sqrt(v_hat) + ADAM_EPS) + ADAM_WD * w_ref[...])
        nm_ref[...] = nm
        nv_ref[...] = nv

    blk = pl.BlockSpec((tr, C), lambda i: (i, 0))
    shape = jax.ShapeDtypeStruct((R, C), F32)
    return pl.pallas_call(
        body, name=name, grid=(R // tr,),
        in_specs=[blk] * (npart + 3), out_specs=[blk] * 4, out_shape=[shape] * 4,
        compiler_params=_cparams("parallel"),
    )(*parts, w, m, v)


def _adamw_halves(place, mine, other, w, m, v, name):
    R, C = w.shape
    h = R // 2
    tr = _row_tile(h, C)
    nt = h // tr
    c1 = 1.0 - ADAM_B1 ** ADAM_STEP
    c2 = 1.0 - ADAM_B2 ** ADAM_STEP

    def body(place_ref, mine_ref, other_ref, w_ref, m_ref, v_ref, g_ref, d_ref, nm_ref, nv_ref):
        is_mine = (pl.program_id(0) // nt) == place_ref[0]
        g = jnp.where(is_mine, mine_ref[...], other_ref[...])
        nm = ADAM_B1 * m_ref[...] + (1.0 - ADAM_B1) * g
        nv = ADAM_B2 * v_ref[...] + (1.0 - ADAM_B2) * (g * g)
        m_hat = nm / c1
        v_hat = nv / c2
        g_ref[...] = g
        d_ref[...] = -ADAM_LR * (m_hat / (jnp.sqrt(v_hat) + ADAM_EPS) + ADAM_WD * w_ref[...])
        nm_ref[...] = nm
        nv_ref[...] = nv

    def half_block(which):
        def index(i, p):
            first = p[0] if which == 0 else 1 - p[0]
            return jnp.clip(i - first * nt, 0, nt - 1), 0

        return pl.BlockSpec((tr, C), index)

    blk = pl.BlockSpec((tr, C), lambda i, p: (i, 0))
    shape = jax.ShapeDtypeStruct((R, C), F32)
    return pl.pallas_call(
        body, name=name,
        grid_spec=pltpu.PrefetchScalarGridSpec(
            num_scalar_prefetch=1, grid=(2 * nt,),
            in_specs=[half_block(0), half_block(1), blk, blk, blk], out_specs=[blk] * 4),
        out_shape=[shape] * 4,
        compiler_params=_cparams("parallel"),
    )(place, mine, other, w, m, v)


SMALL_NAMES = ("ffn1_norm", "mix_norm", "ffn2_norm", "pool_scale", "q_norm", "k_norm", "b_forget")
SMALL_COLS = 1024
LOSS_LANE = 512


def _pack_small(vals):
    rows = [vals[n].reshape(-1, SMALL_COLS) for n in ("ffn1_norm", "mix_norm", "ffn2_norm", "pool_scale")]
    tail = jnp.concatenate([vals["q_norm"].reshape(-1), vals["k_norm"].reshape(-1), vals["b_forget"].reshape(-1)])
    rows.append(jnp.pad(tail, (0, SMALL_COLS - tail.shape[0])).reshape(1, SMALL_COLS))
    return jnp.concatenate(rows, axis=0)


def _unpack_small(packed, like):
    out = {}
    r = 0
    for n in ("ffn1_norm", "mix_norm", "ffn2_norm", "pool_scale"):
        k = like[n].size // SMALL_COLS
        out[n] = packed[r:r + k].reshape(like[n].shape)
        r += k
    o = 0
    for n in ("q_norm", "k_norm", "b_forget"):
        k = like[n].size
        out[n] = packed[r, o:o + k].reshape(like[n].shape)
        o += k
    return out


def kernel(x, meta_tokens, ffn1_norm, ffn1_w_gate, ffn1_w_up, ffn1_w_down, mix_norm, w_in, b_forget, q_norm, k_norm, pool_w, pool_scale, w_out, ffn2_norm, ffn2_w_gate, ffn2_w_up, ffn2_w_down, loss_target, m_meta_tokens, m_ffn1_norm, m_ffn1_w_gate, m_ffn1_w_up, m_ffn1_w_down, m_mix_norm, m_w_in, m_b_forget, m_q_norm, m_k_norm, m_pool_w, m_pool_scale, m_w_out, m_ffn2_norm, m_ffn2_w_gate, m_ffn2_w_up, m_ffn2_w_down, v_meta_tokens, v_ffn1_norm, v_ffn1_w_gate, v_ffn1_w_up, v_ffn1_w_down, v_mix_norm, v_w_in, v_b_forget, v_q_norm, v_k_norm, v_pool_w, v_pool_scale, v_w_out, v_ffn2_norm, v_ffn2_w_gate, v_ffn2_w_up, v_ffn2_w_down):
    wts = dict(meta_tokens=meta_tokens, ffn1_norm=ffn1_norm, ffn1_w_gate=ffn1_w_gate, ffn1_w_up=ffn1_w_up,
               ffn1_w_down=ffn1_w_down, mix_norm=mix_norm, w_in=w_in, b_forget=b_forget, q_norm=q_norm,
               k_norm=k_norm, pool_w=pool_w, pool_scale=pool_scale, w_out=w_out, ffn2_norm=ffn2_norm,
               ffn2_w_gate=ffn2_w_gate, ffn2_w_up=ffn2_w_up, ffn2_w_down=ffn2_w_down)
    mom = dict(meta_tokens=m_meta_tokens, ffn1_norm=m_ffn1_norm, ffn1_w_gate=m_ffn1_w_gate, ffn1_w_up=m_ffn1_w_up,
               ffn1_w_down=m_ffn1_w_down, mix_norm=m_mix_norm, w_in=m_w_in, b_forget=m_b_forget, q_norm=m_q_norm,
               k_norm=m_k_norm, pool_w=m_pool_w, pool_scale=m_pool_scale, w_out=m_w_out, ffn2_norm=m_ffn2_norm,
               ffn2_w_gate=m_ffn2_w_gate, ffn2_w_up=m_ffn2_w_up, ffn2_w_down=m_ffn2_w_down)
    var = dict(meta_tokens=v_meta_tokens, ffn1_norm=v_ffn1_norm, ffn1_w_gate=v_ffn1_w_gate, ffn1_w_up=v_ffn1_w_up,
               ffn1_w_down=v_ffn1_w_down, mix_norm=v_mix_norm, w_in=v_w_in, b_forget=v_b_forget, q_norm=v_q_norm,
               k_norm=v_k_norm, pool_w=v_pool_w, pool_scale=v_pool_scale, w_out=v_w_out, ffn2_norm=v_ffn2_norm,
               ffn2_w_gate=v_ffn2_w_gate, ffn2_w_up=v_ffn2_w_up, ffn2_w_down=v_ffn2_w_down)
    order = list(wts)
    me = 2 * lax.axis_index("x") + lax.axis_index("y")

    D = x.shape[2]
    d_in_shard = w_in.shape[2]
    d_in = N_CHIPS * d_in_shard
    n_heads = b_forget.shape[1]
    d_in_pad = (d_in - n_heads) + LANES

    stages = dict(meta=("meta_tokens",), ffn1_gate=("ffn1_w_gate",), ffn1_up=("ffn1_w_up",),
                  ffn1_down=("ffn1_w_down",),
                  mix=("w_in", "w_out", "pool_w"),
                  ffn2_up=("ffn2_w_gate", "ffn2_w_up"), ffn2_down=("ffn2_w_down",))
    stage_order = list(stages)
    xi, yi = lax.axis_index("x"), lax.axis_index("y")
    place = jnp.stack([lax.axis_index("c"), me, 2 * (1 - xi) + yi, 2 * xi + 1 - yi, 2 * (1 - xi) + 1 - yi]).astype(
        jnp.int32)
    layouts = dict(ffn1_w_gate="cols", ffn1_w_up="cols", ffn1_w_down="rows", w_in="stack", w_out="rows",
                   pool_w="stack", ffn2_w_gate="cols", ffn2_w_up="cols", ffn2_w_down="rows", meta_tokens="stack")
    shards2d = {n: wts[n].reshape(-1, wts[n].shape[-1]) for n in layouts}

    def place_stage(stage, dep):
        return [_cast_place(place, shards2d[n], dep, layouts[n], F32 if n == "meta_tokens" else BF16, f"place_{n}")
                for n in stages[stage]]

    def start_stage(stage, bufs, after):
        shapes = [shards2d[n].shape for n in stages[stage]]
        lays = [layouts[n] for n in stages[stage]]
        return _gather_start(bufs, after, shapes, lays, f"gather_start_{stage}") + (shapes, lays)

    flight = {stage_order[0]: start_stage(stage_order[0], place_stage(stage_order[0], place), place)}
    placed = {stage_order[1]: place_stage(stage_order[1], flight[stage_order[0]][3])}

    def cols(st):
        return jnp.transpose(st, (1, 0, 2)).reshape(st.shape[1], -1)

    forwarding = {}

    def request(stage, after):
        k = stage_order.index(stage)
        send_sems, recv_sems, bufs, _, shapes, lays = flight.pop(stage)
        afters = ([] if after is None else [after]) + [b for st in placed for b in placed[st]]
        afters += [flight[st][3] for st in flight]
        if k == 1:
            afters += [mom["w_in"].reshape(shards2d["w_in"].shape), var["w_in"].reshape(shards2d["w_in"].shape)]
        landed, token = _gather_wait(bufs, send_sems, recv_sems, afters, shapes, lays, f"gather_wait_{stage}")
        if k + 1 < len(stage_order) and stage_order[k + 1] not in flight:
            flight[stage_order[k + 1]] = start_stage(stage_order[k + 1], placed.pop(stage_order[k + 1]), token)
            token = flight[stage_order[k + 1]][3]
        if k == 0:
            placed.update({st: place_stage(st, token) for st in stage_order[2:]})
            early = stage_order[2]
            last_cast = placed[stage_order[-1]][-1]
            flight[early] = start_stage(early, placed.pop(early), last_cast)
        forwarding[stage] = _forward_start(landed, token, shapes, lays, f"forward_start_{stage}") + (shapes, lays)
        return forwarding[stage][3]

    def get_weights(stage, after):
        if stage not in forwarding:
            request(stage, after)
        send_sems, recv_sems, bufs, token, shapes, lays = forwarding.pop(stage)
        full = _forward_wait(bufs, send_sems, recv_sems, token if after is None else after, shapes, lays,
                             f"forward_wait_{stage}")
        G = dict(zip(stages[stage], full))
        if stage == "meta":
            return dict(meta=cols(G["meta_tokens"]))
        if stage == "ffn1_gate":
            return dict(wg=G["ffn1_w_gate"])
        if stage == "ffn1_up":
            return dict(wu=G["ffn1_w_up"])
        if stage == "ffn2_up":
            return dict(wg=G["ffn2_w_gate"], wu=G["ffn2_w_up"])
        if stage != "mix":
            return dict(wd=G[stages[stage][0]])
        return dict(
            win=jnp.concatenate([G["w_in"][s] for s in range(N_CHIPS)] + [jnp.zeros((D, d_in_pad - d_in), BF16)],
                                axis=1), wout=G["w_out"],
            pool_w=jnp.transpose(G["pool_w"].reshape((N_CHIPS,) + pool_w.shape[1:]), (1, 0, 2, 3)).reshape(
                N_POOL_GROUPS, pool_w.shape[3], pool_w.shape[3]))

    def split_rows(a):
        return a.reshape(N_CHIPS, -1, a.shape[1])

    def split_win(a):
        return jnp.stack([a[:, s * d_in_shard:(s + 1) * d_in_shard] for s in range(N_CHIPS)])

    def split_pool(a):
        r, c = pool_w.shape[2], pool_w.shape[3]
        return jnp.transpose(a.reshape(N_POOL_GROUPS, N_CHIPS, r, c), (1, 0, 2, 3)).reshape(N_CHIPS, -1, c)

    scatter = {}
    pending = []

    def finish_pending(after):
        name, names, (send_sems, recv_sems, srcs, lands) = pending.pop()
        own, from_sib = _sibling_wait(srcs, lands, send_sems, recv_sems, after, _halves_copies,
                                      f"halves_wait_{name}")
        pair = [_pair_sum(place, o, s, f"pair_sum_{n}") for n, o, s in zip(names, own, from_sib)]
        send_sems, recv_sems, srcs, lands, token = _scatter_start(pair, f"scatter_start_{name}")
        scatter[name] = (names, own, from_sib, send_sems, recv_sems, srcs, lands, token)
        return token

    def put_grads(name, g):
        if name == "mix":
            names = stages["mix"]
            own = [split_win(g["win"]), split_rows(g["wout"]), split_pool(g["pool_w"])]
        else:
            names = (name,)
            own = [split_rows(g) if name.endswith("_down") else g]
        *flying, token = _sibling_start(own, _halves_copies, _halves_land_shape, f"halves_start_{name}")
        if pending:
            token = finish_pending(token)
        pending.append((name, names, flying))
        return token

    small = dict(ffn1_norm=ffn1_norm, mix_norm=mix_norm, ffn2_norm=ffn2_norm, q_norm=q_norm, k_norm=k_norm,
                 b_forget=b_forget, pool_scale=pool_scale)
    loss_part, dh0, gr = _local_step(x[0], loss_target[0], small, get_weights, request, put_grads)

    out_g, out_d, out_m, out_v = {}, {}, {}, {}

    def update(stage, names, flying, after):
        send_sems, recv_sems, srcs, lands = flying
        halves, other_halves = _sibling_wait(srcs, lands, send_sems, recv_sems, after, _whole_copies,
                                             f"swap_wait_{stage}")
        for n, mine, other in zip(names, halves, other_halves):
            shape = wts[n].shape
            res = _adamw_halves(place, mine, other, shards2d[n], mom[n].reshape(shards2d[n].shape),
                                var[n].reshape(shards2d[n].shape), f"adamw_{n}")
            out_g[n], out_d[n], out_m[n], out_v[n] = (a.reshape(shape) for a in res)
        return res[3]

    after = finish_pending(dh0)
    swapping = None
    for stage in scatter:
        names, own, from_sib, send_sems, recv_sems, srcs, lands, _ = scatter[stage]
        received = _scatter_wait(srcs, lands, send_sems, recv_sems, after, f"scatter_wait_{stage}")
        halves = [_sum_slabs(place, o, s, r, f"sum_{n}") for n, o, s, r in zip(names, own, from_sib, received)]
        *flying, after = _sibling_start(halves, _whole_copies, lambda shape: shape, f"swap_start_{stage}")
        if swapping is not None:
            after = update(*swapping, after)
        swapping = (stage, names, flying)
    update(*swapping, after)

    small_g = _pack_small({n: gr[n] for n in SMALL_NAMES})
    n_small = small_g.shape[0]
    small_g = small_g.at[n_small - 1, LOSS_LANE].set(loss_part)
    meta_rows = gr["meta"].reshape(-1, SMALL_COLS)
    total = _all_reduce_small(jnp.concatenate([small_g, meta_rows], axis=0))
    loss = total[n_small - 1, LOSS_LANE]
    res = _adamw([total[:n_small]], _pack_small({n: wts[n] for n in SMALL_NAMES}),
                 _pack_small({n: mom[n] for n in SMALL_NAMES}), _pack_small({n: var[n] for n in SMALL_NAMES}),
                 "adamw_small")
    for dst, packed in zip((out_g, out_d, out_m, out_v), res):
        dst.update(_unpack_small(packed, wts))
    meta_cols = meta_tokens.shape[1]
    meta_g = lax.dynamic_slice_in_dim(total[n_small:].reshape(N_META, D), me * meta_cols, meta_cols, axis=1)
    res = _adamw([meta_g], meta_tokens, m_meta_tokens, v_meta_tokens, "adamw_meta")
    out_g["meta_tokens"], out_d["meta_tokens"], out_m["meta_tokens"], out_v["meta_tokens"] = res

    grad_x = gr["x"].reshape(x.shape)
    return (loss, grad_x, *[out_g[n] for n in order], *[out_d[n] for n in order], *[out_m[n] for n in order],
            *[out_v[n] for n in order])
```

```python
import functools
import math

import jax
import jax.numpy as jnp
from jax import lax
from jax.experimental import pallas as pl
from jax.experimental.pallas import tpu as pltpu

F32 = jnp.float32
BF16 = jnp.bfloat16

N_META = 16
EPS = 1e-6
HEAD_DIM = 128
N_POOL_GROUPS = 4
LANES = 128
SEQ_ALIGN = 128
TQ = 128
CAUSAL_STEP = 256
VMEM_LIMIT = 56 * 1024 * 1024
ELEMWISE_BLOCK_BYTES = 2304 * 1024

ADAM_LR = 0.001
ADAM_B1 = 0.9
ADAM_B2 = 0.999
ADAM_EPS = 1e-08
ADAM_WD = 0.01
ADAM_STEP = 10

NT_DIMS = (((1,), (1,)), ((), ()))
NEG = -1e30
MESH = pl.DeviceIdType.MESH


def _cparams(*sem):
    return pltpu.CompilerParams(dimension_semantics=sem, vmem_limit_bytes=VMEM_LIMIT)


def _sigmoid(a):
    return 1.0 / (1.0 + jnp.exp(-a))


def _row_tile(rows, cols, itemsize=4):
    best = None
    for t in range(16, rows + 1, 16):
        if rows % t == 0 and t * cols * itemsize <= ELEMWISE_BLOCK_BYTES:
            best = t
    return best if best is not None else rows


def _mm_nn(x, w, tn, name):
    M, K = x.shape
    N = w.shape[1]

    def body(x_ref, w_ref, o_ref):
        o_ref[...] = jnp.dot(x_ref[...], w_ref[...], preferred_element_type=F32)

    return pl.pallas_call(
        body, name=name, grid=(N // tn,),
        in_specs=[pl.BlockSpec((M, K), lambda j: (0, 0)), pl.BlockSpec((K, tn), lambda j: (0, j))],
        out_specs=pl.BlockSpec((M, tn), lambda j: (0, j)),
        out_shape=jax.ShapeDtypeStruct((M, N), F32),
        compiler_params=_cparams("parallel"),
    )(x, w)


def _ffn_up(n, wg, wu, dep, tn, name):
    M, K = n.shape
    N = wg.shape[1]

    def body(n_ref, wg_ref, wu_ref, dep_ref, p_ref, q_ref, s_ref, st_ref):
        nv = n_ref[...]
        a = jnp.dot(nv, wg_ref[...], preferred_element_type=F32)
        b = jnp.dot(nv, wu_ref[...], preferred_element_type=F32)
        sig = _sigmoid(a)
        silu = a * sig
        p_ref[...] = (b * (sig * (1.0 + a * (1.0 - sig)))).astype(BF16)
        q_ref[...] = silu.astype(BF16)
        s = silu * b
        s_ref[...] = s.astype(BF16)
        st_ref[...] = s.T.astype(BF16)

    wspec = pl.BlockSpec((K, tn), lambda j: (0, j))
    ospec = pl.BlockSpec((M, tn), lambda j: (0, j))
    return pl.pallas_call(
        body, name=name, grid=(N // tn,),
        in_specs=[pl.BlockSpec((M, K), lambda j: (0, 0)), wspec, wspec, pl.BlockSpec(memory_space=pl.ANY)],
        out_specs=[ospec, ospec, ospec, pl.BlockSpec((tn, M), lambda j: (j, 0))],
        out_shape=[jax.ShapeDtypeStruct((M, N), BF16), jax.ShapeDtypeStruct((M, N), BF16),
                   jax.ShapeDtypeStruct((M, N), BF16), jax.ShapeDtypeStruct((N, M), BF16)],
        compiler_params=_cparams("parallel"),
    )(n, wg, wu, dep)


def _ffn_up_from_gate(n, a, wu, tn, name):
    M, K = n.shape
    N = wu.shape[1]

    def body(n_ref, a_ref, wu_ref, p_ref, q_ref, s_ref, st_ref):
        a = a_ref[...]
        b = jnp.dot(n_ref[...], wu_ref[...], preferred_element_type=F32)
        sig = _sigmoid(a)
        silu = a * sig
        p_ref[...] = (b * (sig * (1.0 + a * (1.0 - sig)))).astype(BF16)
        q_ref[...] = silu.astype(BF16)
        s = silu * b
        s_ref[...] = s.astype(BF16)
        st_ref[...] = s.T.astype(BF16)

    ospec = pl.BlockSpec((M, tn), lambda j: (0, j))
    return pl.pallas_call(
        body, name=name, grid=(N // tn,),
        in_specs=[pl.BlockSpec((M, K), lambda j: (0, 0)), ospec, pl.BlockSpec((K, tn), lambda j: (0, j))],
        out_specs=[ospec, ospec, ospec, pl.BlockSpec((tn, M), lambda j: (j, 0))],
        out_shape=[jax.ShapeDtypeStruct((M, N), BF16), jax.ShapeDtypeStruct((M, N), BF16),
                   jax.ShapeDtypeStruct((M, N), BF16), jax.ShapeDtypeStruct((N, M), BF16)],
        compiler_params=_cparams("parallel"),
    )(n, a, wu)


def _mm_nn_residual(x, w, res, dep, alpha, tn, name):
    M, K = x.shape
    N = w.shape[1]

    def body(x_ref, w_ref, r_ref, dep_ref, o_ref):
        o_ref[...] = r_ref[...] + alpha * jnp.dot(x_ref[...], w_ref[...], preferred_element_type=F32)

    return pl.pallas_call(
        body, name=name, grid=(N // tn,),
        in_specs=[pl.BlockSpec((M, K), lambda j: (0, 0), pipeline_mode=pl.Buffered(1)),
                  pl.BlockSpec((K, tn), lambda j: (0, j)), pl.BlockSpec((M, tn), lambda j: (0, j)),
                  pl.BlockSpec(memory_space=pl.ANY)],
        out_specs=pl.BlockSpec((M, tn), lambda j: (0, j)),
        out_shape=jax.ShapeDtypeStruct((M, N), F32),
        compiler_params=_cparams("parallel"),
    )(x, w, res, dep)


def _ffn_bwd_hidden(dhb, wd, p, q, dep, tn, name):
    M, K = dhb.shape
    N = wd.shape[0]

    def body(dh_ref, w_ref, p_ref, q_ref, dep_ref, da_ref, db_ref):
        ds = 0.5 * lax.dot_general(dh_ref[...], w_ref[...], NT_DIMS, preferred_element_type=F32)
        da_ref[...] = (ds * p_ref[...].astype(F32)).astype(BF16)
        db_ref[...] = (ds * q_ref[...].astype(F32)).astype(BF16)

    ospec = pl.BlockSpec((M, tn), lambda j: (0, j))
    return pl.pallas_call(
        body, name=name, grid=(N // tn,),
        in_specs=[pl.BlockSpec((M, K), lambda j: (0, 0)), pl.BlockSpec((tn, K), lambda j: (j, 0)), ospec, ospec,
                  pl.BlockSpec(memory_space=pl.ANY)],
        out_specs=[ospec, ospec],
        out_shape=[jax.ShapeDtypeStruct((M, N), BF16), jax.ShapeDtypeStruct((M, N), BF16)],
        compiler_params=_cparams("parallel"),
    )(dhb, wd, p, q, dep)


def _mm_nt(x, w, dep, tn, name):
    M, K = x.shape
    N = w.shape[0]

    def body(x_ref, w_ref, dep_ref, o_ref):
        o_ref[...] = lax.dot_general(x_ref[...], w_ref[...], NT_DIMS, preferred_element_type=F32)

    return pl.pallas_call(
        body, name=name, grid=(N // tn,),
        in_specs=[pl.BlockSpec((M, K), lambda j: (0, 0)), pl.BlockSpec((tn, K), lambda j: (j, 0)),
                  pl.BlockSpec(memory_space=pl.ANY)],
        out_specs=pl.BlockSpec((M, tn), lambda j: (0, j)),
        out_shape=jax.ShapeDtypeStruct((M, N), F32),
        compiler_params=_cparams("parallel"),
    )(x, w, dep)


def _mm_nt_sum(xs, ws, dep, tn, tk, name):
    npair = len(xs)
    M, K = xs[0].shape
    N = ws[0].shape[0]
    nk = K // tk

    def body(*refs):
        x_refs = refs[:npair]
        w_refs = refs[npair:2 * npair]
        o_ref = refs[2 * npair + 1]
        acc = refs[2 * npair + 2]
        k = pl.program_id(1)

        @pl.when(k == 0)
        def _():
            acc[...] = jnp.zeros_like(acc)

        for x_ref, w_ref in zip(x_refs, w_refs):
            acc[...] += lax.dot_general(x_ref[...], w_ref[...], NT_DIMS, preferred_element_type=F32)

        @pl.when(k == nk - 1)
        def _():
            o_ref[...] = acc[...]

    return pl.pallas_call(
        body, name=name, grid=(N // tn, nk),
        in_specs=[pl.BlockSpec((M, tk), lambda j, k: (0, k))] * npair
        + [pl.BlockSpec((tn, tk), lambda j, k: (j, k))] * npair + [pl.BlockSpec(memory_space=pl.ANY)],
        out_specs=pl.BlockSpec((M, tn), lambda j, k: (0, j)),
        out_shape=jax.ShapeDtypeStruct((M, N), F32),
        scratch_shapes=[pltpu.VMEM((M, tn), F32)],
        compiler_params=_cparams("parallel", "arbitrary"),
    )(*xs, *ws, dep)


def _mm_tn(xt, dy, dep, alpha, ti, tn, name, stacked):
    Kin, M = xt.shape
    N = dy.shape[1]

    def body(xt_ref, dy_ref, dep_ref, ob_ref):
        r = jnp.dot(xt_ref[...], dy_ref[...], preferred_element_type=F32)
        if alpha != 1.0:
            r = alpha * r
        ob_ref[...] = r.astype(BF16)

    if stacked:
        ospec = pl.BlockSpec((None, ti, tn), lambda i, j: (j, i, 0))
        oshape = (N // tn, Kin, tn)
    else:
        ospec = pl.BlockSpec((ti, tn), lambda i, j: (i, j))
        oshape = (Kin, N)
    return pl.pallas_call(
        body, name=name, grid=(Kin // ti, N // tn),
        in_specs=[pl.BlockSpec((ti, M), lambda i, j: (i, 0)), pl.BlockSpec((M, tn), lambda i, j: (0, j)),
                  pl.BlockSpec(memory_space=pl.ANY)],
        out_specs=ospec,
        out_shape=jax.ShapeDtypeStruct(oshape, BF16),
        compiler_params=_cparams("parallel", "parallel"),
    )(xt, dy, dep)


def _rmsnorm_fwd(h, g, dep, name):
    M, D = h.shape
    tr = LANES

    def body(h_ref, g_ref, dep_ref, n_ref, nt_ref, r_ref):
        hv = h_ref[...]
        r = lax.rsqrt(jnp.mean(hv * hv, axis=-1, keepdims=True) + EPS)
        n = hv * r * g_ref[...]
        n_ref[...] = n.astype(BF16)
        nt_ref[...] = n.T.astype(BF16)
        r_ref[...] = r

    return pl.pallas_call(
        body, name=name, grid=(M // tr,),
        in_specs=[pl.BlockSpec((tr, D), lambda i: (i, 0)), pl.BlockSpec((1, D), lambda i: (0, 0)),
                  pl.BlockSpec(memory_space=pl.ANY)],
        out_specs=[pl.BlockSpec((tr, D), lambda i: (i, 0)), pl.BlockSpec((D, tr), lambda i: (0, i)),
                   pl.BlockSpec((tr, 1), lambda i: (i, 0))],
        out_shape=[jax.ShapeDtypeStruct((M, D), BF16), jax.ShapeDtypeStruct((D, M), BF16),
                   jax.ShapeDtypeStruct((M, 1), F32)],
        compiler_params=_cparams("parallel"),
    )(h, g, dep)


def _rmsnorm_bwd(dn, h, r, g, dh_prev, name):
    M, D = h.shape
    tr = _row_tile(M, D)

    def body(dn_ref, h_ref, r_ref, g_ref, dp_ref, dh_ref, dhb_ref, dg_ref):
        i = pl.program_id(0)
        dnv = dn_ref[...]
        hv = h_ref[...]
        rv = r_ref[...]
        w = dnv * g_ref[...]
        c = jnp.mean(w * hv, axis=-1, keepdims=True)
        dh = dp_ref[...] + rv * w - hv * (rv * rv * rv * c)
        dh_ref[...] = dh
        dhb_ref[...] = dh.astype(BF16)
        part = jnp.sum(dnv * (hv * rv), axis=0, keepdims=True)

        @pl.when(i == 0)
        def _():
            dg_ref[...] = part

        @pl.when(i > 0)
        def _():
            dg_ref[...] += part

    row = pl.BlockSpec((tr, D), lambda i: (i, 0))
    vec = pl.BlockSpec((1, D), lambda i: (0, 0))
    return pl.pallas_call(
        body, name=name, grid=(M // tr,),
        in_specs=[row, row, pl.BlockSpec((tr, 1), lambda i: (i, 0)), vec, row],
        out_specs=[row, row, vec],
        out_shape=[jax.ShapeDtypeStruct((M, D), F32), jax.ShapeDtypeStruct((M, D), BF16),
                   jax.ShapeDtypeStruct((1, D), F32)],
        compiler_params=_cparams("arbitrary"),
    )(dn, h, r, g, dh_prev)


def _loss_grad(h, tgt, seq, name):
    M, D = h.shape
    tr = _row_tile(M, D)

    def body(h_ref, t_ref, dh_ref, dhb_ref, loss_ref):
        i = pl.program_id(0)
        row = i * tr + lax.broadcasted_iota(jnp.int32, (tr, 1), 0)
        valid = (row >= N_META) & (row < N_META + seq)
        d = jnp.where(valid, h_ref[...] - t_ref[...], 0.0)
        dh = d * (1.0 / D)
        dh_ref[...] = dh
        dhb_ref[...] = dh.astype(BF16)
        part = (0.5 / D) * jnp.sum(jnp.sum(d * d, axis=1, keepdims=True), axis=0, keepdims=True)

        @pl.when(i == 0)
        def _():
            loss_ref[...] = part

        @pl.when(i > 0)
        def _():
            loss_ref[...] += part

    row = pl.BlockSpec((tr, D), lambda i: (i, 0))
    return pl.pallas_call(
        body, name=name, grid=(M // tr,),
        in_specs=[row, row],
        out_specs=[row, row, pl.BlockSpec((1, 1), lambda i: (0, 0))],
        out_shape=[jax.ShapeDtypeStruct((M, D), F32), jax.ShapeDtypeStruct((M, D), BF16),
                   jax.ShapeDtypeStruct((1, 1), F32)],
        compiler_params=_cparams("arbitrary"),
    )(h, tgt)


def _group_window(g):
    return jnp.where(g == 0, 2, jnp.where(g == 1, 4, jnp.where(g == 2, 8, 16)))


def _pool_fwd(z, pw, psc, name):
    M = z.shape[0]
    C = pw.shape[1]

    def body(p_ref, w_ref, sc_ref, pooled_ref, out_ref, outt_ref):
        g = pl.program_id(0)
        p = p_ref[...]
        t = lax.broadcasted_iota(jnp.int32, (M, 1), 0)
        s = p
        wsum = jnp.zeros_like(p)
        for step in range(N_POOL_GROUPS):
            sh = 1 << step
            s = s + jnp.where(t >= sh, pltpu.roll(s, sh, 0), 0.0)
            wsum = jnp.where(g == step, s, wsum)
        cnt = jnp.minimum(t + 1, _group_window(g)).astype(F32)
        pb = (wsum / cnt - p).astype(BF16)
        pooled_ref[...] = pb
        out = jnp.dot(pb, w_ref[...], preferred_element_type=F32) * sc_ref[...]
        out_ref[...] = out.astype(BF16)
        outt_ref[...] = out.T.astype(BF16)

    col = pl.BlockSpec((M, C), lambda g: (0, g))
    return pl.pallas_call(
        body, name=name, grid=(N_POOL_GROUPS,),
        in_specs=[col, pl.BlockSpec((None, C, C), lambda g: (g, 0, 0)), pl.BlockSpec((1, C), lambda g: (0, g))],
        out_specs=[col, col, pl.BlockSpec((C, M), lambda g: (g, 0))],
        out_shape=[jax.ShapeDtypeStruct((M, N_POOL_GROUPS * C), BF16),
                   jax.ShapeDtypeStruct((M, N_POOL_GROUPS * C), BF16),
                   jax.ShapeDtypeStruct((N_POOL_GROUPS * C, M), BF16)],
        compiler_params=_cparams("parallel"),
    )(z, pw, psc)


def _pool_bwd(dmix, pooled, pw, psc, name):
    M = dmix.shape[0]
    C = pw.shape[1]

    def body(dm_ref, pooled_ref, w_ref, sc_ref, dp_ref, dwb_ref, dsc_ref):
        g = pl.program_id(0)
        dmx = dm_ref[...]
        pb = pooled_ref[...]
        wv = w_ref[...]
        mixed = jnp.dot(pb, wv, preferred_element_type=F32)
        dsc_ref[...] = jnp.sum(dmx * mixed, axis=0, keepdims=True)
        dmixed = (dmx * sc_ref[...]).astype(BF16)
        dw = jnp.dot(pb.astype(F32).T.astype(BF16), dmixed, preferred_element_type=F32)
        dwb_ref[...] = dw.astype(BF16)
        dpooled = lax.dot_general(dmixed, wv, NT_DIMS, preferred_element_type=F32)
        t = lax.broadcasted_iota(jnp.int32, (M, 1), 0)
        cnt = jnp.minimum(t + 1, _group_window(g)).astype(F32)
        s = dpooled / cnt
        wsum = jnp.zeros_like(s)
        for step in range(N_POOL_GROUPS):
            sh = 1 << step
            s = s + jnp.where(t < M - sh, pltpu.roll(s, M - sh, 0), 0.0)
            wsum = jnp.where(g == step, s, wsum)
        dp_ref[...] = (wsum - dpooled).astype(BF16)

    col = pl.BlockSpec((M, C), lambda g: (0, g))
    wspec = pl.BlockSpec((None, C, C), lambda g: (g, 0, 0))
    vec = pl.BlockSpec((1, C), lambda g: (0, g))
    return pl.pallas_call(
        body, name=name, grid=(N_POOL_GROUPS,),
        in_specs=[col, col, wspec, vec],
        out_specs=[col, wspec, vec],
        out_shape=[jax.ShapeDtypeStruct((M, N_POOL_GROUPS * C), BF16),
                   jax.ShapeDtypeStruct((N_POOL_GROUPS, C, C), BF16),
                   jax.ShapeDtypeStruct((1, N_POOL_GROUPS * C), F32)],
        compiler_params=_cparams("parallel"),
    )(dmix, pooled, pw, psc)


def _qkv_prep(z, gq, gk, n_heads, q_col, name):
    M = z.shape[0]
    H = n_heads
    qb = q_col // HEAD_DIM

    def body(q_ref, k_ref, v_ref, gq_ref, gk_ref, qh_ref, kh_ref, vb_ref):
        def norm(xv, g):
            r = lax.rsqrt(jnp.mean(xv * xv, axis=-1, keepdims=True) + EPS)
            return (xv * r * g).astype(BF16)

        qh_ref[...] = norm(q_ref[...], gq_ref[...])
        kh_ref[...] = norm(k_ref[...], gk_ref[...])
        vb_ref[...] = v_ref[...].astype(BF16)

    vec = pl.BlockSpec((1, HEAD_DIM), lambda h: (0, 0))
    out = pl.BlockSpec((M, HEAD_DIM), lambda h: (0, h))
    oshape = jax.ShapeDtypeStruct((M, H * HEAD_DIM), BF16)
    return pl.pallas_call(
        body, name=name, grid=(H,),
        in_specs=[pl.BlockSpec((M, HEAD_DIM), lambda h: (0, qb + h)),
                  pl.BlockSpec((M, HEAD_DIM), lambda h: (0, qb + H + h)),
                  pl.BlockSpec((M, HEAD_DIM), lambda h: (0, qb + 2 * H + h)), vec, vec],
        out_specs=[out, out, out],
        out_shape=[oshape, oshape, oshape],
        compiler_params=_cparams("parallel"),
    )(z, z, z, gq, gk)


def _forget_fwd(z, bpad, f_block, name):
    M = z.shape[0]

    def body(f_ref, b_ref, cum_ref):
        xx = f_ref[...] + b_ref[...]
        c = jnp.minimum(xx, 0.0) - jnp.log(1.0 + jnp.exp(-jnp.abs(xx)))
        t = lax.broadcasted_iota(jnp.int32, (M, 1), 0)
        sh = 1
        while sh < M:
            c = c + jnp.where(t >= sh, pltpu.roll(c, sh, 0), 0.0)
            sh *= 2
        cum_ref[...] = c.T

    return pl.pallas_call(
        body, name=name, grid=(1,),
        in_specs=[pl.BlockSpec((M, LANES), lambda i: (0, f_block)), pl.BlockSpec((1, LANES), lambda i: (0, 0))],
        out_specs=pl.BlockSpec((LANES, M), lambda i: (0, 0)),
        out_shape=jax.ShapeDtypeStruct((LANES, M), F32),
        compiler_params=_cparams("arbitrary"),
    )(z, bpad)


def _col_to_row(col):
    n = col.shape[0]
    return jnp.transpose(jnp.broadcast_to(col, (n, LANES)))[0:1, :]


def _causal_extents(M):
    edges = list(range(0, M, CAUSAL_STEP)) + [M]
    return list(zip(edges[:-1], edges[1:]))


def _heads_per_step(n_heads):
    return 2 if n_heads % 2 == 0 else 1


def _attn_fwd(qh, kh, vb, cum_c, cum_r, name):
    M = qh.shape[0]
    H = qh.shape[1] // HEAD_DIM
    hp = _heads_per_step(H)
    scale = 1.0 / math.sqrt(HEAD_DIM)

    def body(q_ref, k_ref, v_ref, cq_ref, ck_ref, o_ref, ot_ref, lc_ref, lr_ref):
        i = pl.program_id(1)

        def compute(n):
            row = i * TQ + lax.broadcasted_iota(jnp.int32, (TQ, 1), 0)
            col = lax.broadcasted_iota(jnp.int32, (1, n), 1)
            for hh in range(hp):
                d0, d1 = hh * HEAD_DIM, (hh + 1) * HEAD_DIM
                s = lax.dot_general(q_ref[:, d0:d1], k_ref[0:n, d0:d1], NT_DIMS, preferred_element_type=F32) * scale
                s = s + (cq_ref[hh] - ck_ref[hh, :, 0:n])
                s = jnp.where(row >= col, s, NEG)
                m = jnp.max(s, axis=1, keepdims=True)
                p = jnp.exp(s - m)
                l = jnp.sum(p, axis=1, keepdims=True)
                pn = (p / l).astype(BF16)
                o = jnp.dot(pn, v_ref[0:n, d0:d1], preferred_element_type=F32)
                o_ref[:, d0:d1] = o.astype(BF16)
                ot_ref[d0:d1, :] = o.T.astype(BF16)
                lse = m + jnp.log(l)
                lc_ref[hh] = lse
                lr_ref[hh] = _col_to_row(lse)

        for lo, hi in _causal_extents(M):
            pl.when((i >= lo // TQ) & (i < hi // TQ))(functools.partial(compute, hi))

    full = pl.BlockSpec((M, hp * HEAD_DIM), lambda h, i: (0, h))
    tile = pl.BlockSpec((TQ, hp * HEAD_DIM), lambda h, i: (i, h))
    colv = pl.BlockSpec((hp, TQ, 1), lambda h, i: (h, i, 0))
    rowv_full = pl.BlockSpec((hp, 1, M), lambda h, i: (h, 0, 0))
    rowv = pl.BlockSpec((hp, 1, TQ), lambda h, i: (h, 0, i))
    return pl.pallas_call(
        body, name=name, grid=(H // hp, M // TQ),
        in_specs=[tile, full, full, colv, rowv_full],
        out_specs=[tile, pl.BlockSpec((hp * HEAD_DIM, TQ), lambda h, i: (h, i)), colv, rowv],
        out_shape=[jax.ShapeDtypeStruct((M, H * HEAD_DIM), BF16), jax.ShapeDtypeStruct((H * HEAD_DIM, M), BF16),
                   jax.ShapeDtypeStruct((H, M, 1), F32), jax.ShapeDtypeStruct((H, 1, M), F32)],
        compiler_params=_cparams("parallel", "parallel"),
    )(qh, kh, vb, cum_c, cum_r)


def _attn_bwd_q(qh, kh, vb, dob, cum_c, cum_r, lse_c, name):
    M = qh.shape[0]
    H = qh.shape[1] // HEAD_DIM
    hp = _heads_per_step(H)
    scale = 1.0 / math.sqrt(HEAD_DIM)

    def body(q_ref, k_ref, v_ref, do_ref, cq_ref, ck_ref, l_ref, dq_ref, dr_ref, dcq_ref):
        i = pl.program_id(1)

        def compute(n):
            row = i * TQ + lax.broadcasted_iota(jnp.int32, (TQ, 1), 0)
            col = lax.broadcasted_iota(jnp.int32, (1, n), 1)
            for hh in range(hp):
                d0, d1 = hh * HEAD_DIM, (hh + 1) * HEAD_DIM
                k = k_ref[0:n, d0:d1]
                s = lax.dot_general(q_ref[:, d0:d1], k, NT_DIMS, preferred_element_type=F32) * scale
                s = s + (cq_ref[hh] - ck_ref[hh, :, 0:n])
                p = jnp.exp(jnp.where(row >= col, s, NEG) - l_ref[hh])
                dp = lax.dot_general(do_ref[:, d0:d1], v_ref[0:n, d0:d1], NT_DIMS, preferred_element_type=F32)
                delta = jnp.sum(p * dp, axis=1, keepdims=True)
                ds = p * (dp - delta)
                dq_ref[:, d0:d1] = jnp.dot((ds * scale).astype(BF16), k, preferred_element_type=F32)
                dr_ref[hh] = _col_to_row(delta)
                dcq_ref[hh] = jnp.sum(ds, axis=1, keepdims=True)

        for lo, hi in _causal_extents(M):
            pl.when((i >= lo // TQ) & (i < hi // TQ))(functools.partial(compute, hi))

    full = pl.BlockSpec((M, hp * HEAD_DIM), lambda h, i: (0, h))
    tile = pl.BlockSpec((TQ, hp * HEAD_DIM), lambda h, i: (i, h))
    colv = pl.BlockSpec((hp, TQ, 1), lambda h, i: (h, i, 0))
    rowv_full = pl.BlockSpec((hp, 1, M), lambda h, i: (h, 0, 0))
    rowv = pl.BlockSpec((hp, 1, TQ), lambda h, i: (h, 0, i))
    return pl.pallas_call(
        body, name=name, grid=(H // hp, M // TQ),
        in_specs=[tile, full, full, tile, colv, rowv_full, colv],
        out_specs=[tile, rowv, colv],
        out_shape=[jax.ShapeDtypeStruct((M, H * HEAD_DIM), F32), jax.ShapeDtypeStruct((H, 1, M), F32),
                   jax.ShapeDtypeStruct((H, M, 1), F32)],
        compiler_params=_cparams("parallel", "parallel"),
    )(qh, kh, vb, dob, cum_c, cum_r, lse_c)


def _attn_bwd_kv(qh, kh, vb, dob, cum_c, cum_r, lse_r, delta_r, name):
    M = qh.shape[0]
    H = qh.shape[1] // HEAD_DIM
    hp = _heads_per_step(H)
    scale = 1.0 / math.sqrt(HEAD_DIM)

    def body(k_ref, v_ref, q_ref, do_ref, cq_ref, ck_ref, l_ref, d_ref, dk_ref, dv_ref, dck_ref):
        j = pl.program_id(1)

        def compute(q0):
            krow = j * TQ + lax.broadcasted_iota(jnp.int32, (TQ, 1), 0)
            qcol = q0 + lax.broadcasted_iota(jnp.int32, (1, M - q0), 1)
            for hh in range(hp):
                d0, d1 = hh * HEAD_DIM, (hh + 1) * HEAD_DIM
                q = q_ref[q0:M, d0:d1]
                do = do_ref[q0:M, d0:d1]
                st = lax.dot_general(k_ref[:, d0:d1], q, NT_DIMS, preferred_element_type=F32) * scale
                st = st + (cq_ref[hh, :, q0:M] - ck_ref[hh])
                pt = jnp.exp(jnp.where(qcol >= krow, st, NEG) - l_ref[hh, :, q0:M])
                dpt = lax.dot_general(v_ref[:, d0:d1], do, NT_DIMS, preferred_element_type=F32)
                dst = pt * (dpt - d_ref[hh, :, q0:M])
                dv_ref[:, d0:d1] = jnp.dot(pt.astype(BF16), do, preferred_element_type=F32).astype(BF16)
                dk_ref[:, d0:d1] = jnp.dot((dst * scale).astype(BF16), q, preferred_element_type=F32)
                dck_ref[hh] = -jnp.sum(dst, axis=1, keepdims=True)

        for lo, hi in _causal_extents(M):
            pl.when((j >= lo // TQ) & (j < hi // TQ))(functools.partial(compute, lo))

    full = pl.BlockSpec((M, hp * HEAD_DIM), lambda h, j: (0, h))
    tile = pl.BlockSpec((TQ, hp * HEAD_DIM), lambda h, j: (j, h))
    colv = pl.BlockSpec((hp, TQ, 1), lambda h, j: (h, j, 0))
    rowv_full = pl.BlockSpec((hp, 1, M), lambda h, j: (h, 0, 0))
    return pl.pallas_call(
        body, name=name, grid=(H // hp, M // TQ),
        in_specs=[tile, tile, full, full, rowv_full, colv, rowv_full, rowv_full],
        out_specs=[tile, tile, colv],
        out_shape=[jax.ShapeDtypeStruct((M, H * HEAD_DIM), F32), jax.ShapeDtypeStruct((M, H * HEAD_DIM), BF16),
                   jax.ShapeDtypeStruct((H, M, 1), F32)],
        compiler_params=_cparams("parallel", "parallel"),
    )(kh, vb, qh, dob, cum_r, cum_c, lse_r, delta_r)


def _qk_norm_bwd(dqh, dkh, z, gq, gk, n_heads, q_col, name):
    M = z.shape[0]
    H = n_heads
    qb = q_col // HEAD_DIM

    def body(dqh_ref, dkh_ref, q_ref, k_ref, gq_ref, gk_ref, dq_ref, dk_ref, dgq_ref, dgk_ref):
        h = pl.program_id(0)

        def one(dy, xv, g):
            r = lax.rsqrt(jnp.mean(xv * xv, axis=-1, keepdims=True) + EPS)
            w = dy * g
            c = jnp.mean(w * xv, axis=-1, keepdims=True)
            dx = r * w - xv * (r * r * r * c)
            return dx.astype(BF16), jnp.sum(dy * (xv * r), axis=0, keepdims=True)

        dq, dgq = one(dqh_ref[...], q_ref[...], gq_ref[...])
        dk, dgk = one(dkh_ref[...], k_ref[...], gk_ref[...])
        dq_ref[...] = dq
        dk_ref[...] = dk

        @pl.when(h == 0)
        def _():
            dgq_ref[...] = dgq
            dgk_ref[...] = dgk

        @pl.when(h > 0)
        def _():
            dgq_ref[...] += dgq
            dgk_ref[...] += dgk

    vec = pl.BlockSpec((1, HEAD_DIM), lambda h: (0, 0))
    head = pl.BlockSpec((M, HEAD_DIM), lambda h: (0, h))
    return pl.pallas_call(
        body, name=name, grid=(H,),
        in_specs=[head, head, pl.BlockSpec((M, HEAD_DIM), lambda h: (0, qb + h)),
                  pl.BlockSpec((M, HEAD_DIM), lambda h: (0, qb + H + h)), vec, vec],
        out_specs=[head, head, vec, vec],
        out_shape=[jax.ShapeDtypeStruct((M, H * HEAD_DIM), BF16), jax.ShapeDtypeStruct((M, H * HEAD_DIM), BF16),
                   jax.ShapeDtypeStruct((1, HEAD_DIM), F32), jax.ShapeDtypeStruct((1, HEAD_DIM), F32)],
        compiler_params=_cparams("arbitrary"),
    )(dqh, dkh, z, z, gq, gk)


def _forget_bwd(dcq, dck, z, bpad, f_block, name):
    H, M, _ = dcq.shape

    def body(dcq_ref, dck_ref, f_ref, b_ref, dfl_ref, db_ref):
        lane = lax.broadcasted_iota(jnp.int32, (1, LANES), 1)
        d = jnp.zeros((M, LANES), F32)
        for h in range(H):
            d = d + (dcq_ref[h] + dck_ref[h]) * (lane == h).astype(F32)
        t = lax.broadcasted_iota(jnp.int32, (M, 1), 0)
        sh = 1
        while sh < M:
            d = d + jnp.where(t < M - sh, pltpu.roll(d, M - sh, 0), 0.0)
            sh *= 2
        xx = f_ref[...] + b_ref[...]
        dfl = d * (1.0 / (1.0 + jnp.exp(xx)))
        dfl_ref[...] = dfl.astype(BF16)
        db_ref[...] = jnp.sum(dfl, axis=0, keepdims=True)

    colv = pl.BlockSpec((H, M, 1), lambda i: (0, 0, 0))
    return pl.pallas_call(
        body, name=name, grid=(1,),
        in_specs=[colv, colv, pl.BlockSpec((M, LANES), lambda i: (0, f_block)),
                  pl.BlockSpec((1, LANES), lambda i: (0, 0))],
        out_specs=[pl.BlockSpec((M, LANES), lambda i: (0, 0)), pl.BlockSpec((1, LANES), lambda i: (0, 0))],
        out_shape=[jax.ShapeDtypeStruct((M, LANES), BF16), jax.ShapeDtypeStruct((1, LANES), F32)],
        compiler_params=_cparams("arbitrary"),
    )(dcq, dck, z, bpad)


def _ffn_fwd(h, g, dep, get_weights, gate_first, tag):
    n, nt, r = _rmsnorm_fwd(h, g, dep, f"{tag}_norm")
    if gate_first:
        wg = get_weights(f"{tag}_gate", n)["wg"]
        a = _mm_nn(n, wg, 256, f"{tag}_gate")
        wu = get_weights(f"{tag}_up", a)["wu"]
        p, q, s, st = _ffn_up_from_gate(n, a, wu, 256, f"{tag}_up")
    else:
        up = get_weights(f"{tag}_up", n)
        wg, wu = up["wg"], up["wu"]
        p, q, s, st = _ffn_up(n, wg, wu, n, 256, f"{tag}_up")
    wd = get_weights(f"{tag}_down", s)["wd"]
    h_out = _mm_nn_residual(s, wd, h, s, 0.5, 256, f"{tag}_down")
    return h_out, (nt, r, p, q, st, wg, wu, wd)


def _ffn_bwd(dh, dhb, h, g, saved, dep, put_grads, tag):
    nt, r, p, q, st, wg, wu, wd = saved
    n_shards = 4
    da, db = _ffn_bwd_hidden(dhb, wd, p, q, dep, 256, f"{tag}_bwd_hidden")
    dwd = _mm_tn(st, dhb, dep, 0.5, st.shape[0] // n_shards, 1024, f"{tag}_dw_down", stacked=False)
    dep = put_grads(f"{tag}_w_down", dwd)
    dwg = _mm_tn(nt, da, dep, 1.0, 1024, wg.shape[1] // n_shards, f"{tag}_dw_gate", stacked=True)
    dep = put_grads(f"{tag}_w_gate", dwg)
    dwu = _mm_tn(nt, db, dep, 1.0, 1024, wu.shape[1] // n_shards, f"{tag}_dw_up", stacked=True)
    dep = put_grads(f"{tag}_w_up", dwu)
    dn = _mm_nt_sum([da, db], [wg, wu], dep, 512, wg.shape[1] // 4, f"{tag}_dn")
    dh_in, dhb_in, dg = _rmsnorm_bwd(dn, h, r, g, dh, f"{tag}_norm_bwd")
    return dh_in, dhb_in, dg


def _local_step(x, target, S, get_weights, request, put_grads):
    seq, D = x.shape
    L = N_META + seq
    Lp = -(-L // SEQ_ALIGN) * SEQ_ALIGN
    pad = jnp.zeros((Lp - L, D), F32)
    tgt = jnp.concatenate([jnp.zeros((N_META, D), F32), target, pad], axis=0)

    d_pool = S["pool_scale"].shape[1]
    n_heads = S["b_forget"].shape[1]
    d_att = n_heads * HEAD_DIM
    f_col = d_pool + 3 * d_att
    f_block = f_col // LANES
    bpad = jnp.pad(S["b_forget"], ((0, 0), (0, LANES - n_heads)))

    h0 = jnp.concatenate([get_weights("meta", None)["meta"], x, pad], axis=0)
    h1, ffn1 = _ffn_fwd(h0, S["ffn1_norm"], h0, get_weights, True, "ffn1")
    u, ut, r_mix = _rmsnorm_fwd(h1, S["mix_norm"], request("mix_in", h1), "mix_norm")
    Wm = get_weights("mix_in", request("mix_out", u))
    z = _mm_nn(u, Wm["win"], 384, "in_proj")
    pooled, pool_out, pool_out_t = _pool_fwd(z, Wm["pool_w"], S["pool_scale"], "pool_fwd")
    qh, kh, vb = _qkv_prep(z, S["q_norm"], S["k_norm"], n_heads, d_pool, "qkv_prep")
    cum_t = _forget_fwd(z, bpad, f_block, "forget_fwd")[:n_heads]
    cum_c = cum_t.reshape(n_heads, Lp, 1)
    cum_r = cum_t.reshape(n_heads, 1, Lp)
    att, att_t, lse_c, lse_r = _attn_fwd(qh, kh, vb, cum_c, cum_r, "attn_fwd")
    mix = jnp.concatenate([pool_out, att], axis=1)
    mix_t = jnp.concatenate([pool_out_t, att_t], axis=0)
    Wm.update(get_weights("mix_out", att))
    dep = request("ffn2_up", att)
    h2 = _mm_nn_residual(mix, Wm["wout"], h1, dep, 1.0, 512, "out_proj")
    h3, ffn2 = _ffn_fwd(h2, S["ffn2_norm"], h2, get_weights, False, "ffn2")

    dh3, dh3b, loss = _loss_grad(h3, tgt, seq, "loss")
    dh2, dh2b, dg_ffn2 = _ffn_bwd(dh3, dh3b, h2, S["ffn2_norm"], ffn2, loss, put_grads, "ffn2")

    dmix = _mm_nt(dh2b, Wm["wout"], loss, 512, "out_proj_bwd")
    dwout = _mm_tn(mix_t, dh2b, loss, 1.0, 1024, 1024, "dw_out", stacked=False)
    dp, dpw, dpsc = _pool_bwd(dmix, pooled, Wm["pool_w"], S["pool_scale"], "pool_bwd")
    dob = dmix[:, d_pool:].astype(BF16)
    dqh, delta_r, dcq = _attn_bwd_q(qh, kh, vb, dob, cum_c, cum_r, lse_c, "attn_bwd_q")
    dkh, dv, dck = _attn_bwd_kv(qh, kh, vb, dob, cum_c, cum_r, lse_r, delta_r, "attn_bwd_kv")
    dq, dk, dgq, dgk = _qk_norm_bwd(dqh, dkh, z, S["q_norm"], S["k_norm"], n_heads, d_pool, "qk_norm_bwd")
    dfl, dbf = _forget_bwd(dcq, dck, z, bpad, f_block, "forget_bwd")
    dz = jnp.concatenate([dp, dq, dk, dv, dfl], axis=1)
    dwin = _mm_tn(ut, dz, loss, 1.0, 1024, dz.shape[1] // 3, "dw_in", stacked=False)
    dep = put_grads("mix", dict(win=dwin, wout=dwout, pool_w=dpw))
    du = _mm_nt_sum([dz], [Wm["win"]], dep, 512, Wm["win"].shape[1] // 3, "in_proj_bwd")
    dh1, dh1b, dg_mix = _rmsnorm_bwd(du, h1, r_mix, S["mix_norm"], dh2, "mix_norm_bwd")

    dh0, _, dg_ffn1 = _ffn_bwd(dh1, dh1b, h0, S["ffn1_norm"], ffn1, loss, put_grads, "ffn1")

    grads = dict(
        x=dh0[N_META:L], meta=dh0[:N_META],
        ffn1_norm=dg_ffn1, mix_norm=dg_mix, ffn2_norm=dg_ffn2, q_norm=dgq, k_norm=dgk,
        b_forget=dbf[:, :n_heads], pool_scale=dpsc,
    )
    return loss[0, 0], dh0, grads


HBM_SPEC = pl.BlockSpec(memory_space=pltpu.HBM)
N_CHIPS = 4


def _chip_peers():
    x, y, c = lax.axis_index("x"), lax.axis_index("y"), lax.axis_index("c")
    flips = [(1 - x, y), (x, 1 - y), (1 - x, 1 - y)]
    return 2 * x + y, [((px, py, c), 2 * px + py) for px, py in flips]


def _gathered_shape(shape, layout):
    if layout == "rows":
        return (N_CHIPS * shape[0],) + shape[1:]
    if layout == "cols":
        return (shape[0], N_CHIPS * shape[1])
    return (N_CHIPS,) + shape


def _cast_place(place, w, dep, layout, dtype, name):
    R, C = w.shape
    tr = _row_tile(R, C)
    nt = R // tr

    def body(place_ref, w_ref, dep_ref, o_ref):
        o_ref[...] = w_ref[...].astype(dtype)

    if layout == "rows":
        ospec = pl.BlockSpec((tr, C), lambda i, p: (p[1] * nt + i, 0))
    elif layout == "cols":
        ospec = pl.BlockSpec((tr, C), lambda i, p: (i, p[1]))
    else:
        ospec = pl.BlockSpec((None, tr, C), lambda i, p: (p[1], i, 0))
    return pl.pallas_call(
        body, name=name,
        grid_spec=pltpu.PrefetchScalarGridSpec(
            num_scalar_prefetch=1, grid=(nt,),
            in_specs=[pl.BlockSpec((tr, C), lambda i, p: (i, 0)), pl.BlockSpec(memory_space=pl.ANY)],
            out_specs=ospec),
        out_shape=jax.ShapeDtypeStruct(_gathered_shape((R, C), layout), dtype),
        compiler_params=_cparams("parallel"),
    )(place, w, dep)


SEM_SPEC = pl.BlockSpec(memory_space=pltpu.SEMAPHORE)
ANY_SPEC = pl.BlockSpec(memory_space=pl.ANY)
SPLIT_COPY = pltpu.CompilerParams(has_side_effects=pltpu.SideEffectType.DATAFLOW_SIDE_EFFECTING)


def _hbm(a):
    return pltpu.with_memory_space_constraint(a, pltpu.HBM)


def _gather_region(refs, shard_shapes, layouts, a, chip, half):
    rows_a = shard_shapes[a][0]
    h = rows_a // 2
    if layouts[a] == "rows":
        return refs[a].at[pl.ds(chip * rows_a + half * h, h)]
    if layouts[a] == "cols":
        cols_a = shard_shapes[a][1]
        return refs[a].at[pl.ds(half * h, h), pl.ds(chip * cols_a, cols_a)]
    return refs[a].at[chip, pl.ds(half * h, h)]


def _gather_start(bufs, after, shard_shapes, layouts, name):
    n = len(bufs)
    ns = 3 * n

    def body(*refs):
        in_refs = refs[:n]
        send_sems = refs[n + 1:n + 1 + ns]
        recv_sems = refs[n + 1 + ns:n + 1 + 2 * ns]
        token = refs[2 * n + 1 + 2 * ns]
        c = lax.axis_index("c")
        me, peers = _chip_peers()
        for a in range(n):
            mine = _gather_region(in_refs, shard_shapes, layouts, a, me, c)
            for k, (dev, _) in enumerate(peers):
                pltpu.make_async_remote_copy(
                    src_ref=mine, dst_ref=mine, send_sem=send_sems[3 * a + k], recv_sem=recv_sems[3 * a + k],
                    device_id=dev, device_id_type=MESH).start()
        token[...] = jnp.zeros_like(token)

    sem = pltpu.SemaphoreType.DMA(())
    out = pl.pallas_call(
        body, name=name,
        out_shape=(*[sem] * (2 * ns), *[pltpu.HBM(b.shape, b.dtype) for b in bufs],
                   jax.ShapeDtypeStruct((8, LANES), F32)),
        in_specs=[HBM_SPEC] * n + [ANY_SPEC],
        out_specs=(*[SEM_SPEC] * (2 * ns), *[HBM_SPEC] * n, pl.BlockSpec(memory_space=pltpu.VMEM)),
        input_output_aliases={a: 2 * ns + a for a in range(n)},
        compiler_params=SPLIT_COPY,
    )(*[_hbm(b) for b in bufs], after)
    return list(out[:ns]), list(out[ns:2 * ns]), list(out[2 * ns:2 * ns + n]), out[2 * ns + n]


def _gather_wait(bufs, send_sems, recv_sems, afters, shard_shapes, layouts, name):
    n = len(bufs)
    ns = 3 * n
    na = len(afters)

    def body(*refs):
        in_refs = refs[:n]
        send_sems = refs[n:n + ns]
        recv_sems = refs[n + ns:n + 2 * ns]
        token = refs[2 * n + 2 * ns + na]
        token[...] = jnp.zeros_like(token)
        c = lax.axis_index("c")
        me, peers = _chip_peers()
        for a in range(n):
            mine = _gather_region(in_refs, shard_shapes, layouts, a, me, c)
            for k, (dev, pidx) in enumerate(peers):
                landed = _gather_region(in_refs, shard_shapes, layouts, a, pidx, c)
                pltpu.make_async_remote_copy(
                    src_ref=mine, dst_ref=landed, send_sem=send_sems[3 * a + k], recv_sem=recv_sems[3 * a + k],
                    device_id=dev, device_id_type=MESH).wait_recv()
        for a in range(n):
            mine = _gather_region(in_refs, shard_shapes, layouts, a, me, c)
            for k, (dev, _) in enumerate(peers):
                pltpu.make_async_remote_copy(
                    src_ref=mine, dst_ref=mine, send_sem=send_sems[3 * a + k], recv_sem=recv_sems[3 * a + k],
                    device_id=dev, device_id_type=MESH).wait_send()

    out = pl.pallas_call(
        body, name=name,
        out_shape=(*[pltpu.HBM(b.shape, b.dtype) for b in bufs], jax.ShapeDtypeStruct((8, LANES), F32)),
        in_specs=[HBM_SPEC] * n + [SEM_SPEC] * (2 * ns) + [ANY_SPEC] * na,
        out_specs=(*[HBM_SPEC] * n, pl.BlockSpec(memory_space=pltpu.VMEM)),
        input_output_aliases={a: a for a in range(n)},
        compiler_params=SPLIT_COPY,
    )(*bufs, *send_sems, *recv_sems, *afters)
    return list(out[:n]), out[n]


def _forward_start(bufs, after, shard_shapes, layouts, name):
    n = len(bufs)
    ns = 3 * n

    def body(*refs):
        in_refs = refs[:n]
        send_sems = refs[n + 1:n + 1 + ns]
        recv_sems = refs[n + 1 + ns:n + 1 + 2 * ns]
        token = refs[2 * n + 1 + 2 * ns]
        c = lax.axis_index("c")
        sib = (lax.axis_index("x"), lax.axis_index("y"), 1 - c)
        _, peers = _chip_peers()
        for a in range(n):
            for k, (_, pidx) in enumerate(peers):
                landed = _gather_region(in_refs, shard_shapes, layouts, a, pidx, c)
                pltpu.make_async_remote_copy(
                    src_ref=landed, dst_ref=landed, send_sem=send_sems[3 * a + k], recv_sem=recv_sems[3 * a + k],
                    device_id=sib, device_id_type=MESH).start()
        token[...] = jnp.zeros_like(token)

    sem = pltpu.SemaphoreType.DMA(())
    out = pl.pallas_call(
        body, name=name,
        out_shape=(*[sem] * (2 * ns), *[pltpu.HBM(b.shape, b.dtype) for b in bufs],
                   jax.ShapeDtypeStruct((8, LANES), F32)),
        in_specs=[HBM_SPEC] * n + [ANY_SPEC],
        out_specs=(*[SEM_SPEC] * (2 * ns), *[HBM_SPEC] * n, pl.BlockSpec(memory_space=pltpu.VMEM)),
        input_output_aliases={a: 2 * ns + a for a in range(n)},
        compiler_params=SPLIT_COPY,
    )(*[_hbm(b) for b in bufs], after)
    return list(out[:ns]), list(out[ns:2 * ns]), list(out[2 * ns:2 * ns + n]), out[2 * ns + n]


def _forward_wait(bufs, send_sems, recv_sems, after, shard_shapes, layouts, name):
    n = len(bufs)
    ns = 3 * n

    def body(*refs):
        in_refs = refs[:n]
        send_sems = refs[n:n + ns]
        recv_sems = refs[n + ns:n + 2 * ns]
        c = lax.axis_index("c")
        sib = (lax.axis_index("x"), lax.axis_index("y"), 1 - c)
        _, peers = _chip_peers()
        for a in range(n):
            for k, (_, pidx) in enumerate(peers):
                landed = _gather_region(in_refs, shard_shapes, layouts, a, pidx, c)
                other = _gather_region(in_refs, shard_shapes, layouts, a, pidx, 1 - c)
                cp = pltpu.make_async_remote_copy(
                    src_ref=landed, dst_ref=other, send_sem=send_sems[3 * a + k], recv_sem=recv_sems[3 * a + k],
                    device_id=sib, device_id_type=MESH)
                cp.wait_recv()
                cp.wait_send()

    return list(pl.pallas_call(
        body, name=name,
        out_shape=tuple(pltpu.HBM(b.shape, b.dtype) for b in bufs),
        in_specs=[HBM_SPEC] * n + [SEM_SPEC] * (2 * ns) + [ANY_SPEC],
        out_specs=tuple([HBM_SPEC] * n),
        input_output_aliases={a: a for a in range(n)},
        compiler_params=SPLIT_COPY,
    )(*bufs, *send_sems, *recv_sems, after))


def _halves_copies(src_refs, land_refs, send_sems, recv_sems):
    c = lax.axis_index("c")
    sib = (lax.axis_index("x"), lax.axis_index("y"), 1 - c)
    copies = []
    for a, (src, land) in enumerate(zip(src_refs, land_refs)):
        h = src.shape[1] // 2
        copies.append(pltpu.make_async_remote_copy(
            src_ref=src.at[:, pl.ds((1 - c) * h, h)], dst_ref=land, send_sem=send_sems[a], recv_sem=recv_sems[a],
            device_id=sib, device_id_type=MESH))
    return copies


def _whole_copies(src_refs, land_refs, send_sems, recv_sems):
    sib = (lax.axis_index("x"), lax.axis_index("y"), 1 - lax.axis_index("c"))
    return [pltpu.make_async_remote_copy(src_ref=src, dst_ref=land, send_sem=send_sems[a], recv_sem=recv_sems[a],
                                         device_id=sib, device_id_type=MESH)
            for a, (src, land) in enumerate(zip(src_refs, land_refs))]


def _halves_land_shape(shape):
    return (shape[0], shape[1] // 2, shape[2])


def _sibling_start(stacked, copies, land_shape, name):
    n = len(stacked)
    lands = [lax.empty(land_shape(s.shape), s.dtype) for s in stacked]

    def body(*refs):
        for cp in copies(refs[:n], refs[n:2 * n], refs[2 * n:3 * n], refs[3 * n:4 * n]):
            cp.start()
        token = refs[6 * n]
        token[...] = jnp.zeros_like(token)

    sem = pltpu.SemaphoreType.DMA(())
    out = pl.pallas_call(
        body, name=name,
        out_shape=(*[sem] * (2 * n), *[pltpu.HBM(b.shape, b.dtype) for b in stacked],
                   *[pltpu.HBM(b.shape, b.dtype) for b in lands], jax.ShapeDtypeStruct((8, LANES), F32)),
        in_specs=[HBM_SPEC] * (2 * n),
        out_specs=(*[SEM_SPEC] * (2 * n), *[HBM_SPEC] * (2 * n), pl.BlockSpec(memory_space=pltpu.VMEM)),
        input_output_aliases={a: 2 * n + a for a in range(2 * n)},
        compiler_params=SPLIT_COPY,
    )(*[_hbm(b) for b in stacked], *[_hbm(b) for b in lands])
    return list(out[:n]), list(out[n:2 * n]), list(out[2 * n:3 * n]), list(out[3 * n:4 * n]), out[4 * n]


def _sibling_wait(srcs, lands, send_sems, recv_sems, after, copies_of, name):
    n = len(srcs)

    def body(*refs):
        copies = copies_of(refs[:n], refs[n:2 * n], refs[2 * n:3 * n], refs[3 * n:4 * n])
        for cp in copies:
            cp.wait_recv()
        for cp in copies:
            cp.wait_send()

    out = pl.pallas_call(
        body, name=name,
        out_shape=tuple(pltpu.HBM(b.shape, b.dtype) for b in list(srcs) + list(lands)),
        in_specs=[HBM_SPEC] * (2 * n) + [SEM_SPEC] * (2 * n) + [ANY_SPEC],
        out_specs=tuple([HBM_SPEC] * (2 * n)),
        input_output_aliases={a: a for a in range(2 * n)},
        compiler_params=SPLIT_COPY,
    )(*srcs, *lands, *send_sems, *recv_sems, after)
    return list(out[:n]), list(out[n:])


def _scatter_start(stacked, name):
    n = len(stacked)
    ns = 3 * n
    lands = [lax.empty((3,) + s.shape[1:], s.dtype) for s in stacked]

    def body(*refs):
        src_refs = refs[:n]
        land_refs = refs[n:2 * n]
        send_sems = refs[2 * n:2 * n + ns]
        recv_sems = refs[2 * n + ns:2 * n + 2 * ns]
        token = refs[4 * n + 2 * ns]
        _, peers = _chip_peers()
        for a in range(n):
            for k, (dev, pidx) in enumerate(peers):
                pltpu.make_async_remote_copy(
                    src_ref=src_refs[a].at[k], dst_ref=land_refs[a].at[k], send_sem=send_sems[3 * a + k],
                    recv_sem=recv_sems[3 * a + k], device_id=dev, device_id_type=MESH).start()
        token[...] = jnp.zeros_like(token)

    sem = pltpu.SemaphoreType.DMA(())
    out = pl.pallas_call(
        body, name=name,
        out_shape=(*[sem] * (2 * ns), *[pltpu.HBM(b.shape, b.dtype) for b in stacked],
                   *[pltpu.HBM(b.shape, b.dtype) for b in lands], jax.ShapeDtypeStruct((8, LANES), F32)),
        in_specs=[HBM_SPEC] * (2 * n),
        out_specs=(*[SEM_SPEC] * (2 * ns), *[HBM_SPEC] * (2 * n), pl.BlockSpec(memory_space=pltpu.VMEM)),
        input_output_aliases={a: 2 * ns + a for a in range(2 * n)},
        compiler_params=SPLIT_COPY,
    )(*[_hbm(b) for b in stacked], *[_hbm(b) for b in lands])
    o = 2 * ns
    return list(out[:ns]), list(out[ns:o]), list(out[o:o + n]), list(out[o + n:o + 2 * n]), out[o + 2 * n]


def _scatter_wait(srcs, lands, send_sems, recv_sems, after, name):
    n = len(srcs)
    ns = 3 * n

    def body(*refs):
        src_refs = refs[:n]
        land_refs = refs[n:2 * n]
        send_sems = refs[2 * n:2 * n + ns]
        recv_sems = refs[2 * n + ns:2 * n + 2 * ns]
        _, peers = _chip_peers()
        copies = [
            pltpu.make_async_remote_copy(
                src_ref=src_refs[a].at[k], dst_ref=land_refs[a].at[k], send_sem=send_sems[3 * a + k],
                recv_sem=recv_sems[3 * a + k], device_id=dev, device_id_type=MESH)
            for a in range(n) for k, (dev, pidx) in enumerate(peers)]
        for cp in copies:
            cp.wait_recv()
        for cp in copies:
            cp.wait_send()

    out = pl.pallas_call(
        body, name=name,
        out_shape=tuple(pltpu.HBM(b.shape, b.dtype) for b in list(srcs) + list(lands)),
        in_specs=[HBM_SPEC] * (2 * n) + [SEM_SPEC] * (2 * ns) + [ANY_SPEC],
        out_specs=tuple([HBM_SPEC] * (2 * n)),
        input_output_aliases={a: a for a in range(2 * n)},
        compiler_params=SPLIT_COPY,
    )(*srcs, *lands, *send_sems, *recv_sems, after)
    return list(out[n:])


def _all_reduce_small(v):
    R, C = v.shape
    n_dev = 8

    def body(v_ref, o_ref, buf, send_sems, recv_sems):
        x, y, c = lax.axis_index("x"), lax.axis_index("y"), lax.axis_index("c")
        me = 4 * x + 2 * y + c
        buf[me] = v_ref[...]
        sends = []
        for k in range(1, n_dev):
            px, py, pc = x ^ ((k >> 2) & 1), y ^ ((k >> 1) & 1), c ^ (k & 1)
            cp = pltpu.make_async_remote_copy(
                src_ref=v_ref, dst_ref=buf.at[me], send_sem=send_sems.at[k - 1], recv_sem=recv_sems.at[k - 1],
                device_id=(px, py, pc), device_id_type=MESH)
            cp.start()
            sends.append((cp, 4 * px + 2 * py + pc))
        for k in range(1, n_dev):
            cp, pidx = sends[k - 1]
            pltpu.make_async_remote_copy(
                src_ref=v_ref, dst_ref=buf.at[pidx], send_sem=send_sems.at[k - 1], recv_sem=recv_sems.at[k - 1],
                device_id=(x, y, c), device_id_type=MESH).wait_recv()
        for cp, _ in sends:
            cp.wait_send()
        acc = buf[0]
        for d in range(1, n_dev):
            acc = acc + buf[d]
        o_ref[...] = acc

    vm = pl.BlockSpec(memory_space=pltpu.VMEM)
    return pl.pallas_call(
        body, name="all_reduce_small",
        in_specs=[vm], out_specs=vm,
        out_shape=jax.ShapeDtypeStruct((R, C), F32),
        scratch_shapes=[pltpu.VMEM((n_dev, R, C), F32), pltpu.SemaphoreType.DMA((n_dev - 1,)),
                        pltpu.SemaphoreType.DMA((n_dev - 1,))],
    )(v)


def _pair_sum(place, own, sib, name):
    S, R, C = own.shape
    h = R // 2
    tr = _row_tile(h, C, own.dtype.itemsize)
    nt = h // tr

    def body(place_ref, o_ref, s_ref, out_ref):
        out_ref[...] = (o_ref[...].astype(F32) + s_ref[...].astype(F32)).astype(BF16)

    return pl.pallas_call(
        body, name=name,
        grid_spec=pltpu.PrefetchScalarGridSpec(
            num_scalar_prefetch=1, grid=(3, nt),
            in_specs=[pl.BlockSpec((None, tr, C), lambda k, i, p: (p[2 + k], p[0] * nt + i, 0)),
                      pl.BlockSpec((None, tr, C), lambda k, i, p: (p[2 + k], i, 0))],
            out_specs=pl.BlockSpec((None, tr, C), lambda k, i, p: (k, i, 0))),
        out_shape=jax.ShapeDtypeStruct((3, h, C), BF16),
        compiler_params=_cparams("parallel", "parallel"),
    )(place, own, sib)


def _sum_slabs(place, own, sib, recv, name):
    S, R, C = own.shape
    h = R // 2
    tr = _row_tile(h, C)
    nt = h // tr

    def body(place_ref, o_ref, s_ref, r_ref, out_ref):
        acc = o_ref[...].astype(F32) + s_ref[...].astype(F32)
        for k in range(3):
            acc = acc + r_ref[k].astype(F32)
        out_ref[...] = acc

    return pl.pallas_call(
        body, name=name,
        grid_spec=pltpu.PrefetchScalarGridSpec(
            num_scalar_prefetch=1, grid=(nt,),
            in_specs=[pl.BlockSpec((None, tr, C), lambda i, p: (p[1], p[0] * nt + i, 0)),
                      pl.BlockSpec((None, tr, C), lambda i, p: (p[1], i, 0)),
                      pl.BlockSpec((3, tr, C), lambda i, p: (0, i, 0))],
            out_specs=pl.BlockSpec((tr, C), lambda i, p: (i, 0))),
        out_shape=jax.ShapeDtypeStruct((h, C), F32),
        compiler_params=_cparams("parallel"),
    )(place, own, sib, recv)


def _adamw(parts, w, m, v, name):
    R, C = w.shape
    tr = _row_tile(R, C)
    npart = len(parts)
    c1 = 1.0 - ADAM_B1 ** ADAM_STEP
    c2 = 1.0 - ADAM_B2 ** ADAM_STEP

    def body(*refs):
        p_refs = refs[:npart]
        w_ref, m_ref, v_ref, g_ref, d_ref, nm_ref, nv_ref = refs[npart:]
        g = p_refs[0][...]
        for p_ref in p_refs[1:]:
            g = g + p_ref[...]
        nm = ADAM_B1 * m_ref[...] + (1.0 - ADAM_B1) * g
        nv = ADAM_B2 * v_ref[...] + (1.0 - ADAM_B2) * (g * g)
        m_hat = nm / c1
        v_hat = nv / c2
        g_ref[...] = g
        d_ref[...] = -ADAM_LR * (m_hat / (jnp.sqrt(v_hat) + ADAM_EPS) + ADAM_WD * w_ref[...])
        nm_ref[...] = nm
        nv_ref[...] = nv

    blk = pl.BlockSpec((tr, C), lambda i: (i, 0))
    shape = jax.ShapeDtypeStruct((R, C), F32)
    return pl.pallas_call(
        body, name=name, grid=(R // tr,),
        in_specs=[blk] * (npart + 3), out_specs=[blk] * 4, out_shape=[shape] * 4,
        compiler_params=_cparams("parallel"),
    )(*parts, w, m, v)


def _adamw_halves(place, mine, other, w, m, v, name):
    R, C = w.shape
    h = R // 2
    tr = _row_tile(h, C)
    nt = h // tr
    c1 = 1.0 - ADAM_B1 ** ADAM_STEP
    c2 = 1.0 - ADAM_B2 ** ADAM_STEP

    def body(place_ref, mine_ref, other_ref, w_ref, m_ref, v_ref, g_ref, d_ref, nm_ref, nv_ref):
        is_mine = (pl.program_id(0) // nt) == place_ref[0]
        g = jnp.where(is_mine, mine_ref[...], other_ref[...])
        nm = ADAM_B1 * m_ref[...] + (1.0 - ADAM_B1) * g
        nv = ADAM_B2 * v_ref[...] + (1.0 - ADAM_B2) * (g * g)
        m_hat = nm / c1
        v_hat = nv / c2
        g_ref[...] = g
        d_ref[...] = -ADAM_LR * (m_hat / (jnp.sqrt(v_hat) + ADAM_EPS) + ADAM_WD * w_ref[...])
        nm_ref[...] = nm
        nv_ref[...] = nv

    def half_block(which):
        def index(i, p):
            first = p[0] if which == 0 else 1 - p[0]
            return jnp.clip(i - first * nt, 0, nt - 1), 0

        return pl.BlockSpec((tr, C), index)

    blk = pl.BlockSpec((tr, C), lambda i, p: (i, 0))
    shape = jax.ShapeDtypeStruct((R, C), F32)
    return pl.pallas_call(
        body, name=name,
        grid_spec=pltpu.PrefetchScalarGridSpec(
            num_scalar_prefetch=1, grid=(2 * nt,),
            in_specs=[half_block(0), half_block(1), blk, blk, blk], out_specs=[blk] * 4),
        out_shape=[shape] * 4,
        compiler_params=_cparams("parallel"),
    )(place, mine, other, w, m, v)


SMALL_NAMES = ("ffn1_norm", "mix_norm", "ffn2_norm", "pool_scale", "q_norm", "k_norm", "b_forget")
SMALL_COLS = 1024
LOSS_LANE = 512


def _pack_small(vals):
    rows = [vals[n].reshape(-1, SMALL_COLS) for n in ("ffn1_norm", "mix_norm", "ffn2_norm", "pool_scale")]
    tail = jnp.concatenate([vals["q_norm"].reshape(-1), vals["k_norm"].reshape(-1), vals["b_forget"].reshape(-1)])
    rows.append(jnp.pad(tail, (0, SMALL_COLS - tail.shape[0])).reshape(1, SMALL_COLS))
    return jnp.concatenate(rows, axis=0)


def _unpack_small(packed, like):
    out = {}
    r = 0
    for n in ("ffn1_norm", "mix_norm", "ffn2_norm", "pool_scale"):
        k = like[n].size // SMALL_COLS
        out[n] = packed[r:r + k].reshape(like[n].shape)
        r += k
    o = 0
    for n in ("q_norm", "k_norm", "b_forget"):
        k = like[n].size
        out[n] = packed[r, o:o + k].reshape(like[n].shape)
        o += k
    return out


def kernel(x, meta_tokens, ffn1_norm, ffn1_w_gate, ffn1_w_up, ffn1_w_down, mix_norm, w_in, b_forget, q_norm, k_norm, pool_w, pool_scale, w_out, ffn2_norm, ffn2_w_gate, ffn2_w_up, ffn2_w_down, loss_target, m_meta_tokens, m_ffn1_norm, m_ffn1_w_gate, m_ffn1_w_up, m_ffn1_w_down, m_mix_norm, m_w_in, m_b_forget, m_q_norm, m_k_norm, m_pool_w, m_pool_scale, m_w_out, m_ffn2_norm, m_ffn2_w_gate, m_ffn2_w_up, m_ffn2_w_down, v_meta_tokens, v_ffn1_norm, v_ffn1_w_gate, v_ffn1_w_up, v_ffn1_w_down, v_mix_norm, v_w_in, v_b_forget, v_q_norm, v_k_norm, v_pool_w, v_pool_scale, v_w_out, v_ffn2_norm, v_ffn2_w_gate, v_ffn2_w_up, v_ffn2_w_down):
    wts = dict(meta_tokens=meta_tokens, ffn1_norm=ffn1_norm, ffn1_w_gate=ffn1_w_gate, ffn1_w_up=ffn1_w_up,
               ffn1_w_down=ffn1_w_down, mix_norm=mix_norm, w_in=w_in, b_forget=b_forget, q_norm=q_norm,
               k_norm=k_norm, pool_w=pool_w, pool_scale=pool_scale, w_out=w_out, ffn2_norm=ffn2_norm,
               ffn2_w_gate=ffn2_w_gate, ffn2_w_up=ffn2_w_up, ffn2_w_down=ffn2_w_down)
    mom = dict(meta_tokens=m_meta_tokens, ffn1_norm=m_ffn1_norm, ffn1_w_gate=m_ffn1_w_gate, ffn1_w_up=m_ffn1_w_up,
               ffn1_w_down=m_ffn1_w_down, mix_norm=m_mix_norm, w_in=m_w_in, b_forget=m_b_forget, q_norm=m_q_norm,
               k_norm=m_k_norm, pool_w=m_pool_w, pool_scale=m_pool_scale, w_out=m_w_out, ffn2_norm=m_ffn2_norm,
               ffn2_w_gate=m_ffn2_w_gate, ffn2_w_up=m_ffn2_w_up, ffn2_w_down=m_ffn2_w_down)
    var = dict(meta_tokens=v_meta_tokens, ffn1_norm=v_ffn1_norm, ffn1_w_gate=v_ffn1_w_gate, ffn1_w_up=v_ffn1_w_up,
               ffn1_w_down=v_ffn1_w_down, mix_norm=v_mix_norm, w_in=v_w_in, b_forget=v_b_forget, q_norm=v_q_norm,
               k_norm=v_k_norm, pool_w=v_pool_w, pool_scale=v_pool_scale, w_out=v_w_out, ffn2_norm=v_ffn2_norm,
               ffn2_w_gate=v_ffn2_w_gate, ffn2_w_up=v_ffn2_w_up, ffn2_w_down=v_ffn2_w_down)
    order = list(wts)
    me = 2 * lax.axis_index("x") + lax.axis_index("y")

    D = x.shape[2]
    d_in_shard = w_in.shape[2]
    d_in = N_CHIPS * d_in_shard
    n_heads = b_forget.shape[1]
    d_in_pad = (d_in - n_heads) + LANES

    stages = dict(meta=("meta_tokens",), ffn1_gate=("ffn1_w_gate",), ffn1_up=("ffn1_w_up",),
                  ffn1_down=("ffn1_w_down",),
                  mix_in=("w_in", "pool_w"), mix_out=("w_out",),
                  ffn2_up=("ffn2_w_gate", "ffn2_w_up"), ffn2_down=("ffn2_w_down",))
    stage_order = list(stages)
    xi, yi = lax.axis_index("x"), lax.axis_index("y")
    place = jnp.stack([lax.axis_index("c"), me, 2 * (1 - xi) + yi, 2 * xi + 1 - yi, 2 * (1 - xi) + 1 - yi]).astype(
        jnp.int32)
    layouts = dict(ffn1_w_gate="cols", ffn1_w_up="cols", ffn1_w_down="rows", w_in="stack", w_out="rows",
                   pool_w="stack", ffn2_w_gate="cols", ffn2_w_up="cols", ffn2_w_down="rows", meta_tokens="stack")
    shards2d = {n: wts[n].reshape(-1, wts[n].shape[-1]) for n in layouts}

    def place_stage(stage, dep):
        return [_cast_place(place, shards2d[n], dep, layouts[n], F32 if n == "meta_tokens" else BF16, f"place_{n}")
                for n in stages[stage]]

    def start_stage(stage, bufs, after):
        shapes = [shards2d[n].shape for n in stages[stage]]
        lays = [layouts[n] for n in stages[stage]]
        return _gather_start(bufs, after, shapes, lays, f"gather_start_{stage}") + (shapes, lays)

    flight = {stage_order[0]: start_stage(stage_order[0], place_stage(stage_order[0], place), place)}
    placed = {stage_order[1]: place_stage(stage_order[1], flight[stage_order[0]][3])}

    def cols(st):
        return jnp.transpose(st, (1, 0, 2)).reshape(st.shape[1], -1)

    forwarding = {}

    def request(stage, after):
        k = stage_order.index(stage)
        send_sems, recv_sems, bufs, _, shapes, lays = flight.pop(stage)
        afters = ([] if after is None else [after]) + [b for st in placed for b in placed[st]]
        afters += [flight[st][3] for st in flight]
        if k == 1:
            afters += [mom["w_in"].reshape(shards2d["w_in"].shape), var["w_in"].reshape(shards2d["w_in"].shape)]
        landed, token = _gather_wait(bufs, send_sems, recv_sems, afters, shapes, lays, f"gather_wait_{stage}")
        if k + 1 < len(stage_order) and stage_order[k + 1] not in flight:
            flight[stage_order[k + 1]] = start_stage(stage_order[k + 1], placed.pop(stage_order[k + 1]), token)
            token = flight[stage_order[k + 1]][3]
        if k == 0:
            placed.update({st: place_stage(st, token) for st in stage_order[2:]})
            early = stage_order[2]
            last_cast = placed[stage_order[-1]][-1]
            flight[early] = start_stage(early, placed.pop(early), last_cast)
        forwarding[stage] = _forward_start(landed, token, shapes, lays, f"forward_start_{stage}") + (shapes, lays)
        return forwarding[stage][3]

    def get_weights(stage, after):
        if stage not in forwarding:
            request(stage, after)
        send_sems, recv_sems, bufs, token, shapes, lays = forwarding.pop(stage)
        full = _forward_wait(bufs, send_sems, recv_sems, token if after is None else after, shapes, lays,
                             f"forward_wait_{stage}")
        G = dict(zip(stages[stage], full))
        if stage == "meta":
            return dict(meta=cols(G["meta_tokens"]))
        if stage == "ffn1_gate":
            return dict(wg=G["ffn1_w_gate"])
        if stage == "ffn1_up":
            return dict(wu=G["ffn1_w_up"])
        if stage == "ffn2_up":
            return dict(wg=G["ffn2_w_gate"], wu=G["ffn2_w_up"])
        if stage == "mix_out":
            return dict(wout=G["w_out"])
        if stage != "mix_in":
            return dict(wd=G[stages[stage][0]])
        return dict(
            win=jnp.concatenate([G["w_in"][s] for s in range(N_CHIPS)] + [jnp.zeros((D, d_in_pad - d_in), BF16)],
                                axis=1),
            pool_w=jnp.transpose(G["pool_w"].reshape((N_CHIPS,) + pool_w.shape[1:]), (1, 0, 2, 3)).reshape(
                N_POOL_GROUPS, pool_w.shape[3], pool_w.shape[3]))

    def split_rows(a):
        return a.reshape(N_CHIPS, -1, a.shape[1])

    def split_win(a):
        return jnp.stack([a[:, s * d_in_shard:(s + 1) * d_in_shard] for s in range(N_CHIPS)])

    def split_pool(a):
        r, c = pool_w.shape[2], pool_w.shape[3]
        return jnp.transpose(a.reshape(N_POOL_GROUPS, N_CHIPS, r, c), (1, 0, 2, 3)).reshape(N_CHIPS, -1, c)

    scatter = {}
    pending = []

    def finish_pending(after):
        name, names, (send_sems, recv_sems, srcs, lands) = pending.pop()
        own, from_sib = _sibling_wait(srcs, lands, send_sems, recv_sems, after, _halves_copies,
                                      f"halves_wait_{name}")
        pair = [_pair_sum(place, o, s, f"pair_sum_{n}") for n, o, s in zip(names, own, from_sib)]
        send_sems, recv_sems, srcs, lands, token = _scatter_start(pair, f"scatter_start_{name}")
        scatter[name] = (names, own, from_sib, send_sems, recv_sems, srcs, lands, token)
        return token

    def put_grads(name, g):
        if name == "mix":
            names = ("w_in", "w_out", "pool_w")
            own = [split_win(g["win"]), split_rows(g["wout"]), split_pool(g["pool_w"])]
        else:
            names = (name,)
            own = [split_rows(g) if name.endswith("_down") else g]
        *flying, token = _sibling_start(own, _halves_copies, _halves_land_shape, f"halves_start_{name}")
        if pending:
            token = finish_pending(token)
        pending.append((name, names, flying))
        return token

    small = dict(ffn1_norm=ffn1_norm, mix_norm=mix_norm, ffn2_norm=ffn2_norm, q_norm=q_norm, k_norm=k_norm,
                 b_forget=b_forget, pool_scale=pool_scale)
    loss_part, dh0, gr = _local_step(x[0], loss_target[0], small, get_weights, request, put_grads)

    out_g, out_d, out_m, out_v = {}, {}, {}, {}

    def update(stage, names, flying, after):
        send_sems, recv_sems, srcs, lands = flying
        halves, other_halves = _sibling_wait(srcs, lands, send_sems, recv_sems, after, _whole_copies,
                                             f"swap_wait_{stage}")
        for n, mine, other in zip(names, halves, other_halves):
            shape = wts[n].shape
            res = _adamw_halves(place, mine, other, shards2d[n], mom[n].reshape(shards2d[n].shape),
                                var[n].reshape(shards2d[n].shape), f"adamw_{n}")
            out_g[n], out_d[n], out_m[n], out_v[n] = (a.reshape(shape) for a in res)
        return res[3]

    after = finish_pending(dh0)
    swapping = None
    for stage in scatter:
        names, own, from_sib, send_sems, recv_sems, srcs, lands, _ = scatter[stage]
        received = _scatter_wait(srcs, lands, send_sems, recv_sems, after, f"scatter_wait_{stage}")
        halves = [_sum_slabs(place, o, s, r, f"sum_{n}") for n, o, s, r in zip(names, own, from_sib, received)]
        *flying, after = _sibling_start(halves, _whole_copies, lambda shape: shape, f"swap_start_{stage}")
        if swapping is not None:
            after = update(*swapping, after)
        swapping = (stage, names, flying)
    update(*swapping, after)

    small_g = _pack_small({n: gr[n] for n in SMALL_NAMES})
    n_small = small_g.shape[0]
    small_g = small_g.at[n_small - 1, LOSS_LANE].set(loss_part)
    meta_rows = gr["meta"].reshape(-1, SMALL_COLS)
    total = _all_reduce_small(jnp.concatenate([small_g, meta_rows], axis=0))
    loss = total[n_small - 1, LOSS_LANE]
    res = _adamw([total[:n_small]], _pack_small({n: wts[n] for n in SMALL_NAMES}),
                 _pack_small({n: mom[n] for n in SMALL_NAMES}), _pack_small({n: var[n] for n in SMALL_NAMES}),
                 "adamw_small")
    for dst, packed in zip((out_g, out_d, out_m, out_v), res):
        dst.update(_unpack_small(packed, wts))
    meta_cols = meta_tokens.shape[1]
    meta_g = lax.dynamic_slice_in_dim(total[n_small:].reshape(N_META, D), me * meta_cols, meta_cols, axis=1)
    res = _adamw([meta_g], meta_tokens, m_meta_tokens, v_meta_tokens, "adamw_meta")
    out_g["meta_tokens"], out_d["meta_tokens"], out_m["meta_tokens"], out_v["meta_tokens"] = res

    grad_x = gr["x"].reshape(x.shape)
    return (loss, grad_x, *[out_g[n] for n in order], *[out_d[n] for n in order], *[out_m[n] for n in order],
            *[out_v[n] for n in order])
```

```python
import functools
import math

import jax
import jax.numpy as jnp
from jax import lax
from jax.experimental import pallas as pl
from jax.experimental.pallas import tpu as pltpu

F32 = jnp.float32
BF16 = jnp.bfloat16

N_META = 16
EPS = 1e-6
HEAD_DIM = 128
N_POOL_GROUPS = 4
LANES = 128
SEQ_ALIGN = 128
TQ = 128
CAUSAL_STEP = 256
VMEM_LIMIT = 56 * 1024 * 1024
ELEMWISE_BLOCK_BYTES = 2304 * 1024

ADAM_LR = 0.001
ADAM_B1 = 0.9
ADAM_B2 = 0.999
ADAM_EPS = 1e-08
ADAM_WD = 0.01
ADAM_STEP = 10

NT_DIMS = (((1,), (1,)), ((), ()))
NEG = -1e30
MESH = pl.DeviceIdType.MESH


def _cparams(*sem):
    return pltpu.CompilerParams(dimension_semantics=sem, vmem_limit_bytes=VMEM_LIMIT)


def _sigmoid(a):
    return 1.0 / (1.0 + jnp.exp(-a))


def _row_tile(rows, cols, itemsize=4):
    best = None
    for t in range(16, rows + 1, 16):
        if rows % t == 0 and t * cols * itemsize <= ELEMWISE_BLOCK_BYTES:
            best = t
    return best if best is not None else rows


def _mm_nn(x, w, tn, name):
    M, K = x.shape
    N = w.shape[1]

    def body(x_ref, w_ref, o_ref):
        o_ref[...] = jnp.dot(x_ref[...], w_ref[...], preferred_element_type=F32)

    return pl.pallas_call(
        body, name=name, grid=(N // tn,),
        in_specs=[pl.BlockSpec((M, K), lambda j: (0, 0)), pl.BlockSpec((K, tn), lambda j: (0, j))],
        out_specs=pl.BlockSpec((M, tn), lambda j: (0, j)),
        out_shape=jax.ShapeDtypeStruct((M, N), F32),
        compiler_params=_cparams("parallel"),
    )(x, w)


def _ffn_up(n, wg, wu, dep, tn, name):
    M, K = n.shape
    N = wg.shape[1]

    def body(n_ref, wg_ref, wu_ref, dep_ref, p_ref, q_ref, s_ref, st_ref):
        nv = n_ref[...]
        a = jnp.dot(nv, wg_ref[...], preferred_element_type=F32)
        b = jnp.dot(nv, wu_ref[...], preferred_element_type=F32)
        sig = _sigmoid(a)
        silu = a * sig
        p_ref[...] = (b * (sig * (1.0 + a * (1.0 - sig)))).astype(BF16)
        q_ref[...] = silu.astype(BF16)
        s = silu * b
        s_ref[...] = s.astype(BF16)
        st_ref[...] = s.T.astype(BF16)

    wspec = pl.BlockSpec((K, tn), lambda j: (0, j))
    ospec = pl.BlockSpec((M, tn), lambda j: (0, j))
    return pl.pallas_call(
        body, name=name, grid=(N // tn,),
        in_specs=[pl.BlockSpec((M, K), lambda j: (0, 0)), wspec, wspec, pl.BlockSpec(memory_space=pl.ANY)],
        out_specs=[ospec, ospec, ospec, pl.BlockSpec((tn, M), lambda j: (j, 0))],
        out_shape=[jax.ShapeDtypeStruct((M, N), BF16), jax.ShapeDtypeStruct((M, N), BF16),
                   jax.ShapeDtypeStruct((M, N), BF16), jax.ShapeDtypeStruct((N, M), BF16)],
        compiler_params=_cparams("parallel"),
    )(n, wg, wu, dep)


def _ffn_up_from_gate(n, a, wu, tn, name):
    M, K = n.shape
    N = wu.shape[1]

    def body(n_ref, a_ref, wu_ref, p_ref, q_ref, s_ref, st_ref):
        a = a_ref[...]
        b = jnp.dot(n_ref[...], wu_ref[...], preferred_element_type=F32)
        sig = _sigmoid(a)
        silu = a * sig
        p_ref[...] = (b * (sig * (1.0 + a * (1.0 - sig)))).astype(BF16)
        q_ref[...] = silu.astype(BF16)
        s = silu * b
        s_ref[...] = s.astype(BF16)
        st_ref[...] = s.T.astype(BF16)

    ospec = pl.BlockSpec((M, tn), lambda j: (0, j))
    return pl.pallas_call(
        body, name=name, grid=(N // tn,),
        in_specs=[pl.BlockSpec((M, K), lambda j: (0, 0)), ospec, pl.BlockSpec((K, tn), lambda j: (0, j))],
        out_specs=[ospec, ospec, ospec, pl.BlockSpec((tn, M), lambda j: (j, 0))],
        out_shape=[jax.ShapeDtypeStruct((M, N), BF16), jax.ShapeDtypeStruct((M, N), BF16),
                   jax.ShapeDtypeStruct((M, N), BF16), jax.ShapeDtypeStruct((N, M), BF16)],
        compiler_params=_cparams("parallel"),
    )(n, a, wu)


def _mm_nn_residual(x, w, res, dep, alpha, tn, name):
    M, K = x.shape
    N = w.shape[1]

    def body(x_ref, w_ref, r_ref, dep_ref, o_ref):
        o_ref[...] = r_ref[...] + alpha * jnp.dot(x_ref[...], w_ref[...], preferred_element_type=F32)

    return pl.pallas_call(
        body, name=name, grid=(N // tn,),
        in_specs=[pl.BlockSpec((M, K), lambda j: (0, 0), pipeline_mode=pl.Buffered(1)),
                  pl.BlockSpec((K, tn), lambda j: (0, j)), pl.BlockSpec((M, tn), lambda j: (0, j)),
                  pl.BlockSpec(memory_space=pl.ANY)],
        out_specs=pl.BlockSpec((M, tn), lambda j: (0, j)),
        out_shape=jax.ShapeDtypeStruct((M, N), F32),
        compiler_params=_cparams("parallel"),
    )(x, w, res, dep)


def _ffn_bwd_hidden(dhb, wd, p, q, dep, tn, name):
    M, K = dhb.shape
    N = wd.shape[0]

    def body(dh_ref, w_ref, p_ref, q_ref, dep_ref, da_ref, db_ref):
        ds = 0.5 * lax.dot_general(dh_ref[...], w_ref[...], NT_DIMS, preferred_element_type=F32)
        da_ref[...] = (ds * p_ref[...].astype(F32)).astype(BF16)
        db_ref[...] = (ds * q_ref[...].astype(F32)).astype(BF16)

    ospec = pl.BlockSpec((M, tn), lambda j: (0, j))
    return pl.pallas_call(
        body, name=name, grid=(N // tn,),
        in_specs=[pl.BlockSpec((M, K), lambda j: (0, 0)), pl.BlockSpec((tn, K), lambda j: (j, 0)), ospec, ospec,
                  pl.BlockSpec(memory_space=pl.ANY)],
        out_specs=[ospec, ospec],
        out_shape=[jax.ShapeDtypeStruct((M, N), BF16), jax.ShapeDtypeStruct((M, N), BF16)],
        compiler_params=_cparams("parallel"),
    )(dhb, wd, p, q, dep)


def _mm_nt(x, w, dep, tn, name):
    M, K = x.shape
    N = w.shape[0]

    def body(x_ref, w_ref, dep_ref, o_ref):
        o_ref[...] = lax.dot_general(x_ref[...], w_ref[...], NT_DIMS, preferred_element_type=F32)

    return pl.pallas_call(
        body, name=name, grid=(N // tn,),
        in_specs=[pl.BlockSpec((M, K), lambda j: (0, 0)), pl.BlockSpec((tn, K), lambda j: (j, 0)),
                  pl.BlockSpec(memory_space=pl.ANY)],
        out_specs=pl.BlockSpec((M, tn), lambda j: (0, j)),
        out_shape=jax.ShapeDtypeStruct((M, N), F32),
        compiler_params=_cparams("parallel"),
    )(x, w, dep)


def _mm_nt_sum(xs, ws, dep, tn, tk, name):
    npair = len(xs)
    M, K = xs[0].shape
    N = ws[0].shape[0]
    nk = K // tk

    def body(*refs):
        x_refs = refs[:npair]
        w_refs = refs[npair:2 * npair]
        o_ref = refs[2 * npair + 1]
        acc = refs[2 * npair + 2]
        k = pl.program_id(1)

        @pl.when(k == 0)
        def _():
            acc[...] = jnp.zeros_like(acc)

        for x_ref, w_ref in zip(x_refs, w_refs):
            acc[...] += lax.dot_general(x_ref[...], w_ref[...], NT_DIMS, preferred_element_type=F32)

        @pl.when(k == nk - 1)
        def _():
            o_ref[...] = acc[...]

    return pl.pallas_call(
        body, name=name, grid=(N // tn, nk),
        in_specs=[pl.BlockSpec((M, tk), lambda j, k: (0, k))] * npair
        + [pl.BlockSpec((tn, tk), lambda j, k: (j, k))] * npair + [pl.BlockSpec(memory_space=pl.ANY)],
        out_specs=pl.BlockSpec((M, tn), lambda j, k: (0, j)),
        out_shape=jax.ShapeDtypeStruct((M, N), F32),
        scratch_shapes=[pltpu.VMEM((M, tn), F32)],
        compiler_params=_cparams("parallel", "arbitrary"),
    )(*xs, *ws, dep)


def _mm_tn(xt, dy, dep, alpha, ti, tn, name, stacked):
    Kin, M = xt.shape
    N = dy.shape[1]

    def body(xt_ref, dy_ref, dep_ref, ob_ref):
        r = jnp.dot(xt_ref[...], dy_ref[...], preferred_element_type=F32)
        if alpha != 1.0:
            r = alpha * r
        ob_ref[...] = r.astype(BF16)

    if stacked:
        ospec = pl.BlockSpec((None, ti, tn), lambda i, j: (j, i, 0))
        oshape = (N // tn, Kin, tn)
    else:
        ospec = pl.BlockSpec((ti, tn), lambda i, j: (i, j))
        oshape = (Kin, N)
    return pl.pallas_call(
        body, name=name, grid=(Kin // ti, N // tn),
        in_specs=[pl.BlockSpec((ti, M), lambda i, j: (i, 0)), pl.BlockSpec((M, tn), lambda i, j: (0, j)),
                  pl.BlockSpec(memory_space=pl.ANY)],
        out_specs=ospec,
        out_shape=jax.ShapeDtypeStruct(oshape, BF16),
        compiler_params=_cparams("parallel", "parallel"),
    )(xt, dy, dep)


def _rmsnorm_fwd(h, g, dep, name):
    M, D = h.shape
    tr = LANES

    def body(h_ref, g_ref, dep_ref, n_ref, nt_ref, r_ref):
        hv = h_ref[...]
        r = lax.rsqrt(jnp.mean(hv * hv, axis=-1, keepdims=True) + EPS)
        n = hv * r * g_ref[...]
        n_ref[...] = n.astype(BF16)
        nt_ref[...] = n.T.astype(BF16)
        r_ref[...] = r

    return pl.pallas_call(
        body, name=name, grid=(M // tr,),
        in_specs=[pl.BlockSpec((tr, D), lambda i: (i, 0)), pl.BlockSpec((1, D), lambda i: (0, 0)),
                  pl.BlockSpec(memory_space=pl.ANY)],
        out_specs=[pl.BlockSpec((tr, D), lambda i: (i, 0)), pl.BlockSpec((D, tr), lambda i: (0, i)),
                   pl.BlockSpec((tr, 1), lambda i: (i, 0))],
        out_shape=[jax.ShapeDtypeStruct((M, D), BF16), jax.ShapeDtypeStruct((D, M), BF16),
                   jax.ShapeDtypeStruct((M, 1), F32)],
        compiler_params=_cparams("parallel"),
    )(h, g, dep)


def _rmsnorm_bwd(dn, h, r, g, dh_prev, name):
    M, D = h.shape
    tr = _row_tile(M, D)

    def body(dn_ref, h_ref, r_ref, g_ref, dp_ref, dh_ref, dhb_ref, dg_ref):
        i = pl.program_id(0)
        dnv = dn_ref[...]
        hv = h_ref[...]
        rv = r_ref[...]
        w = dnv * g_ref[...]
        c = jnp.mean(w * hv, axis=-1, keepdims=True)
        dh = dp_ref[...] + rv * w - hv * (rv * rv * rv * c)
        dh_ref[...] = dh
        dhb_ref[...] = dh.astype(BF16)
        part = jnp.sum(dnv * (hv * rv), axis=0, keepdims=True)

        @pl.when(i == 0)
        def _():
            dg_ref[...] = part

        @pl.when(i > 0)
        def _():
            dg_ref[...] += part

    row = pl.BlockSpec((tr, D), lambda i: (i, 0))
    vec = pl.BlockSpec((1, D), lambda i: (0, 0))
    return pl.pallas_call(
        body, name=name, grid=(M // tr,),
        in_specs=[row, row, pl.BlockSpec((tr, 1), lambda i: (i, 0)), vec, row],
        out_specs=[row, row, vec],
        out_shape=[jax.ShapeDtypeStruct((M, D), F32), jax.ShapeDtypeStruct((M, D), BF16),
                   jax.ShapeDtypeStruct((1, D), F32)],
        compiler_params=_cparams("arbitrary"),
    )(dn, h, r, g, dh_prev)


def _loss_grad(h, tgt, seq, name):
    M, D = h.shape
    tr = _row_tile(M, D)

    def body(h_ref, t_ref, dh_ref, dhb_ref, loss_ref):
        i = pl.program_id(0)
        row = i * tr + lax.broadcasted_iota(jnp.int32, (tr, 1), 0)
        valid = (row >= N_META) & (row < N_META + seq)
        d = jnp.where(valid, h_ref[...] - t_ref[...], 0.0)
        dh = d * (1.0 / D)
        dh_ref[...] = dh
        dhb_ref[...] = dh.astype(BF16)
        part = (0.5 / D) * jnp.sum(jnp.sum(d * d, axis=1, keepdims=True), axis=0, keepdims=True)

        @pl.when(i == 0)
        def _():
            loss_ref[...] = part

        @pl.when(i > 0)
        def _():
            loss_ref[...] += part

    row = pl.BlockSpec((tr, D), lambda i: (i, 0))
    return pl.pallas_call(
        body, name=name, grid=(M // tr,),
        in_specs=[row, row],
        out_specs=[row, row, pl.BlockSpec((1, 1), lambda i: (0, 0))],
        out_shape=[jax.ShapeDtypeStruct((M, D), F32), jax.ShapeDtypeStruct((M, D), BF16),
                   jax.ShapeDtypeStruct((1, 1), F32)],
        compiler_params=_cparams("arbitrary"),
    )(h, tgt)


def _group_window(g):
    return jnp.where(g == 0, 2, jnp.where(g == 1, 4, jnp.where(g == 2, 8, 16)))


def _pool_fwd(z, pw, psc, name):
    M = z.shape[0]
    C = pw.shape[1]

    def body(p_ref, w_ref, sc_ref, pooled_ref, out_ref, outt_ref):
        g = pl.program_id(0)
        p = p_ref[...]
        t = lax.broadcasted_iota(jnp.int32, (M, 1), 0)
        s = p
        wsum = jnp.zeros_like(p)
        for step in range(N_POOL_GROUPS):
            sh = 1 << step
            s = s + jnp.where(t >= sh, pltpu.roll(s, sh, 0), 0.0)
            wsum = jnp.where(g == step, s, wsum)
        cnt = jnp.minimum(t + 1, _group_window(g)).astype(F32)
        pb = (wsum / cnt - p).astype(BF16)
        pooled_ref[...] = pb
        out = jnp.dot(pb, w_ref[...], preferred_element_type=F32) * sc_ref[...]
        out_ref[...] = out.astype(BF16)
        outt_ref[...] = out.T.astype(BF16)

    col = pl.BlockSpec((M, C), lambda g: (0, g))
    return pl.pallas_call(
        body, name=name, grid=(N_POOL_GROUPS,),
        in_specs=[col, pl.BlockSpec((None, C, C), lambda g: (g, 0, 0)), pl.BlockSpec((1, C), lambda g: (0, g))],
        out_specs=[col, col, pl.BlockSpec((C, M), lambda g: (g, 0))],
        out_shape=[jax.ShapeDtypeStruct((M, N_POOL_GROUPS * C), BF16),
                   jax.ShapeDtypeStruct((M, N_POOL_GROUPS * C), BF16),
                   jax.ShapeDtypeStruct((N_POOL_GROUPS * C, M), BF16)],
        compiler_params=_cparams("parallel"),
    )(z, pw, psc)


def _pool_bwd(dmix, pooled, pw, psc, name):
    M = dmix.shape[0]
    C = pw.shape[1]

    def body(dm_ref, pooled_ref, w_ref, sc_ref, dp_ref, dwb_ref, dsc_ref):
        g = pl.program_id(0)
        dmx = dm_ref[...]
        pb = pooled_ref[...]
        wv = w_ref[...]
        mixed = jnp.dot(pb, wv, preferred_element_type=F32)
        dsc_ref[...] = jnp.sum(dmx * mixed, axis=0, keepdims=True)
        dmixed = (dmx * sc_ref[...]).astype(BF16)
        dw = jnp.dot(pb.astype(F32).T.astype(BF16), dmixed, preferred_element_type=F32)
        dwb_ref[...] = dw.astype(BF16)
        dpooled = lax.dot_general(dmixed, wv, NT_DIMS, preferred_element_type=F32)
        t = lax.broadcasted_iota(jnp.int32, (M, 1), 0)
        cnt = jnp.minimum(t + 1, _group_window(g)).astype(F32)
        s = dpooled / cnt
        wsum = jnp.zeros_like(s)
        for step in range(N_POOL_GROUPS):
            sh = 1 << step
            s = s + jnp.where(t < M - sh, pltpu.roll(s, M - sh, 0), 0.0)
            wsum = jnp.where(g == step, s, wsum)
        dp_ref[...] = (wsum - dpooled).astype(BF16)

    col = pl.BlockSpec((M, C), lambda g: (0, g))
    wspec = pl.BlockSpec((None, C, C), lambda g: (g, 0, 0))
    vec = pl.BlockSpec((1, C), lambda g: (0, g))
    return pl.pallas_call(
        body, name=name, grid=(N_POOL_GROUPS,),
        in_specs=[col, col, wspec, vec],
        out_specs=[col, wspec, vec],
        out_shape=[jax.ShapeDtypeStruct((M, N_POOL_GROUPS * C), BF16),
                   jax.ShapeDtypeStruct((N_POOL_GROUPS, C, C), BF16),
                   jax.ShapeDtypeStruct((1, N_POOL_GROUPS * C), F32)],
        compiler_params=_cparams("parallel"),
    )(dmix, pooled, pw, psc)


def _qkv_prep(z, gq, gk, n_heads, q_col, name):
    M = z.shape[0]
    H = n_heads
    qb = q_col // HEAD_DIM

    def body(q_ref, k_ref, v_ref, gq_ref, gk_ref, qh_ref, kh_ref, vb_ref):
        def norm(xv, g):
            r = lax.rsqrt(jnp.mean(xv * xv, axis=-1, keepdims=True) + EPS)
            return (xv * r * g).astype(BF16)

        qh_ref[...] = norm(q_ref[...], gq_ref[...])
        kh_ref[...] = norm(k_ref[...], gk_ref[...])
        vb_ref[...] = v_ref[...].astype(BF16)

    vec = pl.BlockSpec((1, HEAD_DIM), lambda h: (0, 0))
    out = pl.BlockSpec((M, HEAD_DIM), lambda h: (0, h))
    oshape = jax.ShapeDtypeStruct((M, H * HEAD_DIM), BF16)
    return pl.pallas_call(
        body, name=name, grid=(H,),
        in_specs=[pl.BlockSpec((M, HEAD_DIM), lambda h: (0, qb + h)),
                  pl.BlockSpec((M, HEAD_DIM), lambda h: (0, qb + H + h)),
                  pl.BlockSpec((M, HEAD_DIM), lambda h: (0, qb + 2 * H + h)), vec, vec],
        out_specs=[out, out, out],
        out_shape=[oshape, oshape, oshape],
        compiler_params=_cparams("parallel"),
    )(z, z, z, gq, gk)


def _forget_fwd(z, bpad, f_block, name):
    M = z.shape[0]

    def body(f_ref, b_ref, cum_ref):
        xx = f_ref[...] + b_ref[...]
        c = jnp.minimum(xx, 0.0) - jnp.log(1.0 + jnp.exp(-jnp.abs(xx)))
        t = lax.broadcasted_iota(jnp.int32, (M, 1), 0)
        sh = 1
        while sh < M:
            c = c + jnp.where(t >= sh, pltpu.roll(c, sh, 0), 0.0)
            sh *= 2
        cum_ref[...] = c.T

    return pl.pallas_call(
        body, name=name, grid=(1,),
        in_specs=[pl.BlockSpec((M, LANES), lambda i: (0, f_block)), pl.BlockSpec((1, LANES), lambda i: (0, 0))],
        out_specs=pl.BlockSpec((LANES, M), lambda i: (0, 0)),
        out_shape=jax.ShapeDtypeStruct((LANES, M), F32),
        compiler_params=_cparams("arbitrary"),
    )(z, bpad)


def _col_to_row(col):
    n = col.shape[0]
    return jnp.transpose(jnp.broadcast_to(col, (n, LANES)))[0:1, :]


def _causal_extents(M):
    edges = list(range(0, M, CAUSAL_STEP)) + [M]
    return list(zip(edges[:-1], edges[1:]))


def _heads_per_step(n_heads):
    return 2 if n_heads % 2 == 0 else 1


def _attn_fwd(qh, kh, vb, cum_c, cum_r, name):
    M = qh.shape[0]
    H = qh.shape[1] // HEAD_DIM
    hp = _heads_per_step(H)
    scale = 1.0 / math.sqrt(HEAD_DIM)

    def body(q_ref, k_ref, v_ref, cq_ref, ck_ref, o_ref, ot_ref, lc_ref, lr_ref):
        i = pl.program_id(1)

        def compute(n):
            row = i * TQ + lax.broadcasted_iota(jnp.int32, (TQ, 1), 0)
            col = lax.broadcasted_iota(jnp.int32, (1, n), 1)
            for hh in range(hp):
                d0, d1 = hh * HEAD_DIM, (hh + 1) * HEAD_DIM
                s = lax.dot_general(q_ref[:, d0:d1], k_ref[0:n, d0:d1], NT_DIMS, preferred_element_type=F32) * scale
                s = s + (cq_ref[hh] - ck_ref[hh, :, 0:n])
                s = jnp.where(row >= col, s, NEG)
                m = jnp.max(s, axis=1, keepdims=True)
                p = jnp.exp(s - m)
                l = jnp.sum(p, axis=1, keepdims=True)
                pn = (p / l).astype(BF16)
                o = jnp.dot(pn, v_ref[0:n, d0:d1], preferred_element_type=F32)
                o_ref[:, d0:d1] = o.astype(BF16)
                ot_ref[d0:d1, :] = o.T.astype(BF16)
                lse = m + jnp.log(l)
                lc_ref[hh] = lse
                lr_ref[hh] = _col_to_row(lse)

        for lo, hi in _causal_extents(M):
            pl.when((i >= lo // TQ) & (i < hi // TQ))(functools.partial(compute, hi))

    full = pl.BlockSpec((M, hp * HEAD_DIM), lambda h, i: (0, h))
    tile = pl.BlockSpec((TQ, hp * HEAD_DIM), lambda h, i: (i, h))
    colv = pl.BlockSpec((hp, TQ, 1), lambda h, i: (h, i, 0))
    rowv_full = pl.BlockSpec((hp, 1, M), lambda h, i: (h, 0, 0))
    rowv = pl.BlockSpec((hp, 1, TQ), lambda h, i: (h, 0, i))
    return pl.pallas_call(
        body, name=name, grid=(H // hp, M // TQ),
        in_specs=[tile, full, full, colv, rowv_full],
        out_specs=[tile, pl.BlockSpec((hp * HEAD_DIM, TQ), lambda h, i: (h, i)), colv, rowv],
        out_shape=[jax.ShapeDtypeStruct((M, H * HEAD_DIM), BF16), jax.ShapeDtypeStruct((H * HEAD_DIM, M), BF16),
                   jax.ShapeDtypeStruct((H, M, 1), F32), jax.ShapeDtypeStruct((H, 1, M), F32)],
        compiler_params=_cparams("parallel", "parallel"),
    )(qh, kh, vb, cum_c, cum_r)


def _attn_bwd_q(qh, kh, vb, dob, cum_c, cum_r, lse_c, name):
    M = qh.shape[0]
    H = qh.shape[1] // HEAD_DIM
    hp = _heads_per_step(H)
    scale = 1.0 / math.sqrt(HEAD_DIM)

    def body(q_ref, k_ref, v_ref, do_ref, cq_ref, ck_ref, l_ref, dq_ref, dr_ref, dcq_ref):
        i = pl.program_id(1)

        def compute(n):
            row = i * TQ + lax.broadcasted_iota(jnp.int32, (TQ, 1), 0)
            col = lax.broadcasted_iota(jnp.int32, (1, n), 1)
            for hh in range(hp):
                d0, d1 = hh * HEAD_DIM, (hh + 1) * HEAD_DIM
                k = k_ref[0:n, d0:d1]
                s = lax.dot_general(q_ref[:, d0:d1], k, NT_DIMS, preferred_element_type=F32) * scale
                s = s + (cq_ref[hh] - ck_ref[hh, :, 0:n])
                p = jnp.exp(jnp.where(row >= col, s, NEG) - l_ref[hh])
                dp = lax.dot_general(do_ref[:, d0:d1], v_ref[0:n, d0:d1], NT_DIMS, preferred_element_type=F32)
                delta = jnp.sum(p * dp, axis=1, keepdims=True)
                ds = p * (dp - delta)
                dq_ref[:, d0:d1] = jnp.dot((ds * scale).astype(BF16), k, preferred_element_type=F32)
                dr_ref[hh] = _col_to_row(delta)
                dcq_ref[hh] = jnp.sum(ds, axis=1, keepdims=True)

        for lo, hi in _causal_extents(M):
            pl.when((i >= lo // TQ) & (i < hi // TQ))(functools.partial(compute, hi))

    full = pl.BlockSpec((M, hp * HEAD_DIM), lambda h, i: (0, h))
    tile = pl.BlockSpec((TQ, hp * HEAD_DIM), lambda h, i: (i, h))
    colv = pl.BlockSpec((hp, TQ, 1), lambda h, i: (h, i, 0))
    rowv_full = pl.BlockSpec((hp, 1, M), lambda h, i: (h, 0, 0))
    rowv = pl.BlockSpec((hp, 1, TQ), lambda h, i: (h, 0, i))
    return pl.pallas_call(
        body, name=name, grid=(H // hp, M // TQ),
        in_specs=[tile, full, full, tile, colv, rowv_full, colv],
        out_specs=[tile, rowv, colv],
        out_shape=[jax.ShapeDtypeStruct((M, H * HEAD_DIM), F32), jax.ShapeDtypeStruct((H, 1, M), F32),
                   jax.ShapeDtypeStruct((H, M, 1), F32)],
        compiler_params=_cparams("parallel", "parallel"),
    )(qh, kh, vb, dob, cum_c, cum_r, lse_c)


def _attn_bwd_kv(qh, kh, vb, dob, cum_c, cum_r, lse_r, delta_r, name):
    M = qh.shape[0]
    H = qh.shape[1] // HEAD_DIM
    hp = _heads_per_step(H)
    scale = 1.0 / math.sqrt(HEAD_DIM)

    def body(k_ref, v_ref, q_ref, do_ref, cq_ref, ck_ref, l_ref, d_ref, dk_ref, dv_ref, dck_ref):
        j = pl.program_id(1)

        def compute(q0):
            krow = j * TQ + lax.broadcasted_iota(jnp.int32, (TQ, 1), 0)
            qcol = q0 + lax.broadcasted_iota(jnp.int32, (1, M - q0), 1)
            for hh in range(hp):
                d0, d1 = hh * HEAD_DIM, (hh + 1) * HEAD_DIM
                q = q_ref[q0:M, d0:d1]
                do = do_ref[q0:M, d0:d1]
                st = lax.dot_general(k_ref[:, d0:d1], q, NT_DIMS, preferred_element_type=F32) * scale
                st = st + (cq_ref[hh, :, q0:M] - ck_ref[hh])
                pt = jnp.exp(jnp.where(qcol >= krow, st, NEG) - l_ref[hh, :, q0:M])
                dpt = lax.dot_general(v_ref[:, d0:d1], do, NT_DIMS, preferred_element_type=F32)
                dst = pt * (dpt - d_ref[hh, :, q0:M])
                dv_ref[:, d0:d1] = jnp.dot(pt.astype(BF16), do, preferred_element_type=F32).astype(BF16)
                dk_ref[:, d0:d1] = jnp.dot((dst * scale).astype(BF16), q, preferred_element_type=F32)
                dck_ref[hh] = -jnp.sum(dst, axis=1, keepdims=True)

        for lo, hi in _causal_extents(M):
            pl.when((j >= lo // TQ) & (j < hi // TQ))(functools.partial(compute, lo))

    full = pl.BlockSpec((M, hp * HEAD_DIM), lambda h, j: (0, h))
    tile = pl.BlockSpec((TQ, hp * HEAD_DIM), lambda h, j: (j, h))
    colv = pl.BlockSpec((hp, TQ, 1), lambda h, j: (h, j, 0))
    rowv_full = pl.BlockSpec((hp, 1, M), lambda h, j: (h, 0, 0))
    return pl.pallas_call(
        body, name=name, grid=(H // hp, M // TQ),
        in_specs=[tile, tile, full, full, rowv_full, colv, rowv_full, rowv_full],
        out_specs=[tile, tile, colv],
        out_shape=[jax.ShapeDtypeStruct((M, H * HEAD_DIM), F32), jax.ShapeDtypeStruct((M, H * HEAD_DIM), BF16),
                   jax.ShapeDtypeStruct((H, M, 1), F32)],
        compiler_params=_cparams("parallel", "parallel"),
    )(kh, vb, qh, dob, cum_r, cum_c, lse_r, delta_r)


def _qk_norm_bwd(dqh, dkh, z, gq, gk, n_heads, q_col, name):
    M = z.shape[0]
    H = n_heads
    qb = q_col // HEAD_DIM

    def body(dqh_ref, dkh_ref, q_ref, k_ref, gq_ref, gk_ref, dq_ref, dk_ref, dgq_ref, dgk_ref):
        h = pl.program_id(0)

        def one(dy, xv, g):
            r = lax.rsqrt(jnp.mean(xv * xv, axis=-1, keepdims=True) + EPS)
            w = dy * g
            c = jnp.mean(w * xv, axis=-1, keepdims=True)
            dx = r * w - xv * (r * r * r * c)
            return dx.astype(BF16), jnp.sum(dy * (xv * r), axis=0, keepdims=True)

        dq, dgq = one(dqh_ref[...], q_ref[...], gq_ref[...])
        dk, dgk = one(dkh_ref[...], k_ref[...], gk_ref[...])
        dq_ref[...] = dq
        dk_ref[...] = dk

        @pl.when(h == 0)
        def _():
            dgq_ref[...] = dgq
            dgk_ref[...] = dgk

        @pl.when(h > 0)
        def _():
            dgq_ref[...] += dgq
            dgk_ref[...] += dgk

    vec = pl.BlockSpec((1, HEAD_DIM), lambda h: (0, 0))
    head = pl.BlockSpec((M, HEAD_DIM), lambda h: (0, h))
    return pl.pallas_call(
        body, name=name, grid=(H,),
        in_specs=[head, head, pl.BlockSpec((M, HEAD_DIM), lambda h: (0, qb + h)),
                  pl.BlockSpec((M, HEAD_DIM), lambda h: (0, qb + H + h)), vec, vec],
        out_specs=[head, head, vec, vec],
        out_shape=[jax.ShapeDtypeStruct((M, H * HEAD_DIM), BF16), jax.ShapeDtypeStruct((M, H * HEAD_DIM), BF16),
                   jax.ShapeDtypeStruct((1, HEAD_DIM), F32), jax.ShapeDtypeStruct((1, HEAD_DIM), F32)],
        compiler_params=_cparams("arbitrary"),
    )(dqh, dkh, z, z, gq, gk)


def _forget_bwd(dcq, dck, z, bpad, f_block, name):
    H, M, _ = dcq.shape

    def body(dcq_ref, dck_ref, f_ref, b_ref, dfl_ref, db_ref):
        lane = lax.broadcasted_iota(jnp.int32, (1, LANES), 1)
        d = jnp.zeros((M, LANES), F32)
        for h in range(H):
            d = d + (dcq_ref[h] + dck_ref[h]) * (lane == h).astype(F32)
        t = lax.broadcasted_iota(jnp.int32, (M, 1), 0)
        sh = 1
        while sh < M:
            d = d + jnp.where(t < M - sh, pltpu.roll(d, M - sh, 0), 0.0)
            sh *= 2
        xx = f_ref[...] + b_ref[...]
        dfl = d * (1.0 / (1.0 + jnp.exp(xx)))
        dfl_ref[...] = dfl.astype(BF16)
        db_ref[...] = jnp.sum(dfl, axis=0, keepdims=True)

    colv = pl.BlockSpec((H, M, 1), lambda i: (0, 0, 0))
    return pl.pallas_call(
        body, name=name, grid=(1,),
        in_specs=[colv, colv, pl.BlockSpec((M, LANES), lambda i: (0, f_block)),
                  pl.BlockSpec((1, LANES), lambda i: (0, 0))],
        out_specs=[pl.BlockSpec((M, LANES), lambda i: (0, 0)), pl.BlockSpec((1, LANES), lambda i: (0, 0))],
        out_shape=[jax.ShapeDtypeStruct((M, LANES), BF16), jax.ShapeDtypeStruct((1, LANES), F32)],
        compiler_params=_cparams("arbitrary"),
    )(dcq, dck, z, bpad)


def _ffn_fwd(h, g, dep, get_weights, gate_first, tag):
    n, nt, r = _rmsnorm_fwd(h, g, dep, f"{tag}_norm")
    if gate_first:
        wg = get_weights(f"{tag}_gate", n)["wg"]
        a = _mm_nn(n, wg, 256, f"{tag}_gate")
        wu = get_weights(f"{tag}_up", a)["wu"]
        p, q, s, st = _ffn_up_from_gate(n, a, wu, 256, f"{tag}_up")
    else:
        up = get_weights(f"{tag}_up", n)
        wg, wu = up["wg"], up["wu"]
        p, q, s, st = _ffn_up(n, wg, wu, n, 256, f"{tag}_up")
    wd = get_weights(f"{tag}_down", s)["wd"]
    h_out = _mm_nn_residual(s, wd, h, s, 0.5, 256, f"{tag}_down")
    return h_out, (nt, r, p, q, st, wg, wu, wd)


def _ffn_bwd(dh, dhb, h, g, saved, dep, put_grads, tag):
    nt, r, p, q, st, wg, wu, wd = saved
    n_shards = 4
    da, db = _ffn_bwd_hidden(dhb, wd, p, q, dep, 256, f"{tag}_bwd_hidden")
    dwd = _mm_tn(st, dhb, dep, 0.5, st.shape[0] // n_shards, 1024, f"{tag}_dw_down", stacked=False)
    dep = put_grads(f"{tag}_w_down", dwd)
    dwg = _mm_tn(nt, da, dep, 1.0, 1024, wg.shape[1] // n_shards, f"{tag}_dw_gate", stacked=True)
    dep = put_grads(f"{tag}_w_gate", dwg)
    dwu = _mm_tn(nt, db, dep, 1.0, 1024, wu.shape[1] // n_shards, f"{tag}_dw_up", stacked=True)
    dep = put_grads(f"{tag}_w_up", dwu)
    dn = _mm_nt_sum([da, db], [wg, wu], dep, 512, wg.shape[1] // 4, f"{tag}_dn")
    dh_in, dhb_in, dg = _rmsnorm_bwd(dn, h, r, g, dh, f"{tag}_norm_bwd")
    return dh_in, dhb_in, dg


def _local_step(x, target, S, get_weights, request, put_grads):
    seq, D = x.shape
    L = N_META + seq
    Lp = -(-L // SEQ_ALIGN) * SEQ_ALIGN
    pad = jnp.zeros((Lp - L, D), F32)
    tgt = jnp.concatenate([jnp.zeros((N_META, D), F32), target, pad], axis=0)

    d_pool = S["pool_scale"].shape[1]
    n_heads = S["b_forget"].shape[1]
    d_att = n_heads * HEAD_DIM
    f_col = d_pool + 3 * d_att
    f_block = f_col // LANES
    bpad = jnp.pad(S["b_forget"], ((0, 0), (0, LANES - n_heads)))

    h0 = jnp.concatenate([get_weights("meta", None)["meta"], x, pad], axis=0)
    h1, ffn1 = _ffn_fwd(h0, S["ffn1_norm"], h0, get_weights, True, "ffn1")
    u, ut, r_mix = _rmsnorm_fwd(h1, S["mix_norm"], request("mix_in", h1), "mix_norm")
    Wm = get_weights("mix_in", u)
    z = _mm_nn(u, Wm["win"], 384, "in_proj")
    pooled, pool_out, pool_out_t = _pool_fwd(z, Wm["pool_w"], S["pool_scale"], "pool_fwd")
    qh, kh, vb = _qkv_prep(z, S["q_norm"], S["k_norm"], n_heads, d_pool, "qkv_prep")
    cum_t = _forget_fwd(z, bpad, f_block, "forget_fwd")[:n_heads]
    cum_c = cum_t.reshape(n_heads, Lp, 1)
    cum_r = cum_t.reshape(n_heads, 1, Lp)
    att, att_t, lse_c, lse_r = _attn_fwd(qh, kh, vb, cum_c, cum_r, "attn_fwd")
    mix = jnp.concatenate([pool_out, att], axis=1)
    mix_t = jnp.concatenate([pool_out_t, att_t], axis=0)
    Wm.update(get_weights("mix_out", att))
    dep = request("ffn2_up", att)
    h2 = _mm_nn_residual(mix, Wm["wout"], h1, dep, 1.0, 512, "out_proj")
    h3, ffn2 = _ffn_fwd(h2, S["ffn2_norm"], h2, get_weights, False, "ffn2")

    dh3, dh3b, loss = _loss_grad(h3, tgt, seq, "loss")
    dh2, dh2b, dg_ffn2 = _ffn_bwd(dh3, dh3b, h2, S["ffn2_norm"], ffn2, loss, put_grads, "ffn2")

    dmix = _mm_nt(dh2b, Wm["wout"], loss, 512, "out_proj_bwd")
    dwout = _mm_tn(mix_t, dh2b, loss, 1.0, 1024, 1024, "dw_out", stacked=False)
    dp, dpw, dpsc = _pool_bwd(dmix, pooled, Wm["pool_w"], S["pool_scale"], "pool_bwd")
    dob = dmix[:, d_pool:].astype(BF16)
    dqh, delta_r, dcq = _attn_bwd_q(qh, kh, vb, dob, cum_c, cum_r, lse_c, "attn_bwd_q")
    dkh, dv, dck = _attn_bwd_kv(qh, kh, vb, dob, cum_c, cum_r, lse_r, delta_r, "attn_bwd_kv")
    dq, dk, dgq, dgk = _qk_norm_bwd(dqh, dkh, z, S["q_norm"], S["k_norm"], n_heads, d_pool, "qk_norm_bwd")
    dfl, dbf = _forget_bwd(dcq, dck, z, bpad, f_block, "forget_bwd")
    dz = jnp.concatenate([dp, dq, dk, dv, dfl], axis=1)
    dwin = _mm_tn(ut, dz, loss, 1.0, 1024, dz.shape[1] // 3, "dw_in", stacked=False)
    dep = put_grads("mix", dict(win=dwin, wout=dwout, pool_w=dpw))
    du = _mm_nt_sum([dz], [Wm["win"]], dep, 512, Wm["win"].shape[1] // 3, "in_proj_bwd")
    dh1, dh1b, dg_mix = _rmsnorm_bwd(du, h1, r_mix, S["mix_norm"], dh2, "mix_norm_bwd")

    dh0, _, dg_ffn1 = _ffn_bwd(dh1, dh1b, h0, S["ffn1_norm"], ffn1, loss, put_grads, "ffn1")

    grads = dict(
        x=dh0[N_META:L], meta=dh0[:N_META],
        ffn1_norm=dg_ffn1, mix_norm=dg_mix, ffn2_norm=dg_ffn2, q_norm=dgq, k_norm=dgk,
        b_forget=dbf[:, :n_heads], pool_scale=dpsc,
    )
    return loss[0, 0], dh0, grads


HBM_SPEC = pl.BlockSpec(memory_space=pltpu.HBM)
N_CHIPS = 4


def _chip_peers():
    x, y, c = lax.axis_index("x"), lax.axis_index("y"), lax.axis_index("c")
    flips = [(1 - x, y), (x, 1 - y), (1 - x, 1 - y)]
    return 2 * x + y, [((px, py, c), 2 * px + py) for px, py in flips]


def _gathered_shape(shape, layout):
    if layout == "rows":
        return (N_CHIPS * shape[0],) + shape[1:]
    if layout == "cols":
        return (shape[0], N_CHIPS * shape[1])
    return (N_CHIPS,) + shape


def _cast_place(place, w, dep, layout, dtype, name):
    R, C = w.shape
    tr = _row_tile(R, C)
    nt = R // tr

    def body(place_ref, w_ref, dep_ref, o_ref):
        o_ref[...] = w_ref[...].astype(dtype)

    if layout == "rows":
        ospec = pl.BlockSpec((tr, C), lambda i, p: (p[1] * nt + i, 0))
    elif layout == "cols":
        ospec = pl.BlockSpec((tr, C), lambda i, p: (i, p[1]))
    else:
        ospec = pl.BlockSpec((None, tr, C), lambda i, p: (p[1], i, 0))
    return pl.pallas_call(
        body, name=name,
        grid_spec=pltpu.PrefetchScalarGridSpec(
            num_scalar_prefetch=1, grid=(nt,),
            in_specs=[pl.BlockSpec((tr, C), lambda i, p: (i, 0)), pl.BlockSpec(memory_space=pl.ANY)],
            out_specs=ospec),
        out_shape=jax.ShapeDtypeStruct(_gathered_shape((R, C), layout), dtype),
        compiler_params=_cparams("parallel"),
    )(place, w, dep)


SEM_SPEC = pl.BlockSpec(memory_space=pltpu.SEMAPHORE)
ANY_SPEC = pl.BlockSpec(memory_space=pl.ANY)
SPLIT_COPY = pltpu.CompilerParams(has_side_effects=pltpu.SideEffectType.DATAFLOW_SIDE_EFFECTING)


def _hbm(a):
    return pltpu.with_memory_space_constraint(a, pltpu.HBM)


def _gather_region(refs, shard_shapes, layouts, a, chip, half):
    rows_a = shard_shapes[a][0]
    h = rows_a // 2
    if layouts[a] == "rows":
        return refs[a].at[pl.ds(chip * rows_a + half * h, h)]
    if layouts[a] == "cols":
        cols_a = shard_shapes[a][1]
        return refs[a].at[pl.ds(half * h, h), pl.ds(chip * cols_a, cols_a)]
    return refs[a].at[chip, pl.ds(half * h, h)]


def _gather_start(bufs, after, shard_shapes, layouts, name):
    n = len(bufs)
    ns = 3 * n

    def body(*refs):
        in_refs = refs[:n]
        send_sems = refs[n + 1:n + 1 + ns]
        recv_sems = refs[n + 1 + ns:n + 1 + 2 * ns]
        token = refs[2 * n + 1 + 2 * ns]
        c = lax.axis_index("c")
        me, peers = _chip_peers()
        for a in range(n):
            mine = _gather_region(in_refs, shard_shapes, layouts, a, me, c)
            for k, (dev, _) in enumerate(peers):
                pltpu.make_async_remote_copy(
                    src_ref=mine, dst_ref=mine, send_sem=send_sems[3 * a + k], recv_sem=recv_sems[3 * a + k],
                    device_id=dev, device_id_type=MESH).start()
        token[...] = jnp.zeros_like(token)

    sem = pltpu.SemaphoreType.DMA(())
    out = pl.pallas_call(
        body, name=name,
        out_shape=(*[sem] * (2 * ns), *[pltpu.HBM(b.shape, b.dtype) for b in bufs],
                   jax.ShapeDtypeStruct((8, LANES), F32)),
        in_specs=[HBM_SPEC] * n + [ANY_SPEC],
        out_specs=(*[SEM_SPEC] * (2 * ns), *[HBM_SPEC] * n, pl.BlockSpec(memory_space=pltpu.VMEM)),
        input_output_aliases={a: 2 * ns + a for a in range(n)},
        compiler_params=SPLIT_COPY,
    )(*[_hbm(b) for b in bufs], after)
    return list(out[:ns]), list(out[ns:2 * ns]), list(out[2 * ns:2 * ns + n]), out[2 * ns + n]


def _gather_wait(bufs, send_sems, recv_sems, afters, shard_shapes, layouts, name):
    n = len(bufs)
    ns = 3 * n
    na = len(afters)

    def body(*refs):
        in_refs = refs[:n]
        send_sems = refs[n:n + ns]
        recv_sems = refs[n + ns:n + 2 * ns]
        token = refs[2 * n + 2 * ns + na]
        token[...] = jnp.zeros_like(token)
        c = lax.axis_index("c")
        me, peers = _chip_peers()
        for a in range(n):
            mine = _gather_region(in_refs, shard_shapes, layouts, a, me, c)
            for k, (dev, pidx) in enumerate(peers):
                landed = _gather_region(in_refs, shard_shapes, layouts, a, pidx, c)
                pltpu.make_async_remote_copy(
                    src_ref=mine, dst_ref=landed, send_sem=send_sems[3 * a + k], recv_sem=recv_sems[3 * a + k],
                    device_id=dev, device_id_type=MESH).wait_recv()
        for a in range(n):
            mine = _gather_region(in_refs, shard_shapes, layouts, a, me, c)
            for k, (dev, _) in enumerate(peers):
                pltpu.make_async_remote_copy(
                    src_ref=mine, dst_ref=mine, send_sem=send_sems[3 * a + k], recv_sem=recv_sems[3 * a + k],
                    device_id=dev, device_id_type=MESH).wait_send()

    out = pl.pallas_call(
        body, name=name,
        out_shape=(*[pltpu.HBM(b.shape, b.dtype) for b in bufs], jax.ShapeDtypeStruct((8, LANES), F32)),
        in_specs=[HBM_SPEC] * n + [SEM_SPEC] * (2 * ns) + [ANY_SPEC] * na,
        out_specs=(*[HBM_SPEC] * n, pl.BlockSpec(memory_space=pltpu.VMEM)),
        input_output_aliases={a: a for a in range(n)},
        compiler_params=SPLIT_COPY,
    )(*bufs, *send_sems, *recv_sems, *afters)
    return list(out[:n]), out[n]


def _forward_start(bufs, after, shard_shapes, layouts, name):
    n = len(bufs)
    ns = 3 * n

    def body(*refs):
        in_refs = refs[:n]
        send_sems = refs[n + 1:n + 1 + ns]
        recv_sems = refs[n + 1 + ns:n + 1 + 2 * ns]
        token = refs[2 * n + 1 + 2 * ns]
        c = lax.axis_index("c")
        sib = (lax.axis_index("x"), lax.axis_index("y"), 1 - c)
        _, peers = _chip_peers()
        for a in range(n):
            for k, (_, pidx) in enumerate(peers):
                landed = _gather_region(in_refs, shard_shapes, layouts, a, pidx, c)
                pltpu.make_async_remote_copy(
                    src_ref=landed, dst_ref=landed, send_sem=send_sems[3 * a + k], recv_sem=recv_sems[3 * a + k],
                    device_id=sib, device_id_type=MESH).start()
        token[...] = jnp.zeros_like(token)

    sem = pltpu.SemaphoreType.DMA(())
    out = pl.pallas_call(
        body, name=name,
        out_shape=(*[sem] * (2 * ns), *[pltpu.HBM(b.shape, b.dtype) for b in bufs],
                   jax.ShapeDtypeStruct((8, LANES), F32)),
        in_specs=[HBM_SPEC] * n + [ANY_SPEC],
        out_specs=(*[SEM_SPEC] * (2 * ns), *[HBM_SPEC] * n, pl.BlockSpec(memory_space=pltpu.VMEM)),
        input_output_aliases={a: 2 * ns + a for a in range(n)},
        compiler_params=SPLIT_COPY,
    )(*[_hbm(b) for b in bufs], after)
    return list(out[:ns]), list(out[ns:2 * ns]), list(out[2 * ns:2 * ns + n]), out[2 * ns + n]


def _forward_wait(bufs, send_sems, recv_sems, after, shard_shapes, layouts, name):
    n = len(bufs)
    ns = 3 * n

    def body(*refs):
        in_refs = refs[:n]
        send_sems = refs[n:n + ns]
        recv_sems = refs[n + ns:n + 2 * ns]
        c = lax.axis_index("c")
        sib = (lax.axis_index("x"), lax.axis_index("y"), 1 - c)
        _, peers = _chip_peers()
        for a in range(n):
            for k, (_, pidx) in enumerate(peers):
                landed = _gather_region(in_refs, shard_shapes, layouts, a, pidx, c)
                other = _gather_region(in_refs, shard_shapes, layouts, a, pidx, 1 - c)
                cp = pltpu.make_async_remote_copy(
                    src_ref=landed, dst_ref=other, send_sem=send_sems[3 * a + k], recv_sem=recv_sems[3 * a + k],
                    device_id=sib, device_id_type=MESH)
                cp.wait_recv()
                cp.wait_send()

    return list(pl.pallas_call(
        body, name=name,
        out_shape=tuple(pltpu.HBM(b.shape, b.dtype) for b in bufs),
        in_specs=[HBM_SPEC] * n + [SEM_SPEC] * (2 * ns) + [ANY_SPEC],
        out_specs=tuple([HBM_SPEC] * n),
        input_output_aliases={a: a for a in range(n)},
        compiler_params=SPLIT_COPY,
    )(*bufs, *send_sems, *recv_sems, after))


def _halves_copies(src_refs, land_refs, send_sems, recv_sems):
    c = lax.axis_index("c")
    sib = (lax.axis_index("x"), lax.axis_index("y"), 1 - c)
    copies = []
    for a, (src, land) in enumerate(zip(src_refs, land_refs)):
        h = src.shape[1] // 2
        copies.append(pltpu.make_async_remote_copy(
            src_ref=src.at[:, pl.ds((1 - c) * h, h)], dst_ref=land, send_sem=send_sems[a], recv_sem=recv_sems[a],
            device_id=sib, device_id_type=MESH))
    return copies


def _whole_copies(src_refs, land_refs, send_sems, recv_sems):
    sib = (lax.axis_index("x"), lax.axis_index("y"), 1 - lax.axis_index("c"))
    return [pltpu.make_async_remote_copy(src_ref=src, dst_ref=land, send_sem=send_sems[a], recv_sem=recv_sems[a],
                                         device_id=sib, device_id_type=MESH)
            for a, (src, land) in enumerate(zip(src_refs, land_refs))]


def _halves_land_shape(shape):
    return (shape[0], shape[1] // 2, shape[2])


def _sibling_start(stacked, copies, land_shape, name):
    n = len(stacked)
    lands = [lax.empty(land_shape(s.shape), s.dtype) for s in stacked]

    def body(*refs):
        for cp in copies(refs[:n], refs[n:2 * n], refs[2 * n:3 * n], refs[3 * n:4 * n]):
            cp.start()
        token = refs[6 * n]
        token[...] = jnp.zeros_like(token)

    sem = pltpu.SemaphoreType.DMA(())
    out = pl.pallas_call(
        body, name=name,
        out_shape=(*[sem] * (2 * n), *[pltpu.HBM(b.shape, b.dtype) for b in stacked],
                   *[pltpu.HBM(b.shape, b.dtype) for b in lands], jax.ShapeDtypeStruct((8, LANES), F32)),
        in_specs=[HBM_SPEC] * (2 * n),
        out_specs=(*[SEM_SPEC] * (2 * n), *[HBM_SPEC] * (2 * n), pl.BlockSpec(memory_space=pltpu.VMEM)),
        input_output_aliases={a: 2 * n + a for a in range(2 * n)},
        compiler_params=SPLIT_COPY,
    )(*[_hbm(b) for b in stacked], *[_hbm(b) for b in lands])
    return list(out[:n]), list(out[n:2 * n]), list(out[2 * n:3 * n]), list(out[3 * n:4 * n]), out[4 * n]


def _sibling_wait(srcs, lands, send_sems, recv_sems, after, copies_of, name):
    n = len(srcs)

    def body(*refs):
        copies = copies_of(refs[:n], refs[n:2 * n], refs[2 * n:3 * n], refs[3 * n:4 * n])
        for cp in copies:
            cp.wait_recv()
        for cp in copies:
            cp.wait_send()

    out = pl.pallas_call(
        body, name=name,
        out_shape=tuple(pltpu.HBM(b.shape, b.dtype) for b in list(srcs) + list(lands)),
        in_specs=[HBM_SPEC] * (2 * n) + [SEM_SPEC] * (2 * n) + [ANY_SPEC],
        out_specs=tuple([HBM_SPEC] * (2 * n)),
        input_output_aliases={a: a for a in range(2 * n)},
        compiler_params=SPLIT_COPY,
    )(*srcs, *lands, *send_sems, *recv_sems, after)
    return list(out[:n]), list(out[n:])


def _scatter_start(stacked, name):
    n = len(stacked)
    ns = 3 * n
    lands = [lax.empty((3,) + s.shape[1:], s.dtype) for s in stacked]

    def body(*refs):
        src_refs = refs[:n]
        land_refs = refs[n:2 * n]
        send_sems = refs[2 * n:2 * n + ns]
        recv_sems = refs[2 * n + ns:2 * n + 2 * ns]
        token = refs[4 * n + 2 * ns]
        _, peers = _chip_peers()
        for a in range(n):
            for k, (dev, pidx) in enumerate(peers):
                pltpu.make_async_remote_copy(
                    src_ref=src_refs[a].at[k], dst_ref=land_refs[a].at[k], send_sem=send_sems[3 * a + k],
                    recv_sem=recv_sems[3 * a + k], device_id=dev, device_id_type=MESH).start()
        token[...] = jnp.zeros_like(token)

    sem = pltpu.SemaphoreType.DMA(())
    out = pl.pallas_call(
        body, name=name,
        out_shape=(*[sem] * (2 * ns), *[pltpu.HBM(b.shape, b.dtype) for b in stacked],
                   *[pltpu.HBM(b.shape, b.dtype) for b in lands], jax.ShapeDtypeStruct((8, LANES), F32)),
        in_specs=[HBM_SPEC] * (2 * n),
        out_specs=(*[SEM_SPEC] * (2 * ns), *[HBM_SPEC] * (2 * n), pl.BlockSpec(memory_space=pltpu.VMEM)),
        input_output_aliases={a: 2 * ns + a for a in range(2 * n)},
        compiler_params=SPLIT_COPY,
    )(*[_hbm(b) for b in stacked], *[_hbm(b) for b in lands])
    o = 2 * ns
    return list(out[:ns]), list(out[ns:o]), list(out[o:o + n]), list(out[o + n:o + 2 * n]), out[o + 2 * n]


def _scatter_wait(srcs, lands, send_sems, recv_sems, after, name):
    n = len(srcs)
    ns = 3 * n

    def body(*refs):
        src_refs = refs[:n]
        land_refs = refs[n:2 * n]
        send_sems = refs[2 * n:2 * n + ns]
        recv_sems = refs[2 * n + ns:2 * n + 2 * ns]
        _, peers = _chip_peers()
        copies = [
            pltpu.make_async_remote_copy(
                src_ref=src_refs[a].at[k], dst_ref=land_refs[a].at[k], send_sem=send_sems[3 * a + k],
                recv_sem=recv_sems[3 * a + k], device_id=dev, device_id_type=MESH)
            for a in range(n) for k, (dev, pidx) in enumerate(peers)]
        for cp in copies:
            cp.wait_recv()
        for cp in copies:
            cp.wait_send()

    out = pl.pallas_call(
        body, name=name,
        out_shape=tuple(pltpu.HBM(b.shape, b.dtype) for b in list(srcs) + list(lands)),
        in_specs=[HBM_SPEC] * (2 * n) + [SEM_SPEC] * (2 * ns) + [ANY_SPEC],
        out_specs=tuple([HBM_SPEC] * (2 * n)),
        input_output_aliases={a: a for a in range(2 * n)},
        compiler_params=SPLIT_COPY,
    )(*srcs, *lands, *send_sems, *recv_sems, after)
    return list(out[n:])


def _all_reduce_small(v):
    R, C = v.shape
    n_dev = 8

    def body(v_ref, o_ref, buf, send_sems, recv_sems):
        x, y, c = lax.axis_index("x"), lax.axis_index("y"), lax.axis_index("c")
        me = 4 * x + 2 * y + c
        buf[me] = v_ref[...]
        sends = []
        for k in range(1, n_dev):
            px, py, pc = x ^ ((k >> 2) & 1), y ^ ((k >> 1) & 1), c ^ (k & 1)
            cp = pltpu.make_async_remote_copy(
                src_ref=v_ref, dst_ref=buf.at[me], send_sem=send_sems.at[k - 1], recv_sem=recv_sems.at[k - 1],
                device_id=(px, py, pc), device_id_type=MESH)
            cp.start()
            sends.append((cp, 4 * px + 2 * py + pc))
        for k in range(1, n_dev):
            cp, pidx = sends[k - 1]
            pltpu.make_async_remote_copy(
                src_ref=v_ref, dst_ref=buf.at[pidx], send_sem=send_sems.at[k - 1], recv_sem=recv_sems.at[k - 1],
                device_id=(x, y, c), device_id_type=MESH).wait_recv()
        for cp, _ in sends:
            cp.wait_send()
        acc = buf[0]
        for d in range(1, n_dev):
            acc = acc + buf[d]
        o_ref[...] = acc

    vm = pl.BlockSpec(memory_space=pltpu.VMEM)
    return pl.pallas_call(
        body, name="all_reduce_small",
        in_specs=[vm], out_specs=vm,
        out_shape=jax.ShapeDtypeStruct((R, C), F32),
        scratch_shapes=[pltpu.VMEM((n_dev, R, C), F32), pltpu.SemaphoreType.DMA((n_dev - 1,)),
                        pltpu.SemaphoreType.DMA((n_dev - 1,))],
    )(v)


def _pair_sum(place, own, sib, name):
    S, R, C = own.shape
    h = R // 2
    tr = _row_tile(h, C, own.dtype.itemsize)
    nt = h // tr

    def body(place_ref, o_ref, s_ref, out_ref):
        out_ref[...] = (o_ref[...].astype(F32) + s_ref[...].astype(F32)).astype(BF16)

    return pl.pallas_call(
        body, name=name,
        grid_spec=pltpu.PrefetchScalarGridSpec(
            num_scalar_prefetch=1, grid=(3, nt),
            in_specs=[pl.BlockSpec((None, tr, C), lambda k, i, p: (p[2 + k], p[0] * nt + i, 0)),
                      pl.BlockSpec((None, tr, C), lambda k, i, p: (p[2 + k], i, 0))],
            out_specs=pl.BlockSpec((None, tr, C), lambda k, i, p: (k, i, 0))),
        out_shape=jax.ShapeDtypeStruct((3, h, C), BF16),
        compiler_params=_cparams("parallel", "parallel"),
    )(place, own, sib)


def _sum_slabs(place, own, sib, recv, name):
    S, R, C = own.shape
    h = R // 2
    tr = _row_tile(h, C)
    nt = h // tr

    def body(place_ref, o_ref, s_ref, r_ref, out_ref):
        acc = o_ref[...].astype(F32) + s_ref[...].astype(F32)
        for k in range(3):
            acc = acc + r_ref[k].astype(F32)
        out_ref[...] = acc

    return pl.pallas_call(
        body, name=name,
        grid_spec=pltpu.PrefetchScalarGridSpec(
            num_scalar_prefetch=1, grid=(nt,),
            in_specs=[pl.BlockSpec((None, tr, C), lambda i, p: (p[1], p[0] * nt + i, 0)),
                      pl.BlockSpec((None, tr, C), lambda i, p: (p[1], i, 0)),
                      pl.BlockSpec((3, tr, C), lambda i, p: (0, i, 0))],
            out_specs=pl.BlockSpec((tr, C), lambda i, p: (i, 0))),
        out_shape=jax.ShapeDtypeStruct((h, C), F32),
        compiler_params=_cparams("parallel"),
    )(place, own, sib, recv)


def _adamw(parts, w, m, v, name):
    R, C = w.shape
    tr = _row_tile(R, C)
    npart = len(parts)
    c1 = 1.0 - ADAM_B1 ** ADAM_STEP
    c2 = 1.0 - ADAM_B2 ** ADAM_STEP

    def body(*refs):
        p_refs = refs[:npart]
        w_ref, m_ref, v_ref, g_ref, d_ref, nm_ref, nv_ref = refs[npart:]
        g = p_refs[0][...]
        for p_ref in p_refs[1:]:
            g = g + p_ref[...]
        nm = ADAM_B1 * m_ref[...] + (1.0 - ADAM_B1) * g
        nv = ADAM_B2 * v_ref[...] + (1.0 - ADAM_B2) * (g * g)
        m_hat = nm / c1
        v_hat = nv / c2
        g_ref[...] = g
        d_ref[...] = -ADAM_LR * (m_hat / (jnp.sqrt(v_hat) + ADAM_EPS) + ADAM_WD * w_ref[...])
        nm_ref[...] = nm
        nv_ref[...] = nv

    blk = pl.BlockSpec((tr, C), lambda i: (i, 0))
    shape = jax.ShapeDtypeStruct((R, C), F32)
    return pl.pallas_call(
        body, name=name, grid=(R // tr,),
        in_specs=[blk] * (npart + 3), out_specs=[blk] * 4, out_shape=[shape] * 4,
        compiler_params=_cparams("parallel"),
    )(*parts, w, m, v)


def _adamw_halves(place, mine, other, w, m, v, name):
    R, C = w.shape
    h = R // 2
    tr = _row_tile(h, C)
    nt = h // tr
    c1 = 1.0 - ADAM_B1 ** ADAM_STEP
    c2 = 1.0 - ADAM_B2 ** ADAM_STEP

    def body(place_ref, mine_ref, other_ref, w_ref, m_ref, v_ref, g_ref, d_ref, nm_ref, nv_ref):
        is_mine = (pl.program_id(0) // nt) == place_ref[0]
        g = jnp.where(is_mine, mine_ref[...], other_ref[...])
        nm = ADAM_B1 * m_ref[...] + (1.0 - ADAM_B1) * g
        nv = ADAM_B2 * v_ref[...] + (1.0 - ADAM_B2) * (g * g)
        m_hat = nm / c1
        v_hat = nv / c2
        g_ref[...] = g
        d_ref[...] = -ADAM_LR * (m_hat / (jnp.sqrt(v_hat) + ADAM_EPS) + ADAM_WD * w_ref[...])
        nm_ref[...] = nm
        nv_ref[...] = nv

    def half_block(which):
        def index(i, p):
            first = p[0] if which == 0 else 1 - p[0]
            return jnp.clip(i - first * nt, 0, nt - 1), 0

        return pl.BlockSpec((tr, C), index)

    blk = pl.BlockSpec((tr, C), lambda i, p: (i, 0))
    shape = jax.ShapeDtypeStruct((R, C), F32)
    return pl.pallas_call(
        body, name=name,
        grid_spec=pltpu.PrefetchScalarGridSpec(
            num_scalar_prefetch=1, grid=(2 * nt,),
            in_specs=[half_block(0), half_block(1), blk, blk, blk], out_specs=[blk] * 4),
        out_shape=[shape] * 4,
        compiler_params=_cparams("parallel"),
    )(place, mine, other, w, m, v)


SMALL_NAMES = ("ffn1_norm", "mix_norm", "ffn2_norm", "pool_scale", "q_norm", "k_norm", "b_forget")
SMALL_COLS = 1024
LOSS_LANE = 512


def _pack_small(vals):
    rows = [vals[n].reshape(-1, SMALL_COLS) for n in ("ffn1_norm", "mix_norm", "ffn2_norm", "pool_scale")]
    tail = jnp.concatenate([vals["q_norm"].reshape(-1), vals["k_norm"].reshape(-1), vals["b_forget"].reshape(-1)])
    rows.append(jnp.pad(tail, (0, SMALL_COLS - tail.shape[0])).reshape(1, SMALL_COLS))
    return jnp.concatenate(rows, axis=0)


def _unpack_small(packed, like):
    out = {}
    r = 0
    for n in ("ffn1_norm", "mix_norm", "ffn2_norm", "pool_scale"):
        k = like[n].size // SMALL_COLS
        out[n] = packed[r:r + k].reshape(like[n].shape)
        r += k
    o = 0
    for n in ("q_norm", "k_norm", "b_forget"):
        k = like[n].size
        out[n] = packed[r, o:o + k].reshape(like[n].shape)
        o += k
    return out


def kernel(x, meta_tokens, ffn1_norm, ffn1_w_gate, ffn1_w_up, ffn1_w_down, mix_norm, w_in, b_forget, q_norm, k_norm, pool_w, pool_scale, w_out, ffn2_norm, ffn2_w_gate, ffn2_w_up, ffn2_w_down, loss_target, m_meta_tokens, m_ffn1_norm, m_ffn1_w_gate, m_ffn1_w_up, m_ffn1_w_down, m_mix_norm, m_w_in, m_b_forget, m_q_norm, m_k_norm, m_pool_w, m_pool_scale, m_w_out, m_ffn2_norm, m_ffn2_w_gate, m_ffn2_w_up, m_ffn2_w_down, v_meta_tokens, v_ffn1_norm, v_ffn1_w_gate, v_ffn1_w_up, v_ffn1_w_down, v_mix_norm, v_w_in, v_b_forget, v_q_norm, v_k_norm, v_pool_w, v_pool_scale, v_w_out, v_ffn2_norm, v_ffn2_w_gate, v_ffn2_w_up, v_ffn2_w_down):
    wts = dict(meta_tokens=meta_tokens, ffn1_norm=ffn1_norm, ffn1_w_gate=ffn1_w_gate, ffn1_w_up=ffn1_w_up,
               ffn1_w_down=ffn1_w_down, mix_norm=mix_norm, w_in=w_in, b_forget=b_forget, q_norm=q_norm,
               k_norm=k_norm, pool_w=pool_w, pool_scale=pool_scale, w_out=w_out, ffn2_norm=ffn2_norm,
               ffn2_w_gate=ffn2_w_gate, ffn2_w_up=ffn2_w_up, ffn2_w_down=ffn2_w_down)
    mom = dict(meta_tokens=m_meta_tokens, ffn1_norm=m_ffn1_norm, ffn1_w_gate=m_ffn1_w_gate, ffn1_w_up=m_ffn1_w_up,
               ffn1_w_down=m_ffn1_w_down, mix_norm=m_mix_norm, w_in=m_w_in, b_forget=m_b_forget, q_norm=m_q_norm,
               k_norm=m_k_norm, pool_w=m_pool_w, pool_scale=m_pool_scale, w_out=m_w_out, ffn2_norm=m_ffn2_norm,
               ffn2_w_gate=m_ffn2_w_gate, ffn2_w_up=m_ffn2_w_up, ffn2_w_down=m_ffn2_w_down)
    var = dict(meta_tokens=v_meta_tokens, ffn1_norm=v_ffn1_norm, ffn1_w_gate=v_ffn1_w_gate, ffn1_w_up=v_ffn1_w_up,
               ffn1_w_down=v_ffn1_w_down, mix_norm=v_mix_norm, w_in=v_w_in, b_forget=v_b_forget, q_norm=v_q_norm,
               k_norm=v_k_norm, pool_w=v_pool_w, pool_scale=v_pool_scale, w_out=v_w_out, ffn2_norm=v_ffn2_norm,
               ffn2_w_gate=v_ffn2_w_gate, ffn2_w_up=v_ffn2_w_up, ffn2_w_down=v_ffn2_w_down)
    order = list(wts)
    me = 2 * lax.axis_index("x") + lax.axis_index("y")

    D = x.shape[2]
    d_in_shard = w_in.shape[2]
    d_in = N_CHIPS * d_in_shard
    n_heads = b_forget.shape[1]
    d_in_pad = (d_in - n_heads) + LANES

    stages = dict(meta=("meta_tokens",), ffn1_gate=("ffn1_w_gate",), ffn1_up=("ffn1_w_up",),
                  ffn1_down=("ffn1_w_down",),
                  mix_in=("w_in", "pool_w"), mix_out=("w_out",),
                  ffn2_up=("ffn2_w_gate", "ffn2_w_up"), ffn2_down=("ffn2_w_down",))
    stage_order = list(stages)
    xi, yi = lax.axis_index("x"), lax.axis_index("y")
    place = jnp.stack([lax.axis_index("c"), me, 2 * (1 - xi) + yi, 2 * xi + 1 - yi, 2 * (1 - xi) + 1 - yi]).astype(
        jnp.int32)
    layouts = dict(ffn1_w_gate="cols", ffn1_w_up="cols", ffn1_w_down="rows", w_in="stack", w_out="rows",
                   pool_w="stack", ffn2_w_gate="cols", ffn2_w_up="cols", ffn2_w_down="rows", meta_tokens="stack")
    shards2d = {n: wts[n].reshape(-1, wts[n].shape[-1]) for n in layouts}

    def place_stage(stage, dep):
        return [_cast_place(place, shards2d[n], dep, layouts[n], F32 if n == "meta_tokens" else BF16, f"place_{n}")
                for n in stages[stage]]

    def start_stage(stage, bufs, after):
        shapes = [shards2d[n].shape for n in stages[stage]]
        lays = [layouts[n] for n in stages[stage]]
        return _gather_start(bufs, after, shapes, lays, f"gather_start_{stage}") + (shapes, lays)

    flight = {stage_order[0]: start_stage(stage_order[0], place_stage(stage_order[0], place), place)}
    placed = {stage_order[1]: place_stage(stage_order[1], flight[stage_order[0]][3])}

    def cols(st):
        return jnp.transpose(st, (1, 0, 2)).reshape(st.shape[1], -1)

    forwarding = {}

    def request(stage, after):
        k = stage_order.index(stage)
        send_sems, recv_sems, bufs, _, shapes, lays = flight.pop(stage)
        afters = ([] if after is None else [after]) + [b for st in placed for b in placed[st]]
        afters += [flight[st][3] for st in flight]
        if k == 1:
            afters += [mom["w_in"].reshape(shards2d["w_in"].shape), var["w_in"].reshape(shards2d["w_in"].shape)]
        landed, token = _gather_wait(bufs, send_sems, recv_sems, afters, shapes, lays, f"gather_wait_{stage}")
        if k + 1 < len(stage_order) and stage_order[k + 1] not in flight:
            flight[stage_order[k + 1]] = start_stage(stage_order[k + 1], placed.pop(stage_order[k + 1]), token)
            token = flight[stage_order[k + 1]][3]
        if stage == "mix_in":
            flight["ffn2_up"] = start_stage("ffn2_up", placed.pop("ffn2_up"), token)
            token = flight["ffn2_up"][3]
        if k == 0:
            placed.update({st: place_stage(st, token) for st in stage_order[2:]})
            early = stage_order[2]
            last_cast = placed[stage_order[-1]][-1]
            flight[early] = start_stage(early, placed.pop(early), last_cast)
        forwarding[stage] = _forward_start(landed, token, shapes, lays, f"forward_start_{stage}") + (shapes, lays)
        return forwarding[stage][3]

    def get_weights(stage, after):
        if stage not in forwarding:
            request(stage, after)
        send_sems, recv_sems, bufs, token, shapes, lays = forwarding.pop(stage)
        full = _forward_wait(bufs, send_sems, recv_sems, token if after is None else after, shapes, lays,
                             f"forward_wait_{stage}")
        G = dict(zip(stages[stage], full))
        if stage == "meta":
            return dict(meta=cols(G["meta_tokens"]))
        if stage == "ffn1_gate":
            return dict(wg=G["ffn1_w_gate"])
        if stage == "ffn1_up":
            return dict(wu=G["ffn1_w_up"])
        if stage == "ffn2_up":
            return dict(wg=G["ffn2_w_gate"], wu=G["ffn2_w_up"])
        if stage == "mix_out":
            return dict(wout=G["w_out"])
        if stage != "mix_in":
            return dict(wd=G[stages[stage][0]])
        return dict(
            win=jnp.concatenate([G["w_in"][s] for s in range(N_CHIPS)] + [jnp.zeros((D, d_in_pad - d_in), BF16)],
                                axis=1),
            pool_w=jnp.transpose(G["pool_w"].reshape((N_CHIPS,) + pool_w.shape[1:]), (1, 0, 2, 3)).reshape(
                N_POOL_GROUPS, pool_w.shape[3], pool_w.shape[3]))

    def split_rows(a):
        return a.reshape(N_CHIPS, -1, a.shape[1])

    def split_win(a):
        return jnp.stack([a[:, s * d_in_shard:(s + 1) * d_in_shard] for s in range(N_CHIPS)])

    def split_pool(a):
        r, c = pool_w.shape[2], pool_w.shape[3]
        return jnp.transpose(a.reshape(N_POOL_GROUPS, N_CHIPS, r, c), (1, 0, 2, 3)).reshape(N_CHIPS, -1, c)

    scatter = {}
    pending = []

    def finish_pending(after):
        name, names, (send_sems, recv_sems, srcs, lands) = pending.pop()
        own, from_sib = _sibling_wait(srcs, lands, send_sems, recv_sems, after, _halves_copies,
                                      f"halves_wait_{name}")
        pair = [_pair_sum(place, o, s, f"pair_sum_{n}") for n, o, s in zip(names, own, from_sib)]
        send_sems, recv_sems, srcs, lands, token = _scatter_start(pair, f"scatter_start_{name}")
        scatter[name] = (names, own, from_sib, send_sems, recv_sems, srcs, lands, token)
        return token

    def put_grads(name, g):
        if name == "mix":
            names = ("w_in", "w_out", "pool_w")
            own = [split_win(g["win"]), split_rows(g["wout"]), split_pool(g["pool_w"])]
        else:
            names = (name,)
            own = [split_rows(g) if name.endswith("_down") else g]
        *flying, token = _sibling_start(own, _halves_copies, _halves_land_shape, f"halves_start_{name}")
        if pending:
            token = finish_pending(token)
        pending.append((name, names, flying))
        return token

    small = dict(ffn1_norm=ffn1_norm, mix_norm=mix_norm, ffn2_norm=ffn2_norm, q_norm=q_norm, k_norm=k_norm,
                 b_forget=b_forget, pool_scale=pool_scale)
    loss_part, dh0, gr = _local_step(x[0], loss_target[0], small, get_weights, request, put_grads)

    out_g, out_d, out_m, out_v = {}, {}, {}, {}

    def update(stage, names, flying, after):
        send_sems, recv_sems, srcs, lands = flying
        halves, other_halves = _sibling_wait(srcs, lands, send_sems, recv_sems, after, _whole_copies,
                                             f"swap_wait_{stage}")
        for n, mine, other in zip(names, halves, other_halves):
            shape = wts[n].shape
            res = _adamw_halves(place, mine, other, shards2d[n], mom[n].reshape(shards2d[n].shape),
                                var[n].reshape(shards2d[n].shape), f"adamw_{n}")
            out_g[n], out_d[n], out_m[n], out_v[n] = (a.reshape(shape) for a in res)
        return res[3]

    after = finish_pending(dh0)
    swapping = None
    for stage in scatter:
        names, own, from_sib, send_sems, recv_sems, srcs, lands, _ = scatter[stage]
        received = _scatter_wait(srcs, lands, send_sems, recv_sems, after, f"scatter_wait_{stage}")
        halves = [_sum_slabs(place, o, s, r, f"sum_{n}") for n, o, s, r in zip(names, own, from_sib, received)]
        *flying, after = _sibling_start(halves, _whole_copies, lambda shape: shape, f"swap_start_{stage}")
        if swapping is not None:
            after = update(*swapping, after)
        swapping = (stage, names, flying)
    update(*swapping, after)

    small_g = _pack_small({n: gr[n] for n in SMALL_NAMES})
    n_small = small_g.shape[0]
    small_g = small_g.at[n_small - 1, LOSS_LANE].set(loss_part)
    meta_rows = gr["meta"].reshape(-1, SMALL_COLS)
    total = _all_reduce_small(jnp.concatenate([small_g, meta_rows], axis=0))
    loss = total[n_small - 1, LOSS_LANE]
    res = _adamw([total[:n_small]], _pack_small({n: wts[n] for n in SMALL_NAMES}),
                 _pack_small({n: mom[n] for n in SMALL_NAMES}), _pack_small({n: var[n] for n in SMALL_NAMES}),
                 "adamw_small")
    for dst, packed in zip((out_g, out_d, out_m, out_v), res):
        dst.update(_unpack_small(packed, wts))
    meta_cols = meta_tokens.shape[1]
    meta_g = lax.dynamic_slice_in_dim(total[n_small:].reshape(N_META, D), me * meta_cols, meta_cols, axis=1)
    res = _adamw([meta_g], meta_tokens, m_meta_tokens, v_meta_tokens, "adamw_meta")
    out_g["meta_tokens"], out_d["meta_tokens"], out_m["meta_tokens"], out_v["meta_tokens"] = res

    grad_x = gr["x"].reshape(x.shape)
    return (loss, grad_x, *[out_g[n] for n in order], *[out_d[n] for n in order], *[out_m[n] for n in order],
            *[out_v[n] for n in order])
```

```python
import functools
import math

import jax
import jax.numpy as jnp
from jax import lax
from jax.experimental import pallas as pl
from jax.experimental.pallas import tpu as pltpu

F32 = jnp.float32
BF16 = jnp.bfloat16

N_META = 16
EPS = 1e-6
HEAD_DIM = 128
N_POOL_GROUPS = 4
LANES = 128
SEQ_ALIGN = 128
TQ = 128
CAUSAL_STEP = 128
VMEM_LIMIT = 56 * 1024 * 1024
ELEMWISE_BLOCK_BYTES = 2304 * 1024

ADAM_LR = 0.001
ADAM_B1 = 0.9
ADAM_B2 = 0.999
ADAM_EPS = 1e-08
ADAM_WD = 0.01
ADAM_STEP = 10

NT_DIMS = (((1,), (1,)), ((), ()))
NEG = -1e30
MESH = pl.DeviceIdType.MESH


def _cparams(*sem):
    return pltpu.CompilerParams(dimension_semantics=sem, vmem_limit_bytes=VMEM_LIMIT)


def _sigmoid(a):
    return 1.0 / (1.0 + jnp.exp(-a))


def _row_tile(rows, cols, itemsize=4):
    best = None
    for t in range(16, rows + 1, 16):
        if rows % t == 0 and t * cols * itemsize <= ELEMWISE_BLOCK_BYTES:
            best = t
    return best if best is not None else rows


def _mm_nn(x, w, tn, name):
    M, K = x.shape
    N = w.shape[1]

    def body(x_ref, w_ref, o_ref):
        o_ref[...] = jnp.dot(x_ref[...], w_ref[...], preferred_element_type=F32)

    return pl.pallas_call(
        body, name=name, grid=(N // tn,),
        in_specs=[pl.BlockSpec((M, K), lambda j: (0, 0)), pl.BlockSpec((K, tn), lambda j: (0, j))],
        out_specs=pl.BlockSpec((M, tn), lambda j: (0, j)),
        out_shape=jax.ShapeDtypeStruct((M, N), F32),
        compiler_params=_cparams("parallel"),
    )(x, w)


def _ffn_up(n, wg, wu, dep, tn, name):
    M, K = n.shape
    N = wg.shape[1]

    def body(n_ref, wg_ref, wu_ref, dep_ref, p_ref, q_ref, s_ref, st_ref):
        nv = n_ref[...]
        a = jnp.dot(nv, wg_ref[...], preferred_element_type=F32)
        b = jnp.dot(nv, wu_ref[...], preferred_element_type=F32)
        sig = _sigmoid(a)
        silu = a * sig
        p_ref[...] = (b * (sig * (1.0 + a * (1.0 - sig)))).astype(BF16)
        q_ref[...] = silu.astype(BF16)
        s = silu * b
        s_ref[...] = s.astype(BF16)
        st_ref[...] = s.T.astype(BF16)

    wspec = pl.BlockSpec((K, tn), lambda j: (0, j))
    ospec = pl.BlockSpec((M, tn), lambda j: (0, j))
    return pl.pallas_call(
        body, name=name, grid=(N // tn,),
        in_specs=[pl.BlockSpec((M, K), lambda j: (0, 0)), wspec, wspec, pl.BlockSpec(memory_space=pl.ANY)],
        out_specs=[ospec, ospec, ospec, pl.BlockSpec((tn, M), lambda j: (j, 0))],
        out_shape=[jax.ShapeDtypeStruct((M, N), BF16), jax.ShapeDtypeStruct((M, N), BF16),
                   jax.ShapeDtypeStruct((M, N), BF16), jax.ShapeDtypeStruct((N, M), BF16)],
        compiler_params=_cparams("parallel"),
    )(n, wg, wu, dep)


def _ffn_up_from_gate(n, a, wu, tn, name):
    M, K = n.shape
    N = wu.shape[1]

    def body(n_ref, a_ref, wu_ref, p_ref, q_ref, s_ref, st_ref):
        a = a_ref[...]
        b = jnp.dot(n_ref[...], wu_ref[...], preferred_element_type=F32)
        sig = _sigmoid(a)
        silu = a * sig
        p_ref[...] = (b * (sig * (1.0 + a * (1.0 - sig)))).astype(BF16)
        q_ref[...] = silu.astype(BF16)
        s = silu * b
        s_ref[...] = s.astype(BF16)
        st_ref[...] = s.T.astype(BF16)

    ospec = pl.BlockSpec((M, tn), lambda j: (0, j))
    return pl.pallas_call(
        body, name=name, grid=(N // tn,),
        in_specs=[pl.BlockSpec((M, K), lambda j: (0, 0)), ospec, pl.BlockSpec((K, tn), lambda j: (0, j))],
        out_specs=[ospec, ospec, ospec, pl.BlockSpec((tn, M), lambda j: (j, 0))],
        out_shape=[jax.ShapeDtypeStruct((M, N), BF16), jax.ShapeDtypeStruct((M, N), BF16),
                   jax.ShapeDtypeStruct((M, N), BF16), jax.ShapeDtypeStruct((N, M), BF16)],
        compiler_params=_cparams("parallel"),
    )(n, a, wu)


def _mm_nn_residual(x, w, res, dep, alpha, tn, name):
    M, K = x.shape
    N = w.shape[1]

    def body(x_ref, w_ref, r_ref, dep_ref, o_ref):
        o_ref[...] = r_ref[...] + alpha * jnp.dot(x_ref[...], w_ref[...], preferred_element_type=F32)

    return pl.pallas_call(
        body, name=name, grid=(N // tn,),
        in_specs=[pl.BlockSpec((M, K), lambda j: (0, 0), pipeline_mode=pl.Buffered(1)),
                  pl.BlockSpec((K, tn), lambda j: (0, j)), pl.BlockSpec((M, tn), lambda j: (0, j)),
                  pl.BlockSpec(memory_space=pl.ANY)],
        out_specs=pl.BlockSpec((M, tn), lambda j: (0, j)),
        out_shape=jax.ShapeDtypeStruct((M, N), F32),
        compiler_params=_cparams("parallel"),
    )(x, w, res, dep)


def _ffn_bwd_hidden(dhb, wd, p, q, dep, tn, name):
    M, K = dhb.shape
    N = wd.shape[0]

    def body(dh_ref, w_ref, p_ref, q_ref, dep_ref, da_ref, db_ref):
        ds = 0.5 * lax.dot_general(dh_ref[...], w_ref[...], NT_DIMS, preferred_element_type=F32)
        da_ref[...] = (ds * p_ref[...].astype(F32)).astype(BF16)
        db_ref[...] = (ds * q_ref[...].astype(F32)).astype(BF16)

    ospec = pl.BlockSpec((M, tn), lambda j: (0, j))
    return pl.pallas_call(
        body, name=name, grid=(N // tn,),
        in_specs=[pl.BlockSpec((M, K), lambda j: (0, 0)), pl.BlockSpec((tn, K), lambda j: (j, 0)), ospec, ospec,
                  pl.BlockSpec(memory_space=pl.ANY)],
        out_specs=[ospec, ospec],
        out_shape=[jax.ShapeDtypeStruct((M, N), BF16), jax.ShapeDtypeStruct((M, N), BF16)],
        compiler_params=_cparams("parallel"),
    )(dhb, wd, p, q, dep)


def _mm_nt(x, w, dep, tn, name):
    M, K = x.shape
    N = w.shape[0]

    def body(x_ref, w_ref, dep_ref, o_ref):
        o_ref[...] = lax.dot_general(x_ref[...], w_ref[...], NT_DIMS, preferred_element_type=F32)

    return pl.pallas_call(
        body, name=name, grid=(N // tn,),
        in_specs=[pl.BlockSpec((M, K), lambda j: (0, 0)), pl.BlockSpec((tn, K), lambda j: (j, 0)),
                  pl.BlockSpec(memory_space=pl.ANY)],
        out_specs=pl.BlockSpec((M, tn), lambda j: (0, j)),
        out_shape=jax.ShapeDtypeStruct((M, N), F32),
        compiler_params=_cparams("parallel"),
    )(x, w, dep)


def _mm_nt_sum(xs, ws, dep, tn, tk, name):
    npair = len(xs)
    M, K = xs[0].shape
    N = ws[0].shape[0]
    nk = K // tk

    def body(*refs):
        x_refs = refs[:npair]
        w_refs = refs[npair:2 * npair]
        o_ref = refs[2 * npair + 1]
        acc = refs[2 * npair + 2]
        k = pl.program_id(1)

        @pl.when(k == 0)
        def _():
            acc[...] = jnp.zeros_like(acc)

        for x_ref, w_ref in zip(x_refs, w_refs):
            acc[...] += lax.dot_general(x_ref[...], w_ref[...], NT_DIMS, preferred_element_type=F32)

        @pl.when(k == nk - 1)
        def _():
            o_ref[...] = acc[...]

    return pl.pallas_call(
        body, name=name, grid=(N // tn, nk),
        in_specs=[pl.BlockSpec((M, tk), lambda j, k: (0, k))] * npair
        + [pl.BlockSpec((tn, tk), lambda j, k: (j, k))] * npair + [pl.BlockSpec(memory_space=pl.ANY)],
        out_specs=pl.BlockSpec((M, tn), lambda j, k: (0, j)),
        out_shape=jax.ShapeDtypeStruct((M, N), F32),
        scratch_shapes=[pltpu.VMEM((M, tn), F32)],
        compiler_params=_cparams("parallel", "arbitrary"),
    )(*xs, *ws, dep)


def _mm_tn(xt, dy, dep, alpha, ti, tn, name, stacked):
    Kin, M = xt.shape
    N = dy.shape[1]

    def body(xt_ref, dy_ref, dep_ref, ob_ref):
        r = jnp.dot(xt_ref[...], dy_ref[...], preferred_element_type=F32)
        if alpha != 1.0:
            r = alpha * r
        ob_ref[...] = r.astype(BF16)

    if stacked:
        ospec = pl.BlockSpec((None, ti, tn), lambda i, j: (j, i, 0))
        oshape = (N // tn, Kin, tn)
    else:
        ospec = pl.BlockSpec((ti, tn), lambda i, j: (i, j))
        oshape = (Kin, N)
    return pl.pallas_call(
        body, name=name, grid=(Kin // ti, N // tn),
        in_specs=[pl.BlockSpec((ti, M), lambda i, j: (i, 0)), pl.BlockSpec((M, tn), lambda i, j: (0, j)),
                  pl.BlockSpec(memory_space=pl.ANY)],
        out_specs=ospec,
        out_shape=jax.ShapeDtypeStruct(oshape, BF16),
        compiler_params=_cparams("parallel", "parallel"),
    )(xt, dy, dep)


def _rmsnorm_fwd(h, g, dep, name):
    M, D = h.shape
    tr = LANES

    def body(h_ref, g_ref, dep_ref, n_ref, nt_ref, r_ref):
        hv = h_ref[...]
        r = lax.rsqrt(jnp.mean(hv * hv, axis=-1, keepdims=True) + EPS)
        n = hv * r * g_ref[...]
        n_ref[...] = n.astype(BF16)
        nt_ref[...] = n.T.astype(BF16)
        r_ref[...] = r

    return pl.pallas_call(
        body, name=name, grid=(M // tr,),
        in_specs=[pl.BlockSpec((tr, D), lambda i: (i, 0)), pl.BlockSpec((1, D), lambda i: (0, 0)),
                  pl.BlockSpec(memory_space=pl.ANY)],
        out_specs=[pl.BlockSpec((tr, D), lambda i: (i, 0)), pl.BlockSpec((D, tr), lambda i: (0, i)),
                   pl.BlockSpec((tr, 1), lambda i: (i, 0))],
        out_shape=[jax.ShapeDtypeStruct((M, D), BF16), jax.ShapeDtypeStruct((D, M), BF16),
                   jax.ShapeDtypeStruct((M, 1), F32)],
        compiler_params=_cparams("parallel"),
    )(h, g, dep)


def _rmsnorm_bwd(dn, h, r, g, dh_prev, name):
    M, D = h.shape
    tr = _row_tile(M, D)

    def body(dn_ref, h_ref, r_ref, g_ref, dp_ref, dh_ref, dhb_ref, dg_ref):
        i = pl.program_id(0)
        dnv = dn_ref[...]
        hv = h_ref[...]
        rv = r_ref[...]
        w = dnv * g_ref[...]
        c = jnp.mean(w * hv, axis=-1, keepdims=True)
        dh = dp_ref[...] + rv * w - hv * (rv * rv * rv * c)
        dh_ref[...] = dh
        dhb_ref[...] = dh.astype(BF16)
        part = jnp.sum(dnv * (hv * rv), axis=0, keepdims=True)

        @pl.when(i == 0)
        def _():
            dg_ref[...] = part

        @pl.when(i > 0)
        def _():
            dg_ref[...] += part

    row = pl.BlockSpec((tr, D), lambda i: (i, 0))
    vec = pl.BlockSpec((1, D), lambda i: (0, 0))
    return pl.pallas_call(
        body, name=name, grid=(M // tr,),
        in_specs=[row, row, pl.BlockSpec((tr, 1), lambda i: (i, 0)), vec, row],
        out_specs=[row, row, vec],
        out_shape=[jax.ShapeDtypeStruct((M, D), F32), jax.ShapeDtypeStruct((M, D), BF16),
                   jax.ShapeDtypeStruct((1, D), F32)],
        compiler_params=_cparams("arbitrary"),
    )(dn, h, r, g, dh_prev)


def _loss_grad(h, tgt, seq, name):
    M, D = h.shape
    tr = _row_tile(M, D)

    def body(h_ref, t_ref, dh_ref, dhb_ref, loss_ref):
        i = pl.program_id(0)
        row = i * tr + lax.broadcasted_iota(jnp.int32, (tr, 1), 0)
        valid = (row >= N_META) & (row < N_META + seq)
        d = jnp.where(valid, h_ref[...] - t_ref[...], 0.0)
        dh = d * (1.0 / D)
        dh_ref[...] = dh
        dhb_ref[...] = dh.astype(BF16)
        part = (0.5 / D) * jnp.sum(jnp.sum(d * d, axis=1, keepdims=True), axis=0, keepdims=True)

        @pl.when(i == 0)
        def _():
            loss_ref[...] = part

        @pl.when(i > 0)
        def _():
            loss_ref[...] += part

    row = pl.BlockSpec((tr, D), lambda i: (i, 0))
    return pl.pallas_call(
        body, name=name, grid=(M // tr,),
        in_specs=[row, row],
        out_specs=[row, row, pl.BlockSpec((1, 1), lambda i: (0, 0))],
        out_shape=[jax.ShapeDtypeStruct((M, D), F32), jax.ShapeDtypeStruct((M, D), BF16),
                   jax.ShapeDtypeStruct((1, 1), F32)],
        compiler_params=_cparams("arbitrary"),
    )(h, tgt)


def _group_window(g):
    return jnp.where(g == 0, 2, jnp.where(g == 1, 4, jnp.where(g == 2, 8, 16)))


def _pool_fwd(z, pw, psc, name):
    M = z.shape[0]
    C = pw.shape[1]

    def body(p_ref, w_ref, sc_ref, pooled_ref, out_ref, outt_ref):
        g = pl.program_id(0)
        p = p_ref[...]
        t = lax.broadcasted_iota(jnp.int32, (M, 1), 0)
        s = p
        wsum = jnp.zeros_like(p)
        for step in range(N_POOL_GROUPS):
            sh = 1 << step
            s = s + jnp.where(t >= sh, pltpu.roll(s, sh, 0), 0.0)
            wsum = jnp.where(g == step, s, wsum)
        cnt = jnp.minimum(t + 1, _group_window(g)).astype(F32)
        pb = (wsum / cnt - p).astype(BF16)
        pooled_ref[...] = pb
        out = jnp.dot(pb, w_ref[...], preferred_element_type=F32) * sc_ref[...]
        out_ref[...] = out.astype(BF16)
        outt_ref[...] = out.T.astype(BF16)

    col = pl.BlockSpec((M, C), lambda g: (0, g))
    return pl.pallas_call(
        body, name=name, grid=(N_POOL_GROUPS,),
        in_specs=[col, pl.BlockSpec((None, C, C), lambda g: (g, 0, 0)), pl.BlockSpec((1, C), lambda g: (0, g))],
        out_specs=[col, col, pl.BlockSpec((C, M), lambda g: (g, 0))],
        out_shape=[jax.ShapeDtypeStruct((M, N_POOL_GROUPS * C), BF16),
                   jax.ShapeDtypeStruct((M, N_POOL_GROUPS * C), BF16),
                   jax.ShapeDtypeStruct((N_POOL_GROUPS * C, M), BF16)],
        compiler_params=_cparams("parallel"),
    )(z, pw, psc)


def _pool_bwd(dmix, pooled, pw, psc, name):
    M = dmix.shape[0]
    C = pw.shape[1]

    def body(dm_ref, pooled_ref, w_ref, sc_ref, dp_ref, dwb_ref, dsc_ref):
        g = pl.program_id(0)
        dmx = dm_ref[...]
        pb = pooled_ref[...]
        wv = w_ref[...]
        mixed = jnp.dot(pb, wv, preferred_element_type=F32)
        dsc_ref[...] = jnp.sum(dmx * mixed, axis=0, keepdims=True)
        dmixed = (dmx * sc_ref[...]).astype(BF16)
        dw = jnp.dot(pb.astype(F32).T.astype(BF16), dmixed, preferred_element_type=F32)
        dwb_ref[...] = dw.astype(BF16)
        dpooled = lax.dot_general(dmixed, wv, NT_DIMS, preferred_element_type=F32)
        t = lax.broadcasted_iota(jnp.int32, (M, 1), 0)
        cnt = jnp.minimum(t + 1, _group_window(g)).astype(F32)
        s = dpooled / cnt
        wsum = jnp.zeros_like(s)
        for step in range(N_POOL_GROUPS):
            sh = 1 << step
            s = s + jnp.where(t < M - sh, pltpu.roll(s, M - sh, 0), 0.0)
            wsum = jnp.where(g == step, s, wsum)
        dp_ref[...] = (wsum - dpooled).astype(BF16)

    col = pl.BlockSpec((M, C), lambda g: (0, g))
    wspec = pl.BlockSpec((None, C, C), lambda g: (g, 0, 0))
    vec = pl.BlockSpec((1, C), lambda g: (0, g))
    return pl.pallas_call(
        body, name=name, grid=(N_POOL_GROUPS,),
        in_specs=[col, col, wspec, vec],
        out_specs=[col, wspec, vec],
        out_shape=[jax.ShapeDtypeStruct((M, N_POOL_GROUPS * C), BF16),
                   jax.ShapeDtypeStruct((N_POOL_GROUPS, C, C), BF16),
                   jax.ShapeDtypeStruct((1, N_POOL_GROUPS * C), F32)],
        compiler_params=_cparams("parallel"),
    )(dmix, pooled, pw, psc)


def _qkv_prep(z, gq, gk, n_heads, q_col, name):
    M = z.shape[0]
    H = n_heads
    qb = q_col // HEAD_DIM

    def body(q_ref, k_ref, v_ref, gq_ref, gk_ref, qh_ref, kh_ref, vb_ref):
        def norm(xv, g):
            r = lax.rsqrt(jnp.mean(xv * xv, axis=-1, keepdims=True) + EPS)
            return (xv * r * g).astype(BF16)

        qh_ref[...] = norm(q_ref[...], gq_ref[...])
        kh_ref[...] = norm(k_ref[...], gk_ref[...])
        vb_ref[...] = v_ref[...].astype(BF16)

    vec = pl.BlockSpec((1, HEAD_DIM), lambda h: (0, 0))
    out = pl.BlockSpec((M, HEAD_DIM), lambda h: (0, h))
    oshape = jax.ShapeDtypeStruct((M, H * HEAD_DIM), BF16)
    return pl.pallas_call(
        body, name=name, grid=(H,),
        in_specs=[pl.BlockSpec((M, HEAD_DIM), lambda h: (0, qb + h)),
                  pl.BlockSpec((M, HEAD_DIM), lambda h: (0, qb + H + h)),
                  pl.BlockSpec((M, HEAD_DIM), lambda h: (0, qb + 2 * H + h)), vec, vec],
        out_specs=[out, out, out],
        out_shape=[oshape, oshape, oshape],
        compiler_params=_cparams("parallel"),
    )(z, z, z, gq, gk)


def _forget_fwd(z, bpad, f_block, name):
    M = z.shape[0]

    def body(f_ref, b_ref, cum_ref):
        xx = f_ref[...] + b_ref[...]
        c = jnp.minimum(xx, 0.0) - jnp.log(1.0 + jnp.exp(-jnp.abs(xx)))
        t = lax.broadcasted_iota(jnp.int32, (M, 1), 0)
        sh = 1
        while sh < M:
            c = c + jnp.where(t >= sh, pltpu.roll(c, sh, 0), 0.0)
            sh *= 2
        cum_ref[...] = c.T

    return pl.pallas_call(
        body, name=name, grid=(1,),
        in_specs=[pl.BlockSpec((M, LANES), lambda i: (0, f_block)), pl.BlockSpec((1, LANES), lambda i: (0, 0))],
        out_specs=pl.BlockSpec((LANES, M), lambda i: (0, 0)),
        out_shape=jax.ShapeDtypeStruct((LANES, M), F32),
        compiler_params=_cparams("arbitrary"),
    )(z, bpad)


def _col_to_row(col):
    n = col.shape[0]
    return jnp.transpose(jnp.broadcast_to(col, (n, LANES)))[0:1, :]


def _causal_extents(M):
    edges = list(range(0, M, CAUSAL_STEP)) + [M]
    return list(zip(edges[:-1], edges[1:]))


def _heads_per_step(n_heads):
    return 2 if n_heads % 2 == 0 else 1


def _attn_fwd(qh, kh, vb, cum_c, cum_r, name):
    M = qh.shape[0]
    H = qh.shape[1] // HEAD_DIM
    hp = _heads_per_step(H)
    scale = 1.0 / math.sqrt(HEAD_DIM)

    def body(q_ref, k_ref, v_ref, cq_ref, ck_ref, o_ref, ot_ref, lc_ref, lr_ref):
        i = pl.program_id(1)

        def compute(n):
            row = i * TQ + lax.broadcasted_iota(jnp.int32, (TQ, 1), 0)
            col = lax.broadcasted_iota(jnp.int32, (1, n), 1)
            for hh in range(hp):
                d0, d1 = hh * HEAD_DIM, (hh + 1) * HEAD_DIM
                s = lax.dot_general(q_ref[:, d0:d1], k_ref[0:n, d0:d1], NT_DIMS, preferred_element_type=F32) * scale
                s = s + (cq_ref[hh] - ck_ref[hh, :, 0:n])
                s = jnp.where(row >= col, s, NEG)
                m = jnp.max(s, axis=1, keepdims=True)
                p = jnp.exp(s - m)
                l = jnp.sum(p, axis=1, keepdims=True)
                pn = (p / l).astype(BF16)
                o = jnp.dot(pn, v_ref[0:n, d0:d1], preferred_element_type=F32)
                o_ref[:, d0:d1] = o.astype(BF16)
                ot_ref[d0:d1, :] = o.T.astype(BF16)
                lse = m + jnp.log(l)
                lc_ref[hh] = lse
                lr_ref[hh] = _col_to_row(lse)

        for lo, hi in _causal_extents(M):
            pl.when((i >= lo // TQ) & (i < hi // TQ))(functools.partial(compute, hi))

    full = pl.BlockSpec((M, hp * HEAD_DIM), lambda h, i: (0, h))
    tile = pl.BlockSpec((TQ, hp * HEAD_DIM), lambda h, i: (i, h))
    colv = pl.BlockSpec((hp, TQ, 1), lambda h, i: (h, i, 0))
    rowv_full = pl.BlockSpec((hp, 1, M), lambda h, i: (h, 0, 0))
    rowv = pl.BlockSpec((hp, 1, TQ), lambda h, i: (h, 0, i))
    return pl.pallas_call(
        body, name=name, grid=(H // hp, M // TQ),
        in_specs=[tile, full, full, colv, rowv_full],
        out_specs=[tile, pl.BlockSpec((hp * HEAD_DIM, TQ), lambda h, i: (h, i)), colv, rowv],
        out_shape=[jax.ShapeDtypeStruct((M, H * HEAD_DIM), BF16), jax.ShapeDtypeStruct((H * HEAD_DIM, M), BF16),
                   jax.ShapeDtypeStruct((H, M, 1), F32), jax.ShapeDtypeStruct((H, 1, M), F32)],
        compiler_params=_cparams("parallel", "parallel"),
    )(qh, kh, vb, cum_c, cum_r)


def _attn_bwd_q(qh, kh, vb, dob, cum_c, cum_r, lse_c, name):
    M = qh.shape[0]
    H = qh.shape[1] // HEAD_DIM
    hp = _heads_per_step(H)
    scale = 1.0 / math.sqrt(HEAD_DIM)

    def body(q_ref, k_ref, v_ref, do_ref, cq_ref, ck_ref, l_ref, dq_ref, dr_ref, dcq_ref):
        i = pl.program_id(1)

        def compute(n):
            row = i * TQ + lax.broadcasted_iota(jnp.int32, (TQ, 1), 0)
            col = lax.broadcasted_iota(jnp.int32, (1, n), 1)
            for hh in range(hp):
                d0, d1 = hh * HEAD_DIM, (hh + 1) * HEAD_DIM
                k = k_ref[0:n, d0:d1]
                s = lax.dot_general(q_ref[:, d0:d1], k, NT_DIMS, preferred_element_type=F32) * scale
                s = s + (cq_ref[hh] - ck_ref[hh, :, 0:n])
                p = jnp.exp(jnp.where(row >= col, s, NEG) - l_ref[hh])
                dp = lax.dot_general(do_ref[:, d0:d1], v_ref[0:n, d0:d1], NT_DIMS, preferred_element_type=F32)
                delta = jnp.sum(p * dp, axis=1, keepdims=True)
                ds = p * (dp - delta)
                dq_ref[:, d0:d1] = jnp.dot((ds * scale).astype(BF16), k, preferred_element_type=F32)
                dr_ref[hh] = _col_to_row(delta)
                dcq_ref[hh] = jnp.sum(ds, axis=1, keepdims=True)

        for lo, hi in _causal_extents(M):
            pl.when((i >= lo // TQ) & (i < hi // TQ))(functools.partial(compute, hi))

    full = pl.BlockSpec((M, hp * HEAD_DIM), lambda h, i: (0, h))
    tile = pl.BlockSpec((TQ, hp * HEAD_DIM), lambda h, i: (i, h))
    colv = pl.BlockSpec((hp, TQ, 1), lambda h, i: (h, i, 0))
    rowv_full = pl.BlockSpec((hp, 1, M), lambda h, i: (h, 0, 0))
    rowv = pl.BlockSpec((hp, 1, TQ), lambda h, i: (h, 0, i))
    return pl.pallas_call(
        body, name=name, grid=(H // hp, M // TQ),
        in_specs=[tile, full, full, tile, colv, rowv_full, colv],
        out_specs=[tile, rowv, colv],
        out_shape=[jax.ShapeDtypeStruct((M, H * HEAD_DIM), F32), jax.ShapeDtypeStruct((H, 1, M), F32),
                   jax.ShapeDtypeStruct((H, M, 1), F32)],
        compiler_params=_cparams("parallel", "parallel"),
    )(qh, kh, vb, dob, cum_c, cum_r, lse_c)


def _attn_bwd_kv(qh, kh, vb, dob, cum_c, cum_r, lse_r, delta_r, name):
    M = qh.shape[0]
    H = qh.shape[1] // HEAD_DIM
    hp = _heads_per_step(H)
    scale = 1.0 / math.sqrt(HEAD_DIM)

    def body(k_ref, v_ref, q_ref, do_ref, cq_ref, ck_ref, l_ref, d_ref, dk_ref, dv_ref, dck_ref):
        j = pl.program_id(1)

        def compute(q0):
            krow = j * TQ + lax.broadcasted_iota(jnp.int32, (TQ, 1), 0)
            qcol = q0 + lax.broadcasted_iota(jnp.int32, (1, M - q0), 1)
            for hh in range(hp):
                d0, d1 = hh * HEAD_DIM, (hh + 1) * HEAD_DIM
                q = q_ref[q0:M, d0:d1]
                do = do_ref[q0:M, d0:d1]
                st = lax.dot_general(k_ref[:, d0:d1], q, NT_DIMS, preferred_element_type=F32) * scale
                st = st + (cq_ref[hh, :, q0:M] - ck_ref[hh])
                pt = jnp.exp(jnp.where(qcol >= krow, st, NEG) - l_ref[hh, :, q0:M])
                dpt = lax.dot_general(v_ref[:, d0:d1], do, NT_DIMS, preferred_element_type=F32)
                dst = pt * (dpt - d_ref[hh, :, q0:M])
                dv_ref[:, d0:d1] = jnp.dot(pt.astype(BF16), do, preferred_element_type=F32).astype(BF16)
                dk_ref[:, d0:d1] = jnp.dot((dst * scale).astype(BF16), q, preferred_element_type=F32)
                dck_ref[hh] = -jnp.sum(dst, axis=1, keepdims=True)

        for lo, hi in _causal_extents(M):
            pl.when((j >= lo // TQ) & (j < hi // TQ))(functools.partial(compute, lo))

    full = pl.BlockSpec((M, hp * HEAD_DIM), lambda h, j: (0, h))
    tile = pl.BlockSpec((TQ, hp * HEAD_DIM), lambda h, j: (j, h))
    colv = pl.BlockSpec((hp, TQ, 1), lambda h, j: (h, j, 0))
    rowv_full = pl.BlockSpec((hp, 1, M), lambda h, j: (h, 0, 0))
    return pl.pallas_call(
        body, name=name, grid=(H // hp, M // TQ),
        in_specs=[tile, tile, full, full, rowv_full, colv, rowv_full, rowv_full],
        out_specs=[tile, tile, colv],
        out_shape=[jax.ShapeDtypeStruct((M, H * HEAD_DIM), F32), jax.ShapeDtypeStruct((M, H * HEAD_DIM), BF16),
                   jax.ShapeDtypeStruct((H, M, 1), F32)],
        compiler_params=_cparams("parallel", "parallel"),
    )(kh, vb, qh, dob, cum_r, cum_c, lse_r, delta_r)


def _qk_norm_bwd(dqh, dkh, z, gq, gk, n_heads, q_col, name):
    M = z.shape[0]
    H = n_heads
    qb = q_col // HEAD_DIM

    def body(dqh_ref, dkh_ref, q_ref, k_ref, gq_ref, gk_ref, dq_ref, dk_ref, dgq_ref, dgk_ref):
        h = pl.program_id(0)

        def one(dy, xv, g):
            r = lax.rsqrt(jnp.mean(xv * xv, axis=-1, keepdims=True) + EPS)
            w = dy * g
            c = jnp.mean(w * xv, axis=-1, keepdims=True)
            dx = r * w - xv * (r * r * r * c)
            return dx.astype(BF16), jnp.sum(dy * (xv * r), axis=0, keepdims=True)

        dq, dgq = one(dqh_ref[...], q_ref[...], gq_ref[...])
        dk, dgk = one(dkh_ref[...], k_ref[...], gk_ref[...])
        dq_ref[...] = dq
        dk_ref[...] = dk

        @pl.when(h == 0)
        def _():
            dgq_ref[...] = dgq
            dgk_ref[...] = dgk

        @pl.when(h > 0)
        def _():
            dgq_ref[...] += dgq
            dgk_ref[...] += dgk

    vec = pl.BlockSpec((1, HEAD_DIM), lambda h: (0, 0))
    head = pl.BlockSpec((M, HEAD_DIM), lambda h: (0, h))
    return pl.pallas_call(
        body, name=name, grid=(H,),
        in_specs=[head, head, pl.BlockSpec((M, HEAD_DIM), lambda h: (0, qb + h)),
                  pl.BlockSpec((M, HEAD_DIM), lambda h: (0, qb + H + h)), vec, vec],
        out_specs=[head, head, vec, vec],
        out_shape=[jax.ShapeDtypeStruct((M, H * HEAD_DIM), BF16), jax.ShapeDtypeStruct((M, H * HEAD_DIM), BF16),
                   jax.ShapeDtypeStruct((1, HEAD_DIM), F32), jax.ShapeDtypeStruct((1, HEAD_DIM), F32)],
        compiler_params=_cparams("arbitrary"),
    )(dqh, dkh, z, z, gq, gk)


def _forget_bwd(dcq, dck, z, bpad, f_block, name):
    H, M, _ = dcq.shape

    def body(dcq_ref, dck_ref, f_ref, b_ref, dfl_ref, db_ref):
        lane = lax.broadcasted_iota(jnp.int32, (1, LANES), 1)
        d = jnp.zeros((M, LANES), F32)
        for h in range(H):
            d = d + (dcq_ref[h] + dck_ref[h]) * (lane == h).astype(F32)
        t = lax.broadcasted_iota(jnp.int32, (M, 1), 0)
        sh = 1
        while sh < M:
            d = d + jnp.where(t < M - sh, pltpu.roll(d, M - sh, 0), 0.0)
            sh *= 2
        xx = f_ref[...] + b_ref[...]
        dfl = d * (1.0 / (1.0 + jnp.exp(xx)))
        dfl_ref[...] = dfl.astype(BF16)
        db_ref[...] = jnp.sum(dfl, axis=0, keepdims=True)

    colv = pl.BlockSpec((H, M, 1), lambda i: (0, 0, 0))
    return pl.pallas_call(
        body, name=name, grid=(1,),
        in_specs=[colv, colv, pl.BlockSpec((M, LANES), lambda i: (0, f_block)),
                  pl.BlockSpec((1, LANES), lambda i: (0, 0))],
        out_specs=[pl.BlockSpec((M, LANES), lambda i: (0, 0)), pl.BlockSpec((1, LANES), lambda i: (0, 0))],
        out_shape=[jax.ShapeDtypeStruct((M, LANES), BF16), jax.ShapeDtypeStruct((1, LANES), F32)],
        compiler_params=_cparams("arbitrary"),
    )(dcq, dck, z, bpad)


def _ffn_fwd(h, g, dep, get_weights, gate_first, tag):
    n, nt, r = _rmsnorm_fwd(h, g, dep, f"{tag}_norm")
    if gate_first:
        wg = get_weights(f"{tag}_gate", n)["wg"]
        a = _mm_nn(n, wg, 256, f"{tag}_gate")
        wu = get_weights(f"{tag}_up", a)["wu"]
        p, q, s, st = _ffn_up_from_gate(n, a, wu, 256, f"{tag}_up")
    else:
        up = get_weights(f"{tag}_up", n)
        wg, wu = up["wg"], up["wu"]
        p, q, s, st = _ffn_up(n, wg, wu, n, 256, f"{tag}_up")
    wd = get_weights(f"{tag}_down", s)["wd"]
    h_out = _mm_nn_residual(s, wd, h, s, 0.5, 256, f"{tag}_down")
    return h_out, (nt, r, p, q, st, wg, wu, wd)


def _ffn_bwd(dh, dhb, h, g, saved, dep, put_grads, tag):
    nt, r, p, q, st, wg, wu, wd = saved
    n_shards = 4
    da, db = _ffn_bwd_hidden(dhb, wd, p, q, dep, 256, f"{tag}_bwd_hidden")
    dwd = _mm_tn(st, dhb, dep, 0.5, st.shape[0] // n_shards, 1024, f"{tag}_dw_down", stacked=False)
    dep = put_grads(f"{tag}_w_down", dwd)
    dwg = _mm_tn(nt, da, dep, 1.0, 1024, wg.shape[1] // n_shards, f"{tag}_dw_gate", stacked=True)
    dep = put_grads(f"{tag}_w_gate", dwg)
    dwu = _mm_tn(nt, db, dep, 1.0, 1024, wu.shape[1] // n_shards, f"{tag}_dw_up", stacked=True)
    dep = put_grads(f"{tag}_w_up", dwu)
    dn = _mm_nt_sum([da, db], [wg, wu], dep, 512, wg.shape[1] // 4, f"{tag}_dn")
    dh_in, dhb_in, dg = _rmsnorm_bwd(dn, h, r, g, dh, f"{tag}_norm_bwd")
    return dh_in, dhb_in, dg


def _local_step(x, target, S, get_weights, request, put_grads):
    seq, D = x.shape
    L = N_META + seq
    Lp = -(-L // SEQ_ALIGN) * SEQ_ALIGN
    pad = jnp.zeros((Lp - L, D), F32)
    tgt = jnp.concatenate([jnp.zeros((N_META, D), F32), target, pad], axis=0)

    d_pool = S["pool_scale"].shape[1]
    n_heads = S["b_forget"].shape[1]
    d_att = n_heads * HEAD_DIM
    f_col = d_pool + 3 * d_att
    f_block = f_col // LANES
    bpad = jnp.pad(S["b_forget"], ((0, 0), (0, LANES - n_heads)))

    h0 = jnp.concatenate([get_weights("meta", None)["meta"], x, pad], axis=0)
    h1, ffn1 = _ffn_fwd(h0, S["ffn1_norm"], h0, get_weights, True, "ffn1")
    u, ut, r_mix = _rmsnorm_fwd(h1, S["mix_norm"], request("mix_in", h1), "mix_norm")
    Wm = get_weights("mix_in", u)
    z = _mm_nn(u, Wm["win"], 384, "in_proj")
    pooled, pool_out, pool_out_t = _pool_fwd(z, Wm["pool_w"], S["pool_scale"], "pool_fwd")
    qh, kh, vb = _qkv_prep(z, S["q_norm"], S["k_norm"], n_heads, d_pool, "qkv_prep")
    cum_t = _forget_fwd(z, bpad, f_block, "forget_fwd")[:n_heads]
    cum_c = cum_t.reshape(n_heads, Lp, 1)
    cum_r = cum_t.reshape(n_heads, 1, Lp)
    att, att_t, lse_c, lse_r = _attn_fwd(qh, kh, vb, cum_c, cum_r, "attn_fwd")
    mix = jnp.concatenate([pool_out, att], axis=1)
    mix_t = jnp.concatenate([pool_out_t, att_t], axis=0)
    Wm.update(get_weights("mix_out", att))
    dep = request("ffn2_up", att)
    h2 = _mm_nn_residual(mix, Wm["wout"], h1, dep, 1.0, 512, "out_proj")
    h3, ffn2 = _ffn_fwd(h2, S["ffn2_norm"], h2, get_weights, False, "ffn2")

    dh3, dh3b, loss = _loss_grad(h3, tgt, seq, "loss")
    dh2, dh2b, dg_ffn2 = _ffn_bwd(dh3, dh3b, h2, S["ffn2_norm"], ffn2, loss, put_grads, "ffn2")

    dmix = _mm_nt(dh2b, Wm["wout"], loss, 512, "out_proj_bwd")
    dwout = _mm_tn(mix_t, dh2b, loss, 1.0, 1024, 1024, "dw_out", stacked=False)
    dp, dpw, dpsc = _pool_bwd(dmix, pooled, Wm["pool_w"], S["pool_scale"], "pool_bwd")
    dob = dmix[:, d_pool:].astype(BF16)
    dqh, delta_r, dcq = _attn_bwd_q(qh, kh, vb, dob, cum_c, cum_r, lse_c, "attn_bwd_q")
    dkh, dv, dck = _attn_bwd_kv(qh, kh, vb, dob, cum_c, cum_r, lse_r, delta_r, "attn_bwd_kv")
    dq, dk, dgq, dgk = _qk_norm_bwd(dqh, dkh, z, S["q_norm"], S["k_norm"], n_heads, d_pool, "qk_norm_bwd")
    dfl, dbf = _forget_bwd(dcq, dck, z, bpad, f_block, "forget_bwd")
    dz = jnp.concatenate([dp, dq, dk, dv, dfl], axis=1)
    dwin = _mm_tn(ut, dz, loss, 1.0, 1024, dz.shape[1] // 3, "dw_in", stacked=False)
    dep = put_grads("mix", dict(win=dwin, wout=dwout, pool_w=dpw))
    du = _mm_nt_sum([dz], [Wm["win"]], dep, 512, Wm["win"].shape[1] // 3, "in_proj_bwd")
    dh1, dh1b, dg_mix = _rmsnorm_bwd(du, h1, r_mix, S["mix_norm"], dh2, "mix_norm_bwd")

    dh0, _, dg_ffn1 = _ffn_bwd(dh1, dh1b, h0, S["ffn1_norm"], ffn1, loss, put_grads, "ffn1")

    grads = dict(
        x=dh0[N_META:L], meta=dh0[:N_META],
        ffn1_norm=dg_ffn1, mix_norm=dg_mix, ffn2_norm=dg_ffn2, q_norm=dgq, k_norm=dgk,
        b_forget=dbf[:, :n_heads], pool_scale=dpsc,
    )
    return loss[0, 0], dh0, grads


HBM_SPEC = pl.BlockSpec(memory_space=pltpu.HBM)
N_CHIPS = 4


def _chip_peers():
    x, y, c = lax.axis_index("x"), lax.axis_index("y"), lax.axis_index("c")
    flips = [(1 - x, y), (x, 1 - y), (1 - x, 1 - y)]
    return 2 * x + y, [((px, py, c), 2 * px + py) for px, py in flips]


def _gathered_shape(shape, layout):
    if layout == "rows":
        return (N_CHIPS * shape[0],) + shape[1:]
    if layout == "cols":
        return (shape[0], N_CHIPS * shape[1])
    return (N_CHIPS,) + shape


def _cast_place(place, w, dep, layout, dtype, name):
    R, C = w.shape
    tr = _row_tile(R, C)
    nt = R // tr

    def body(place_ref, w_ref, dep_ref, o_ref):
        o_ref[...] = w_ref[...].astype(dtype)

    if layout == "rows":
        ospec = pl.BlockSpec((tr, C), lambda i, p: (p[1] * nt + i, 0))
    elif layout == "cols":
        ospec = pl.BlockSpec((tr, C), lambda i, p: (i, p[1]))
    else:
        ospec = pl.BlockSpec((None, tr, C), lambda i, p: (p[1], i, 0))
    return pl.pallas_call(
        body, name=name,
        grid_spec=pltpu.PrefetchScalarGridSpec(
            num_scalar_prefetch=1, grid=(nt,),
            in_specs=[pl.BlockSpec((tr, C), lambda i, p: (i, 0)), pl.BlockSpec(memory_space=pl.ANY)],
            out_specs=ospec),
        out_shape=jax.ShapeDtypeStruct(_gathered_shape((R, C), layout), dtype),
        compiler_params=_cparams("parallel"),
    )(place, w, dep)


SEM_SPEC = pl.BlockSpec(memory_space=pltpu.SEMAPHORE)
ANY_SPEC = pl.BlockSpec(memory_space=pl.ANY)
SPLIT_COPY = pltpu.CompilerParams(has_side_effects=pltpu.SideEffectType.DATAFLOW_SIDE_EFFECTING)


def _hbm(a):
    return pltpu.with_memory_space_constraint(a, pltpu.HBM)


def _gather_region(refs, shard_shapes, layouts, a, chip, half):
    rows_a = shard_shapes[a][0]
    h = rows_a // 2
    if layouts[a] == "rows":
        return refs[a].at[pl.ds(chip * rows_a + half * h, h)]
    if layouts[a] == "cols":
        cols_a = shard_shapes[a][1]
        return refs[a].at[pl.ds(half * h, h), pl.ds(chip * cols_a, cols_a)]
    return refs[a].at[chip, pl.ds(half * h, h)]


def _gather_start(bufs, after, shard_shapes, layouts, name):
    n = len(bufs)
    ns = 3 * n

    def body(*refs):
        in_refs = refs[:n]
        send_sems = refs[n + 1:n + 1 + ns]
        recv_sems = refs[n + 1 + ns:n + 1 + 2 * ns]
        token = refs[2 * n + 1 + 2 * ns]
        c = lax.axis_index("c")
        me, peers = _chip_peers()
        for a in range(n):
            mine = _gather_region(in_refs, shard_shapes, layouts, a, me, c)
            for k, (dev, _) in enumerate(peers):
                pltpu.make_async_remote_copy(
                    src_ref=mine, dst_ref=mine, send_sem=send_sems[3 * a + k], recv_sem=recv_sems[3 * a + k],
                    device_id=dev, device_id_type=MESH).start()
        token[...] = jnp.zeros_like(token)

    sem = pltpu.SemaphoreType.DMA(())
    out = pl.pallas_call(
        body, name=name,
        out_shape=(*[sem] * (2 * ns), *[pltpu.HBM(b.shape, b.dtype) for b in bufs],
                   jax.ShapeDtypeStruct((8, LANES), F32)),
        in_specs=[HBM_SPEC] * n + [ANY_SPEC],
        out_specs=(*[SEM_SPEC] * (2 * ns), *[HBM_SPEC] * n, pl.BlockSpec(memory_space=pltpu.VMEM)),
        input_output_aliases={a: 2 * ns + a for a in range(n)},
        compiler_params=SPLIT_COPY,
    )(*[_hbm(b) for b in bufs], after)
    return list(out[:ns]), list(out[ns:2 * ns]), list(out[2 * ns:2 * ns + n]), out[2 * ns + n]


def _gather_wait(bufs, send_sems, recv_sems, afters, shard_shapes, layouts, name):
    n = len(bufs)
    ns = 3 * n
    na = len(afters)

    def body(*refs):
        in_refs = refs[:n]
        send_sems = refs[n:n + ns]
        recv_sems = refs[n + ns:n + 2 * ns]
        token = refs[2 * n + 2 * ns + na]
        token[...] = jnp.zeros_like(token)
        c = lax.axis_index("c")
        me, peers = _chip_peers()
        for a in range(n):
            mine = _gather_region(in_refs, shard_shapes, layouts, a, me, c)
            for k, (dev, pidx) in enumerate(peers):
                landed = _gather_region(in_refs, shard_shapes, layouts, a, pidx, c)
                pltpu.make_async_remote_copy(
                    src_ref=mine, dst_ref=landed, send_sem=send_sems[3 * a + k], recv_sem=recv_sems[3 * a + k],
                    device_id=dev, device_id_type=MESH).wait_recv()
        for a in range(n):
            mine = _gather_region(in_refs, shard_shapes, layouts, a, me, c)
            for k, (dev, _) in enumerate(peers):
                pltpu.make_async_remote_copy(
                    src_ref=mine, dst_ref=mine, send_sem=send_sems[3 * a + k], recv_sem=recv_sems[3 * a + k],
                    device_id=dev, device_id_type=MESH).wait_send()

    out = pl.pallas_call(
        body, name=name,
        out_shape=(*[pltpu.HBM(b.shape, b.dtype) for b in bufs], jax.ShapeDtypeStruct((8, LANES), F32)),
        in_specs=[HBM_SPEC] * n + [SEM_SPEC] * (2 * ns) + [ANY_SPEC] * na,
        out_specs=(*[HBM_SPEC] * n, pl.BlockSpec(memory_space=pltpu.VMEM)),
        input_output_aliases={a: a for a in range(n)},
        compiler_params=SPLIT_COPY,
    )(*bufs, *send_sems, *recv_sems, *afters)
    return list(out[:n]), out[n]


def _forward_start(bufs, after, shard_shapes, layouts, name):
    n = len(bufs)
    ns = 3 * n

    def body(*refs):
        in_refs = refs[:n]
        send_sems = refs[n + 1:n + 1 + ns]
        recv_sems = refs[n + 1 + ns:n + 1 + 2 * ns]
        token = refs[2 * n + 1 + 2 * ns]
        c = lax.axis_index("c")
        sib = (lax.axis_index("x"), lax.axis_index("y"), 1 - c)
        _, peers = _chip_peers()
        for a in range(n):
            for k, (_, pidx) in enumerate(peers):
                landed = _gather_region(in_refs, shard_shapes, layouts, a, pidx, c)
                pltpu.make_async_remote_copy(
                    src_ref=landed, dst_ref=landed, send_sem=send_sems[3 * a + k], recv_sem=recv_sems[3 * a + k],
                    device_id=sib, device_id_type=MESH).start()
        token[...] = jnp.zeros_like(token)

    sem = pltpu.SemaphoreType.DMA(())
    out = pl.pallas_call(
        body, name=name,
        out_shape=(*[sem] * (2 * ns), *[pltpu.HBM(b.shape, b.dtype) for b in bufs],
                   jax.ShapeDtypeStruct((8, LANES), F32)),
        in_specs=[HBM_SPEC] * n + [ANY_SPEC],
        out_specs=(*[SEM_SPEC] * (2 * ns), *[HBM_SPEC] * n, pl.BlockSpec(memory_space=pltpu.VMEM)),
        input_output_aliases={a: 2 * ns + a for a in range(n)},
        compiler_params=SPLIT_COPY,
    )(*[_hbm(b) for b in bufs], after)
    return list(out[:ns]), list(out[ns:2 * ns]), list(out[2 * ns:2 * ns + n]), out[2 * ns + n]


def _forward_wait(bufs, send_sems, recv_sems, after, shard_shapes, layouts, name):
    n = len(bufs)
    ns = 3 * n

    def body(*refs):
        in_refs = refs[:n]
        send_sems = refs[n:n + ns]
        recv_sems = refs[n + ns:n + 2 * ns]
        c = lax.axis_index("c")
        sib = (lax.axis_index("x"), lax.axis_index("y"), 1 - c)
        _, peers = _chip_peers()
        for a in range(n):
            for k, (_, pidx) in enumerate(peers):
                landed = _gather_region(in_refs, shard_shapes, layouts, a, pidx, c)
                other = _gather_region(in_refs, shard_shapes, layouts, a, pidx, 1 - c)
                cp = pltpu.make_async_remote_copy(
                    src_ref=landed, dst_ref=other, send_sem=send_sems[3 * a + k], recv_sem=recv_sems[3 * a + k],
                    device_id=sib, device_id_type=MESH)
                cp.wait_recv()
                cp.wait_send()

    return list(pl.pallas_call(
        body, name=name,
        out_shape=tuple(pltpu.HBM(b.shape, b.dtype) for b in bufs),
        in_specs=[HBM_SPEC] * n + [SEM_SPEC] * (2 * ns) + [ANY_SPEC],
        out_specs=tuple([HBM_SPEC] * n),
        input_output_aliases={a: a for a in range(n)},
        compiler_params=SPLIT_COPY,
    )(*bufs, *send_sems, *recv_sems, after))


def _halves_copies(src_refs, land_refs, send_sems, recv_sems):
    c = lax.axis_index("c")
    sib = (lax.axis_index("x"), lax.axis_index("y"), 1 - c)
    copies = []
    for a, (src, land) in enumerate(zip(src_refs, land_refs)):
        h = src.shape[1] // 2
        copies.append(pltpu.make_async_remote_copy(
            src_ref=src.at[:, pl.ds((1 - c) * h, h)], dst_ref=land, send_sem=send_sems[a], recv_sem=recv_sems[a],
            device_id=sib, device_id_type=MESH))
    return copies


def _whole_copies(src_refs, land_refs, send_sems, recv_sems):
    sib = (lax.axis_index("x"), lax.axis_index("y"), 1 - lax.axis_index("c"))
    return [pltpu.make_async_remote_copy(src_ref=src, dst_ref=land, send_sem=send_sems[a], recv_sem=recv_sems[a],
                                         device_id=sib, device_id_type=MESH)
            for a, (src, land) in enumerate(zip(src_refs, land_refs))]


def _halves_land_shape(shape):
    return (shape[0], shape[1] // 2, shape[2])


def _sibling_start(stacked, copies, land_shape, name):
    n = len(stacked)
    lands = [lax.empty(land_shape(s.shape), s.dtype) for s in stacked]

    def body(*refs):
        for cp in copies(refs[:n], refs[n:2 * n], refs[2 * n:3 * n], refs[3 * n:4 * n]):
            cp.start()
        token = refs[6 * n]
        token[...] = jnp.zeros_like(token)

    sem = pltpu.SemaphoreType.DMA(())
    out = pl.pallas_call(
        body, name=name,
        out_shape=(*[sem] * (2 * n), *[pltpu.HBM(b.shape, b.dtype) for b in stacked],
                   *[pltpu.HBM(b.shape, b.dtype) for b in lands], jax.ShapeDtypeStruct((8, LANES), F32)),
        in_specs=[HBM_SPEC] * (2 * n),
        out_specs=(*[SEM_SPEC] * (2 * n), *[HBM_SPEC] * (2 * n), pl.BlockSpec(memory_space=pltpu.VMEM)),
        input_output_aliases={a: 2 * n + a for a in range(2 * n)},
        compiler_params=SPLIT_COPY,
    )(*[_hbm(b) for b in stacked], *[_hbm(b) for b in lands])
    return list(out[:n]), list(out[n:2 * n]), list(out[2 * n:3 * n]), list(out[3 * n:4 * n]), out[4 * n]


def _sibling_wait(srcs, lands, send_sems, recv_sems, after, copies_of, name):
    n = len(srcs)

    def body(*refs):
        copies = copies_of(refs[:n], refs[n:2 * n], refs[2 * n:3 * n], refs[3 * n:4 * n])
        for cp in copies:
            cp.wait_recv()
        for cp in copies:
            cp.wait_send()

    out = pl.pallas_call(
        body, name=name,
        out_shape=tuple(pltpu.HBM(b.shape, b.dtype) for b in list(srcs) + list(lands)),
        in_specs=[HBM_SPEC] * (2 * n) + [SEM_SPEC] * (2 * n) + [ANY_SPEC],
        out_specs=tuple([HBM_SPEC] * (2 * n)),
        input_output_aliases={a: a for a in range(2 * n)},
        compiler_params=SPLIT_COPY,
    )(*srcs, *lands, *send_sems, *recv_sems, after)
    return list(out[:n]), list(out[n:])


def _scatter_start(stacked, name):
    n = len(stacked)
    ns = 3 * n
    lands = [lax.empty((3,) + s.shape[1:], s.dtype) for s in stacked]

    def body(*refs):
        src_refs = refs[:n]
        land_refs = refs[n:2 * n]
        send_sems = refs[2 * n:2 * n + ns]
        recv_sems = refs[2 * n + ns:2 * n + 2 * ns]
        token = refs[4 * n + 2 * ns]
        _, peers = _chip_peers()
        for a in range(n):
            for k, (dev, pidx) in enumerate(peers):
                pltpu.make_async_remote_copy(
                    src_ref=src_refs[a].at[k], dst_ref=land_refs[a].at[k], send_sem=send_sems[3 * a + k],
                    recv_sem=recv_sems[3 * a + k], device_id=dev, device_id_type=MESH).start()
        token[...] = jnp.zeros_like(token)

    sem = pltpu.SemaphoreType.DMA(())
    out = pl.pallas_call(
        body, name=name,
        out_shape=(*[sem] * (2 * ns), *[pltpu.HBM(b.shape, b.dtype) for b in stacked],
                   *[pltpu.HBM(b.shape, b.dtype) for b in lands], jax.ShapeDtypeStruct((8, LANES), F32)),
        in_specs=[HBM_SPEC] * (2 * n),
        out_specs=(*[SEM_SPEC] * (2 * ns), *[HBM_SPEC] * (2 * n), pl.BlockSpec(memory_space=pltpu.VMEM)),
        input_output_aliases={a: 2 * ns + a for a in range(2 * n)},
        compiler_params=SPLIT_COPY,
    )(*[_hbm(b) for b in stacked], *[_hbm(b) for b in lands])
    o = 2 * ns
    return list(out[:ns]), list(out[ns:o]), list(out[o:o + n]), list(out[o + n:o + 2 * n]), out[o + 2 * n]


def _scatter_wait(srcs, lands, send_sems, recv_sems, after, name):
    n = len(srcs)
    ns = 3 * n

    def body(*refs):
        src_refs = refs[:n]
        land_refs = refs[n:2 * n]
        send_sems = refs[2 * n:2 * n + ns]
        recv_sems = refs[2 * n + ns:2 * n + 2 * ns]
        _, peers = _chip_peers()
        copies = [
            pltpu.make_async_remote_copy(
                src_ref=src_refs[a].at[k], dst_ref=land_refs[a].at[k], send_sem=send_sems[3 * a + k],
                recv_sem=recv_sems[3 * a + k], device_id=dev, device_id_type=MESH)
            for a in range(n) for k, (dev, pidx) in enumerate(peers)]
        for cp in copies:
            cp.wait_recv()
        for cp in copies:
            cp.wait_send()

    out = pl.pallas_call(
        body, name=name,
        out_shape=tuple(pltpu.HBM(b.shape, b.dtype) for b in list(srcs) + list(lands)),
        in_specs=[HBM_SPEC] * (2 * n) + [SEM_SPEC] * (2 * ns) + [ANY_SPEC],
        out_specs=tuple([HBM_SPEC] * (2 * n)),
        input_output_aliases={a: a for a in range(2 * n)},
        compiler_params=SPLIT_COPY,
    )(*srcs, *lands, *send_sems, *recv_sems, after)
    return list(out[n:])


def _all_reduce_small(v):
    R, C = v.shape
    n_dev = 8

    def body(v_ref, o_ref, buf, send_sems, recv_sems):
        x, y, c = lax.axis_index("x"), lax.axis_index("y"), lax.axis_index("c")
        me = 4 * x + 2 * y + c
        buf[me] = v_ref[...]
        sends = []
        for k in range(1, n_dev):
            px, py, pc = x ^ ((k >> 2) & 1), y ^ ((k >> 1) & 1), c ^ (k & 1)
            cp = pltpu.make_async_remote_copy(
                src_ref=v_ref, dst_ref=buf.at[me], send_sem=send_sems.at[k - 1], recv_sem=recv_sems.at[k - 1],
                device_id=(px, py, pc), device_id_type=MESH)
            cp.start()
            sends.append((cp, 4 * px + 2 * py + pc))
        for k in range(1, n_dev):
            cp, pidx = sends[k - 1]
            pltpu.make_async_remote_copy(
                src_ref=v_ref, dst_ref=buf.at[pidx], send_sem=send_sems.at[k - 1], recv_sem=recv_sems.at[k - 1],
                device_id=(x, y, c), device_id_type=MESH).wait_recv()
        for cp, _ in sends:
            cp.wait_send()
        acc = buf[0]
        for d in range(1, n_dev):
            acc = acc + buf[d]
        o_ref[...] = acc

    vm = pl.BlockSpec(memory_space=pltpu.VMEM)
    return pl.pallas_call(
        body, name="all_reduce_small",
        in_specs=[vm], out_specs=vm,
        out_shape=jax.ShapeDtypeStruct((R, C), F32),
        scratch_shapes=[pltpu.VMEM((n_dev, R, C), F32), pltpu.SemaphoreType.DMA((n_dev - 1,)),
                        pltpu.SemaphoreType.DMA((n_dev - 1,))],
    )(v)


def _pair_sum(place, own, sib, name):
    S, R, C = own.shape
    h = R // 2
    tr = _row_tile(h, C, own.dtype.itemsize)
    nt = h // tr

    def body(place_ref, o_ref, s_ref, out_ref):
        out_ref[...] = (o_ref[...].astype(F32) + s_ref[...].astype(F32)).astype(BF16)

    return pl.pallas_call(
        body, name=name,
        grid_spec=pltpu.PrefetchScalarGridSpec(
            num_scalar_prefetch=1, grid=(3, nt),
            in_specs=[pl.BlockSpec((None, tr, C), lambda k, i, p: (p[2 + k], p[0] * nt + i, 0)),
                      pl.BlockSpec((None, tr, C), lambda k, i, p: (p[2 + k], i, 0))],
            out_specs=pl.BlockSpec((None, tr, C), lambda k, i, p: (k, i, 0))),
        out_shape=jax.ShapeDtypeStruct((3, h, C), BF16),
        compiler_params=_cparams("parallel", "parallel"),
    )(place, own, sib)


def _sum_slabs(place, own, sib, recv, name):
    S, R, C = own.shape
    h = R // 2
    tr = _row_tile(h, C)
    nt = h // tr

    def body(place_ref, o_ref, s_ref, r_ref, out_ref):
        acc = o_ref[...].astype(F32) + s_ref[...].astype(F32)
        for k in range(3):
            acc = acc + r_ref[k].astype(F32)
        out_ref[...] = acc

    return pl.pallas_call(
        body, name=name,
        grid_spec=pltpu.PrefetchScalarGridSpec(
            num_scalar_prefetch=1, grid=(nt,),
            in_specs=[pl.BlockSpec((None, tr, C), lambda i, p: (p[1], p[0] * nt + i, 0)),
                      pl.BlockSpec((None, tr, C), lambda i, p: (p[1], i, 0)),
                      pl.BlockSpec((3, tr, C), lambda i, p: (0, i, 0))],
            out_specs=pl.BlockSpec((tr, C), lambda i, p: (i, 0))),
        out_shape=jax.ShapeDtypeStruct((h, C), F32),
        compiler_params=_cparams("parallel"),
    )(place, own, sib, recv)


def _adamw(parts, w, m, v, name):
    R, C = w.shape
    tr = _row_tile(R, C)
    npart = len(parts)
    c1 = 1.0 - ADAM_B1 ** ADAM_STEP
    c2 = 1.0 - ADAM_B2 ** ADAM_STEP

    def body(*refs):
        p_refs = refs[:npart]
        w_ref, m_ref, v_ref, g_ref, d_ref, nm_ref, nv_ref = refs[npart:]
        g = p_refs[0][...]
        for p_ref in p_refs[1:]:
            g = g + p_ref[...]
        nm = ADAM_B1 * m_ref[...] + (1.0 - ADAM_B1) * g
        nv = ADAM_B2 * v_ref[...] + (1.0 - ADAM_B2) * (g * g)
        m_hat = nm / c1
        v_hat = nv / c2
        g_ref[...] = g
        d_ref[...] = -ADAM_LR * (m_hat / (jnp.sqrt(v_hat) + ADAM_EPS) + ADAM_WD * w_ref[...])
        nm_ref[...] = nm
        nv_ref[...] = nv

    blk = pl.BlockSpec((tr, C), lambda i: (i, 0))
    shape = jax.ShapeDtypeStruct((R, C), F32)
    return pl.pallas_call(
        body, name=name, grid=(R // tr,),
        in_specs=[blk] * (npart + 3), out_specs=[blk] * 4, out_shape=[shape] * 4,
        compiler_params=_cparams("parallel"),
    )(*parts, w, m, v)


def _adamw_halves(place, mine, other, w, m, v, name):
    R, C = w.shape
    h = R // 2
    tr = _row_tile(h, C)
    nt = h // tr
    c1 = 1.0 - ADAM_B1 ** ADAM_STEP
    c2 = 1.0 - ADAM_B2 ** ADAM_STEP

    def body(place_ref, mine_ref, other_ref, w_ref, m_ref, v_ref, g_ref, d_ref, nm_ref, nv_ref):
        is_mine = (pl.program_id(0) // nt) == place_ref[0]
        g = jnp.where(is_mine, mine_ref[...], other_ref[...])
        nm = ADAM_B1 * m_ref[...] + (1.0 - ADAM_B1) * g
        nv = ADAM_B2 * v_ref[...] + (1.0 - ADAM_B2) * (g * g)
        m_hat = nm / c1
        v_hat = nv / c2
        g_ref[...] = g
        d_ref[...] = -ADAM_LR * (m_hat / (jnp.sqrt(v_hat) + ADAM_EPS) + ADAM_WD * w_ref[...])
        nm_ref[...] = nm
        nv_ref[...] = nv

    def half_block(which):
        def index(i, p):
            first = p[0] if which == 0 else 1 - p[0]
            return jnp.clip(i - first * nt, 0, nt - 1), 0

        return pl.BlockSpec((tr, C), index)

    blk = pl.BlockSpec((tr, C), lambda i, p: (i, 0))
    shape = jax.ShapeDtypeStruct((R, C), F32)
    return pl.pallas_call(
        body, name=name,
        grid_spec=pltpu.PrefetchScalarGridSpec(
            num_scalar_prefetch=1, grid=(2 * nt,),
            in_specs=[half_block(0), half_block(1), blk, blk, blk], out_specs=[blk] * 4),
        out_shape=[shape] * 4,
        compiler_params=_cparams("parallel"),
    )(place, mine, other, w, m, v)


SMALL_NAMES = ("ffn1_norm", "mix_norm", "ffn2_norm", "pool_scale", "q_norm", "k_norm", "b_forget")
SMALL_COLS = 1024
LOSS_LANE = 512


def _pack_small(vals):
    rows = [vals[n].reshape(-1, SMALL_COLS) for n in ("ffn1_norm", "mix_norm", "ffn2_norm", "pool_scale")]
    tail = jnp.concatenate([vals["q_norm"].reshape(-1), vals["k_norm"].reshape(-1), vals["b_forget"].reshape(-1)])
    rows.append(jnp.pad(tail, (0, SMALL_COLS - tail.shape[0])).reshape(1, SMALL_COLS))
    return jnp.concatenate(rows, axis=0)


def _unpack_small(packed, like):
    out = {}
    r = 0
    for n in ("ffn1_norm", "mix_norm", "ffn2_norm", "pool_scale"):
        k = like[n].size // SMALL_COLS
        out[n] = packed[r:r + k].reshape(like[n].shape)
        r += k
    o = 0
    for n in ("q_norm", "k_norm", "b_forget"):
        k = like[n].size
        out[n] = packed[r, o:o + k].reshape(like[n].shape)
        o += k
    return out


def kernel(x, meta_tokens, ffn1_norm, ffn1_w_gate, ffn1_w_up, ffn1_w_down, mix_norm, w_in, b_forget, q_norm, k_norm, pool_w, pool_scale, w_out, ffn2_norm, ffn2_w_gate, ffn2_w_up, ffn2_w_down, loss_target, m_meta_tokens, m_ffn1_norm, m_ffn1_w_gate, m_ffn1_w_up, m_ffn1_w_down, m_mix_norm, m_w_in, m_b_forget, m_q_norm, m_k_norm, m_pool_w, m_pool_scale, m_w_out, m_ffn2_norm, m_ffn2_w_gate, m_ffn2_w_up, m_ffn2_w_down, v_meta_tokens, v_ffn1_norm, v_ffn1_w_gate, v_ffn1_w_up, v_ffn1_w_down, v_mix_norm, v_w_in, v_b_forget, v_q_norm, v_k_norm, v_pool_w, v_pool_scale, v_w_out, v_ffn2_norm, v_ffn2_w_gate, v_ffn2_w_up, v_ffn2_w_down):
    wts = dict(meta_tokens=meta_tokens, ffn1_norm=ffn1_norm, ffn1_w_gate=ffn1_w_gate, ffn1_w_up=ffn1_w_up,
               ffn1_w_down=ffn1_w_down, mix_norm=mix_norm, w_in=w_in, b_forget=b_forget, q_norm=q_norm,
               k_norm=k_norm, pool_w=pool_w, pool_scale=pool_scale, w_out=w_out, ffn2_norm=ffn2_norm,
               ffn2_w_gate=ffn2_w_gate, ffn2_w_up=ffn2_w_up, ffn2_w_down=ffn2_w_down)
    mom = dict(meta_tokens=m_meta_tokens, ffn1_norm=m_ffn1_norm, ffn1_w_gate=m_ffn1_w_gate, ffn1_w_up=m_ffn1_w_up,
               ffn1_w_down=m_ffn1_w_down, mix_norm=m_mix_norm, w_in=m_w_in, b_forget=m_b_forget, q_norm=m_q_norm,
               k_norm=m_k_norm, pool_w=m_pool_w, pool_scale=m_pool_scale, w_out=m_w_out, ffn2_norm=m_ffn2_norm,
               ffn2_w_gate=m_ffn2_w_gate, ffn2_w_up=m_ffn2_w_up, ffn2_w_down=m_ffn2_w_down)
    var = dict(meta_tokens=v_meta_tokens, ffn1_norm=v_ffn1_norm, ffn1_w_gate=v_ffn1_w_gate, ffn1_w_up=v_ffn1_w_up,
               ffn1_w_down=v_ffn1_w_down, mix_norm=v_mix_norm, w_in=v_w_in, b_forget=v_b_forget, q_norm=v_q_norm,
               k_norm=v_k_norm, pool_w=v_pool_w, pool_scale=v_pool_scale, w_out=v_w_out, ffn2_norm=v_ffn2_norm,
               ffn2_w_gate=v_ffn2_w_gate, ffn2_w_up=v_ffn2_w_up, ffn2_w_down=v_ffn2_w_down)
    order = list(wts)
    me = 2 * lax.axis_index("x") + lax.axis_index("y")

    D = x.shape[2]
    d_in_shard = w_in.shape[2]
    d_in = N_CHIPS * d_in_shard
    n_heads = b_forget.shape[1]
    d_in_pad = (d_in - n_heads) + LANES

    stages = dict(meta=("meta_tokens",), ffn1_gate=("ffn1_w_gate",), ffn1_up=("ffn1_w_up",),
                  ffn1_down=("ffn1_w_down",),
                  mix_in=("w_in", "pool_w"), mix_out=("w_out",),
                  ffn2_up=("ffn2_w_gate", "ffn2_w_up"), ffn2_down=("ffn2_w_down",))
    stage_order = list(stages)
    xi, yi = lax.axis_index("x"), lax.axis_index("y")
    place = jnp.stack([lax.axis_index("c"), me, 2 * (1 - xi) + yi, 2 * xi + 1 - yi, 2 * (1 - xi) + 1 - yi]).astype(
        jnp.int32)
    layouts = dict(ffn1_w_gate="cols", ffn1_w_up="cols", ffn1_w_down="rows", w_in="stack", w_out="rows",
                   pool_w="stack", ffn2_w_gate="cols", ffn2_w_up="cols", ffn2_w_down="rows", meta_tokens="stack")
    shards2d = {n: wts[n].reshape(-1, wts[n].shape[-1]) for n in layouts}

    def place_stage(stage, dep):
        return [_cast_place(place, shards2d[n], dep, layouts[n], F32 if n == "meta_tokens" else BF16, f"place_{n}")
                for n in stages[stage]]

    def start_stage(stage, bufs, after):
        shapes = [shards2d[n].shape for n in stages[stage]]
        lays = [layouts[n] for n in stages[stage]]
        return _gather_start(bufs, after, shapes, lays, f"gather_start_{stage}") + (shapes, lays)

    flight = {stage_order[0]: start_stage(stage_order[0], place_stage(stage_order[0], place), place)}
    placed = {stage_order[1]: place_stage(stage_order[1], flight[stage_order[0]][3])}

    def cols(st):
        return jnp.transpose(st, (1, 0, 2)).reshape(st.shape[1], -1)

    forwarding = {}

    def request(stage, after):
        k = stage_order.index(stage)
        send_sems, recv_sems, bufs, _, shapes, lays = flight.pop(stage)
        afters = ([] if after is None else [after]) + [b for st in placed for b in placed[st]]
        afters += [flight[st][3] for st in flight]
        if k == 1:
            afters += [mom["w_in"].reshape(shards2d["w_in"].shape), var["w_in"].reshape(shards2d["w_in"].shape)]
        landed, token = _gather_wait(bufs, send_sems, recv_sems, afters, shapes, lays, f"gather_wait_{stage}")
        if k + 1 < len(stage_order) and stage_order[k + 1] not in flight:
            flight[stage_order[k + 1]] = start_stage(stage_order[k + 1], placed.pop(stage_order[k + 1]), token)
            token = flight[stage_order[k + 1]][3]
        if stage == "mix_in":
            flight["ffn2_up"] = start_stage("ffn2_up", placed.pop("ffn2_up"), token)
            token = flight["ffn2_up"][3]
        if k == 0:
            placed.update({st: place_stage(st, token) for st in stage_order[2:]})
            early = stage_order[2]
            last_cast = placed[stage_order[-1]][-1]
            flight[early] = start_stage(early, placed.pop(early), last_cast)
        forwarding[stage] = _forward_start(landed, token, shapes, lays, f"forward_start_{stage}") + (shapes, lays)
        return forwarding[stage][3]

    def get_weights(stage, after):
        if stage not in forwarding:
            request(stage, after)
        send_sems, recv_sems, bufs, token, shapes, lays = forwarding.pop(stage)
        full = _forward_wait(bufs, send_sems, recv_sems, token if after is None else after, shapes, lays,
                             f"forward_wait_{stage}")
        G = dict(zip(stages[stage], full))
        if stage == "meta":
            return dict(meta=cols(G["meta_tokens"]))
        if stage == "ffn1_gate":
            return dict(wg=G["ffn1_w_gate"])
        if stage == "ffn1_up":
            return dict(wu=G["ffn1_w_up"])
        if stage == "ffn2_up":
            return dict(wg=G["ffn2_w_gate"], wu=G["ffn2_w_up"])
        if stage == "mix_out":
            return dict(wout=G["w_out"])
        if stage != "mix_in":
            return dict(wd=G[stages[stage][0]])
        return dict(
            win=jnp.concatenate([G["w_in"][s] for s in range(N_CHIPS)] + [jnp.zeros((D, d_in_pad - d_in), BF16)],
                                axis=1),
            pool_w=jnp.transpose(G["pool_w"].reshape((N_CHIPS,) + pool_w.shape[1:]), (1, 0, 2, 3)).reshape(
                N_POOL_GROUPS, pool_w.shape[3], pool_w.shape[3]))

    def split_rows(a):
        return a.reshape(N_CHIPS, -1, a.shape[1])

    def split_win(a):
        return jnp.stack([a[:, s * d_in_shard:(s + 1) * d_in_shard] for s in range(N_CHIPS)])

    def split_pool(a):
        r, c = pool_w.shape[2], pool_w.shape[3]
        return jnp.transpose(a.reshape(N_POOL_GROUPS, N_CHIPS, r, c), (1, 0, 2, 3)).reshape(N_CHIPS, -1, c)

    scatter = {}
    pending = []

    def finish_pending(after):
        name, names, (send_sems, recv_sems, srcs, lands) = pending.pop()
        own, from_sib = _sibling_wait(srcs, lands, send_sems, recv_sems, after, _halves_copies,
                                      f"halves_wait_{name}")
        pair = [_pair_sum(place, o, s, f"pair_sum_{n}") for n, o, s in zip(names, own, from_sib)]
        send_sems, recv_sems, srcs, lands, token = _scatter_start(pair, f"scatter_start_{name}")
        scatter[name] = (names, own, from_sib, send_sems, recv_sems, srcs, lands, token)
        return token

    def put_grads(name, g):
        if name == "mix":
            names = ("w_in", "w_out", "pool_w")
            own = [split_win(g["win"]), split_rows(g["wout"]), split_pool(g["pool_w"])]
        else:
            names = (name,)
            own = [split_rows(g) if name.endswith("_down") else g]
        *flying, token = _sibling_start(own, _halves_copies, _halves_land_shape, f"halves_start_{name}")
        if pending:
            token = finish_pending(token)
        pending.append((name, names, flying))
        return token

    small = dict(ffn1_norm=ffn1_norm, mix_norm=mix_norm, ffn2_norm=ffn2_norm, q_norm=q_norm, k_norm=k_norm,
                 b_forget=b_forget, pool_scale=pool_scale)
    loss_part, dh0, gr = _local_step(x[0], loss_target[0], small, get_weights, request, put_grads)

    out_g, out_d, out_m, out_v = {}, {}, {}, {}

    def update(stage, names, flying, after):
        send_sems, recv_sems, srcs, lands = flying
        halves, other_halves = _sibling_wait(srcs, lands, send_sems, recv_sems, after, _whole_copies,
                                             f"swap_wait_{stage}")
        for n, mine, other in zip(names, halves, other_halves):
            shape = wts[n].shape
            res = _adamw_halves(place, mine, other, shards2d[n], mom[n].reshape(shards2d[n].shape),
                                var[n].reshape(shards2d[n].shape), f"adamw_{n}")
            out_g[n], out_d[n], out_m[n], out_v[n] = (a.reshape(shape) for a in res)
        return res[3]

    after = finish_pending(dh0)
    swapping = None
    for stage in scatter:
        names, own, from_sib, send_sems, recv_sems, srcs, lands, _ = scatter[stage]
        received = _scatter_wait(srcs, lands, send_sems, recv_sems, after, f"scatter_wait_{stage}")
        halves = [_sum_slabs(place, o, s, r, f"sum_{n}") for n, o, s, r in zip(names, own, from_sib, received)]
        *flying, after = _sibling_start(halves, _whole_copies, lambda shape: shape, f"swap_start_{stage}")
        if swapping is not None:
            after = update(*swapping, after)
        swapping = (stage, names, flying)
    update(*swapping, after)

    small_g = _pack_small({n: gr[n] for n in SMALL_NAMES})
    n_small = small_g.shape[0]
    small_g = small_g.at[n_small - 1, LOSS_LANE].set(loss_part)
    meta_rows = gr["meta"].reshape(-1, SMALL_COLS)
    total = _all_reduce_small(jnp.concatenate([small_g, meta_rows], axis=0))
    loss = total[n_small - 1, LOSS_LANE]
    res = _adamw([total[:n_small]], _pack_small({n: wts[n] for n in SMALL_NAMES}),
                 _pack_small({n: mom[n] for n in SMALL_NAMES}), _pack_small({n: var[n] for n in SMALL_NAMES}),
                 "adamw_small")
    for dst, packed in zip((out_g, out_d, out_m, out_v), res):
        dst.update(_unpack_small(packed, wts))
    meta_cols = meta_tokens.shape[1]
    meta_g = lax.dynamic_slice_in_dim(total[n_small:].reshape(N_META, D), me * meta_cols, meta_cols, axis=1)
    res = _adamw([meta_g], meta_tokens, m_meta_tokens, v_meta_tokens, "adamw_meta")
    out_g["meta_tokens"], out_d["meta_tokens"], out_m["meta_tokens"], out_v["meta_tokens"] = res

    grad_x = gr["x"].reshape(x.shape)
    return (loss, grad_x, *[out_g[n] for n in order], *[out_d[n] for n in order], *[out_m[n] for n in order],
            *[out_v[n] for n in order])
```
